```python
import math
import jax
import jax.numpy as jnp
from jax import lax
import numpy as np

D_MODEL = 2048
BATCH = 2
SEQ = 8192
DEPTH = 1
DEC_BATCH = 8
DEC_SEQ = 2048
PAST_LEN = 128

HEAD_DIM = 128
HEADS_PER_GROUP = 4
ATT_PATTERNS = ((128, 1), (512, 4), (2048, 16))
N_ATT_HEADS = HEADS_PER_GROUP * len(ATT_PATTERNS)
ATT_WIDTH = N_ATT_HEADS * HEAD_DIM
ATT_OUT = HEADS_PER_GROUP * HEAD_DIM
SSM_WIDTH = 1024
SSM_GROUP = 16
SSM_GROUPS = SSM_WIDTH // SSM_GROUP
SSM_STATE = 64
N_EXPERTS = 16
EXPERT_FF = 4096
CAPACITY_FACTOR = 2
N_BUCKETS = 32
MAX_DISTANCE = 1024
PLE_DIM = 256
EPS = 1e-6
NEG = -1e30
IN_COLS = 3 * ATT_WIDTH + SSM_WIDTH + 2 * D_MODEL

kernel_name = 'hybrid_dilated_attn_s5_ec_moe_encoder'


def rms_norm(x, g):
    xf = x.astype(jnp.float32)
    y = xf * lax.rsqrt(jnp.mean(xf * xf, axis=-1, keepdims=True) + EPS)
    return (y * g.astype(jnp.float32)).astype(x.dtype)


def head_rms(t, g):
    return t * lax.rsqrt(jnp.mean(t * t, axis=-1, keepdims=True) + EPS) * g.astype(jnp.float32)


def t5_buckets(rel):
    half = N_BUCKETS // 2
    max_exact = half // 2
    n = np.abs(rel)
    large = max_exact + (np.log(np.maximum(n, 1) / max_exact) / math.log(MAX_DISTANCE / max_exact)
                         * (half - max_exact)).astype(np.int32)
    large = np.minimum(large, half - 1)
    return ((rel > 0).astype(np.int32) * half + np.where(n < max_exact, n, large)).astype(np.int32)


def to_sub(t, dil):
    B, S = t.shape[0], t.shape[1]
    t = t.reshape((B, S // dil, dil) + t.shape[2:])
    return jnp.swapaxes(t, 1, 2)


def from_sub(t, dil, L):
    t = jnp.swapaxes(t[:, :, :L], 1, 2)
    return t.reshape((t.shape[0], L * dil) + t.shape[3:])


def dilated_window_attention(q, k, v, bias_g, window, dil):
    B, S, H, Dh = q.shape
    side = window // (2 * dil)
    L = S // dil
    nb = -(-L // side)
    Lp = nb * side
    qs = jnp.pad(to_sub(q, dil), ((0, 0), (0, 0), (0, Lp - L), (0, 0), (0, 0)))
    qs = qs.reshape(B, dil, nb, side, H, Dh)

    def windows(t):
        tp = jnp.pad(to_sub(t, dil), ((0, 0), (0, 0), (side, Lp - L + side), (0, 0), (0, 0)))
        tb = tp.reshape(B, dil, nb + 2, side, H, Dh)
        return jnp.concatenate([tb[:, :, :-2], tb[:, :, 1:-1], tb[:, :, 2:]], axis=3)

    kw = windows(k)
    vw = windows(v)
    rel = np.arange(3 * side)[None, :] - side - np.arange(side)[:, None]
    key_pos = np.arange(nb)[:, None] * side - side + np.arange(3 * side)[None, :]
    mask = (np.abs(rel) <= side)[None] & ((key_pos >= 0) & (key_pos < L))[:, None, :]
    bias = jnp.transpose(bias_g[t5_buckets(dil * rel)], (2, 0, 1)).astype(jnp.float32)
    logits = jnp.einsum('bdnqhc,bdnkhc->bdnhqk', qs, kw) + bias
    logits = jnp.where(mask[None, None, :, None], logits, NEG)
    lse = jax.nn.logsumexp(logits, axis=-1)
    probs = jnp.exp(logits - lse[..., None])
    o = jnp.einsum('bdnhqk,bdnkhc->bdnqhc', probs, vw).reshape(B, dil, Lp, H, Dh)
    lse = jnp.swapaxes(lse, 3, 4).reshape(B, dil, Lp, H)
    return from_sub(o, dil, L), from_sub(lse, dil, L)


def dilated_attention_mixer(qkv, q_gain, k_gain, rel_bias):
    B, S, _ = qkv.shape
    qkv = qkv.astype(jnp.float32).reshape(B, S, 3, N_ATT_HEADS, HEAD_DIM)
    q = head_rms(qkv[:, :, 0], q_gain) * (HEAD_DIM ** -0.5)
    k = head_rms(qkv[:, :, 1], k_gain)
    v = qkv[:, :, 2]
    outs, lses = [], []
    for gi, (window, dil) in enumerate(ATT_PATTERNS):
        hs = slice(gi * HEADS_PER_GROUP, (gi + 1) * HEADS_PER_GROUP)
        o, l = dilated_window_attention(q[:, :, hs], k[:, :, hs], v[:, :, hs], rel_bias[:, hs], window, dil)
        outs.append(o)
        lses.append(l)
    w = jax.nn.softmax(jnp.stack(lses), axis=0)
    o = jnp.einsum('gbsh,gbshc->bshc', w, jnp.stack(outs))
    return o.reshape(B, S, ATT_OUT)


def s5_scan(u_g, a_re, a_im, log_step, b_re, b_im, c_re, c_im, reverse):
    f32 = jnp.float32
    lam = lax.complex(a_re.astype(f32), a_im.astype(f32))
    step = jnp.exp(log_step.astype(f32))[:, None]
    lam_bar = jnp.exp(lam * step)
    b_bar = ((lam_bar - 1.0) / lam)[:, :, None] * lax.complex(b_re.astype(f32), b_im.astype(f32))
    bu = jnp.einsum('gph,bsgh->bsgp', b_bar, u_g.astype(jnp.complex64))
    a = jnp.broadcast_to(lam_bar, bu.shape)

    def combine(left, right):
        return (right[0] * left[0], right[0] * left[1] + right[1])

    _, states = lax.associative_scan(combine, (a, bu), reverse=reverse, axis=1)
    c = lax.complex(c_re.astype(f32), c_im.astype(f32))
    return jnp.einsum('gcp,bsgp->bsgc', c, states).real


def s5_mixer(u, a_re, a_im, log_step, b_re, b_im, c_re, c_im, d_skip, w_glu):
    B, S, _ = u.shape
    uf = u.astype(jnp.float32)
    u_g = uf.reshape(B, S, SSM_GROUPS, SSM_GROUP)
    y = (s5_scan(u_g, a_re[0], a_im[0], log_step[0], b_re[0], b_im[0], c_re[0], c_im[0], False)
         + s5_scan(u_g, a_re[1], a_im[1], log_step[1], b_re[1], b_im[1], c_re[1], c_im[1], True))
    y = jax.nn.gelu(y.reshape(B, S, SSM_WIDTH) + d_skip.astype(jnp.float32) * uf)
    z = y @ w_glu.astype(jnp.float32)
    return (z[..., :SSM_WIDTH] * jax.nn.sigmoid(z[..., SSM_WIDTH:])).astype(u.dtype)


def expert_choice_ffn(h, w_router, w_gate, w_up, w_down):
    B, S, D = h.shape
    n_tok = B * S
    cap = CAPACITY_FACTOR * n_tok // N_EXPERTS
    hf = h.reshape(n_tok, D)
    affinity = jax.nn.softmax((hf @ w_router).astype(jnp.float32), axis=-1)
    g, idx = lax.top_k(affinity.T, cap)
    xe = hf[idx]
    hid = jax.nn.silu(jnp.einsum('ecd,edf->ecf', xe, w_gate)) * jnp.einsum('ecd,edf->ecf', xe, w_up)
    ye = jnp.einsum('ecf,efd->ecd', hid, w_down) * g[..., None].astype(h.dtype)
    y = jnp.zeros_like(hf).at[idx.reshape(-1)].add(ye.reshape(-1, D))
    return y.reshape(B, S, D)


def encoder_trunk(x, p, rel_bias, g_mix, w_in, q_gain, k_gain, ssm_a_re, ssm_a_im, ssm_log_step,
                  ssm_b_re, ssm_b_im, ssm_c_re, ssm_c_im, ssm_d, w_glu, w_br_a, w_br_b, w_out,
                  g_ffn, w_router, w_exp_gate, w_exp_up, w_exp_down, g_ple, w_ple_gate, w_ple_proj):
    for i in range(DEPTH):
        h = rms_norm(x, g_mix[i])
        z = h @ w_in[i]
        qkv = z[..., :3 * ATT_WIDTH]
        u = z[..., 3 * ATT_WIDTH:3 * ATT_WIDTH + SSM_WIDTH]
        gates = jax.nn.sigmoid(z[..., 3 * ATT_WIDTH + SSM_WIDTH:])
        o_a = dilated_attention_mixer(qkv, q_gain[i], k_gain[i], rel_bias).astype(x.dtype)
        o_b = s5_mixer(u, ssm_a_re[i], ssm_a_im[i], ssm_log_step[i], ssm_b_re[i], ssm_b_im[i],
                       ssm_c_re[i], ssm_c_im[i], ssm_d[i], w_glu[i])
        merged = gates[..., :D_MODEL] * (o_a @ w_br_a[i]) + gates[..., D_MODEL:] * (o_b @ w_br_b[i])
        x = x + merged @ w_out[i]
        x = x + expert_choice_ffn(rms_norm(x, g_ffn[i]), w_router[i], w_exp_gate[i], w_exp_up[i], w_exp_down[i])
        x = x + jax.nn.sigmoid(rms_norm(x, g_ple[i]) @ w_ple_gate[i]) * (p[i] @ w_ple_proj[i])
    return x


def setup_inputs(seed: int = 0) -> dict:
    key = jax.random.key(seed)
    ks = jax.random.split(key, 32)
    f32 = jnp.float32

    def nrm(k, shape, scale):
        return jax.random.normal(k, shape, f32) * scale

    def gain(k, shape):
        return 1.0 + 0.05 * jax.random.normal(k, shape, f32)

    G, P, Hs = SSM_GROUPS, SSM_STATE, SSM_GROUP
    n_idx = jnp.arange(P, dtype=f32)
    return {
        'x_prompt': nrm(ks[0], (BATCH, SEQ, D_MODEL), 1.0),
        'x_sample': nrm(ks[1], (DEC_BATCH, DEC_SEQ, D_MODEL), 1.0),
        'p_prompt': nrm(ks[2], (DEPTH, BATCH, SEQ, PLE_DIM), 1.0),
        'p_sample': nrm(ks[3], (DEPTH, DEC_BATCH, DEC_SEQ, PLE_DIM), 1.0),
        'rel_bias': nrm(ks[4], (N_BUCKETS, N_ATT_HEADS), 0.5),
        'g_mix': gain(ks[5], (DEPTH, D_MODEL)),
        'w_in': nrm(ks[6], (DEPTH, D_MODEL, IN_COLS), D_MODEL ** -0.5),
        'q_gain': gain(ks[7], (DEPTH, N_ATT_HEADS, HEAD_DIM)),
        'k_gain': gain(ks[8], (DEPTH, N_ATT_HEADS, HEAD_DIM)),
        'ssm_a_re': -0.5 + 0.01 * jax.random.normal(ks[9], (DEPTH, 2, G, P), f32),
        'ssm_a_im': math.pi * n_idx + 0.01 * jax.random.normal(ks[10], (DEPTH, 2, G, P), f32),
        'ssm_log_step': jax.random.uniform(ks[11], (DEPTH, 2, G), f32, math.log(1e-3), math.log(1e-1)),
        'ssm_b_re': nrm(ks[12], (DEPTH, 2, G, P, Hs), (2 * Hs) ** -0.5),
        'ssm_b_im': nrm(ks[13], (DEPTH, 2, G, P, Hs), (2 * Hs) ** -0.5),
        'ssm_c_re': nrm(ks[14], (DEPTH, 2, G, Hs, P), (2 * P) ** -0.5),
        'ssm_c_im': nrm(ks[15], (DEPTH, 2, G, Hs, P), (2 * P) ** -0.5),
        'ssm_d': nrm(ks[16], (DEPTH, SSM_WIDTH), 1.0),
        'w_glu': nrm(ks[17], (DEPTH, SSM_WIDTH, 2 * SSM_WIDTH), SSM_WIDTH ** -0.5),
        'w_br_a': nrm(ks[18], (DEPTH, ATT_OUT, D_MODEL), ATT_OUT ** -0.5),
        'w_br_b': nrm(ks[19], (DEPTH, SSM_WIDTH, D_MODEL), SSM_WIDTH ** -0.5),
        'w_out': nrm(ks[20], (DEPTH, D_MODEL, D_MODEL), D_MODEL ** -0.5),
        'g_ffn': gain(ks[21], (DEPTH, D_MODEL)),
        'w_router': nrm(ks[22], (DEPTH, D_MODEL, N_EXPERTS), D_MODEL ** -0.5),
        'w_exp_gate': nrm(ks[23], (DEPTH, N_EXPERTS, D_MODEL, EXPERT_FF), D_MODEL ** -0.5),
        'w_exp_up': nrm(ks[24], (DEPTH, N_EXPERTS, D_MODEL, EXPERT_FF), D_MODEL ** -0.5),
        'w_exp_down': nrm(ks[25], (DEPTH, N_EXPERTS, EXPERT_FF, D_MODEL), EXPERT_FF ** -0.5),
        'g_ple': gain(ks[26], (DEPTH, D_MODEL)),
        'w_ple_gate': nrm(ks[27], (DEPTH, D_MODEL, D_MODEL), D_MODEL ** -0.5),
        'w_ple_proj': nrm(ks[28], (DEPTH, PLE_DIM, D_MODEL), PLE_DIM ** -0.5),
    }


def reference(x_prompt, x_sample, p_prompt, p_sample, rel_bias, g_mix, w_in, q_gain, k_gain,
              ssm_a_re, ssm_a_im, ssm_log_step, ssm_b_re, ssm_b_im, ssm_c_re, ssm_c_im, ssm_d, w_glu,
              w_br_a, w_br_b, w_out, g_ffn, w_router, w_exp_gate, w_exp_up, w_exp_down,
              g_ple, w_ple_gate, w_ple_proj):
    weights = (rel_bias, g_mix, w_in, q_gain, k_gain, ssm_a_re, ssm_a_im, ssm_log_step,
               ssm_b_re, ssm_b_im, ssm_c_re, ssm_c_im, ssm_d, w_glu, w_br_a, w_br_b, w_out,
               g_ffn, w_router, w_exp_gate, w_exp_up, w_exp_down, g_ple, w_ple_gate, w_ple_proj)
    y_prompt = encoder_trunk(x_prompt, p_prompt, *weights)
    y_sample = encoder_trunk(x_sample, p_sample, *weights)
    return (y_prompt, y_sample)
```

```python
import functools
import math

import jax
import jax.numpy as jnp
import numpy as np
from jax import lax
from jax.experimental import pallas as pl
from jax.experimental.pallas import tpu as pltpu

F32 = jnp.float32
BF16 = jnp.bfloat16

HEAD_DIM = 128
HEADS_PER_GROUP = 4
ATT_PATTERNS = ((128, 1), (512, 4), (2048, 16))
ATT_SIDE = 64
N_BUCKETS = 32
MAX_DISTANCE = 1024
CAPACITY_FACTOR = 2
EPS = 1e-6
NEG = -1e30
SSM_LANE_GROUPS = 8
V7X_VMEM_LIMIT = 56 * 1024 * 1024

assert all(w // (2 * d) == ATT_SIDE for w, d in ATT_PATTERNS)


def _cparams(semantics, vmem=V7X_VMEM_LIMIT):
    return pltpu.CompilerParams(dimension_semantics=("arbitrary",) * len(semantics), vmem_limit_bytes=vmem)


def _sigmoid(x):
    return 1.0 / (1.0 + jnp.exp(-x))


def _rms_kernel(x_ref, g_ref, o_ref):
    x = x_ref[...]
    y = x * lax.rsqrt(jnp.mean(x * x, axis=-1, keepdims=True) + EPS)
    o_ref[...] = (y * g_ref[...]).astype(o_ref.dtype)


def rms_bf16(x, g, tm=512):
    n, d = x.shape
    return pl.pallas_call(
        _rms_kernel,
        grid=(n // tm,),
        in_specs=[pl.BlockSpec((tm, d), lambda i: (i, 0)), pl.BlockSpec((1, d), lambda i: (0, 0))],
        out_specs=pl.BlockSpec((tm, d), lambda i: (i, 0)),
        out_shape=jax.ShapeDtypeStruct((n, d), BF16),
        compiler_params=_cparams(("parallel",)),
        name="rms_bf16",
    )(x, g.reshape(1, d).astype(F32))


def _head_rms(acc, gain):
    outs = []
    for h in range(acc.shape[1] // HEAD_DIM):
        t = acc[:, h * HEAD_DIM:(h + 1) * HEAD_DIM]
        outs.append(t * lax.rsqrt(jnp.mean(t * t, axis=-1, keepdims=True) + EPS))
    return jnp.concatenate(outs, axis=1) * gain


def _qkv_kernel(x_ref, w_ref, g_ref, o_ref):
    j = pl.program_id(0)
    rows = x_ref.shape[0] * x_ref.shape[1]
    x = x_ref[...].reshape(rows, x_ref.shape[2])
    acc = jnp.dot(x, w_ref[...], preferred_element_type=F32)

    @pl.when(j < 2)
    def _():
        o_ref[...] = _head_rms(acc, g_ref[0]).astype(o_ref.dtype).reshape(o_ref.shape)

    @pl.when(j == 2)
    def _():
        o_ref[...] = acc.astype(o_ref.dtype).reshape(o_ref.shape)


def qkv_proj(hn, w, gains, batch, seq, dil):
    d_model = hn.shape[1]
    l = seq // dil
    tm = min(l, 512)
    bb = max(1, min(batch, 1024 // tm))
    tn = HEADS_PER_GROUP * HEAD_DIM
    x = hn.reshape(batch, l, dil * d_model)
    nl = l // tm
    return pl.pallas_call(
        _qkv_kernel,
        grid=(3, batch // bb, dil, nl),
        in_specs=[
            pl.BlockSpec((bb, tm, d_model), lambda j, b, r, i: (b, i, r)),
            pl.BlockSpec((d_model, tn), lambda j, b, r, i: (0, j)),
            pl.BlockSpec((1, 1, tn), lambda j, b, r, i: (j, 0, 0)),
        ],
        out_specs=pl.BlockSpec((bb, tm, tn), lambda j, b, r, i: (b, r * nl + i, j)),
        out_shape=jax.ShapeDtypeStruct((batch, dil * l, 3 * tn), BF16),
        compiler_params=_cparams(("arbitrary", "parallel", "parallel", "parallel")),
        name=f"qkv_proj_d{dil}",
    )(x, w, gains)


def _u_kernel(x_ref, w_ref, o_ref):
    o_ref[...] = jnp.dot(x_ref[0], w_ref[...], preferred_element_type=F32)


def u_proj(hn, w, vbatch, vseq, tm=512):
    d_model = hn.shape[1]
    width = w.shape[1]
    tn = 512
    nj = width // tn
    x = hn.reshape(vbatch, vseq, d_model)
    return pl.pallas_call(
        _u_kernel,
        grid=(nj, vbatch, vseq // tm),
        in_specs=[
            pl.BlockSpec((1, tm, d_model), lambda j, b, i: (b, i, 0)),
            pl.BlockSpec((d_model, tn), lambda j, b, i: (0, j)),
        ],
        out_specs=pl.BlockSpec((tm, tn), lambda j, b, i: (i, b * nj + j)),
        out_shape=jax.ShapeDtypeStruct((vseq, vbatch * width), F32),
        compiler_params=_cparams(("arbitrary", "parallel", "parallel")),
        name="u_proj",
    )(x, w)


def _gates_kernel(x_ref, w_ref, o_ref):
    acc = jnp.dot(x_ref[...], w_ref[...], preferred_element_type=F32)
    o_ref[...] = _sigmoid(acc).astype(o_ref.dtype)


def gates_proj(hn, w, tm=1024, tn=512):
    n, d_model = hn.shape
    width = w.shape[1]
    return pl.pallas_call(
        _gates_kernel,
        grid=(width // tn, n // tm),
        in_specs=[
            pl.BlockSpec((tm, d_model), lambda j, i: (i, 0)),
            pl.BlockSpec((d_model, tn), lambda j, i: (0, j)),
        ],
        out_specs=pl.BlockSpec((tm, tn), lambda j, i: (i, j)),
        out_shape=jax.ShapeDtypeStruct((n, width), BF16),
        compiler_params=_cparams(("arbitrary", "parallel")),
        name="gates_proj",
    )(hn, w)


def _t5_buckets(rel):
    half = N_BUCKETS // 2
    max_exact = half // 2
    n = np.abs(rel)
    large = max_exact + (np.log(np.maximum(n, 1) / max_exact) / math.log(MAX_DISTANCE / max_exact)
                         * (half - max_exact)).astype(np.int32)
    large = np.minimum(large, half - 1)
    return ((rel > 0).astype(np.int32) * half + np.where(n < max_exact, n, large)).astype(np.int32)


def _attn_geometry(l):
    tqb = min(2 * ATT_SIDE, l)
    win = min(4 * ATT_SIDE, l)
    return tqb, win


def attn_bias_tables(rel_bias_g, dil, l):
    tqb, win = _attn_geometry(l)
    i = np.arange(tqb)[:, None]
    j = np.arange(win)[None, :]
    tables = []
    for shift in range(3):
        rel = j - i - ATT_SIDE * shift
        valid = np.abs(rel) <= ATT_SIDE
        b = jnp.transpose(rel_bias_g.astype(F32)[_t5_buckets(dil * rel)], (2, 0, 1))
        tables.append(jnp.where(valid[None], b, NEG))
    return jnp.stack(tables)


def _attn_kernel(q_ref, k_ref, v_ref, bias_ref, o_ref, lse_ref, *, l, tq):
    tqb, win = _attn_geometry(l)
    qi = pl.program_id(2)
    for c in range(tq // tqb):
        q0 = qi * tq + c * tqb
        start = jnp.clip(q0 - ATT_SIDE, 0, l - win)
        start = pl.multiple_of(start, ATT_SIDE)
        shift = (q0 - start) // ATT_SIDE
        q = q_ref[c * tqb:(c + 1) * tqb, :]
        kw = k_ref[pl.ds(start, win), :]
        vw = v_ref[pl.ds(start, win), :]
        logits = lax.dot_general(q, kw, (((1,), (1,)), ((), ())), preferred_element_type=F32)
        logits = logits + bias_ref[shift]
        m = jnp.max(logits, axis=-1, keepdims=True)
        p = jnp.exp(logits - m)
        s = jnp.sum(p, axis=-1, keepdims=True)
        o = jnp.dot(p.astype(vw.dtype), vw, preferred_element_type=F32)
        o_ref[c * tqb:(c + 1) * tqb, :] = o / s
        lse_ref[c * tqb:(c + 1) * tqb, :] = jnp.broadcast_to(m + jnp.log(s), (tqb, HEAD_DIM))


def dilated_attention(qkv, bias, batch, seq, dil):
    l = seq // dil
    tqb, win = _attn_geometry(l)
    tq = min(l, 4 * tqb)
    hw = HEADS_PER_GROUP * HEAD_DIM
    x = qkv.reshape(batch * dil, l, 3 * hw)
    out_sds = jax.ShapeDtypeStruct((batch, l, dil * hw), F32)
    out_spec = pl.BlockSpec((None, tq, HEAD_DIM),
                            lambda bd, h, i: (bd // dil, i, (bd % dil) * HEADS_PER_GROUP + h))
    o, lse = pl.pallas_call(
        functools.partial(_attn_kernel, l=l, tq=tq),
        grid=(batch * dil, HEADS_PER_GROUP, l // tq),
        in_specs=[
            pl.BlockSpec((None, tq, HEAD_DIM), lambda bd, h, i: (bd, i, h)),
            pl.BlockSpec((None, l, HEAD_DIM), lambda bd, h, i: (bd, 0, HEADS_PER_GROUP + h)),
            pl.BlockSpec((None, l, HEAD_DIM), lambda bd, h, i: (bd, 0, 2 * HEADS_PER_GROUP + h)),
            pl.BlockSpec((3, None, tqb, win), lambda bd, h, i: (0, h, 0, 0)),
        ],
        out_specs=[out_spec, out_spec],
        out_shape=[out_sds, out_sds],
        compiler_params=_cparams(("parallel", "parallel", "parallel")),
        name=f"dilated_attn_d{dil}",
    )(x, x, x, bias)
    return o.reshape(batch * seq, hw), lse.reshape(batch * seq, hw)


def ssm_discretise(a_re, a_im, log_step, b_re, b_im, c_re, c_im):
    lam = lax.complex(a_re.astype(F32), a_im.astype(F32))
    step = jnp.exp(log_step.astype(F32))[..., None]
    lam_bar = jnp.exp(lam * step)
    b_bar = ((lam_bar - 1.0) / lam)[..., None] * lax.complex(b_re.astype(F32), b_im.astype(F32))
    c = lax.complex(c_re.astype(F32), c_im.astype(F32))
    return lam_bar, b_bar, c


def ssm_block_weights(lam_bar, b_bar, c):
    _, g, p, h = b_bar.shape
    lg = SSM_LANE_GROUPS
    nj = g // lg
    eye = jnp.eye(lg, dtype=F32)
    bb = b_bar.reshape(2, nj, lg, p, h)
    wb_re = jnp.einsum('djgph,gk->djghkp', bb.real, eye).reshape(2, nj, lg * h, lg * p)
    wb_im = jnp.einsum('djgph,gk->djghkp', bb.imag, eye).reshape(2, nj, lg * h, lg * p)
    wb = jnp.concatenate([wb_re, wb_im], axis=-1).astype(BF16)
    cc = c.reshape(2, nj, lg, h, p)
    wc_re = jnp.einsum('djgcp,gk->djgpkc', cc.real, eye).reshape(2, nj, lg * p, lg * h)
    wc_im = jnp.einsum('djgcp,gk->djgpkc', -cc.imag, eye).reshape(2, nj, lg * p, lg * h)
    wc = jnp.concatenate([wc_re, wc_im], axis=-2).astype(BF16)
    lb = lam_bar.reshape(2, nj, 1, 1, lg * p)
    lam = jnp.concatenate([lb.real, lb.imag], axis=2)
    lam = jnp.broadcast_to(lam, (2, nj, 2, 8, lg * p)).astype(F32)
    return wb, wc, lam


def _s5_kernel(u_ref, wb_ref, wc_ref, lam_ref, x0_ref, y_ref, xend_ref, bu_ref, xs_ref, st_ref, *, tt):
    direction = pl.program_id(0)
    ti = pl.program_id(2)
    nt = pl.num_programs(2)
    vb = u_ref.shape[1]
    k = lam_ref.shape[-1]

    @pl.when(ti == 0)
    def _():
        st_ref[...] = x0_ref[...]

    u = u_ref[...].reshape(tt * vb, u_ref.shape[2]).astype(BF16)
    bu_ref[...] = jnp.dot(u, wb_ref[...], preferred_element_type=F32).reshape(tt, vb, 2 * k)
    lr = lam_ref[0]
    li = lam_ref[1]

    def step(s, carry):
        xr, xi = carry
        t = jnp.where(direction == 0, s, tt - 1 - s)
        b = bu_ref[t]
        nr = lr * xr - li * xi + b[:, :k]
        ni = lr * xi + li * xr + b[:, k:]
        xs_ref[t] = jnp.concatenate([nr, ni], axis=1)
        return nr, ni

    xr, xi = lax.fori_loop(0, tt, step, (st_ref[0], st_ref[1]), unroll=8)
    st_ref[0] = xr
    st_ref[1] = xi
    xs = xs_ref[...].reshape(tt * vb, 2 * k).astype(BF16)
    y_ref[...] = jnp.dot(xs, wc_ref[...], preferred_element_type=F32).reshape(y_ref.shape)

    @pl.when(ti == nt - 1)
    def _():
        xend_ref[0] = xr
        xend_ref[1] = xi


def s5_scan(u_tm, wb, wc, lam, x0, vbatch, tt=128):
    vseq, _, width = u_tm.shape
    nj = wb.shape[1]
    k = lam.shape[-1]
    lanes = width // nj
    nt = vseq // tt

    def tmap(d, j, t):
        return jnp.where(d == 0, t, nt - 1 - t)

    return pl.pallas_call(
        functools.partial(_s5_kernel, tt=tt),
        grid=(2, nj, nt),
        in_specs=[
            pl.BlockSpec((tt, vbatch, lanes), lambda d, j, t: (tmap(d, j, t), 0, j)),
            pl.BlockSpec((None, None, lanes, 2 * k), lambda d, j, t: (d, j, 0, 0)),
            pl.BlockSpec((None, None, 2 * k, lanes), lambda d, j, t: (d, j, 0, 0)),
            pl.BlockSpec((None, None, 2, vbatch, k), lambda d, j, t: (d, j, 0, 0, 0)),
            pl.BlockSpec((None, None, 2, vbatch, k), lambda d, j, t: (d, j, 0, 0, 0)),
        ],
        out_specs=[
            pl.BlockSpec((None, tt, vbatch, lanes), lambda d, j, t: (d, tmap(d, j, t), 0, j)),
            pl.BlockSpec((None, None, 2, vbatch, k), lambda d, j, t: (d, j, 0, 0, 0)),
        ],
        out_shape=[
            jax.ShapeDtypeStruct((2, vseq, vbatch, width), F32),
            jax.ShapeDtypeStruct((2, nj, 2, vbatch, k), F32),
        ],
        scratch_shapes=[
            pltpu.VMEM((tt, vbatch, 2 * k), F32),
            pltpu.VMEM((tt, vbatch, 2 * k), F32),
            pltpu.VMEM((2, vbatch, k), F32),
        ],
        compiler_params=_cparams(("parallel", "parallel", "arbitrary")),
        name="s5_scan",
    )(u_tm, wb, wc, lam, x0)


def s5_segment_carries(xend, lam_bar, seg_len, batch, nseg):
    _, nj, _, _, k = xend.shape
    lam_pow = (lam_bar ** seg_len).reshape(2, nj, 1, k)
    e = lax.complex(xend[:, :, 0], xend[:, :, 1]).reshape(2, nj, batch, nseg, k)
    fwd = [jnp.zeros((nj, batch, k), jnp.complex64)]
    for s in range(1, nseg):
        fwd.append(lam_pow[0] * fwd[-1] + e[0, :, :, s - 1])
    bwd = [jnp.zeros((nj, batch, k), jnp.complex64)]
    for s in range(nseg - 2, -1, -1):
        bwd.append(lam_pow[1] * bwd[-1] + e[1, :, :, s + 1])
    bwd = bwd[::-1]
    x_in = jnp.stack([jnp.stack(fwd, axis=2), jnp.stack(bwd, axis=2)])
    x_in = x_in.reshape(2, nj, batch * nseg, k)
    return jnp.stack([x_in.real, x_in.imag], axis=2).astype(F32)


def _gelu_tanh(x):
    return 0.5 * x * (1.0 + jnp.tanh(math.sqrt(2.0 / math.pi) * (x + 0.044715 * x * x * x)))


def _glu_kernel(yf_ref, yb_ref, u_ref, d_ref, wa_ref, wb_ref, o_ref):
    y = _gelu_tanh(yf_ref[...] + yb_ref[...] + d_ref[...] * u_ref[...]).astype(BF16)
    za = jnp.dot(y, wa_ref[...], preferred_element_type=F32)
    zb = jnp.dot(y, wb_ref[...], preferred_element_type=F32)
    o_ref[...] = (za * _sigmoid(zb)).astype(o_ref.dtype)


def ssm_glu(y, u_tm, d_skip, w_glu, vbatch, tm=512):
    _, vseq, _, width = y.shape
    y2 = y.reshape(2, vseq, vbatch * width)
    u2 = u_tm.reshape(vseq, vbatch * width)
    ni = vseq // tm
    wa = w_glu[:, :width]
    wb = w_glu[:, width:]
    const = lambda b, i: (0, 0)
    return pl.pallas_call(
        _glu_kernel,
        grid=(vbatch, ni),
        in_specs=[
            pl.BlockSpec((None, tm, width), lambda b, i: (0, i, b)),
            pl.BlockSpec((None, tm, width), lambda b, i: (1, i, b)),
            pl.BlockSpec((tm, width), lambda b, i: (i, b)),
            pl.BlockSpec((1, width), const),
            pl.BlockSpec((width, width), const),
            pl.BlockSpec((width, width), const),
        ],
        out_specs=pl.BlockSpec((tm, width), lambda b, i: (b * ni + i, 0)),
        out_shape=jax.ShapeDtypeStruct((vbatch * vseq, width), BF16),
        compiler_params=_cparams(("parallel", "parallel")),
        name="ssm_glu",
    )(y2, y2, u2, d_skip.reshape(1, width).astype(F32), wa, wb)


def _split_bf16(x):
    hi = x.astype(BF16)
    lo = (x - hi.astype(F32)).astype(BF16)
    return hi, lo


def _mixer_out_kernel(o0, o1, o2, l0, l1, l2, ob_ref, gate_ref, x_ref, wa_ref, wb_ref, wo_ref,
                      g_ref, wrh_ref, wrl_ref, x1_ref, h_ref, lg_ref):
    d_model = x_ref.shape[1]
    la, lb, lc = l0[...], l1[...], l2[...]
    m = jnp.maximum(jnp.maximum(la, lb), lc)
    ea, eb, ec = jnp.exp(la - m), jnp.exp(lb - m), jnp.exp(lc - m)
    oa = (ea * o0[...] + eb * o1[...] + ec * o2[...]) / (ea + eb + ec)
    pa = jnp.dot(oa.astype(BF16), wa_ref[...], preferred_element_type=F32)
    pb = jnp.dot(ob_ref[...], wb_ref[...], preferred_element_type=F32)
    gates = gate_ref[...].astype(F32)
    merged = gates[:, :d_model] * pa + gates[:, d_model:] * pb
    x1 = x_ref[...] + jnp.dot(merged.astype(BF16), wo_ref[...], preferred_element_type=F32)
    x1_ref[...] = x1
    h = x1 * lax.rsqrt(jnp.mean(x1 * x1, axis=-1, keepdims=True) + EPS) * g_ref[...]
    h_ref[...] = h.astype(h_ref.dtype)
    hh, hl = _split_bf16(h)
    lg_ref[...] = (jnp.dot(hh, wrh_ref[...], preferred_element_type=F32)
                   + jnp.dot(hh, wrl_ref[...], preferred_element_type=F32)
                   + jnp.dot(hl, wrh_ref[...], preferred_element_type=F32))


def mixer_out(o_list, lse_list, o_b, gates, x, w_br_a, w_br_b, w_out, g_ffn, wr_hi, wr_lo, tm=256):
    n, d_model = x.shape
    row = lambda w: pl.BlockSpec((tm, w), lambda i: (i, 0))
    full = lambda a: pl.BlockSpec(a.shape, lambda i: (0,) * a.ndim, pipeline_mode=pl.Buffered(1))
    g2 = g_ffn.reshape(1, d_model).astype(F32)
    er = wr_hi.shape[1]
    return pl.pallas_call(
        _mixer_out_kernel,
        grid=(n // tm,),
        in_specs=[row(o.shape[1]) for o in o_list] + [row(l.shape[1]) for l in lse_list] + [
            row(o_b.shape[1]), row(gates.shape[1]), row(d_model),
            full(w_br_a), full(w_br_b), full(w_out), full(g2), full(wr_hi), full(wr_lo)],
        out_specs=[row(d_model), row(d_model), row(er)],
        out_shape=[jax.ShapeDtypeStruct((n, d_model), F32), jax.ShapeDtypeStruct((n, d_model), BF16),
                   jax.ShapeDtypeStruct((n, er), F32)],
        compiler_params=_cparams(("parallel",)),
        name="mixer_out",
    )(*o_list, *lse_list, o_b, gates, x, w_br_a, w_br_b, w_out, g2, wr_hi, wr_lo)


def _expert_up_kernel(x_ref, wg_ref, wu_ref, o_ref, wg_s, wu_s):
    @pl.when(pl.program_id(2) == 0)
    def _():
        wg_s[...] = wg_ref[...].astype(BF16)
        wu_s[...] = wu_ref[...].astype(BF16)

    x = x_ref[...]
    a = jnp.dot(x, wg_s[...], preferred_element_type=F32)
    b = jnp.dot(x, wu_s[...], preferred_element_type=F32)
    o_ref[...] = (a * _sigmoid(a) * b).astype(o_ref.dtype)


def expert_up(xe, w_gate, w_up, tm=1024, tf=512):
    e, cap, d_model = xe.shape
    ff = w_gate.shape[2]
    tm = min(tm, cap)
    return pl.pallas_call(
        _expert_up_kernel,
        grid=(e, ff // tf, cap // tm),
        in_specs=[
            pl.BlockSpec((None, tm, d_model), lambda ei, f, m: (ei, m, 0)),
            pl.BlockSpec((None, d_model, tf), lambda ei, f, m: (ei, 0, f)),
            pl.BlockSpec((None, d_model, tf), lambda ei, f, m: (ei, 0, f)),
        ],
        out_specs=pl.BlockSpec((None, tm, tf), lambda ei, f, m: (ei, m, f)),
        out_shape=jax.ShapeDtypeStruct((e, cap, ff), BF16),
        scratch_shapes=[pltpu.VMEM((d_model, tf), BF16), pltpu.VMEM((d_model, tf), BF16)],
        compiler_params=_cparams(("parallel", "parallel", "arbitrary")),
        name="expert_up",
    )(xe, w_gate, w_up)


def _expert_down_kernel(h_ref, w_ref, g_ref, o_ref, w_s):
    @pl.when(pl.program_id(2) == 0)
    def _():
        w_s[...] = w_ref[...].astype(BF16)

    o_ref[...] = jnp.dot(h_ref[...], w_s[...], preferred_element_type=F32) * g_ref[...]


def expert_down(hid, w_down, gate, tm=1024, tn=512):
    e, cap, ff = hid.shape
    d_model = w_down.shape[2]
    tm = min(tm, cap)
    tn = min(tn, d_model)
    return pl.pallas_call(
        _expert_down_kernel,
        grid=(e, d_model // tn, cap // tm),
        in_specs=[
            pl.BlockSpec((None, tm, ff), lambda ei, n, m: (ei, m, 0)),
            pl.BlockSpec((None, ff, tn), lambda ei, n, m: (ei, 0, n)),
            pl.BlockSpec((None, tm, 1), lambda ei, n, m: (ei, m, 0)),
        ],
        out_specs=pl.BlockSpec((None, tm, tn), lambda ei, n, m: (ei, m, n)),
        out_shape=jax.ShapeDtypeStruct((e, cap, d_model), F32),
        scratch_shapes=[pltpu.VMEM((ff, tn), BF16)],
        compiler_params=_cparams(("parallel", "parallel", "arbitrary")),
        name="expert_down",
    )(hid, w_down, gate.reshape(e, cap, 1).astype(F32))


def _final_kernel(x1_ref, moe_ref, p_ref, g_ref, wg_ref, wp_ref, o_ref):
    x2 = x1_ref[...] + moe_ref[...]
    h = (x2 * lax.rsqrt(jnp.mean(x2 * x2, axis=-1, keepdims=True) + EPS) * g_ref[...]).astype(BF16)
    gate = _sigmoid(jnp.dot(h, wg_ref[...], preferred_element_type=F32))
    proj = jnp.dot(p_ref[...].astype(BF16), wp_ref[...], preferred_element_type=F32)
    o_ref[...] = x2 + gate * proj


def final_embed(x1, moe, p, g_ple, w_gate, w_proj, tm=256):
    n, d_model = x1.shape
    row = lambda w: pl.BlockSpec((tm, w), lambda i: (i, 0))
    full = lambda a: pl.BlockSpec(a.shape, lambda i: (0,) * a.ndim, pipeline_mode=pl.Buffered(1))
    g2 = g_ple.reshape(1, d_model).astype(F32)
    return pl.pallas_call(
        _final_kernel,
        grid=(n // tm,),
        in_specs=[row(d_model), row(d_model), row(p.shape[1]), full(g2), full(w_gate), full(w_proj)],
        out_specs=row(d_model),
        out_shape=jax.ShapeDtypeStruct((n, d_model), F32),
        compiler_params=_cparams(("parallel",)),
        name="final_embed",
    )(x1, moe, p, g2, w_gate, w_proj)


def _prepare_weights(rel_bias, g_mix, w_in, q_gain, k_gain, ssm_a_re, ssm_a_im, ssm_log_step,
                     ssm_b_re, ssm_b_im, ssm_c_re, ssm_c_im, ssm_d, w_glu, w_br_a, w_br_b, w_out,
                     g_ffn, w_router, g_ple, w_ple_gate, w_ple_proj):
    n_heads = q_gain.shape[0]
    att_w = n_heads * HEAD_DIM
    hw = HEADS_PER_GROUP * HEAD_DIM
    ssm_w = ssm_d.shape[0]
    w_in_bf = w_in.astype(BF16)
    w_qkv, gains = [], []
    qg = q_gain.astype(F32) * (HEAD_DIM ** -0.5)
    kg = k_gain.astype(F32)
    for gi in range(len(ATT_PATTERNS)):
        cols = [w_in_bf[:, s * att_w + gi * hw: s * att_w + (gi + 1) * hw] for s in range(3)]
        w_qkv.append(jnp.concatenate(cols, axis=1))
        hs = slice(gi * HEADS_PER_GROUP, (gi + 1) * HEADS_PER_GROUP)
        gains.append(jnp.stack([qg[hs].reshape(1, hw), kg[hs].reshape(1, hw), jnp.ones((1, hw), F32)]))
    lam_bar, b_bar, c = ssm_discretise(ssm_a_re, ssm_a_im, ssm_log_step, ssm_b_re, ssm_b_im, ssm_c_re, ssm_c_im)
    wb, wc, lam = ssm_block_weights(lam_bar, b_bar, c)
    n_exp = w_router.shape[1]
    wr = jnp.pad(w_router.astype(F32), ((0, 0), (0, 128 - n_exp)))
    wr_hi, wr_lo = _split_bf16(wr)
    return dict(
        rel_bias=rel_bias, g_mix=g_mix, w_qkv=w_qkv, gains=gains,
        w_u=w_in_bf[:, 3 * att_w:3 * att_w + ssm_w], w_gates=w_in_bf[:, 3 * att_w + ssm_w:],
        lam_bar=lam_bar, wb=wb, wc=wc, lam=lam, ssm_d=ssm_d, w_glu=w_glu.astype(BF16),
        w_br_a=w_br_a.astype(BF16), w_br_b=w_br_b.astype(BF16), w_out=w_out.astype(BF16),
        g_ffn=g_ffn, wr_hi=wr_hi, wr_lo=wr_lo, n_exp=n_exp, g_ple=g_ple,
        w_ple_gate=w_ple_gate.astype(BF16), w_ple_proj=w_ple_proj.astype(BF16))


def _ssm_branch(hn, wts, batch, seq):
    nseg = max(1, 8 // batch)
    vbatch = batch * nseg
    vseq = seq // nseg
    width = wts['w_u'].shape[1]
    u_tm = u_proj(hn, wts['w_u'], vbatch, vseq).reshape(vseq, vbatch, width)
    nj, k = wts['lam'].shape[1], wts['lam'].shape[-1]
    x0 = jnp.zeros((2, nj, 2, vbatch, k), F32)
    if nseg > 1:
        _, xend = s5_scan(u_tm, wts['wb'], wts['wc'], wts['lam'], x0, vbatch)
        x0 = s5_segment_carries(xend, wts['lam_bar'], vseq, batch, nseg)
    y, _ = s5_scan(u_tm, wts['wb'], wts['wc'], wts['lam'], x0, vbatch)
    return ssm_glu(y, u_tm, wts['ssm_d'], wts['w_glu'], vbatch)


def _expert_choice(h, logits, n_exp, w_exp_gate, w_exp_up, w_exp_down):
    n_tok, d_model = h.shape
    cap = CAPACITY_FACTOR * n_tok // n_exp
    affinity = jax.nn.softmax(logits[:, :n_exp], axis=-1)
    gate, idx = lax.top_k(affinity.T, cap)
    xe = jnp.take(h, idx, axis=0)
    hid = expert_up(xe, w_exp_gate, w_exp_up)
    ye = expert_down(hid, w_exp_down, gate)
    return jnp.zeros((n_tok, d_model), F32).at[idx.reshape(-1)].add(ye.reshape(-1, d_model))


def _trunk(x, p, wts, w_exp_gate, w_exp_up, w_exp_down):
    batch, seq, d_model = x.shape
    n = batch * seq
    xf = x.reshape(n, d_model)
    hn = rms_bf16(xf, wts['g_mix'])
    o_list, lse_list = [], []
    for gi, (_, dil) in enumerate(ATT_PATTERNS):
        qkv = qkv_proj(hn, wts['w_qkv'][gi], wts['gains'][gi], batch, seq, dil)
        hs = slice(gi * HEADS_PER_GROUP, (gi + 1) * HEADS_PER_GROUP)
        bias = attn_bias_tables(wts['rel_bias'][:, hs], dil, seq // dil)
        o, lse = dilated_attention(qkv, bias, batch, seq, dil)
        o_list.append(o)
        lse_list.append(lse)
    o_b = _ssm_branch(hn, wts, batch, seq)
    gates = gates_proj(hn, wts['w_gates'])
    x1, h2, logits = mixer_out(o_list, lse_list, o_b, gates, xf, wts['w_br_a'], wts['w_br_b'],
                               wts['w_out'], wts['g_ffn'], wts['wr_hi'], wts['wr_lo'])
    moe = _expert_choice(h2, logits, wts['n_exp'], w_exp_gate, w_exp_up, w_exp_down)
    out = final_embed(x1, moe, p.reshape(n, p.shape[-1]), wts['g_ple'], wts['w_ple_gate'], wts['w_ple_proj'])
    return out.reshape(batch, seq, d_model)


def kernel(x_prompt, x_sample, p_prompt, p_sample, rel_bias, g_mix, w_in, q_gain, k_gain, ssm_a_re, ssm_a_im, ssm_log_step, ssm_b_re, ssm_b_im, ssm_c_re, ssm_c_im, ssm_d, w_glu, w_br_a, w_br_b, w_out, g_ffn, w_router, w_exp_gate, w_exp_up, w_exp_down, g_ple, w_ple_gate, w_ple_proj):
    assert g_mix.shape[0] == 1, "single-layer trunk"
    wts = _prepare_weights(rel_bias, g_mix[0], w_in[0], q_gain[0], k_gain[0], ssm_a_re[0], ssm_a_im[0],
                           ssm_log_step[0], ssm_b_re[0], ssm_b_im[0], ssm_c_re[0], ssm_c_im[0], ssm_d[0],
                           w_glu[0], w_br_a[0], w_br_b[0], w_out[0], g_ffn[0], w_router[0], g_ple[0],
                           w_ple_gate[0], w_ple_proj[0])
    y_prompt = _trunk(x_prompt, p_prompt[0], wts, w_exp_gate[0], w_exp_up[0], w_exp_down[0])
    y_sample = _trunk(x_sample, p_sample[0], wts, w_exp_gate[0], w_exp_up[0], w_exp_down[0])
    return (y_prompt, y_sample)
```

```python
import functools
import math

import jax
import jax.numpy as jnp
import numpy as np
from jax import lax
from jax.experimental import pallas as pl
from jax.experimental.pallas import tpu as pltpu

F32 = jnp.float32
BF16 = jnp.bfloat16

LANES = 128
HEAD_DIM = 128
HEADS_PER_GROUP = 4
ATT_PATTERNS = ((128, 1), (512, 4), (2048, 16))
ATT_SIDE = 64
N_BUCKETS = 32
MAX_DISTANCE = 1024
CAPACITY_FACTOR = 2
EPS = 1e-6
NEG = -1e30
SSM_LANE_GROUPS = 8
V7X_VMEM_LIMIT = 56 * 1024 * 1024

assert all(w // (2 * d) == ATT_SIDE for w, d in ATT_PATTERNS)


def _cparams(semantics, vmem=V7X_VMEM_LIMIT):
    return pltpu.CompilerParams(dimension_semantics=("arbitrary",) * len(semantics), vmem_limit_bytes=vmem)


def _sigmoid(x):
    return 1.0 / (1.0 + jnp.exp(-x))


def _rms_kernel(x_ref, g_ref, *refs, dils):
    o_refs, y_ref = refs[:-1], refs[-1]
    x = x_ref[...]
    y = x * lax.rsqrt(jnp.mean(x * x, axis=-1, keepdims=True) + EPS) * g_ref[...]
    tm = x.shape[0]
    n_chunks = y_ref.shape[0]
    for c in range(n_chunks):
        y_ref[c] = y[:, c * LANES:(c + 1) * LANES]
    for o_ref, dil in zip(o_refs, dils):
        if dil == 1:
            o_ref[...] = y.astype(o_ref.dtype)
        else:
            for r in range(dil):
                rows = [y_ref[c, pl.ds(r, tm // dil, stride=dil), :] for c in range(n_chunks)]
                o_ref[r] = jnp.concatenate(rows, axis=1).astype(o_ref.dtype)


def rms_bf16(x, g, dils, tm=512):
    batch, seq, d = x.shape
    out_specs, out_shape = [], []
    for dil in dils:
        if dil == 1:
            out_specs.append(pl.BlockSpec((None, tm, d), lambda b, i: (b, i, 0)))
            out_shape.append(jax.ShapeDtypeStruct((batch, seq, d), BF16))
        else:
            out_specs.append(pl.BlockSpec((None, dil, tm // dil, d), lambda b, i: (b, 0, i, 0)))
            out_shape.append(jax.ShapeDtypeStruct((batch, dil, seq // dil, d), BF16))
    outs = pl.pallas_call(
        functools.partial(_rms_kernel, dils=tuple(dils)),
        grid=(batch, seq // tm),
        in_specs=[pl.BlockSpec((None, tm, d), lambda b, i: (b, i, 0)), pl.BlockSpec((1, d), lambda b, i: (0, 0))],
        out_specs=out_specs,
        out_shape=out_shape,
        scratch_shapes=[pltpu.VMEM((d // LANES, tm, LANES), F32)],
        compiler_params=_cparams(("parallel", "parallel")),
        name="rms_bf16",
    )(x, g.reshape(1, d).astype(F32))
    return [o.reshape(batch * seq, d) for o in outs]


def _head_rms(acc, gain):
    outs = []
    for h in range(acc.shape[1] // HEAD_DIM):
        t = acc[:, h * HEAD_DIM:(h + 1) * HEAD_DIM]
        outs.append(t * lax.rsqrt(jnp.mean(t * t, axis=-1, keepdims=True) + EPS))
    return jnp.concatenate(outs, axis=1) * gain


def _qkv_kernel(x_ref, w_ref, g_ref, o_ref):
    j = pl.program_id(0)
    acc = jnp.dot(x_ref[...], w_ref[...], preferred_element_type=F32)

    @pl.when(j < 2)
    def _():
        o_ref[...] = _head_rms(acc, g_ref[0]).astype(o_ref.dtype)

    @pl.when(j == 2)
    def _():
        o_ref[...] = acc.astype(o_ref.dtype)


def qkv_proj(hn, w, gains, name, tm=1024):
    n, d_model = hn.shape
    tn = HEADS_PER_GROUP * HEAD_DIM
    return pl.pallas_call(
        _qkv_kernel,
        grid=(3, n // tm),
        in_specs=[
            pl.BlockSpec((tm, d_model), lambda j, i: (i, 0)),
            pl.BlockSpec((d_model, tn), lambda j, i: (0, j)),
            pl.BlockSpec((1, 1, tn), lambda j, i: (j, 0, 0)),
        ],
        out_specs=pl.BlockSpec((tm, tn), lambda j, i: (i, j)),
        out_shape=jax.ShapeDtypeStruct((n, 3 * tn), BF16),
        compiler_params=_cparams(("arbitrary", "parallel")),
        name=name,
    )(hn, w, gains)


def _proj_kernel(x_ref, w_ref, o_ref, *, act):
    acc = jnp.dot(x_ref[...], w_ref[...], preferred_element_type=F32)
    if act:
        acc = _sigmoid(acc)
    o_ref[...] = acc.astype(o_ref.dtype)


def plain_proj(hn, w, out_dtype, act, name, tm=1024, tn=512):
    n, d_model = hn.shape
    width = w.shape[1]
    return pl.pallas_call(
        functools.partial(_proj_kernel, act=act),
        grid=(width // tn, n // tm),
        in_specs=[
            pl.BlockSpec((tm, d_model), lambda j, i: (i, 0)),
            pl.BlockSpec((d_model, tn), lambda j, i: (0, j)),
        ],
        out_specs=pl.BlockSpec((tm, tn), lambda j, i: (i, j)),
        out_shape=jax.ShapeDtypeStruct((n, width), out_dtype),
        compiler_params=_cparams(("arbitrary", "parallel")),
        name=name,
    )(hn, w)


def _t5_buckets(rel):
    half = N_BUCKETS // 2
    max_exact = half // 2
    n = np.abs(rel)
    large = max_exact + (np.log(np.maximum(n, 1) / max_exact) / math.log(MAX_DISTANCE / max_exact)
                         * (half - max_exact)).astype(np.int32)
    large = np.minimum(large, half - 1)
    return ((rel > 0).astype(np.int32) * half + np.where(n < max_exact, n, large)).astype(np.int32)


def _attn_geometry(l):
    tqb = min(2 * ATT_SIDE, l)
    win = min(4 * ATT_SIDE, l)
    return tqb, win


def attn_bias_tables(rel_bias_g, dil, l):
    tqb, win = _attn_geometry(l)
    i = np.arange(tqb)[:, None]
    j = np.arange(win)[None, :]
    tables = []
    for shift in range(3):
        rel = j - i - ATT_SIDE * shift
        valid = np.abs(rel) <= ATT_SIDE
        b = jnp.transpose(rel_bias_g.astype(F32)[_t5_buckets(dil * rel)], (2, 0, 1))
        tables.append(jnp.where(valid[None], b, NEG))
    return jnp.stack(tables)


def _attn_kernel(q_ref, k_ref, v_ref, bias_ref, o_ref, lse_ref, *, l, dil, tql):
    tqb, win = _attn_geometry(l)
    qi = pl.program_id(2)
    for r in range(dil):
        for c in range(tql // tqb):
            q0 = qi * tql + c * tqb
            start = pl.multiple_of(jnp.clip(q0 - ATT_SIDE, 0, l - win), ATT_SIDE)
            shift = (q0 - start) // ATT_SIDE
            q = q_ref[r, c * tqb:(c + 1) * tqb, :]
            kw = k_ref[r, pl.ds(start, win), :]
            vw = v_ref[r, pl.ds(start, win), :]
            logits = lax.dot_general(q, kw, (((1,), (1,)), ((), ())), preferred_element_type=F32)
            logits = logits + bias_ref[shift]
            m = jnp.max(logits, axis=-1, keepdims=True)
            p = jnp.exp(logits - m)
            s = jnp.sum(p, axis=-1, keepdims=True)
            o = jnp.dot(p.astype(vw.dtype), vw, preferred_element_type=F32)
            if dil == 1:
                rows = pl.ds(c * tqb, tqb)
            else:
                rows = pl.ds(c * tqb * dil + r, tqb, stride=dil)
            o_ref[rows, :] = o / s
            lse_ref[rows, :] = jnp.broadcast_to(m + jnp.log(s), (tqb, HEAD_DIM))


def dilated_attention(qkv, bias, batch, seq, dil):
    l = seq // dil
    tqb, win = _attn_geometry(l)
    tql = min(l, max(tqb, 8 * tqb // dil))
    tok = tql * dil
    hw = HEADS_PER_GROUP * HEAD_DIM
    x = qkv.reshape(batch, dil, l, 3 * hw)
    out_sds = jax.ShapeDtypeStruct((batch, seq, hw), F32)
    out_spec = pl.BlockSpec((None, tok, HEAD_DIM), lambda b, h, i: (b, i, h))
    o, lse = pl.pallas_call(
        functools.partial(_attn_kernel, l=l, dil=dil, tql=tql),
        grid=(batch, HEADS_PER_GROUP, l // tql),
        in_specs=[
            pl.BlockSpec((None, dil, tql, HEAD_DIM), lambda b, h, i: (b, 0, i, h)),
            pl.BlockSpec((None, dil, l, HEAD_DIM), lambda b, h, i: (b, 0, 0, HEADS_PER_GROUP + h)),
            pl.BlockSpec((None, dil, l, HEAD_DIM), lambda b, h, i: (b, 0, 0, 2 * HEADS_PER_GROUP + h)),
            pl.BlockSpec((3, None, tqb, win), lambda b, h, i: (0, h, 0, 0)),
        ],
        out_specs=[out_spec, out_spec],
        out_shape=[out_sds, out_sds],
        compiler_params=_cparams(("parallel", "parallel", "parallel")),
        name=f"dilated_attn_d{dil}",
    )(x, x, x, bias)
    return o.reshape(batch * seq, hw), lse.reshape(batch * seq, hw)


def ssm_discretise(a_re, a_im, log_step, b_re, b_im, c_re, c_im):
    lam = lax.complex(a_re.astype(F32), a_im.astype(F32))
    step = jnp.exp(log_step.astype(F32))[..., None]
    lam_bar = jnp.exp(lam * step)
    b_bar = ((lam_bar - 1.0) / lam)[..., None] * lax.complex(b_re.astype(F32), b_im.astype(F32))
    c = lax.complex(c_re.astype(F32), c_im.astype(F32))
    return lam_bar, b_bar, c


def _block_diag(m):
    _, nj, lg, a, b = m.shape
    return jnp.einsum('djgab,gk->djgakb', m, jnp.eye(lg, dtype=F32)).reshape(2, nj, lg * a, lg * b)


def ssm_block_weights(lam_bar, b_bar, c):
    _, g, p, h = b_bar.shape
    lg = SSM_LANE_GROUPS
    nj = g // lg
    lam = lam_bar[..., None]
    ct = jnp.swapaxes(c, -1, -2)

    def grp(m):
        return m.reshape((2, nj, lg) + m.shape[2:])

    def state_in(bmat):
        t = grp(jnp.swapaxes(bmat, -1, -2))
        return jnp.concatenate([_block_diag(t.real), _block_diag(t.imag)], axis=-1)

    def state_out(cmat):
        t = grp(cmat)
        return jnp.concatenate([_block_diag(t.real), _block_diag(-t.imag)], axis=-2)

    def direct(k):
        m = jnp.einsum('dgpc,dgph->dghc', ct * lam ** k, b_bar).real
        return _block_diag(grp(m))

    first_in, second_in = state_in(lam * b_bar), state_in(b_bar)
    first_out, second_out = state_out(ct * lam), state_out(ct * lam ** 2)
    zero = jnp.zeros_like(direct(0))
    sel = jnp.array([0, 1]).reshape(2, 1, 1, 1)
    wb = jnp.where(sel == 0, jnp.concatenate([first_in, second_in], axis=-2),
                   jnp.concatenate([second_in, first_in], axis=-2))
    wx = jnp.where(sel == 0, jnp.concatenate([first_out, second_out], axis=-1),
                   jnp.concatenate([second_out, first_out], axis=-1))
    u_first = jnp.concatenate([direct(0), direct(1)], axis=-1)
    u_second = jnp.concatenate([zero, direct(0)], axis=-1)
    swap = lambda m: jnp.concatenate([m[..., 128:], m[..., :128]], axis=-1)
    wu = jnp.where(sel == 0, jnp.concatenate([u_first, u_second], axis=-2),
                   jnp.concatenate([swap(u_second), swap(u_first)], axis=-2))
    l2 = (lam_bar ** 2).reshape(2, nj, 1, 1, lg * p)
    lam2 = jnp.broadcast_to(jnp.concatenate([l2.real, l2.imag], axis=2), (2, nj, 2, 8, lg * p))
    return wb.astype(BF16), wx.astype(BF16), wu.astype(BF16), lam2.astype(F32)


def _s5_kernel(u_ref, wb_ref, wx_ref, wu_ref, lam_ref, x0_ref, *refs, tt, emit_y):
    if emit_y:
        y_ref, xend_ref, lhs_e, lhs_o, bu_ref, st_ref, xs_ref, ysc_e, ysc_o = refs
    else:
        xend_ref, lhs_e, lhs_o, bu_ref, st_ref = refs
    direction = pl.program_id(0)
    ti = pl.program_id(2)
    nt = pl.num_programs(2)
    vb, _, lanes = u_ref.shape
    k = lam_ref.shape[-1]
    n = tt // 2

    @pl.when(ti == 0)
    def _():
        st_ref[...] = x0_ref[...]

    for b in range(vb):
        lhs_e[pl.ds(b, n, stride=vb), :] = u_ref[b, pl.ds(0, n, stride=2), :]
        lhs_o[pl.ds(b, n, stride=vb), :] = u_ref[b, pl.ds(1, n, stride=2), :]
    lhs = jnp.concatenate([lhs_e[...], lhs_o[...]], axis=1).astype(BF16)
    bu_ref[...] = jnp.dot(lhs, wb_ref[...], preferred_element_type=F32)
    lr = lam_ref[0]
    li = lam_ref[1]
    x_in = (st_ref[0], st_ref[1])
    if emit_y:
        in_row = pl.multiple_of(jnp.where(direction == 0, 0, n * vb), vb)
        xs_ref[pl.ds(in_row, vb), :] = jnp.concatenate(x_in, axis=1)

    def step(s, carry):
        xr, xi = carry
        t = jnp.where(direction == 0, s, n - 1 - s)
        b = bu_ref[pl.ds(pl.multiple_of(t * vb, vb), vb), :]
        nr = lr * xr - li * xi + b[:, :k]
        ni = lr * xi + li * xr + b[:, k:]
        if emit_y:
            out_row = pl.multiple_of(jnp.where(direction == 0, t + 1, t) * vb, vb)
            xs_ref[pl.ds(out_row, vb), :] = jnp.concatenate([nr, ni], axis=1)
        return nr, ni

    xr, xi = lax.fori_loop(0, n, step, x_in, unroll=8)
    st_ref[0] = xr
    st_ref[1] = xi
    if emit_y:
        base = pl.multiple_of(jnp.where(direction == 0, 0, vb), vb)
        xs = xs_ref[pl.ds(base, n * vb), :].astype(BF16)
        y = (jnp.dot(xs, wx_ref[...], preferred_element_type=F32)
             + jnp.dot(lhs, wu_ref[...], preferred_element_type=F32))
        ysc_e[...] = y[:, :lanes]
        ysc_o[...] = y[:, lanes:]
        for b in range(vb):
            y_ref[b, pl.ds(0, n, stride=2), :] = ysc_e[pl.ds(b, n, stride=vb), :]
            y_ref[b, pl.ds(1, n, stride=2), :] = ysc_o[pl.ds(b, n, stride=vb), :]

    @pl.when(ti == nt - 1)
    def _():
        xend_ref[0] = xr
        xend_ref[1] = xi


def s5_scan(u, wts, x0, vbatch, emit_y, tt=256):
    width = u.shape[1]
    vseq = u.shape[0] // vbatch
    wb, wx, wu, lam2 = wts['wb'], wts['wx'], wts['wu'], wts['lam2']
    nj = wb.shape[1]
    k = lam2.shape[-1]
    lanes = width // nj
    nt = vseq // tt
    n = tt // 2

    def tmap(d, t):
        return jnp.where(d == 0, t, nt - 1 - t)

    state_spec = pl.BlockSpec((None, None, 2, vbatch, k), lambda d, j, t: (d, j, 0, 0, 0))
    state_sds = jax.ShapeDtypeStruct((2, nj, 2, vbatch, k), F32)
    out_specs, out_shape = [state_spec], [state_sds]
    half = pltpu.VMEM((n * vbatch, lanes), F32)
    scratch = [half, half, pltpu.VMEM((n * vbatch, 2 * k), F32), pltpu.VMEM((2, vbatch, k), F32)]
    if emit_y:
        out_specs.insert(0, pl.BlockSpec((None, vbatch, tt, lanes), lambda d, j, t: (d, 0, tmap(d, t), j)))
        out_shape.insert(0, jax.ShapeDtypeStruct((2, vbatch, vseq, width), F32))
        scratch += [pltpu.VMEM(((n + 1) * vbatch, 2 * k), F32), half, half]
    outs = pl.pallas_call(
        functools.partial(_s5_kernel, tt=tt, emit_y=emit_y),
        grid=(2, nj, nt),
        in_specs=[
            pl.BlockSpec((vbatch, tt, lanes), lambda d, j, t: (0, tmap(d, t), j)),
            pl.BlockSpec((None, None, 2 * lanes, 2 * k), lambda d, j, t: (d, j, 0, 0)),
            pl.BlockSpec((None, None, 2 * k, 2 * lanes), lambda d, j, t: (d, j, 0, 0)),
            pl.BlockSpec((None, None, 2 * lanes, 2 * lanes), lambda d, j, t: (d, j, 0, 0)),
            state_spec, state_spec,
        ],
        out_specs=out_specs,
        out_shape=out_shape,
        scratch_shapes=scratch,
        compiler_params=_cparams(("parallel", "parallel", "arbitrary")),
        name="s5_scan" if emit_y else "s5_states",
    )(u.reshape(vbatch, vseq, width), wb, wx, wu, lam2, x0)
    if emit_y:
        return outs[0].reshape(2, vbatch * vseq, width), outs[1]
    return None, outs[0]


def s5_segment_carries(xend, lam_bar, seg_len, batch, nseg):
    _, nj, _, _, k = xend.shape
    lam_pow = (lam_bar ** seg_len).reshape(2, nj, 1, k)
    e = lax.complex(xend[:, :, 0], xend[:, :, 1]).reshape(2, nj, batch, nseg, k)
    fwd = [jnp.zeros((nj, batch, k), jnp.complex64)]
    for s in range(1, nseg):
        fwd.append(lam_pow[0] * fwd[-1] + e[0, :, :, s - 1])
    bwd = [jnp.zeros((nj, batch, k), jnp.complex64)]
    for s in range(nseg - 2, -1, -1):
        bwd.append(lam_pow[1] * bwd[-1] + e[1, :, :, s + 1])
    bwd = bwd[::-1]
    x_in = jnp.stack([jnp.stack(fwd, axis=2), jnp.stack(bwd, axis=2)])
    x_in = x_in.reshape(2, nj, batch * nseg, k)
    return jnp.stack([x_in.real, x_in.imag], axis=2).astype(F32)


def _gelu_tanh(x):
    return 0.5 * x * (1.0 + jnp.tanh(math.sqrt(2.0 / math.pi) * (x + 0.044715 * x * x * x)))


def _glu_kernel(yf_ref, yb_ref, u_ref, d_ref, wa_ref, wb_ref, o_ref):
    y = _gelu_tanh(yf_ref[...] + yb_ref[...] + d_ref[...] * u_ref[...]).astype(BF16)
    za = jnp.dot(y, wa_ref[...], preferred_element_type=F32)
    zb = jnp.dot(y, wb_ref[...], preferred_element_type=F32)
    o_ref[...] = (za * _sigmoid(zb)).astype(o_ref.dtype)


def ssm_glu(y, u, d_skip, w_glu, tm=512):
    n, width = u.shape
    wa = w_glu[:, :width]
    wb = w_glu[:, width:]
    const = lambda i: (0, 0)
    return pl.pallas_call(
        _glu_kernel,
        grid=(n // tm,),
        in_specs=[
            pl.BlockSpec((None, tm, width), lambda i: (0, i, 0)),
            pl.BlockSpec((None, tm, width), lambda i: (1, i, 0)),
            pl.BlockSpec((tm, width), lambda i: (i, 0)),
            pl.BlockSpec((1, width), const),
            pl.BlockSpec((width, width), const),
            pl.BlockSpec((width, width), const),
        ],
        out_specs=pl.BlockSpec((tm, width), lambda i: (i, 0)),
        out_shape=jax.ShapeDtypeStruct((n, width), BF16),
        compiler_params=_cparams(("parallel",)),
        name="ssm_glu",
    )(y, y, u, d_skip.reshape(1, width).astype(F32), wa, wb)


def _split_bf16(x):
    hi = x.astype(BF16)
    lo = (x - hi.astype(F32)).astype(BF16)
    return hi, lo


def _mixer_out_kernel(o0, o1, o2, l0, l1, l2, ob_ref, gate_ref, x_ref, wa_ref, wb_ref, wo_ref,
                      g_ref, wrh_ref, wrl_ref, x1_ref, h_ref, lg_ref):
    d_model = x_ref.shape[1]
    la, lb, lc = l0[...], l1[...], l2[...]
    m = jnp.maximum(jnp.maximum(la, lb), lc)
    ea, eb, ec = jnp.exp(la - m), jnp.exp(lb - m), jnp.exp(lc - m)
    oa = (ea * o0[...] + eb * o1[...] + ec * o2[...]) / (ea + eb + ec)
    pa = jnp.dot(oa.astype(BF16), wa_ref[...], preferred_element_type=F32)
    pb = jnp.dot(ob_ref[...], wb_ref[...], preferred_element_type=F32)
    gates = gate_ref[...].astype(F32)
    merged = gates[:, :d_model] * pa + gates[:, d_model:] * pb
    x1 = x_ref[...] + jnp.dot(merged.astype(BF16), wo_ref[...], preferred_element_type=F32)
    x1_ref[...] = x1
    h = x1 * lax.rsqrt(jnp.mean(x1 * x1, axis=-1, keepdims=True) + EPS) * g_ref[...]
    h_ref[...] = h.astype(h_ref.dtype)
    hh, hl = _split_bf16(h)
    lg_ref[...] = (jnp.dot(hh, wrh_ref[...], preferred_element_type=F32)
                   + jnp.dot(hh, wrl_ref[...], preferred_element_type=F32)
                   + jnp.dot(hl, wrh_ref[...], preferred_element_type=F32))


def mixer_out(o_list, lse_list, o_b, gates, x, w_br_a, w_br_b, w_out, g_ffn, wr_hi, wr_lo, tm=256):
    n, d_model = x.shape
    row = lambda w: pl.BlockSpec((tm, w), lambda i: (i, 0))
    full = lambda a: pl.BlockSpec(a.shape, lambda i: (0,) * a.ndim, pipeline_mode=pl.Buffered(1))
    g2 = g_ffn.reshape(1, d_model).astype(F32)
    er = wr_hi.shape[1]
    return pl.pallas_call(
        _mixer_out_kernel,
        grid=(n // tm,),
        in_specs=[row(o.shape[1]) for o in o_list] + [row(l.shape[1]) for l in lse_list] + [
            row(o_b.shape[1]), row(gates.shape[1]), row(d_model),
            full(w_br_a), full(w_br_b), full(w_out), full(g2), full(wr_hi), full(wr_lo)],
        out_specs=[row(d_model), row(d_model), row(er)],
        out_shape=[jax.ShapeDtypeStruct((n, d_model), F32), jax.ShapeDtypeStruct((n, d_model), BF16),
                   jax.ShapeDtypeStruct((n, er), F32)],
        compiler_params=_cparams(("parallel",)),
        name="mixer_out",
    )(*o_list, *lse_list, o_b, gates, x, w_br_a, w_br_b, w_out, g2, wr_hi, wr_lo)


def _expert_up_kernel(x_ref, wg_ref, wu_ref, o_ref, wg_s, wu_s):
    @pl.when(pl.program_id(2) == 0)
    def _():
        wg_s[...] = wg_ref[...].astype(BF16)
        wu_s[...] = wu_ref[...].astype(BF16)

    x = x_ref[...]
    a = jnp.dot(x, wg_s[...], preferred_element_type=F32)
    b = jnp.dot(x, wu_s[...], preferred_element_type=F32)
    o_ref[...] = (a * _sigmoid(a) * b).astype(o_ref.dtype)


def expert_up(xe, w_gate, w_up, tm=1024, tf=512):
    e, cap, d_model = xe.shape
    ff = w_gate.shape[2]
    tm = min(tm, cap)
    return pl.pallas_call(
        _expert_up_kernel,
        grid=(e, ff // tf, cap // tm),
        in_specs=[
            pl.BlockSpec((None, tm, d_model), lambda ei, f, m: (ei, m, 0)),
            pl.BlockSpec((None, d_model, tf), lambda ei, f, m: (ei, 0, f)),
            pl.BlockSpec((None, d_model, tf), lambda ei, f, m: (ei, 0, f)),
        ],
        out_specs=pl.BlockSpec((None, tm, tf), lambda ei, f, m: (ei, m, f)),
        out_shape=jax.ShapeDtypeStruct((e, cap, ff), BF16),
        scratch_shapes=[pltpu.VMEM((d_model, tf), BF16), pltpu.VMEM((d_model, tf), BF16)],
        compiler_params=_cparams(("parallel", "parallel", "arbitrary")),
        name="expert_up",
    )(xe, w_gate, w_up)


def _expert_down_kernel(h_ref, w_ref, g_ref, o_ref, w_s):
    @pl.when(pl.program_id(2) == 0)
    def _():
        w_s[...] = w_ref[...].astype(BF16)

    o_ref[...] = jnp.dot(h_ref[...], w_s[...], preferred_element_type=F32) * g_ref[...]


def expert_down(hid, w_down, gate, tm=1024, tn=512):
    e, cap, ff = hid.shape
    d_model = w_down.shape[2]
    tm = min(tm, cap)
    tn = min(tn, d_model)
    return pl.pallas_call(
        _expert_down_kernel,
        grid=(e, d_model // tn, cap // tm),
        in_specs=[
            pl.BlockSpec((None, tm, ff), lambda ei, n, m: (ei, m, 0)),
            pl.BlockSpec((None, ff, tn), lambda ei, n, m: (ei, 0, n)),
            pl.BlockSpec((None, tm, 1), lambda ei, n, m: (ei, m, 0)),
        ],
        out_specs=pl.BlockSpec((None, tm, tn), lambda ei, n, m: (ei, m, n)),
        out_shape=jax.ShapeDtypeStruct((e, cap, d_model), F32),
        scratch_shapes=[pltpu.VMEM((ff, tn), BF16)],
        compiler_params=_cparams(("parallel", "parallel", "arbitrary")),
        name="expert_down",
    )(hid, w_down, gate.reshape(e, cap, 1).astype(F32))


def _final_kernel(x1_ref, moe_ref, p_ref, g_ref, wg_ref, wp_ref, o_ref):
    x2 = x1_ref[...] + moe_ref[...]
    h = (x2 * lax.rsqrt(jnp.mean(x2 * x2, axis=-1, keepdims=True) + EPS) * g_ref[...]).astype(BF16)
    gate = _sigmoid(jnp.dot(h, wg_ref[...], preferred_element_type=F32))
    proj = jnp.dot(p_ref[...].astype(BF16), wp_ref[...], preferred_element_type=F32)
    o_ref[...] = x2 + gate * proj


def final_embed(x1, moe, p, g_ple, w_gate, w_proj, tm=256):
    n, d_model = x1.shape
    row = lambda w: pl.BlockSpec((tm, w), lambda i: (i, 0))
    full = lambda a: pl.BlockSpec(a.shape, lambda i: (0,) * a.ndim, pipeline_mode=pl.Buffered(1))
    g2 = g_ple.reshape(1, d_model).astype(F32)
    return pl.pallas_call(
        _final_kernel,
        grid=(n // tm,),
        in_specs=[row(d_model), row(d_model), row(p.shape[1]), full(g2), full(w_gate), full(w_proj)],
        out_specs=row(d_model),
        out_shape=jax.ShapeDtypeStruct((n, d_model), F32),
        compiler_params=_cparams(("parallel",)),
        name="final_embed",
    )(x1, moe, p, g2, w_gate, w_proj)


def _prepare_weights(rel_bias, g_mix, w_in, q_gain, k_gain, ssm_a_re, ssm_a_im, ssm_log_step,
                     ssm_b_re, ssm_b_im, ssm_c_re, ssm_c_im, ssm_d, w_glu, w_br_a, w_br_b, w_out,
                     g_ffn, w_router, g_ple, w_ple_gate, w_ple_proj):
    n_heads = q_gain.shape[0]
    att_w = n_heads * HEAD_DIM
    hw = HEADS_PER_GROUP * HEAD_DIM
    ssm_w = ssm_d.shape[0]
    w_in_bf = w_in.astype(BF16)
    w_qkv, gains = [], []
    qg = q_gain.astype(F32) * (HEAD_DIM ** -0.5)
    kg = k_gain.astype(F32)
    for gi in range(len(ATT_PATTERNS)):
        cols = [w_in_bf[:, s * att_w + gi * hw: s * att_w + (gi + 1) * hw] for s in range(3)]
        w_qkv.append(jnp.concatenate(cols, axis=1))
        hs = slice(gi * HEADS_PER_GROUP, (gi + 1) * HEADS_PER_GROUP)
        gains.append(jnp.stack([qg[hs].reshape(1, hw), kg[hs].reshape(1, hw), jnp.ones((1, hw), F32)]))
    lam_bar, b_bar, c = ssm_discretise(ssm_a_re, ssm_a_im, ssm_log_step, ssm_b_re, ssm_b_im, ssm_c_re, ssm_c_im)
    wb, wx, wu, lam2 = ssm_block_weights(lam_bar, b_bar, c)
    n_exp = w_router.shape[1]
    wr = jnp.pad(w_router.astype(F32), ((0, 0), (0, 128 - n_exp)))
    wr_hi, wr_lo = _split_bf16(wr)
    return dict(
        rel_bias=rel_bias, g_mix=g_mix, w_qkv=w_qkv, gains=gains,
        w_u=w_in_bf[:, 3 * att_w:3 * att_w + ssm_w], w_gates=w_in_bf[:, 3 * att_w + ssm_w:],
        lam_bar=lam_bar, wb=wb, wx=wx, wu=wu, lam2=lam2, ssm_d=ssm_d, w_glu=w_glu.astype(BF16),
        w_br_a=w_br_a.astype(BF16), w_br_b=w_br_b.astype(BF16), w_out=w_out.astype(BF16),
        g_ffn=g_ffn, wr_hi=wr_hi, wr_lo=wr_lo, n_exp=n_exp, g_ple=g_ple,
        w_ple_gate=w_ple_gate.astype(BF16), w_ple_proj=w_ple_proj.astype(BF16))


def _ssm_branch(hn, wts, batch, seq):
    nseg = max(1, 8 // batch)
    vbatch = batch * nseg
    vseq = seq // nseg
    u = plain_proj(hn, wts['w_u'], F32, False, "u_proj")
    nj, k = wts['lam2'].shape[1], wts['lam2'].shape[-1]
    x0 = jnp.zeros((2, nj, 2, vbatch, k), F32)
    if nseg > 1:
        _, xend = s5_scan(u, wts, x0, vbatch, emit_y=False)
        x0 = s5_segment_carries(xend, wts['lam_bar'], vseq, batch, nseg)
    y, _ = s5_scan(u, wts, x0, vbatch, emit_y=True)
    return ssm_glu(y, u, wts['ssm_d'], wts['w_glu'])


def _expert_choice(h, logits, n_exp, w_exp_gate, w_exp_up, w_exp_down):
    n_tok, d_model = h.shape
    cap = CAPACITY_FACTOR * n_tok // n_exp
    affinity = jax.nn.softmax(logits[:, :n_exp], axis=-1)
    gate, idx = lax.top_k(affinity.T, cap)
    xe = jnp.take(h, idx, axis=0)
    hid = expert_up(xe, w_exp_gate, w_exp_up)
    ye = expert_down(hid, w_exp_down, gate)
    return jnp.zeros((n_tok, d_model), F32).at[idx.reshape(-1)].add(ye.reshape(-1, d_model))


def _trunk(x, p, wts, w_exp_gate, w_exp_up, w_exp_down):
    batch, seq, d_model = x.shape
    n = batch * seq
    xf = x.reshape(n, d_model)
    dils = [dil for _, dil in ATT_PATTERNS]
    hn_list = rms_bf16(x, wts['g_mix'], dils)
    hn = hn_list[dils.index(1)]
    o_list, lse_list = [], []
    for gi, dil in enumerate(dils):
        qkv = qkv_proj(hn_list[gi], wts['w_qkv'][gi], wts['gains'][gi], f"qkv_proj_d{dil}")
        hs = slice(gi * HEADS_PER_GROUP, (gi + 1) * HEADS_PER_GROUP)
        bias = attn_bias_tables(wts['rel_bias'][:, hs], dil, seq // dil)
        o, lse = dilated_attention(qkv, bias, batch, seq, dil)
        o_list.append(o)
        lse_list.append(lse)
    o_b = _ssm_branch(hn, wts, batch, seq)
    gates = plain_proj(hn, wts['w_gates'], BF16, True, "gates_proj")
    x1, h2, logits = mixer_out(o_list, lse_list, o_b, gates, xf, wts['w_br_a'], wts['w_br_b'],
                               wts['w_out'], wts['g_ffn'], wts['wr_hi'], wts['wr_lo'])
    moe = _expert_choice(h2, logits, wts['n_exp'], w_exp_gate, w_exp_up, w_exp_down)
    out = final_embed(x1, moe, p.reshape(n, p.shape[-1]), wts['g_ple'], wts['w_ple_gate'], wts['w_ple_proj'])
    return out.reshape(batch, seq, d_model)


def kernel(x_prompt, x_sample, p_prompt, p_sample, rel_bias, g_mix, w_in, q_gain, k_gain, ssm_a_re, ssm_a_im, ssm_log_step, ssm_b_re, ssm_b_im, ssm_c_re, ssm_c_im, ssm_d, w_glu, w_br_a, w_br_b, w_out, g_ffn, w_router, w_exp_gate, w_exp_up, w_exp_down, g_ple, w_ple_gate, w_ple_proj):
    assert g_mix.shape[0] == 1, "single-layer trunk"
    wts = _prepare_weights(rel_bias, g_mix[0], w_in[0], q_gain[0], k_gain[0], ssm_a_re[0], ssm_a_im[0],
                           ssm_log_step[0], ssm_b_re[0], ssm_b_im[0], ssm_c_re[0], ssm_c_im[0], ssm_d[0],
                           w_glu[0], w_br_a[0], w_br_b[0], w_out[0], g_ffn[0], w_router[0], g_ple[0],
                           w_ple_gate[0], w_ple_proj[0])
    y_prompt = _trunk(x_prompt, p_prompt[0], wts, w_exp_gate[0], w_exp_up[0], w_exp_down[0])
    y_sample = _trunk(x_sample, p_sample[0], wts, w_exp_gate[0], w_exp_up[0], w_exp_down[0])
    return (y_prompt, y_sample)
```

```python
import functools
import math

import jax
import jax.numpy as jnp
import numpy as np
from jax import lax
from jax.experimental import pallas as pl
from jax.experimental.pallas import tpu as pltpu
from jax.experimental.pallas import tpu_sc as plsc

F32 = jnp.float32
BF16 = jnp.bfloat16

LANES = 128
HEAD_DIM = 128
HEADS_PER_GROUP = 4
ATT_PATTERNS = ((128, 1), (512, 4), (2048, 16))
ATT_SIDE = 64
N_BUCKETS = 32
MAX_DISTANCE = 1024
CAPACITY_FACTOR = 2
EPS = 1e-6
NEG = -1e30
SSM_LANE_GROUPS = 8
V7X_VMEM_LIMIT = 56 * 1024 * 1024

assert all(w // (2 * d) == ATT_SIDE for w, d in ATT_PATTERNS)


def _cparams(semantics, vmem=V7X_VMEM_LIMIT):
    return pltpu.CompilerParams(dimension_semantics=("arbitrary",) * len(semantics), vmem_limit_bytes=vmem)


def _sigmoid(x):
    return 1.0 / (1.0 + jnp.exp(-x))


def _rms_kernel(x_ref, g_ref, *refs, dils):
    o_refs, y_ref = refs[:-1], refs[-1]
    x = x_ref[...]
    y = x * lax.rsqrt(jnp.mean(x * x, axis=-1, keepdims=True) + EPS) * g_ref[...]
    tm = x.shape[0]
    n_chunks = y_ref.shape[0]
    for c in range(n_chunks):
        y_ref[c] = y[:, c * LANES:(c + 1) * LANES]
    for o_ref, dil in zip(o_refs, dils):
        if dil == 1:
            o_ref[...] = y.astype(o_ref.dtype)
        else:
            for r in range(dil):
                rows = [y_ref[c, pl.ds(r, tm // dil, stride=dil), :] for c in range(n_chunks)]
                o_ref[r] = jnp.concatenate(rows, axis=1).astype(o_ref.dtype)


def rms_bf16(x, g, dils, tm=512):
    batch, seq, d = x.shape
    out_specs, out_shape = [], []
    for dil in dils:
        if dil == 1:
            out_specs.append(pl.BlockSpec((None, tm, d), lambda b, i: (b, i, 0)))
            out_shape.append(jax.ShapeDtypeStruct((batch, seq, d), BF16))
        else:
            out_specs.append(pl.BlockSpec((None, dil, tm // dil, d), lambda b, i: (b, 0, i, 0)))
            out_shape.append(jax.ShapeDtypeStruct((batch, dil, seq // dil, d), BF16))
    outs = pl.pallas_call(
        functools.partial(_rms_kernel, dils=tuple(dils)),
        grid=(batch, seq // tm),
        in_specs=[pl.BlockSpec((None, tm, d), lambda b, i: (b, i, 0)), pl.BlockSpec((1, d), lambda b, i: (0, 0))],
        out_specs=out_specs,
        out_shape=out_shape,
        scratch_shapes=[pltpu.VMEM((d // LANES, tm, LANES), F32)],
        compiler_params=_cparams(("parallel", "parallel")),
        name="rms_bf16",
    )(x, g.reshape(1, d).astype(F32))
    return [o.reshape(batch * seq, d) for o in outs]


def _head_rms(acc, gain):
    outs = []
    for h in range(acc.shape[1] // HEAD_DIM):
        t = acc[:, h * HEAD_DIM:(h + 1) * HEAD_DIM]
        outs.append(t * lax.rsqrt(jnp.mean(t * t, axis=-1, keepdims=True) + EPS))
    return jnp.concatenate(outs, axis=1) * gain


def _qkv_kernel(x_ref, w_ref, g_ref, o_ref):
    j = pl.program_id(0)
    acc = jnp.dot(x_ref[...], w_ref[...], preferred_element_type=F32)

    @pl.when(j < 2)
    def _():
        o_ref[...] = _head_rms(acc, g_ref[0]).astype(o_ref.dtype)

    @pl.when(j == 2)
    def _():
        o_ref[...] = acc.astype(o_ref.dtype)


def qkv_proj(hn, w, gains, name, tm=1024):
    n, d_model = hn.shape
    tn = HEADS_PER_GROUP * HEAD_DIM
    return pl.pallas_call(
        _qkv_kernel,
        grid=(3, n // tm),
        in_specs=[
            pl.BlockSpec((tm, d_model), lambda j, i: (i, 0)),
            pl.BlockSpec((d_model, tn), lambda j, i: (0, j)),
            pl.BlockSpec((1, 1, tn), lambda j, i: (j, 0, 0)),
        ],
        out_specs=pl.BlockSpec((tm, tn), lambda j, i: (i, j)),
        out_shape=jax.ShapeDtypeStruct((n, 3 * tn), BF16),
        compiler_params=_cparams(("arbitrary", "parallel")),
        name=name,
    )(hn, w, gains)


def _proj_kernel(x_ref, w_ref, o_ref, *, act):
    acc = jnp.dot(x_ref[...], w_ref[...], preferred_element_type=F32)
    if act:
        acc = _sigmoid(acc)
    o_ref[...] = acc.astype(o_ref.dtype)


def plain_proj(hn, w, out_dtype, act, name, tm=1024, tn=512):
    n, d_model = hn.shape
    width = w.shape[1]
    return pl.pallas_call(
        functools.partial(_proj_kernel, act=act),
        grid=(width // tn, n // tm),
        in_specs=[
            pl.BlockSpec((tm, d_model), lambda j, i: (i, 0)),
            pl.BlockSpec((d_model, tn), lambda j, i: (0, j)),
        ],
        out_specs=pl.BlockSpec((tm, tn), lambda j, i: (i, j)),
        out_shape=jax.ShapeDtypeStruct((n, width), out_dtype),
        compiler_params=_cparams(("arbitrary", "parallel")),
        name=name,
    )(hn, w)


def _t5_buckets(rel):
    half = N_BUCKETS // 2
    max_exact = half // 2
    n = np.abs(rel)
    large = max_exact + (np.log(np.maximum(n, 1) / max_exact) / math.log(MAX_DISTANCE / max_exact)
                         * (half - max_exact)).astype(np.int32)
    large = np.minimum(large, half - 1)
    return ((rel > 0).astype(np.int32) * half + np.where(n < max_exact, n, large)).astype(np.int32)


def _attn_geometry(l):
    tqb = min(2 * ATT_SIDE, l)
    win = min(4 * ATT_SIDE, l)
    return tqb, win


def attn_bias_tables(rel_bias_g, dil, l):
    tqb, win = _attn_geometry(l)
    i = np.arange(tqb)[:, None]
    j = np.arange(win)[None, :]
    tables = []
    for shift in range(3):
        rel = j - i - ATT_SIDE * shift
        valid = np.abs(rel) <= ATT_SIDE
        b = jnp.transpose(rel_bias_g.astype(F32)[_t5_buckets(dil * rel)], (2, 0, 1))
        tables.append(jnp.where(valid[None], b, NEG))
    return jnp.stack(tables)


def _attn_kernel(q_ref, k_ref, v_ref, bias_ref, o_ref, lse_ref, *, l, dil, tql):
    tqb, win = _attn_geometry(l)
    qi = pl.program_id(2)
    for r in range(dil):
        for c in range(tql // tqb):
            q0 = qi * tql + c * tqb
            start = pl.multiple_of(jnp.clip(q0 - ATT_SIDE, 0, l - win), ATT_SIDE)
            shift = (q0 - start) // ATT_SIDE
            q = q_ref[r, c * tqb:(c + 1) * tqb, :]
            kw = k_ref[r, pl.ds(start, win), :]
            vw = v_ref[r, pl.ds(start, win), :]
            logits = lax.dot_general(q, kw, (((1,), (1,)), ((), ())), preferred_element_type=F32)
            logits = logits + bias_ref[shift]
            m = jnp.max(logits, axis=-1, keepdims=True)
            p = jnp.exp(logits - m)
            s = jnp.sum(p, axis=-1, keepdims=True)
            o = jnp.dot(p.astype(vw.dtype), vw, preferred_element_type=F32)
            if dil == 1:
                rows = pl.ds(c * tqb, tqb)
            else:
                rows = pl.ds(c * tqb * dil + r, tqb, stride=dil)
            o_ref[rows, :] = o / s
            lse_ref[rows, :] = jnp.broadcast_to(m + jnp.log(s), (tqb, HEAD_DIM))


def dilated_attention(qkv, bias, batch, seq, dil):
    l = seq // dil
    tqb, win = _attn_geometry(l)
    tql = min(l, max(tqb, 8 * tqb // dil))
    tok = tql * dil
    hw = HEADS_PER_GROUP * HEAD_DIM
    x = qkv.reshape(batch, dil, l, 3 * hw)
    out_sds = jax.ShapeDtypeStruct((batch, seq, hw), F32)
    out_spec = pl.BlockSpec((None, tok, HEAD_DIM), lambda b, h, i: (b, i, h))
    o, lse = pl.pallas_call(
        functools.partial(_attn_kernel, l=l, dil=dil, tql=tql),
        grid=(batch, HEADS_PER_GROUP, l // tql),
        in_specs=[
            pl.BlockSpec((None, dil, tql, HEAD_DIM), lambda b, h, i: (b, 0, i, h)),
            pl.BlockSpec((None, dil, l, HEAD_DIM), lambda b, h, i: (b, 0, 0, HEADS_PER_GROUP + h)),
            pl.BlockSpec((None, dil, l, HEAD_DIM), lambda b, h, i: (b, 0, 0, 2 * HEADS_PER_GROUP + h)),
            pl.BlockSpec((3, None, tqb, win), lambda b, h, i: (0, h, 0, 0)),
        ],
        out_specs=[out_spec, out_spec],
        out_shape=[out_sds, out_sds],
        compiler_params=_cparams(("parallel", "parallel", "parallel")),
        name=f"dilated_attn_d{dil}",
    )(x, x, x, bias)
    return o.reshape(batch * seq, hw), lse.reshape(batch * seq, hw)


def ssm_discretise(a_re, a_im, log_step, b_re, b_im, c_re, c_im):
    lam = lax.complex(a_re.astype(F32), a_im.astype(F32))
    step = jnp.exp(log_step.astype(F32))[..., None]
    lam_bar = jnp.exp(lam * step)
    b_bar = ((lam_bar - 1.0) / lam)[..., None] * lax.complex(b_re.astype(F32), b_im.astype(F32))
    c = lax.complex(c_re.astype(F32), c_im.astype(F32))
    return lam_bar, b_bar, c


def _block_diag(m):
    _, nj, lg, a, b = m.shape
    return jnp.einsum('djgab,gk->djgakb', m, jnp.eye(lg, dtype=F32)).reshape(2, nj, lg * a, lg * b)


def ssm_block_weights(lam_bar, b_bar, c):
    _, g, p, h = b_bar.shape
    lg = SSM_LANE_GROUPS
    nj = g // lg
    lam = lam_bar[..., None]
    ct = jnp.swapaxes(c, -1, -2)

    def grp(m):
        return m.reshape((2, nj, lg) + m.shape[2:])

    def state_in(bmat):
        t = grp(jnp.swapaxes(bmat, -1, -2))
        return jnp.concatenate([_block_diag(t.real), _block_diag(t.imag)], axis=-1)

    def state_out(cmat):
        t = grp(cmat)
        return jnp.concatenate([_block_diag(t.real), _block_diag(-t.imag)], axis=-2)

    def direct(k):
        m = jnp.einsum('dgpc,dgph->dghc', ct * lam ** k, b_bar).real
        return _block_diag(grp(m))

    first_in, second_in = state_in(lam * b_bar), state_in(b_bar)
    first_out, second_out = state_out(ct * lam), state_out(ct * lam ** 2)
    zero = jnp.zeros_like(direct(0))
    sel = jnp.array([0, 1]).reshape(2, 1, 1, 1)
    wb = jnp.where(sel == 0, jnp.concatenate([first_in, second_in], axis=-2),
                   jnp.concatenate([second_in, first_in], axis=-2))
    wx = jnp.where(sel == 0, jnp.concatenate([first_out, second_out], axis=-1),
                   jnp.concatenate([second_out, first_out], axis=-1))
    u_first = jnp.concatenate([direct(0), direct(1)], axis=-1)
    u_second = jnp.concatenate([zero, direct(0)], axis=-1)
    swap = lambda m: jnp.concatenate([m[..., 128:], m[..., :128]], axis=-1)
    wu = jnp.where(sel == 0, jnp.concatenate([u_first, u_second], axis=-2),
                   jnp.concatenate([swap(u_second), swap(u_first)], axis=-2))
    l2 = (lam_bar ** 2).reshape(2, nj, 1, 1, lg * p)
    lam2 = jnp.broadcast_to(jnp.concatenate([l2.real, l2.imag], axis=2), (2, nj, 2, 8, lg * p))
    return wb.astype(BF16), wx.astype(BF16), wu.astype(BF16), lam2.astype(F32)


def _s5_kernel(u_ref, wb_ref, wx_ref, wu_ref, lam_ref, x0_ref, *refs, tt, emit_y):
    if emit_y:
        y_ref, xend_ref, lhs_e, lhs_o, bu_ref, st_ref, xs_ref, ysc_e, ysc_o = refs
    else:
        xend_ref, lhs_e, lhs_o, bu_ref, st_ref = refs
    direction = pl.program_id(0)
    ti = pl.program_id(2)
    nt = pl.num_programs(2)
    vb, _, lanes = u_ref.shape
    k = lam_ref.shape[-1]
    n = tt // 2

    @pl.when(ti == 0)
    def _():
        st_ref[...] = x0_ref[...]

    for b in range(vb):
        lhs_e[pl.ds(b, n, stride=vb), :] = u_ref[b, pl.ds(0, n, stride=2), :]
        lhs_o[pl.ds(b, n, stride=vb), :] = u_ref[b, pl.ds(1, n, stride=2), :]
    lhs = jnp.concatenate([lhs_e[...], lhs_o[...]], axis=1).astype(BF16)
    bu_ref[...] = jnp.dot(lhs, wb_ref[...], preferred_element_type=F32)
    lr = lam_ref[0]
    li = lam_ref[1]
    x_in = (st_ref[0], st_ref[1])
    if emit_y:
        in_row = pl.multiple_of(jnp.where(direction == 0, 0, n * vb), vb)
        xs_ref[pl.ds(in_row, vb), :] = jnp.concatenate(x_in, axis=1)

    def step(s, carry):
        xr, xi = carry
        t = jnp.where(direction == 0, s, n - 1 - s)
        b = bu_ref[pl.ds(pl.multiple_of(t * vb, vb), vb), :]
        nr = lr * xr - li * xi + b[:, :k]
        ni = lr * xi + li * xr + b[:, k:]
        if emit_y:
            out_row = pl.multiple_of(jnp.where(direction == 0, t + 1, t) * vb, vb)
            xs_ref[pl.ds(out_row, vb), :] = jnp.concatenate([nr, ni], axis=1)
        return nr, ni

    xr, xi = lax.fori_loop(0, n, step, x_in, unroll=8)
    st_ref[0] = xr
    st_ref[1] = xi
    if emit_y:
        base = pl.multiple_of(jnp.where(direction == 0, 0, vb), vb)
        xs = xs_ref[pl.ds(base, n * vb), :].astype(BF16)
        y = (jnp.dot(xs, wx_ref[...], preferred_element_type=F32)
             + jnp.dot(lhs, wu_ref[...], preferred_element_type=F32))
        ysc_e[...] = y[:, :lanes]
        ysc_o[...] = y[:, lanes:]
        for b in range(vb):
            y_ref[b, pl.ds(0, n, stride=2), :] = ysc_e[pl.ds(b, n, stride=vb), :]
            y_ref[b, pl.ds(1, n, stride=2), :] = ysc_o[pl.ds(b, n, stride=vb), :]

    @pl.when(ti == nt - 1)
    def _():
        xend_ref[0] = xr
        xend_ref[1] = xi


def s5_scan(u, wts, x0, vbatch, emit_y, tt=256):
    width = u.shape[1]
    vseq = u.shape[0] // vbatch
    wb, wx, wu, lam2 = wts['wb'], wts['wx'], wts['wu'], wts['lam2']
    nj = wb.shape[1]
    k = lam2.shape[-1]
    lanes = width // nj
    nt = vseq // tt
    n = tt // 2

    def tmap(d, t):
        return jnp.where(d == 0, t, nt - 1 - t)

    state_spec = pl.BlockSpec((None, None, 2, vbatch, k), lambda d, j, t: (d, j, 0, 0, 0))
    state_sds = jax.ShapeDtypeStruct((2, nj, 2, vbatch, k), F32)
    out_specs, out_shape = [state_spec], [state_sds]
    half = pltpu.VMEM((n * vbatch, lanes), F32)
    scratch = [half, half, pltpu.VMEM((n * vbatch, 2 * k), F32), pltpu.VMEM((2, vbatch, k), F32)]
    if emit_y:
        out_specs.insert(0, pl.BlockSpec((None, vbatch, tt, lanes), lambda d, j, t: (d, 0, tmap(d, t), j)))
        out_shape.insert(0, jax.ShapeDtypeStruct((2, vbatch, vseq, width), F32))
        scratch += [pltpu.VMEM(((n + 1) * vbatch, 2 * k), F32), half, half]
    outs = pl.pallas_call(
        functools.partial(_s5_kernel, tt=tt, emit_y=emit_y),
        grid=(2, nj, nt),
        in_specs=[
            pl.BlockSpec((vbatch, tt, lanes), lambda d, j, t: (0, tmap(d, t), j)),
            pl.BlockSpec((None, None, 2 * lanes, 2 * k), lambda d, j, t: (d, j, 0, 0)),
            pl.BlockSpec((None, None, 2 * k, 2 * lanes), lambda d, j, t: (d, j, 0, 0)),
            pl.BlockSpec((None, None, 2 * lanes, 2 * lanes), lambda d, j, t: (d, j, 0, 0)),
            state_spec, state_spec,
        ],
        out_specs=out_specs,
        out_shape=out_shape,
        scratch_shapes=scratch,
        compiler_params=_cparams(("parallel", "parallel", "arbitrary")),
        name="s5_scan" if emit_y else "s5_states",
    )(u.reshape(vbatch, vseq, width), wb, wx, wu, lam2, x0)
    if emit_y:
        return outs[0].reshape(2, vbatch * vseq, width), outs[1]
    return None, outs[0]


def s5_segment_carries(xend, lam_bar, seg_len, batch, nseg):
    _, nj, _, _, k = xend.shape
    lam_pow = (lam_bar ** seg_len).reshape(2, nj, 1, k)
    e = lax.complex(xend[:, :, 0], xend[:, :, 1]).reshape(2, nj, batch, nseg, k)
    fwd = [jnp.zeros((nj, batch, k), jnp.complex64)]
    for s in range(1, nseg):
        fwd.append(lam_pow[0] * fwd[-1] + e[0, :, :, s - 1])
    bwd = [jnp.zeros((nj, batch, k), jnp.complex64)]
    for s in range(nseg - 2, -1, -1):
        bwd.append(lam_pow[1] * bwd[-1] + e[1, :, :, s + 1])
    bwd = bwd[::-1]
    x_in = jnp.stack([jnp.stack(fwd, axis=2), jnp.stack(bwd, axis=2)])
    x_in = x_in.reshape(2, nj, batch * nseg, k)
    return jnp.stack([x_in.real, x_in.imag], axis=2).astype(F32)


def _gelu_tanh(x):
    return 0.5 * x * (1.0 + jnp.tanh(math.sqrt(2.0 / math.pi) * (x + 0.044715 * x * x * x)))


def _glu_kernel(yf_ref, yb_ref, u_ref, d_ref, wa_ref, wb_ref, o_ref):
    y = _gelu_tanh(yf_ref[...] + yb_ref[...] + d_ref[...] * u_ref[...]).astype(BF16)
    za = jnp.dot(y, wa_ref[...], preferred_element_type=F32)
    zb = jnp.dot(y, wb_ref[...], preferred_element_type=F32)
    o_ref[...] = (za * _sigmoid(zb)).astype(o_ref.dtype)


def ssm_glu(y, u, d_skip, w_glu, tm=512):
    n, width = u.shape
    wa = w_glu[:, :width]
    wb = w_glu[:, width:]
    const = lambda i: (0, 0)
    return pl.pallas_call(
        _glu_kernel,
        grid=(n // tm,),
        in_specs=[
            pl.BlockSpec((None, tm, width), lambda i: (0, i, 0)),
            pl.BlockSpec((None, tm, width), lambda i: (1, i, 0)),
            pl.BlockSpec((tm, width), lambda i: (i, 0)),
            pl.BlockSpec((1, width), const),
            pl.BlockSpec((width, width), const),
            pl.BlockSpec((width, width), const),
        ],
        out_specs=pl.BlockSpec((tm, width), lambda i: (i, 0)),
        out_shape=jax.ShapeDtypeStruct((n, width), BF16),
        compiler_params=_cparams(("parallel",)),
        name="ssm_glu",
    )(y, y, u, d_skip.reshape(1, width).astype(F32), wa, wb)


def _split_bf16(x):
    hi = x.astype(BF16)
    lo = (x - hi.astype(F32)).astype(BF16)
    return hi, lo


def _pack_bf16_pair(a, b):
    wa = pltpu.bitcast(a.astype(BF16).astype(F32), jnp.int32)
    wb = pltpu.bitcast(b.astype(BF16).astype(F32), jnp.int32)
    return wa | lax.shift_right_logical(wb, 16)


def _unpack_bf16_pair(w):
    hi = pltpu.bitcast(w & jnp.int32(-65536), F32).astype(BF16)
    lo = pltpu.bitcast(lax.shift_left(w, 16), F32).astype(BF16)
    return hi, lo


def _mixer_out_kernel(o0, o1, o2, l0, l1, l2, ob_ref, gate_ref, x_ref, wa_ref, wb_ref, wo_ref,
                      g_ref, wrh_ref, wrl_ref, x1_ref, h_ref, lg_ref):
    d_model = x_ref.shape[1]
    la, lb, lc = l0[...], l1[...], l2[...]
    m = jnp.maximum(jnp.maximum(la, lb), lc)
    ea, eb, ec = jnp.exp(la - m), jnp.exp(lb - m), jnp.exp(lc - m)
    oa = (ea * o0[...] + eb * o1[...] + ec * o2[...]) / (ea + eb + ec)
    pa = jnp.dot(oa.astype(BF16), wa_ref[...], preferred_element_type=F32)
    pb = jnp.dot(ob_ref[...], wb_ref[...], preferred_element_type=F32)
    gates = gate_ref[...].astype(F32)
    merged = gates[:, :d_model] * pa + gates[:, d_model:] * pb
    x1 = x_ref[...] + jnp.dot(merged.astype(BF16), wo_ref[...], preferred_element_type=F32)
    x1_ref[...] = x1
    h = x1 * lax.rsqrt(jnp.mean(x1 * x1, axis=-1, keepdims=True) + EPS) * g_ref[...]
    h_ref[...] = _pack_bf16_pair(h[:, :d_model // 2], h[:, d_model // 2:])
    hh, hl = _split_bf16(h)
    lg_ref[...] = (jnp.dot(hh, wrh_ref[...], preferred_element_type=F32)
                   + jnp.dot(hh, wrl_ref[...], preferred_element_type=F32)
                   + jnp.dot(hl, wrh_ref[...], preferred_element_type=F32))


def mixer_out(o_list, lse_list, o_b, gates, x, w_br_a, w_br_b, w_out, g_ffn, wr_hi, wr_lo, tm=256):
    n, d_model = x.shape
    row = lambda w: pl.BlockSpec((tm, w), lambda i: (i, 0))
    full = lambda a: pl.BlockSpec(a.shape, lambda i: (0,) * a.ndim, pipeline_mode=pl.Buffered(1))
    g2 = g_ffn.reshape(1, d_model).astype(F32)
    er = wr_hi.shape[1]
    return pl.pallas_call(
        _mixer_out_kernel,
        grid=(n // tm,),
        in_specs=[row(o.shape[1]) for o in o_list] + [row(l.shape[1]) for l in lse_list] + [
            row(o_b.shape[1]), row(gates.shape[1]), row(d_model),
            full(w_br_a), full(w_br_b), full(w_out), full(g2), full(wr_hi), full(wr_lo)],
        out_specs=[row(d_model), row(d_model // 2), row(er)],
        out_shape=[jax.ShapeDtypeStruct((n, d_model), F32), jax.ShapeDtypeStruct((n, d_model // 2), jnp.int32),
                   jax.ShapeDtypeStruct((n, er), F32)],
        compiler_params=_cparams(("parallel",)),
        name="mixer_out",
    )(*o_list, *lse_list, o_b, gates, x, w_br_a, w_br_b, w_out, g2, wr_hi, wr_lo)


def _expert_up_kernel(x_ref, wg_ref, wu_ref, o_ref, wg_s, wu_s):
    @pl.when(pl.program_id(2) == 0)
    def _():
        wg_s[...] = wg_ref[...].astype(BF16)
        wu_s[...] = wu_ref[...].astype(BF16)

    x_hi, x_lo = _unpack_bf16_pair(x_ref[...])
    half = x_hi.shape[1]

    def proj(w_s):
        return (jnp.dot(x_hi, w_s[:half, :], preferred_element_type=F32)
                + jnp.dot(x_lo, w_s[half:, :], preferred_element_type=F32))

    a = proj(wg_s)
    b = proj(wu_s)
    o_ref[...] = (a * _sigmoid(a) * b).astype(o_ref.dtype)


def expert_up(xe, w_gate, w_up, tm=1024, tf=512):
    e, cap, _ = xe.shape
    d_model = w_gate.shape[1]
    ff = w_gate.shape[2]
    tm = min(tm, cap)
    return pl.pallas_call(
        _expert_up_kernel,
        grid=(e, ff // tf, cap // tm),
        in_specs=[
            pl.BlockSpec((None, tm, d_model // 2), lambda ei, f, m: (ei, m, 0)),
            pl.BlockSpec((None, d_model, tf), lambda ei, f, m: (ei, 0, f)),
            pl.BlockSpec((None, d_model, tf), lambda ei, f, m: (ei, 0, f)),
        ],
        out_specs=pl.BlockSpec((None, tm, tf), lambda ei, f, m: (ei, m, f)),
        out_shape=jax.ShapeDtypeStruct((e, cap, ff), BF16),
        scratch_shapes=[pltpu.VMEM((d_model, tf), BF16), pltpu.VMEM((d_model, tf), BF16)],
        compiler_params=_cparams(("parallel", "parallel", "arbitrary")),
        name="expert_up",
    )(xe, w_gate, w_up)


def _expert_down_kernel(h_ref, w_ref, o_ref, w_s):
    @pl.when(pl.program_id(2) == 0)
    def _():
        w_s[...] = w_ref[...].astype(BF16)

    acc = jnp.dot(h_ref[...], w_s[...], preferred_element_type=F32)
    half = acc.shape[1] // 2
    o_ref[...] = _pack_bf16_pair(acc[:, :half], acc[:, half:])


def expert_down(hid, w_down, tm=1024, tn=512):
    e, cap, ff = hid.shape
    d_model = w_down.shape[2]
    tm = min(tm, cap)
    tn = min(tn, d_model)
    return pl.pallas_call(
        _expert_down_kernel,
        grid=(e, d_model // tn, cap // tm),
        in_specs=[
            pl.BlockSpec((None, tm, ff), lambda ei, n, m: (ei, m, 0)),
            pl.BlockSpec((None, ff, tn), lambda ei, n, m: (ei, 0, n)),
        ],
        out_specs=pl.BlockSpec((None, tm, tn // 2), lambda ei, n, m: (ei, m, n)),
        out_shape=jax.ShapeDtypeStruct((e, cap, d_model // 2), jnp.int32),
        scratch_shapes=[pltpu.VMEM((ff, tn), BF16)],
        compiler_params=_cparams(("parallel", "parallel", "arbitrary")),
        name="expert_down",
    )(hid, w_down)


SC_GATHER_CHUNK = 16


def sc_gather_rows(table, idx):
    n_idx = idx.shape[0]
    width = table.shape[1]
    info = plsc.get_sparse_core_info()
    n_workers = info.num_cores * info.num_subcores
    per_worker = n_idx // n_workers
    ch = SC_GATHER_CHUNK
    assert n_idx % n_workers == 0 and per_worker % (2 * ch) == 0
    mesh = plsc.VectorSubcoreMesh(core_axis_name="c", subcore_axis_name="s")

    @functools.partial(
        pl.kernel, mesh=mesh,
        out_type=jax.ShapeDtypeStruct((n_idx, width), table.dtype),
        scratch_types=[pltpu.VMEM((per_worker,), jnp.int32), pltpu.VMEM((ch, width), table.dtype),
                       pltpu.VMEM((ch, width), table.dtype), pltpu.SemaphoreType.DMA, pltpu.SemaphoreType.DMA],
        name="sc_gather_rows",
    )
    def gather(table_hbm, idx_hbm, out_hbm, idx_v, rows_a, rows_b, sem_a, sem_b):
        base = (lax.axis_index("s") * info.num_cores + lax.axis_index("c")) * per_worker
        pltpu.sync_copy(idx_hbm.at[pl.ds(base, per_worker)], idx_v)

        @pl.loop(0, per_worker // (2 * ch))
        def _(c):
            off = c * 2 * ch
            copy_a = pltpu.async_copy(table_hbm.at[idx_v.at[pl.ds(off, ch)]], rows_a, sem_a)
            copy_b = pltpu.async_copy(table_hbm.at[idx_v.at[pl.ds(off + ch, ch)]], rows_b, sem_b)
            copy_a.wait()
            pltpu.sync_copy(rows_a, out_hbm.at[pl.ds(base + off, ch)])
            copy_b.wait()
            pltpu.sync_copy(rows_b, out_hbm.at[pl.ds(base + off + ch, ch)])

    return gather(table, idx)


COMBINE_TOKENS = 256
COMBINE_ROWS = 512


def _combine_kernel(tile_ref, chunk_ref, first_ref, valid_ref, rows_ref, tok_ref, g_ref, o_ref, *, tn):
    p = pl.program_id(0)

    @pl.when(first_ref[p] == 1)
    def _():
        o_ref[...] = jnp.zeros_like(o_ref)

    @pl.when(valid_ref[p] == 1)
    def _():
        t, ch = o_ref.shape[0], tok_ref.shape[1]
        token = tile_ref[p] * t + lax.broadcasted_iota(jnp.int32, (t, ch), 0)
        weights = jnp.where(tok_ref[...] == token, g_ref[...], 0.0).astype(BF16)
        hi, lo = _unpack_bf16_pair(rows_ref[...])
        acc_hi = jnp.dot(weights, hi, preferred_element_type=F32)
        acc_lo = jnp.dot(weights, lo, preferred_element_type=F32)
        half = tn // 2
        for n in range(o_ref.shape[1] // tn):
            o_ref[:, n * tn:n * tn + half] += acc_hi[:, n * half:(n + 1) * half]
            o_ref[:, n * tn + half:(n + 1) * tn] += acc_lo[:, n * half:(n + 1) * half]


def combine_expert_outputs(rows, tok_sorted, gate_sorted, n_tok, tn):
    m, half_d = rows.shape
    t, ch = COMBINE_TOKENS, COMBINE_ROWS
    n_tiles, n_chunks = n_tok // t, m // ch
    n_pairs = n_tiles + n_chunks
    bounds = jnp.searchsorted(tok_sorted, jnp.arange(n_tiles + 1, dtype=jnp.int32) * t, side='left').astype(jnp.int32)
    start, end = bounds[:-1], bounds[1:]
    c_lo = jnp.minimum(start // ch, n_chunks - 1)
    c_hi = jnp.where(end > start, (end - 1) // ch, c_lo)
    count = c_hi - c_lo + 1
    offs = jnp.cumsum(count)
    p = jnp.arange(n_pairs, dtype=jnp.int32)
    tile = jnp.minimum(jnp.searchsorted(offs, p, side='right'), n_tiles - 1).astype(jnp.int32)
    first_p = (offs - count)[tile]
    valid = p < offs[-1]
    chunk = jnp.where(valid, c_lo[tile] + p - first_p, c_hi[-1]).astype(jnp.int32)
    first = (valid & (p == first_p)).astype(jnp.int32)
    grid_spec = pltpu.PrefetchScalarGridSpec(
        num_scalar_prefetch=4,
        grid=(n_pairs,),
        in_specs=[
            pl.BlockSpec((ch, half_d), lambda i, tl, ck, fr, vl: (ck[i], 0)),
            pl.BlockSpec((1, ch), lambda i, tl, ck, fr, vl: (0, ck[i])),
            pl.BlockSpec((1, ch), lambda i, tl, ck, fr, vl: (0, ck[i])),
        ],
        out_specs=pl.BlockSpec((t, 2 * half_d), lambda i, tl, ck, fr, vl: (tl[i], 0)),
    )
    return pl.pallas_call(
        functools.partial(_combine_kernel, tn=tn),
        grid_spec=grid_spec,
        out_shape=jax.ShapeDtypeStruct((n_tok, 2 * half_d), F32),
        compiler_params=_cparams(("arbitrary",)),
        name="combine_expert_outputs",
    )(tile, chunk, first, valid.astype(jnp.int32), rows, tok_sorted.reshape(1, m), gate_sorted.reshape(1, m).astype(F32))


def _final_kernel(x1_ref, moe_ref, p_ref, g_ref, wg_ref, wp_ref, o_ref):
    x2 = x1_ref[...] + moe_ref[...]
    h = (x2 * lax.rsqrt(jnp.mean(x2 * x2, axis=-1, keepdims=True) + EPS) * g_ref[...]).astype(BF16)
    gate = _sigmoid(jnp.dot(h, wg_ref[...], preferred_element_type=F32))
    proj = jnp.dot(p_ref[...].astype(BF16), wp_ref[...], preferred_element_type=F32)
    o_ref[...] = x2 + gate * proj


def final_embed(x1, moe, p, g_ple, w_gate, w_proj, tm=256):
    n, d_model = x1.shape
    row = lambda w: pl.BlockSpec((tm, w), lambda i: (i, 0))
    full = lambda a: pl.BlockSpec(a.shape, lambda i: (0,) * a.ndim, pipeline_mode=pl.Buffered(1))
    g2 = g_ple.reshape(1, d_model).astype(F32)
    return pl.pallas_call(
        _final_kernel,
        grid=(n // tm,),
        in_specs=[row(d_model), row(d_model), row(p.shape[1]), full(g2), full(w_gate), full(w_proj)],
        out_specs=row(d_model),
        out_shape=jax.ShapeDtypeStruct((n, d_model), F32),
        compiler_params=_cparams(("parallel",)),
        name="final_embed",
    )(x1, moe, p, g2, w_gate, w_proj)


def _prepare_weights(rel_bias, g_mix, w_in, q_gain, k_gain, ssm_a_re, ssm_a_im, ssm_log_step,
                     ssm_b_re, ssm_b_im, ssm_c_re, ssm_c_im, ssm_d, w_glu, w_br_a, w_br_b, w_out,
                     g_ffn, w_router, g_ple, w_ple_gate, w_ple_proj):
    n_heads = q_gain.shape[0]
    att_w = n_heads * HEAD_DIM
    hw = HEADS_PER_GROUP * HEAD_DIM
    ssm_w = ssm_d.shape[0]
    w_in_bf = w_in.astype(BF16)
    w_qkv, gains = [], []
    qg = q_gain.astype(F32) * (HEAD_DIM ** -0.5)
    kg = k_gain.astype(F32)
    for gi in range(len(ATT_PATTERNS)):
        cols = [w_in_bf[:, s * att_w + gi * hw: s * att_w + (gi + 1) * hw] for s in range(3)]
        w_qkv.append(jnp.concatenate(cols, axis=1))
        hs = slice(gi * HEADS_PER_GROUP, (gi + 1) * HEADS_PER_GROUP)
        gains.append(jnp.stack([qg[hs].reshape(1, hw), kg[hs].reshape(1, hw), jnp.ones((1, hw), F32)]))
    lam_bar, b_bar, c = ssm_discretise(ssm_a_re, ssm_a_im, ssm_log_step, ssm_b_re, ssm_b_im, ssm_c_re, ssm_c_im)
    wb, wx, wu, lam2 = ssm_block_weights(lam_bar, b_bar, c)
    n_exp = w_router.shape[1]
    wr = jnp.pad(w_router.astype(F32), ((0, 0), (0, 128 - n_exp)))
    wr_hi, wr_lo = _split_bf16(wr)
    return dict(
        rel_bias=rel_bias, g_mix=g_mix, w_qkv=w_qkv, gains=gains,
        w_u=w_in_bf[:, 3 * att_w:3 * att_w + ssm_w], w_gates=w_in_bf[:, 3 * att_w + ssm_w:],
        lam_bar=lam_bar, wb=wb, wx=wx, wu=wu, lam2=lam2, ssm_d=ssm_d, w_glu=w_glu.astype(BF16),
        w_br_a=w_br_a.astype(BF16), w_br_b=w_br_b.astype(BF16), w_out=w_out.astype(BF16),
        g_ffn=g_ffn, wr_hi=wr_hi, wr_lo=wr_lo, n_exp=n_exp, g_ple=g_ple,
        w_ple_gate=w_ple_gate.astype(BF16), w_ple_proj=w_ple_proj.astype(BF16))


def _ssm_branch(hn, wts, batch, seq):
    nseg = max(1, 8 // batch)
    vbatch = batch * nseg
    vseq = seq // nseg
    u = plain_proj(hn, wts['w_u'], F32, False, "u_proj")
    nj, k = wts['lam2'].shape[1], wts['lam2'].shape[-1]
    x0 = jnp.zeros((2, nj, 2, vbatch, k), F32)
    if nseg > 1:
        _, xend = s5_scan(u, wts, x0, vbatch, emit_y=False)
        x0 = s5_segment_carries(xend, wts['lam_bar'], vseq, batch, nseg)
    y, _ = s5_scan(u, wts, x0, vbatch, emit_y=True)
    return ssm_glu(y, u, wts['ssm_d'], wts['w_glu'])


def _expert_choice(h, logits, n_exp, w_exp_gate, w_exp_up, w_exp_down):
    n_tok = h.shape[0]
    d_model = w_exp_down.shape[2]
    cap = CAPACITY_FACTOR * n_tok // n_exp
    affinity = jax.nn.softmax(logits[:, :n_exp], axis=-1)
    gate, idx = lax.top_k(affinity.T, cap)
    idx_flat = idx.reshape(-1)
    xe = sc_gather_rows(h, idx_flat).reshape(n_exp, cap, h.shape[1])
    hid = expert_up(xe, w_exp_gate, w_exp_up)
    tn = min(512, d_model)
    ye = expert_down(hid, w_exp_down, tn=tn).reshape(n_exp * cap, h.shape[1])
    order = jnp.argsort(idx_flat).astype(jnp.int32)
    tok_sorted = idx_flat[order]
    gate_sorted = gate.reshape(-1)[order]
    return combine_expert_outputs(sc_gather_rows(ye, order), tok_sorted, gate_sorted, n_tok, tn)


def _trunk(x, p, wts, w_exp_gate, w_exp_up, w_exp_down):
    batch, seq, d_model = x.shape
    n = batch * seq
    xf = x.reshape(n, d_model)
    dils = [dil for _, dil in ATT_PATTERNS]
    hn_list = rms_bf16(x, wts['g_mix'], dils)
    hn = hn_list[dils.index(1)]
    o_list, lse_list = [], []
    for gi, dil in enumerate(dils):
        qkv = qkv_proj(hn_list[gi], wts['w_qkv'][gi], wts['gains'][gi], f"qkv_proj_d{dil}")
        hs = slice(gi * HEADS_PER_GROUP, (gi + 1) * HEADS_PER_GROUP)
        bias = attn_bias_tables(wts['rel_bias'][:, hs], dil, seq // dil)
        o, lse = dilated_attention(qkv, bias, batch, seq, dil)
        o_list.append(o)
        lse_list.append(lse)
    o_b = _ssm_branch(hn, wts, batch, seq)
    gates = plain_proj(hn, wts['w_gates'], BF16, True, "gates_proj")
    x1, h2, logits = mixer_out(o_list, lse_list, o_b, gates, xf, wts['w_br_a'], wts['w_br_b'],
                               wts['w_out'], wts['g_ffn'], wts['wr_hi'], wts['wr_lo'])
    moe = _expert_choice(h2, logits, wts['n_exp'], w_exp_gate, w_exp_up, w_exp_down)
    out = final_embed(x1, moe, p.reshape(n, p.shape[-1]), wts['g_ple'], wts['w_ple_gate'], wts['w_ple_proj'])
    return out.reshape(batch, seq, d_model)


def kernel(x_prompt, x_sample, p_prompt, p_sample, rel_bias, g_mix, w_in, q_gain, k_gain, ssm_a_re, ssm_a_im, ssm_log_step, ssm_b_re, ssm_b_im, ssm_c_re, ssm_c_im, ssm_d, w_glu, w_br_a, w_br_b, w_out, g_ffn, w_router, w_exp_gate, w_exp_up, w_exp_down, g_ple, w_ple_gate, w_ple_proj):
    assert g_mix.shape[0] == 1, "single-layer trunk"
    wts = _prepare_weights(rel_bias, g_mix[0], w_in[0], q_gain[0], k_gain[0], ssm_a_re[0], ssm_a_im[0],
                           ssm_log_step[0], ssm_b_re[0], ssm_b_im[0], ssm_c_re[0], ssm_c_im[0], ssm_d[0],
                           w_glu[0], w_br_a[0], w_br_b[0], w_out[0], g_ffn[0], w_router[0], g_ple[0],
                           w_ple_gate[0], w_ple_proj[0])
    y_prompt = _trunk(x_prompt, p_prompt[0], wts, w_exp_gate[0], w_exp_up[0], w_exp_down[0])
    y_sample = _trunk(x_sample, p_sample[0], wts, w_exp_gate[0], w_exp_up[0], w_exp_down[0])
    return (y_prompt, y_sample)
```

```python
import functools
import math

import jax
import jax.numpy as jnp
import numpy as np
from jax import lax
from jax.experimental import pallas as pl
from jax.experimental.pallas import tpu as pltpu
from jax.experimental.pallas import tpu_sc as plsc

F32 = jnp.float32
BF16 = jnp.bfloat16

LANES = 128
HEAD_DIM = 128
HEADS_PER_GROUP = 4
ATT_PATTERNS = ((128, 1), (512, 4), (2048, 16))
ATT_SIDE = 64
N_BUCKETS = 32
MAX_DISTANCE = 1024
CAPACITY_FACTOR = 2
EPS = 1e-6
NEG = -1e30
SSM_LANE_GROUPS = 8
V7X_VMEM_LIMIT = 56 * 1024 * 1024

assert all(w // (2 * d) == ATT_SIDE for w, d in ATT_PATTERNS)


def _cparams(semantics, vmem=V7X_VMEM_LIMIT):
    return pltpu.CompilerParams(dimension_semantics=("arbitrary",) * len(semantics), vmem_limit_bytes=vmem)


def _sigmoid(x):
    return 1.0 / (1.0 + jnp.exp(-x))


def _rms_kernel(x_ref, g_ref, *refs, dils):
    o_refs, y_ref = refs[:-1], refs[-1]
    x = x_ref[...]
    y = x * lax.rsqrt(jnp.mean(x * x, axis=-1, keepdims=True) + EPS) * g_ref[...]
    tm = x.shape[0]
    n_chunks = y_ref.shape[0]
    for c in range(n_chunks):
        y_ref[c] = y[:, c * LANES:(c + 1) * LANES]
    for o_ref, dil in zip(o_refs, dils):
        if dil == 1:
            o_ref[...] = y.astype(o_ref.dtype)
        else:
            for r in range(dil):
                rows = [y_ref[c, pl.ds(r, tm // dil, stride=dil), :] for c in range(n_chunks)]
                o_ref[r] = jnp.concatenate(rows, axis=1).astype(o_ref.dtype)


def rms_bf16(x, g, dils, tm=512):
    batch, seq, d = x.shape
    out_specs, out_shape = [], []
    for dil in dils:
        if dil == 1:
            out_specs.append(pl.BlockSpec((None, tm, d), lambda b, i: (b, i, 0)))
            out_shape.append(jax.ShapeDtypeStruct((batch, seq, d), BF16))
        else:
            out_specs.append(pl.BlockSpec((None, dil, tm // dil, d), lambda b, i: (b, 0, i, 0)))
            out_shape.append(jax.ShapeDtypeStruct((batch, dil, seq // dil, d), BF16))
    outs = pl.pallas_call(
        functools.partial(_rms_kernel, dils=tuple(dils)),
        grid=(batch, seq // tm),
        in_specs=[pl.BlockSpec((None, tm, d), lambda b, i: (b, i, 0)), pl.BlockSpec((1, d), lambda b, i: (0, 0))],
        out_specs=out_specs,
        out_shape=out_shape,
        scratch_shapes=[pltpu.VMEM((d // LANES, tm, LANES), F32)],
        compiler_params=_cparams(("parallel", "parallel")),
        name="rms_bf16",
    )(x, g.reshape(1, d).astype(F32))
    return [o.reshape(batch * seq, d) for o in outs]


def _head_rms(acc, gain):
    outs = []
    for h in range(acc.shape[1] // HEAD_DIM):
        t = acc[:, h * HEAD_DIM:(h + 1) * HEAD_DIM]
        outs.append(t * lax.rsqrt(jnp.mean(t * t, axis=-1, keepdims=True) + EPS))
    return jnp.concatenate(outs, axis=1) * gain


def _qkv_kernel(x_ref, w_ref, g_ref, o_ref):
    x = x_ref[...]
    tn = o_ref.shape[1] // 3
    for j in range(3):
        acc = jnp.dot(x, w_ref[:, j * tn:(j + 1) * tn], preferred_element_type=F32)
        if j < 2:
            acc = _head_rms(acc, g_ref[j])
        o_ref[:, j * tn:(j + 1) * tn] = acc.astype(o_ref.dtype)


def qkv_proj(hn, w, gains, name, tm=1024):
    n, d_model = hn.shape
    tn = HEADS_PER_GROUP * HEAD_DIM
    return pl.pallas_call(
        _qkv_kernel,
        grid=(n // tm,),
        in_specs=[
            pl.BlockSpec((tm, d_model), lambda i: (i, 0)),
            pl.BlockSpec((d_model, 3 * tn), lambda i: (0, 0), pipeline_mode=pl.Buffered(1)),
            pl.BlockSpec((3, 1, tn), lambda i: (0, 0, 0), pipeline_mode=pl.Buffered(1)),
        ],
        out_specs=pl.BlockSpec((tm, 3 * tn), lambda i: (i, 0)),
        out_shape=jax.ShapeDtypeStruct((n, 3 * tn), BF16),
        compiler_params=_cparams(("parallel",)),
        name=name,
    )(hn, w, gains)


def _proj_kernel(x_ref, w_ref, o_ref, *, act):
    acc = jnp.dot(x_ref[...], w_ref[...], preferred_element_type=F32)
    if act:
        acc = _sigmoid(acc)
    o_ref[...] = acc.astype(o_ref.dtype)


def plain_proj(hn, w, out_dtype, act, name, tm=1024, tn=1024):
    n, d_model = hn.shape
    width = w.shape[1]
    tn = min(tn, width)
    return pl.pallas_call(
        functools.partial(_proj_kernel, act=act),
        grid=(width // tn, n // tm),
        in_specs=[
            pl.BlockSpec((tm, d_model), lambda j, i: (i, 0)),
            pl.BlockSpec((d_model, tn), lambda j, i: (0, j)),
        ],
        out_specs=pl.BlockSpec((tm, tn), lambda j, i: (i, j)),
        out_shape=jax.ShapeDtypeStruct((n, width), out_dtype),
        compiler_params=_cparams(("arbitrary", "parallel")),
        name=name,
    )(hn, w)


def _t5_buckets(rel):
    half = N_BUCKETS // 2
    max_exact = half // 2
    n = np.abs(rel)
    large = max_exact + (np.log(np.maximum(n, 1) / max_exact) / math.log(MAX_DISTANCE / max_exact)
                         * (half - max_exact)).astype(np.int32)
    large = np.minimum(large, half - 1)
    return ((rel > 0).astype(np.int32) * half + np.where(n < max_exact, n, large)).astype(np.int32)


def _attn_geometry(l):
    tqb = min(2 * ATT_SIDE, l)
    win = min(4 * ATT_SIDE, l)
    return tqb, win


def attn_bias_tables(rel_bias_g, dil, l):
    tqb, win = _attn_geometry(l)
    i = np.arange(tqb)[:, None]
    j = np.arange(win)[None, :]
    rel = np.stack([j - i - ATT_SIDE * shift for shift in range(3)])
    valid = np.abs(rel) <= ATT_SIDE
    onehot = (_t5_buckets(dil * rel)[..., None] == np.arange(N_BUCKETS)).astype(np.int8)
    b = jnp.einsum('sijb,bh->shij', jnp.asarray(onehot).astype(F32), rel_bias_g.astype(F32),
                   precision=lax.Precision.HIGHEST)
    return jnp.where(valid[:, None], b, NEG)


def _attn_kernel(q_ref, k_ref, v_ref, bias_ref, o_ref, lse_ref, *, l, dil, tql):
    tqb, win = _attn_geometry(l)
    qi = pl.program_id(2)
    for r in range(dil):
        for c in range(tql // tqb):
            q0 = qi * tql + c * tqb
            start = pl.multiple_of(jnp.clip(q0 - ATT_SIDE, 0, l - win), ATT_SIDE)
            shift = (q0 - start) // ATT_SIDE
            q = q_ref[r, c * tqb:(c + 1) * tqb, :]
            kw = k_ref[r, pl.ds(start, win), :]
            vw = v_ref[r, pl.ds(start, win), :]
            logits = lax.dot_general(q, kw, (((1,), (1,)), ((), ())), preferred_element_type=F32)
            logits = logits + bias_ref[shift]
            m = jnp.max(logits, axis=-1, keepdims=True)
            p = jnp.exp(logits - m)
            s = jnp.sum(p, axis=-1, keepdims=True)
            o = jnp.dot(p.astype(vw.dtype), vw, preferred_element_type=F32)
            if dil == 1:
                rows = pl.ds(c * tqb, tqb)
            else:
                rows = pl.ds(c * tqb * dil + r, tqb, stride=dil)
            o_ref[rows, :] = o / s
            lse_ref[rows, :] = jnp.broadcast_to(m + jnp.log(s), (tqb, HEAD_DIM))


def dilated_attention(qkv, bias, batch, seq, dil):
    l = seq // dil
    tqb, win = _attn_geometry(l)
    tql = min(l, max(tqb, 8 * tqb // dil))
    tok = tql * dil
    hw = HEADS_PER_GROUP * HEAD_DIM
    x = qkv.reshape(batch, dil, l, 3 * hw)
    out_sds = jax.ShapeDtypeStruct((batch, seq, hw), F32)
    out_spec = pl.BlockSpec((None, tok, HEAD_DIM), lambda b, h, i: (b, i, h))
    o, lse = pl.pallas_call(
        functools.partial(_attn_kernel, l=l, dil=dil, tql=tql),
        grid=(batch, HEADS_PER_GROUP, l // tql),
        in_specs=[
            pl.BlockSpec((None, dil, tql, HEAD_DIM), lambda b, h, i: (b, 0, i, h)),
            pl.BlockSpec((None, dil, l, HEAD_DIM), lambda b, h, i: (b, 0, 0, HEADS_PER_GROUP + h)),
            pl.BlockSpec((None, dil, l, HEAD_DIM), lambda b, h, i: (b, 0, 0, 2 * HEADS_PER_GROUP + h)),
            pl.BlockSpec((3, None, tqb, win), lambda b, h, i: (0, h, 0, 0)),
        ],
        out_specs=[out_spec, out_spec],
        out_shape=[out_sds, out_sds],
        compiler_params=_cparams(("parallel", "parallel", "parallel")),
        name=f"dilated_attn_d{dil}",
    )(x, x, x, bias)
    return o.reshape(batch * seq, hw), lse.reshape(batch * seq, hw)


def ssm_discretise(a_re, a_im, log_step, b_re, b_im, c_re, c_im):
    lam = lax.complex(a_re.astype(F32), a_im.astype(F32))
    step = jnp.exp(log_step.astype(F32))[..., None]
    lam_bar = jnp.exp(lam * step)
    b_bar = ((lam_bar - 1.0) / lam)[..., None] * lax.complex(b_re.astype(F32), b_im.astype(F32))
    c = lax.complex(c_re.astype(F32), c_im.astype(F32))
    return lam_bar, b_bar, c


def _block_diag(m):
    _, nj, lg, a, b = m.shape
    return jnp.einsum('djgab,gk->djgakb', m, jnp.eye(lg, dtype=F32)).reshape(2, nj, lg * a, lg * b)


def ssm_block_weights(lam_bar, b_bar, c):
    _, g, p, h = b_bar.shape
    lg = SSM_LANE_GROUPS
    nj = g // lg
    lam = lam_bar[..., None]
    ct = jnp.swapaxes(c, -1, -2)

    def grp(m):
        return m.reshape((2, nj, lg) + m.shape[2:])

    def state_in(bmat):
        t = grp(jnp.swapaxes(bmat, -1, -2))
        return jnp.concatenate([_block_diag(t.real), _block_diag(t.imag)], axis=-1)

    def state_out(cmat):
        t = grp(cmat)
        return jnp.concatenate([_block_diag(t.real), _block_diag(-t.imag)], axis=-2)

    def direct(k):
        m = jnp.einsum('dgpc,dgph->dghc', ct * lam ** k, b_bar).real
        return _block_diag(grp(m))

    first_in, second_in = state_in(lam * b_bar), state_in(b_bar)
    first_out, second_out = state_out(ct * lam), state_out(ct * lam ** 2)
    zero = jnp.zeros_like(direct(0))
    sel = jnp.array([0, 1]).reshape(2, 1, 1, 1)
    wb = jnp.where(sel == 0, jnp.concatenate([first_in, second_in], axis=-2),
                   jnp.concatenate([second_in, first_in], axis=-2))
    wx = jnp.where(sel == 0, jnp.concatenate([first_out, second_out], axis=-1),
                   jnp.concatenate([second_out, first_out], axis=-1))
    u_first = jnp.concatenate([direct(0), direct(1)], axis=-1)
    u_second = jnp.concatenate([zero, direct(0)], axis=-1)
    swap = lambda m: jnp.concatenate([m[..., 128:], m[..., :128]], axis=-1)
    wu = jnp.where(sel == 0, jnp.concatenate([u_first, u_second], axis=-2),
                   jnp.concatenate([swap(u_second), swap(u_first)], axis=-2))
    l2 = (lam_bar ** 2).reshape(2, nj, 1, 1, lg * p)
    lam2 = jnp.broadcast_to(jnp.concatenate([l2.real, l2.imag], axis=2), (2, nj, 2, 8, lg * p))
    return wb.astype(BF16), wx.astype(BF16), wu.astype(BF16), lam2.astype(F32)


def _s5_kernel(u_ref, wb_ref, wx_ref, wu_ref, lam_ref, x0_ref, *refs, tt, emit_y):
    if emit_y:
        y_ref, xend_ref, lhs_e, lhs_o, bu_ref, st_ref, xs_ref, ysc_e, ysc_o = refs
    else:
        xend_ref, lhs_e, lhs_o, bu_ref, st_ref = refs
    direction = pl.program_id(0)
    ti = pl.program_id(2)
    nt = pl.num_programs(2)
    vb, _, lanes = u_ref.shape
    k = lam_ref.shape[-1]
    n = tt // 2

    @pl.when(ti == 0)
    def _():
        st_ref[...] = x0_ref[...]

    for b in range(vb):
        lhs_e[pl.ds(b, n, stride=vb), :] = u_ref[b, pl.ds(0, n, stride=2), :]
        lhs_o[pl.ds(b, n, stride=vb), :] = u_ref[b, pl.ds(1, n, stride=2), :]
    lhs = jnp.concatenate([lhs_e[...], lhs_o[...]], axis=1).astype(BF16)
    bu_ref[...] = jnp.dot(lhs, wb_ref[...], preferred_element_type=F32)
    lr = lam_ref[0]
    li = lam_ref[1]
    x_in = (st_ref[0], st_ref[1])
    if emit_y:
        in_row = pl.multiple_of(jnp.where(direction == 0, 0, n * vb), vb)
        xs_ref[pl.ds(in_row, vb), :] = jnp.concatenate(x_in, axis=1)

    def step(s, carry):
        xr, xi = carry
        t = jnp.where(direction == 0, s, n - 1 - s)
        b = bu_ref[pl.ds(pl.multiple_of(t * vb, vb), vb), :]
        nr = lr * xr - li * xi + b[:, :k]
        ni = lr * xi + li * xr + b[:, k:]
        if emit_y:
            out_row = pl.multiple_of(jnp.where(direction == 0, t + 1, t) * vb, vb)
            xs_ref[pl.ds(out_row, vb), :] = jnp.concatenate([nr, ni], axis=1)
        return nr, ni

    xr, xi = lax.fori_loop(0, n, step, x_in, unroll=8)
    st_ref[0] = xr
    st_ref[1] = xi
    if emit_y:
        base = pl.multiple_of(jnp.where(direction == 0, 0, vb), vb)
        xs = xs_ref[pl.ds(base, n * vb), :].astype(BF16)
        y = (jnp.dot(xs, wx_ref[...], preferred_element_type=F32)
             + jnp.dot(lhs, wu_ref[...], preferred_element_type=F32))
        ysc_e[...] = y[:, :lanes]
        ysc_o[...] = y[:, lanes:]
        for b in range(vb):
            y_ref[b, pl.ds(0, n, stride=2), :] = ysc_e[pl.ds(b, n, stride=vb), :]
            y_ref[b, pl.ds(1, n, stride=2), :] = ysc_o[pl.ds(b, n, stride=vb), :]

    @pl.when(ti == nt - 1)
    def _():
        xend_ref[0] = xr
        xend_ref[1] = xi


def s5_scan(u, wts, x0, vbatch, emit_y, tt=256):
    width = u.shape[1]
    vseq = u.shape[0] // vbatch
    wb, wx, wu, lam2 = wts['wb'], wts['wx'], wts['wu'], wts['lam2']
    nj = wb.shape[1]
    k = lam2.shape[-1]
    lanes = width // nj
    nt = vseq // tt
    n = tt // 2

    def tmap(d, t):
        return jnp.where(d == 0, t, nt - 1 - t)

    state_spec = pl.BlockSpec((None, None, 2, vbatch, k), lambda d, j, t: (d, j, 0, 0, 0))
    state_sds = jax.ShapeDtypeStruct((2, nj, 2, vbatch, k), F32)
    out_specs, out_shape = [state_spec], [state_sds]
    half = pltpu.VMEM((n * vbatch, lanes), F32)
    scratch = [half, half, pltpu.VMEM((n * vbatch, 2 * k), F32), pltpu.VMEM((2, vbatch, k), F32)]
    if emit_y:
        out_specs.insert(0, pl.BlockSpec((None, vbatch, tt, lanes), lambda d, j, t: (d, 0, tmap(d, t), j)))
        out_shape.insert(0, jax.ShapeDtypeStruct((2, vbatch, vseq, width), F32))
        scratch += [pltpu.VMEM(((n + 1) * vbatch, 2 * k), F32), half, half]
    outs = pl.pallas_call(
        functools.partial(_s5_kernel, tt=tt, emit_y=emit_y),
        grid=(2, nj, nt),
        in_specs=[
            pl.BlockSpec((vbatch, tt, lanes), lambda d, j, t: (0, tmap(d, t), j)),
            pl.BlockSpec((None, None, 2 * lanes, 2 * k), lambda d, j, t: (d, j, 0, 0)),
            pl.BlockSpec((None, None, 2 * k, 2 * lanes), lambda d, j, t: (d, j, 0, 0)),
            pl.BlockSpec((None, None, 2 * lanes, 2 * lanes), lambda d, j, t: (d, j, 0, 0)),
            state_spec, state_spec,
        ],
        out_specs=out_specs,
        out_shape=out_shape,
        scratch_shapes=scratch,
        compiler_params=_cparams(("parallel", "parallel", "arbitrary")),
        name="s5_scan" if emit_y else "s5_states",
    )(u.reshape(vbatch, vseq, width), wb, wx, wu, lam2, x0)
    if emit_y:
        return outs[0].reshape(2, vbatch * vseq, width), outs[1]
    return None, outs[0]


def s5_segment_carries(xend, lam_bar, seg_len, batch, nseg):
    _, nj, _, _, k = xend.shape
    lam_pow = (lam_bar ** seg_len).reshape(2, nj, 1, k)
    e = lax.complex(xend[:, :, 0], xend[:, :, 1]).reshape(2, nj, batch, nseg, k)
    fwd = [jnp.zeros((nj, batch, k), jnp.complex64)]
    for s in range(1, nseg):
        fwd.append(lam_pow[0] * fwd[-1] + e[0, :, :, s - 1])
    bwd = [jnp.zeros((nj, batch, k), jnp.complex64)]
    for s in range(nseg - 2, -1, -1):
        bwd.append(lam_pow[1] * bwd[-1] + e[1, :, :, s + 1])
    bwd = bwd[::-1]
    x_in = jnp.stack([jnp.stack(fwd, axis=2), jnp.stack(bwd, axis=2)])
    x_in = x_in.reshape(2, nj, batch * nseg, k)
    return jnp.stack([x_in.real, x_in.imag], axis=2).astype(F32)


def _gelu_tanh(x):
    return 0.5 * x * (1.0 + jnp.tanh(math.sqrt(2.0 / math.pi) * (x + 0.044715 * x * x * x)))


def _glu_kernel(yf_ref, yb_ref, u_ref, d_ref, wa_ref, wb_ref, o_ref):
    y = _gelu_tanh(yf_ref[...] + yb_ref[...] + d_ref[...] * u_ref[...]).astype(BF16)
    za = jnp.dot(y, wa_ref[...], preferred_element_type=F32)
    zb = jnp.dot(y, wb_ref[...], preferred_element_type=F32)
    o_ref[...] = (za * _sigmoid(zb)).astype(o_ref.dtype)


def ssm_glu(y, u, d_skip, w_glu, tm=512):
    n, width = u.shape
    wa = w_glu[:, :width]
    wb = w_glu[:, width:]
    const = lambda i: (0, 0)
    return pl.pallas_call(
        _glu_kernel,
        grid=(n // tm,),
        in_specs=[
            pl.BlockSpec((None, tm, width), lambda i: (0, i, 0)),
            pl.BlockSpec((None, tm, width), lambda i: (1, i, 0)),
            pl.BlockSpec((tm, width), lambda i: (i, 0)),
            pl.BlockSpec((1, width), const),
            pl.BlockSpec((width, width), const),
            pl.BlockSpec((width, width), const),
        ],
        out_specs=pl.BlockSpec((tm, width), lambda i: (i, 0)),
        out_shape=jax.ShapeDtypeStruct((n, width), BF16),
        compiler_params=_cparams(("parallel",)),
        name="ssm_glu",
    )(y, y, u, d_skip.reshape(1, width).astype(F32), wa, wb)


def _split_bf16(x):
    hi = x.astype(BF16)
    lo = (x - hi.astype(F32)).astype(BF16)
    return hi, lo


def _pack_bf16_pair(a, b):
    wa = pltpu.bitcast(a.astype(BF16).astype(F32), jnp.int32)
    wb = pltpu.bitcast(b.astype(BF16).astype(F32), jnp.int32)
    return wa | lax.shift_right_logical(wb, 16)


def _unpack_bf16_pair(w):
    hi = pltpu.bitcast(w & jnp.int32(-65536), F32).astype(BF16)
    lo = pltpu.bitcast(lax.shift_left(w, 16), F32).astype(BF16)
    return hi, lo


def _mixer_out_kernel(o0, o1, o2, l0, l1, l2, ob_ref, gate_ref, x_ref, wa_ref, wb_ref, wo_ref,
                      g_ref, wr_ref, x1_ref, h_ref, lg_ref):
    d_model = x_ref.shape[1]
    la, lb, lc = l0[...], l1[...], l2[...]
    m = jnp.maximum(jnp.maximum(la, lb), lc)
    ea, eb, ec = jnp.exp(la - m), jnp.exp(lb - m), jnp.exp(lc - m)
    oa = (ea * o0[...] + eb * o1[...] + ec * o2[...]) / (ea + eb + ec)
    pa = jnp.dot(oa.astype(BF16), wa_ref[...], preferred_element_type=F32)
    pb = jnp.dot(ob_ref[...], wb_ref[...], preferred_element_type=F32)
    gates = gate_ref[...].astype(F32)
    merged = gates[:, :d_model] * pa + gates[:, d_model:] * pb
    x1 = x_ref[...] + jnp.dot(merged.astype(BF16), wo_ref[...], preferred_element_type=F32)
    x1_ref[...] = x1
    h = x1 * lax.rsqrt(jnp.mean(x1 * x1, axis=-1, keepdims=True) + EPS) * g_ref[...]
    h_ref[...] = _pack_bf16_pair(h[:, :d_model // 2], h[:, d_model // 2:])
    hh, hl = _split_bf16(h)
    er = lg_ref.shape[1]
    both = jnp.dot(hh, wr_ref[...], preferred_element_type=F32)
    lg_ref[...] = both[:, :er] + both[:, er:] + jnp.dot(hl, wr_ref[:, :er], preferred_element_type=F32)


def mixer_out(o_list, lse_list, o_b, gates, x, w_br_a, w_br_b, w_out, g_ffn, wr_split, tm=256):
    n, d_model = x.shape
    row = lambda w: pl.BlockSpec((tm, w), lambda i: (i, 0))
    full = lambda a: pl.BlockSpec(a.shape, lambda i: (0,) * a.ndim, pipeline_mode=pl.Buffered(1))
    g2 = g_ffn.reshape(1, d_model).astype(F32)
    er = wr_split.shape[1] // 2
    return pl.pallas_call(
        _mixer_out_kernel,
        grid=(n // tm,),
        in_specs=[row(o.shape[1]) for o in o_list] + [row(l.shape[1]) for l in lse_list] + [
            row(o_b.shape[1]), row(gates.shape[1]), row(d_model),
            full(w_br_a), full(w_br_b), full(w_out), full(g2), full(wr_split)],
        out_specs=[row(d_model), row(d_model // 2), row(er)],
        out_shape=[jax.ShapeDtypeStruct((n, d_model), F32), jax.ShapeDtypeStruct((n, d_model // 2), jnp.int32),
                   jax.ShapeDtypeStruct((n, er), F32)],
        compiler_params=_cparams(("parallel",)),
        name="mixer_out",
    )(*o_list, *lse_list, o_b, gates, x, w_br_a, w_br_b, w_out, g2, wr_split)


def _expert_up_kernel(x_ref, wg_ref, wu_ref, o_ref, wg_s, wu_s):
    @pl.when(pl.program_id(2) == 0)
    def _():
        wg_s[...] = wg_ref[...].astype(BF16)
        wu_s[...] = wu_ref[...].astype(BF16)

    x_hi, x_lo = _unpack_bf16_pair(x_ref[...])
    half = x_hi.shape[1]

    def proj(w_s):
        return (jnp.dot(x_hi, w_s[:half, :], preferred_element_type=F32)
                + jnp.dot(x_lo, w_s[half:, :], preferred_element_type=F32))

    a = proj(wg_s)
    b = proj(wu_s)
    o_ref[...] = (a * _sigmoid(a) * b).astype(o_ref.dtype)


def expert_up(xe, w_gate, w_up, tm=1024, tf=512):
    e, cap, _ = xe.shape
    d_model = w_gate.shape[1]
    ff = w_gate.shape[2]
    tm = min(tm, cap)
    return pl.pallas_call(
        _expert_up_kernel,
        grid=(e, ff // tf, cap // tm),
        in_specs=[
            pl.BlockSpec((None, tm, d_model // 2), lambda ei, f, m: (ei, m, 0)),
            pl.BlockSpec((None, d_model, tf), lambda ei, f, m: (ei, 0, f)),
            pl.BlockSpec((None, d_model, tf), lambda ei, f, m: (ei, 0, f)),
        ],
        out_specs=pl.BlockSpec((None, tm, tf), lambda ei, f, m: (ei, m, f)),
        out_shape=jax.ShapeDtypeStruct((e, cap, ff), BF16),
        scratch_shapes=[pltpu.VMEM((d_model, tf), BF16), pltpu.VMEM((d_model, tf), BF16)],
        compiler_params=_cparams(("parallel", "parallel", "arbitrary")),
        name="expert_up",
    )(xe, w_gate, w_up)


def _expert_down_kernel(h_ref, w_ref, o_ref, w_s):
    @pl.when(pl.program_id(2) == 0)
    def _():
        w_s[...] = w_ref[...].astype(BF16)

    acc = jnp.dot(h_ref[...], w_s[...], preferred_element_type=F32)
    half = acc.shape[1] // 2
    o_ref[...] = _pack_bf16_pair(acc[:, :half], acc[:, half:])


def expert_down(hid, w_down, tm=512, tn=1024):
    e, cap, ff = hid.shape
    d_model = w_down.shape[2]
    tm = min(tm, cap)
    tn = min(tn, d_model)
    return pl.pallas_call(
        _expert_down_kernel,
        grid=(e, d_model // tn, cap // tm),
        in_specs=[
            pl.BlockSpec((None, tm, ff), lambda ei, n, m: (ei, m, 0)),
            pl.BlockSpec((None, ff, tn), lambda ei, n, m: (ei, 0, n)),
        ],
        out_specs=pl.BlockSpec((None, tm, tn // 2), lambda ei, n, m: (ei, m, n)),
        out_shape=jax.ShapeDtypeStruct((e, cap, d_model // 2), jnp.int32),
        scratch_shapes=[pltpu.VMEM((ff, tn), BF16)],
        compiler_params=_cparams(("parallel", "parallel", "arbitrary")),
        name="expert_down",
    )(hid, w_down)


SC_GATHER_CHUNK = 16


def sc_gather_rows(table, idx):
    n_idx = idx.shape[0]
    width = table.shape[1]
    info = plsc.get_sparse_core_info()
    n_workers = info.num_cores * info.num_subcores
    per_worker = n_idx // n_workers
    ch = SC_GATHER_CHUNK
    assert n_idx % n_workers == 0 and per_worker % (2 * ch) == 0
    mesh = plsc.VectorSubcoreMesh(core_axis_name="c", subcore_axis_name="s")

    @functools.partial(
        pl.kernel, mesh=mesh,
        out_type=jax.ShapeDtypeStruct((n_idx, width), table.dtype),
        scratch_types=[pltpu.VMEM((per_worker,), jnp.int32), pltpu.VMEM((ch, width), table.dtype),
                       pltpu.VMEM((ch, width), table.dtype), pltpu.SemaphoreType.DMA, pltpu.SemaphoreType.DMA],
        name="sc_gather_rows",
    )
    def gather(table_hbm, idx_hbm, out_hbm, idx_v, rows_a, rows_b, sem_a, sem_b):
        base = (lax.axis_index("s") * info.num_cores + lax.axis_index("c")) * per_worker
        pltpu.sync_copy(idx_hbm.at[pl.ds(base, per_worker)], idx_v)

        @pl.loop(0, per_worker // (2 * ch))
        def _(c):
            off = c * 2 * ch
            copy_a = pltpu.async_copy(table_hbm.at[idx_v.at[pl.ds(off, ch)]], rows_a, sem_a)
            copy_b = pltpu.async_copy(table_hbm.at[idx_v.at[pl.ds(off + ch, ch)]], rows_b, sem_b)
            copy_a.wait()
            pltpu.sync_copy(rows_a, out_hbm.at[pl.ds(base + off, ch)])
            copy_b.wait()
            pltpu.sync_copy(rows_b, out_hbm.at[pl.ds(base + off + ch, ch)])

    return gather(table, idx)


COMBINE_TOKENS = 256
COMBINE_ROWS = 512


def _combine_kernel(tile_ref, chunk_ref, first_ref, valid_ref, rows_ref, tok_ref, g_ref, o_ref, *, tn):
    p = pl.program_id(0)

    @pl.when(first_ref[p] == 1)
    def _():
        o_ref[...] = jnp.zeros_like(o_ref)

    @pl.when(valid_ref[p] == 1)
    def _():
        t, ch = o_ref.shape[0], tok_ref.shape[1]
        token = tile_ref[p] * t + lax.broadcasted_iota(jnp.int32, (t, ch), 0)
        weights = jnp.where(tok_ref[...] == token, g_ref[...], 0.0).astype(BF16)
        hi, lo = _unpack_bf16_pair(rows_ref[...])
        acc_hi = jnp.dot(weights, hi, preferred_element_type=F32)
        acc_lo = jnp.dot(weights, lo, preferred_element_type=F32)
        half = tn // 2
        for n in range(o_ref.shape[1] // tn):
            o_ref[:, n * tn:n * tn + half] += acc_hi[:, n * half:(n + 1) * half]
            o_ref[:, n * tn + half:(n + 1) * tn] += acc_lo[:, n * half:(n + 1) * half]


def combine_expert_outputs(rows, tok_sorted, gate_sorted, n_tok, tn):
    m, half_d = rows.shape
    t, ch = COMBINE_TOKENS, COMBINE_ROWS
    n_tiles, n_chunks = n_tok // t, m // ch
    n_pairs = n_tiles + n_chunks
    bounds = jnp.searchsorted(tok_sorted, jnp.arange(n_tiles + 1, dtype=jnp.int32) * t, side='left').astype(jnp.int32)
    start, end = bounds[:-1], bounds[1:]
    c_lo = jnp.minimum(start // ch, n_chunks - 1)
    c_hi = jnp.where(end > start, (end - 1) // ch, c_lo)
    count = c_hi - c_lo + 1
    offs = jnp.cumsum(count)
    p = jnp.arange(n_pairs, dtype=jnp.int32)
    tile = jnp.minimum(jnp.searchsorted(offs, p, side='right'), n_tiles - 1).astype(jnp.int32)
    first_p = (offs - count)[tile]
    valid = p < offs[-1]
    chunk = jnp.where(valid, c_lo[tile] + p - first_p, c_hi[-1]).astype(jnp.int32)
    first = (valid & (p == first_p)).astype(jnp.int32)
    grid_spec = pltpu.PrefetchScalarGridSpec(
        num_scalar_prefetch=4,
        grid=(n_pairs,),
        in_specs=[
            pl.BlockSpec((ch, half_d), lambda i, tl, ck, fr, vl: (ck[i], 0)),
            pl.BlockSpec((1, ch), lambda i, tl, ck, fr, vl: (0, ck[i])),
            pl.BlockSpec((1, ch), lambda i, tl, ck, fr, vl: (0, ck[i])),
        ],
        out_specs=pl.BlockSpec((t, 2 * half_d), lambda i, tl, ck, fr, vl: (tl[i], 0)),
    )
    return pl.pallas_call(
        functools.partial(_combine_kernel, tn=tn),
        grid_spec=grid_spec,
        out_shape=jax.ShapeDtypeStruct((n_tok, 2 * half_d), F32),
        compiler_params=_cparams(("arbitrary",)),
        name="combine_expert_outputs",
    )(tile, chunk, first, valid.astype(jnp.int32), rows, tok_sorted.reshape(1, m), gate_sorted.reshape(1, m).astype(F32))


def _final_kernel(x1_ref, moe_ref, p_ref, g_ref, wg_ref, wp_ref, o_ref):
    x2 = x1_ref[...] + moe_ref[...]
    h = (x2 * lax.rsqrt(jnp.mean(x2 * x2, axis=-1, keepdims=True) + EPS) * g_ref[...]).astype(BF16)
    gate = _sigmoid(jnp.dot(h, wg_ref[...], preferred_element_type=F32))
    proj = jnp.dot(p_ref[...].astype(BF16), wp_ref[...], preferred_element_type=F32)
    o_ref[...] = x2 + gate * proj


def final_embed(x1, moe, p, g_ple, w_gate, w_proj, tm=512):
    n, d_model = x1.shape
    row = lambda w: pl.BlockSpec((tm, w), lambda i: (i, 0))
    full = lambda a: pl.BlockSpec(a.shape, lambda i: (0,) * a.ndim, pipeline_mode=pl.Buffered(1))
    g2 = g_ple.reshape(1, d_model).astype(F32)
    return pl.pallas_call(
        _final_kernel,
        grid=(n // tm,),
        in_specs=[row(d_model), row(d_model), row(p.shape[1]), full(g2), full(w_gate), full(w_proj)],
        out_specs=row(d_model),
        out_shape=jax.ShapeDtypeStruct((n, d_model), F32),
        compiler_params=_cparams(("parallel",)),
        name="final_embed",
    )(x1, moe, p, g2, w_gate, w_proj)


def _prepare_weights(rel_bias, g_mix, w_in, q_gain, k_gain, ssm_a_re, ssm_a_im, ssm_log_step,
                     ssm_b_re, ssm_b_im, ssm_c_re, ssm_c_im, ssm_d, w_glu, w_br_a, w_br_b, w_out,
                     g_ffn, w_router, g_ple, w_ple_gate, w_ple_proj):
    n_heads = q_gain.shape[0]
    att_w = n_heads * HEAD_DIM
    hw = HEADS_PER_GROUP * HEAD_DIM
    ssm_w = ssm_d.shape[0]
    w_in_bf = w_in.astype(BF16)
    w_qkv, gains = [], []
    qg = q_gain.astype(F32) * (HEAD_DIM ** -0.5)
    kg = k_gain.astype(F32)
    for gi in range(len(ATT_PATTERNS)):
        cols = [w_in_bf[:, s * att_w + gi * hw: s * att_w + (gi + 1) * hw] for s in range(3)]
        w_qkv.append(jnp.concatenate(cols, axis=1))
        hs = slice(gi * HEADS_PER_GROUP, (gi + 1) * HEADS_PER_GROUP)
        gains.append(jnp.stack([qg[hs].reshape(1, hw), kg[hs].reshape(1, hw), jnp.ones((1, hw), F32)]))
    lam_bar, b_bar, c = ssm_discretise(ssm_a_re, ssm_a_im, ssm_log_step, ssm_b_re, ssm_b_im, ssm_c_re, ssm_c_im)
    wb, wx, wu, lam2 = ssm_block_weights(lam_bar, b_bar, c)
    n_exp = w_router.shape[1]
    wr = jnp.pad(w_router.astype(F32), ((0, 0), (0, 128 - n_exp)))
    wr_split = jnp.concatenate(_split_bf16(wr), axis=1)
    return dict(
        rel_bias=rel_bias, bias_tables={}, g_mix=g_mix, w_qkv=w_qkv, gains=gains,
        w_u=w_in_bf[:, 3 * att_w:3 * att_w + ssm_w], w_gates=w_in_bf[:, 3 * att_w + ssm_w:],
        lam_bar=lam_bar, wb=wb, wx=wx, wu=wu, lam2=lam2, ssm_d=ssm_d, w_glu=w_glu.astype(BF16),
        w_br_a=w_br_a.astype(BF16), w_br_b=w_br_b.astype(BF16), w_out=w_out.astype(BF16),
        g_ffn=g_ffn, wr_split=wr_split, n_exp=n_exp, g_ple=g_ple,
        w_ple_gate=w_ple_gate.astype(BF16), w_ple_proj=w_ple_proj.astype(BF16))


def _ssm_branch(hn, wts, batch, seq):
    nseg = max(1, 8 // batch)
    vbatch = batch * nseg
    vseq = seq // nseg
    u = plain_proj(hn, wts['w_u'], F32, False, "u_proj")
    nj, k = wts['lam2'].shape[1], wts['lam2'].shape[-1]
    x0 = jnp.zeros((2, nj, 2, vbatch, k), F32)
    if nseg > 1:
        _, xend = s5_scan(u, wts, x0, vbatch, emit_y=False)
        x0 = s5_segment_carries(xend, wts['lam_bar'], vseq, batch, nseg)
    y, _ = s5_scan(u, wts, x0, vbatch, emit_y=True)
    return ssm_glu(y, u, wts['ssm_d'], wts['w_glu'])


def _expert_choice(h, logits, n_exp, w_exp_gate, w_exp_up, w_exp_down):
    n_tok = h.shape[0]
    d_model = w_exp_down.shape[2]
    cap = CAPACITY_FACTOR * n_tok // n_exp
    affinity = jax.nn.softmax(logits[:, :n_exp], axis=-1)
    gate, idx = lax.top_k(affinity.T, cap)
    idx_flat = idx.reshape(-1)
    xe = sc_gather_rows(h, idx_flat).reshape(n_exp, cap, h.shape[1])
    hid = expert_up(xe, w_exp_gate, w_exp_up)
    tn = min(1024, d_model)
    ye = expert_down(hid, w_exp_down, tn=tn).reshape(n_exp * cap, h.shape[1])
    order = jnp.argsort(idx_flat).astype(jnp.int32)
    tok_sorted = idx_flat[order]
    gate_sorted = gate.reshape(-1)[order]
    return combine_expert_outputs(sc_gather_rows(ye, order), tok_sorted, gate_sorted, n_tok, tn)


def _trunk(x, p, wts, w_exp_gate, w_exp_up, w_exp_down):
    batch, seq, d_model = x.shape
    n = batch * seq
    xf = x.reshape(n, d_model)
    dils = [dil for _, dil in ATT_PATTERNS]
    hn_list = rms_bf16(x, wts['g_mix'], dils)
    hn = hn_list[dils.index(1)]
    o_list, lse_list = [], []
    for gi, dil in enumerate(dils):
        qkv = qkv_proj(hn_list[gi], wts['w_qkv'][gi], wts['gains'][gi], f"qkv_proj_d{dil}")
        hs = slice(gi * HEADS_PER_GROUP, (gi + 1) * HEADS_PER_GROUP)
        bias_key = (gi,) + _attn_geometry(seq // dil)
        if bias_key not in wts['bias_tables']:
            wts['bias_tables'][bias_key] = attn_bias_tables(wts['rel_bias'][:, hs], dil, seq // dil)
        bias = wts['bias_tables'][bias_key]
        o, lse = dilated_attention(qkv, bias, batch, seq, dil)
        o_list.append(o)
        lse_list.append(lse)
    o_b = _ssm_branch(hn, wts, batch, seq)
    gates = plain_proj(hn, wts['w_gates'], BF16, True, "gates_proj")
    x1, h2, logits = mixer_out(o_list, lse_list, o_b, gates, xf, wts['w_br_a'], wts['w_br_b'],
                               wts['w_out'], wts['g_ffn'], wts['wr_split'])
    moe = _expert_choice(h2, logits, wts['n_exp'], w_exp_gate, w_exp_up, w_exp_down)
    out = final_embed(x1, moe, p.reshape(n, p.shape[-1]), wts['g_ple'], wts['w_ple_gate'], wts['w_ple_proj'])
    return out.reshape(batch, seq, d_model)


def kernel(x_prompt, x_sample, p_prompt, p_sample, rel_bias, g_mix, w_in, q_gain, k_gain, ssm_a_re, ssm_a_im, ssm_log_step, ssm_b_re, ssm_b_im, ssm_c_re, ssm_c_im, ssm_d, w_glu, w_br_a, w_br_b, w_out, g_ffn, w_router, w_exp_gate, w_exp_up, w_exp_down, g_ple, w_ple_gate, w_ple_proj):
    assert g_mix.shape[0] == 1, "single-layer trunk"
    wts = _prepare_weights(rel_bias, g_mix[0], w_in[0], q_gain[0], k_gain[0], ssm_a_re[0], ssm_a_im[0],
                           ssm_log_step[0], ssm_b_re[0], ssm_b_im[0], ssm_c_re[0], ssm_c_im[0], ssm_d[0],
                           w_glu[0], w_br_a[0], w_br_b[0], w_out[0], g_ffn[0], w_router[0], g_ple[0],
                           w_ple_gate[0], w_ple_proj[0])
    y_prompt = _trunk(x_prompt, p_prompt[0], wts, w_exp_gate[0], w_exp_up[0], w_exp_down[0])
    y_sample = _trunk(x_sample, p_sample[0], wts, w_exp_gate[0], w_exp_up[0], w_exp_down[0])
    return (y_prompt, y_sample)
```

```python
import functools
import math

import jax
import jax.numpy as jnp
import numpy as np
from jax import lax
from jax.experimental import pallas as pl
from jax.experimental.pallas import tpu as pltpu
from jax.experimental.pallas import tpu_sc as plsc

F32 = jnp.float32
BF16 = jnp.bfloat16

LANES = 128
HEAD_DIM = 128
HEADS_PER_GROUP = 4
ATT_PATTERNS = ((128, 1), (512, 4), (2048, 16))
ATT_SIDE = 64
N_BUCKETS = 32
MAX_DISTANCE = 1024
CAPACITY_FACTOR = 2
EPS = 1e-6
NEG = -1e30
SSM_LANE_GROUPS = 8
V7X_VMEM_LIMIT = 56 * 1024 * 1024

assert all(w // (2 * d) == ATT_SIDE for w, d in ATT_PATTERNS)


def _cparams(semantics, vmem=V7X_VMEM_LIMIT):
    return pltpu.CompilerParams(dimension_semantics=("arbitrary",) * len(semantics), vmem_limit_bytes=vmem)


def _sigmoid(x):
    return 1.0 / (1.0 + jnp.exp(-x))


def _rms_kernel(x_ref, g_ref, *refs, dils):
    o_refs, y_ref = refs[:-1], refs[-1]
    x = x_ref[...]
    y = x * lax.rsqrt(jnp.mean(x * x, axis=-1, keepdims=True) + EPS) * g_ref[...]
    tm = x.shape[0]
    n_chunks = y_ref.shape[0]
    for c in range(n_chunks):
        y_ref[c] = y[:, c * LANES:(c + 1) * LANES]
    for o_ref, dil in zip(o_refs, dils):
        if dil == 1:
            o_ref[...] = y.astype(o_ref.dtype)
        else:
            for r in range(dil):
                rows = [y_ref[c, pl.ds(r, tm // dil, stride=dil), :] for c in range(n_chunks)]
                o_ref[r] = jnp.concatenate(rows, axis=1).astype(o_ref.dtype)


def rms_bf16(x, g, dils, tm=1024):
    batch, seq, d = x.shape
    out_specs, out_shape = [], []
    for dil in dils:
        if dil == 1:
            out_specs.append(pl.BlockSpec((None, tm, d), lambda b, i: (b, i, 0)))
            out_shape.append(jax.ShapeDtypeStruct((batch, seq, d), BF16))
        else:
            out_specs.append(pl.BlockSpec((None, dil, tm // dil, d), lambda b, i: (b, 0, i, 0)))
            out_shape.append(jax.ShapeDtypeStruct((batch, dil, seq // dil, d), BF16))
    outs = pl.pallas_call(
        functools.partial(_rms_kernel, dils=tuple(dils)),
        grid=(batch, seq // tm),
        in_specs=[pl.BlockSpec((None, tm, d), lambda b, i: (b, i, 0)), pl.BlockSpec((1, d), lambda b, i: (0, 0))],
        out_specs=out_specs,
        out_shape=out_shape,
        scratch_shapes=[pltpu.VMEM((d // LANES, tm, LANES), F32)],
        compiler_params=_cparams(("parallel", "parallel")),
        name="rms_bf16",
    )(x, g.reshape(1, d).astype(F32))
    return [o.reshape(batch * seq, d) for o in outs]


def _head_rms(acc, gain):
    outs = []
    for h in range(acc.shape[1] // HEAD_DIM):
        t = acc[:, h * HEAD_DIM:(h + 1) * HEAD_DIM]
        outs.append(t * lax.rsqrt(jnp.mean(t * t, axis=-1, keepdims=True) + EPS))
    return jnp.concatenate(outs, axis=1) * gain


def _qkv_kernel(x_ref, w_ref, g_ref, o_ref):
    x = x_ref[...]
    tn = o_ref.shape[1] // 3
    for j in range(3):
        acc = jnp.dot(x, w_ref[:, j * tn:(j + 1) * tn], preferred_element_type=F32)
        if j < 2:
            acc = _head_rms(acc, g_ref[j])
        o_ref[:, j * tn:(j + 1) * tn] = acc.astype(o_ref.dtype)


def qkv_proj(hn, w, gains, name, tm=1024):
    n, d_model = hn.shape
    tn = HEADS_PER_GROUP * HEAD_DIM
    return pl.pallas_call(
        _qkv_kernel,
        grid=(n // tm,),
        in_specs=[
            pl.BlockSpec((tm, d_model), lambda i: (i, 0)),
            pl.BlockSpec((d_model, 3 * tn), lambda i: (0, 0), pipeline_mode=pl.Buffered(1)),
            pl.BlockSpec((3, 1, tn), lambda i: (0, 0, 0), pipeline_mode=pl.Buffered(1)),
        ],
        out_specs=pl.BlockSpec((tm, 3 * tn), lambda i: (i, 0)),
        out_shape=jax.ShapeDtypeStruct((n, 3 * tn), BF16),
        compiler_params=_cparams(("parallel",)),
        name=name,
    )(hn, w, gains)


def _proj_kernel(x_ref, w_ref, o_ref, *, act):
    acc = jnp.dot(x_ref[...], w_ref[...], preferred_element_type=F32)
    if act:
        acc = _sigmoid(acc)
    o_ref[...] = acc.astype(o_ref.dtype)


def plain_proj(hn, w, out_dtype, act, name, tm=1024, tn=1024):
    n, d_model = hn.shape
    width = w.shape[1]
    tn = min(tn, width)
    return pl.pallas_call(
        functools.partial(_proj_kernel, act=act),
        grid=(width // tn, n // tm),
        in_specs=[
            pl.BlockSpec((tm, d_model), lambda j, i: (i, 0)),
            pl.BlockSpec((d_model, tn), lambda j, i: (0, j)),
        ],
        out_specs=pl.BlockSpec((tm, tn), lambda j, i: (i, j)),
        out_shape=jax.ShapeDtypeStruct((n, width), out_dtype),
        compiler_params=_cparams(("arbitrary", "parallel")),
        name=name,
    )(hn, w)


def _t5_buckets(rel):
    half = N_BUCKETS // 2
    max_exact = half // 2
    n = np.abs(rel)
    large = max_exact + (np.log(np.maximum(n, 1) / max_exact) / math.log(MAX_DISTANCE / max_exact)
                         * (half - max_exact)).astype(np.int32)
    large = np.minimum(large, half - 1)
    return ((rel > 0).astype(np.int32) * half + np.where(n < max_exact, n, large)).astype(np.int32)


def _attn_geometry(l):
    tqb = min(2 * ATT_SIDE, l)
    win = min(4 * ATT_SIDE, l)
    return tqb, win


def attn_bias_tables(rel_bias_g, dil, l):
    tqb, win = _attn_geometry(l)
    i = np.arange(tqb)[:, None]
    j = np.arange(win)[None, :]
    rel = np.stack([j - i - ATT_SIDE * shift for shift in range(3)])
    valid = np.abs(rel) <= ATT_SIDE
    onehot = (_t5_buckets(dil * rel)[..., None] == np.arange(N_BUCKETS)).astype(np.int8)
    b = jnp.einsum('sijb,bh->shij', jnp.asarray(onehot).astype(F32), rel_bias_g.astype(F32),
                   precision=lax.Precision.HIGHEST)
    return jnp.where(valid[:, None], b, NEG)


def _attn_kernel(q_ref, k_ref, v_ref, bias_ref, o_ref, lse_ref, *, l, dil, tql):
    tqb, win = _attn_geometry(l)
    qi = pl.program_id(2)
    for r in range(dil):
        for c in range(tql // tqb):
            q0 = qi * tql + c * tqb
            start = pl.multiple_of(jnp.clip(q0 - ATT_SIDE, 0, l - win), ATT_SIDE)
            shift = (q0 - start) // ATT_SIDE
            q = q_ref[r, c * tqb:(c + 1) * tqb, :]
            kw = k_ref[r, pl.ds(start, win), :]
            vw = v_ref[r, pl.ds(start, win), :]
            logits = lax.dot_general(q, kw, (((1,), (1,)), ((), ())), preferred_element_type=F32)
            logits = logits + bias_ref[shift]
            m = jnp.max(logits, axis=-1, keepdims=True)
            p = jnp.exp(logits - m)
            s = jnp.sum(p, axis=-1, keepdims=True)
            o = jnp.dot(p.astype(vw.dtype), vw, preferred_element_type=F32)
            if dil == 1:
                rows = pl.ds(c * tqb, tqb)
            else:
                rows = pl.ds(c * tqb * dil + r, tqb, stride=dil)
            o_ref[rows, :] = o / s
            lse_ref[rows, :] = jnp.broadcast_to(m + jnp.log(s), (tqb, HEAD_DIM))


def dilated_attention(qkv, bias, batch, seq, dil):
    l = seq // dil
    tqb, win = _attn_geometry(l)
    tql = min(l, max(tqb, 8 * tqb // dil))
    tok = tql * dil
    hw = HEADS_PER_GROUP * HEAD_DIM
    x = qkv.reshape(batch, dil, l, 3 * hw)
    out_sds = jax.ShapeDtypeStruct((batch, seq, hw), F32)
    out_spec = pl.BlockSpec((None, tok, HEAD_DIM), lambda b, h, i: (b, i, h))
    o, lse = pl.pallas_call(
        functools.partial(_attn_kernel, l=l, dil=dil, tql=tql),
        grid=(batch, HEADS_PER_GROUP, l // tql),
        in_specs=[
            pl.BlockSpec((None, dil, tql, HEAD_DIM), lambda b, h, i: (b, 0, i, h)),
            pl.BlockSpec((None, dil, l, HEAD_DIM), lambda b, h, i: (b, 0, 0, HEADS_PER_GROUP + h)),
            pl.BlockSpec((None, dil, l, HEAD_DIM), lambda b, h, i: (b, 0, 0, 2 * HEADS_PER_GROUP + h)),
            pl.BlockSpec((3, None, tqb, win), lambda b, h, i: (0, h, 0, 0)),
        ],
        out_specs=[out_spec, out_spec],
        out_shape=[out_sds, out_sds],
        compiler_params=_cparams(("parallel", "parallel", "parallel")),
        name=f"dilated_attn_d{dil}",
    )(x, x, x, bias)
    return o.reshape(batch * seq, hw), lse.reshape(batch * seq, hw)


def ssm_discretise(a_re, a_im, log_step, b_re, b_im, c_re, c_im):
    lam = lax.complex(a_re.astype(F32), a_im.astype(F32))
    step = jnp.exp(log_step.astype(F32))[..., None]
    lam_bar = jnp.exp(lam * step)
    b_bar = ((lam_bar - 1.0) / lam)[..., None] * lax.complex(b_re.astype(F32), b_im.astype(F32))
    c = lax.complex(c_re.astype(F32), c_im.astype(F32))
    return lam_bar, b_bar, c


def _block_diag(m):
    _, nj, lg, a, b = m.shape
    return jnp.einsum('djgab,gk->djgakb', m, jnp.eye(lg, dtype=F32)).reshape(2, nj, lg * a, lg * b)


def ssm_block_weights(lam_bar, b_bar, c):
    _, g, p, h = b_bar.shape
    lg = SSM_LANE_GROUPS
    nj = g // lg
    lam = lam_bar[..., None]
    ct = jnp.swapaxes(c, -1, -2)

    def grp(m):
        return m.reshape((2, nj, lg) + m.shape[2:])

    def state_in(bmat):
        t = grp(jnp.swapaxes(bmat, -1, -2))
        return jnp.concatenate([_block_diag(t.real), _block_diag(t.imag)], axis=-1)

    def state_out(cmat):
        t = grp(cmat)
        return jnp.concatenate([_block_diag(t.real), _block_diag(-t.imag)], axis=-2)

    def direct(k):
        m = jnp.einsum('dgpc,dgph->dghc', ct * lam ** k, b_bar).real
        return _block_diag(grp(m))

    first_in, second_in = state_in(lam * b_bar), state_in(b_bar)
    first_out, second_out = state_out(ct * lam), state_out(ct * lam ** 2)
    zero = jnp.zeros_like(direct(0))
    sel = jnp.array([0, 1]).reshape(2, 1, 1, 1)
    wb = jnp.where(sel == 0, jnp.concatenate([first_in, second_in], axis=-2),
                   jnp.concatenate([second_in, first_in], axis=-2))
    wx = jnp.where(sel == 0, jnp.concatenate([first_out, second_out], axis=-1),
                   jnp.concatenate([second_out, first_out], axis=-1))
    u_first = jnp.concatenate([direct(0), direct(1)], axis=-1)
    u_second = jnp.concatenate([zero, direct(0)], axis=-1)
    swap = lambda m: jnp.concatenate([m[..., 128:], m[..., :128]], axis=-1)
    wu = jnp.where(sel == 0, jnp.concatenate([u_first, u_second], axis=-2),
                   jnp.concatenate([swap(u_second), swap(u_first)], axis=-2))
    l2 = (lam_bar ** 2).reshape(2, nj, 1, 1, lg * p)
    lam2 = jnp.broadcast_to(jnp.concatenate([l2.real, l2.imag], axis=2), (2, nj, 2, 8, lg * p))
    return wb.astype(BF16), wx.astype(BF16), wu.astype(BF16), lam2.astype(F32)


def _s5_kernel(u_ref, wb_ref, wx_ref, wu_ref, lam_ref, x0_ref, *refs, tt, emit_y):
    if emit_y:
        y_ref, xend_ref, lhs_e, lhs_o, bu_ref, st_ref, xs_ref, ysc_e, ysc_o = refs
    else:
        xend_ref, lhs_e, lhs_o, bu_ref, st_ref = refs
    direction = pl.program_id(0)
    ti = pl.program_id(2)
    nt = pl.num_programs(2)
    vb, _, lanes = u_ref.shape
    k = lam_ref.shape[-1]
    n = tt // 2

    @pl.when(ti == 0)
    def _():
        st_ref[...] = x0_ref[...]

    for b in range(vb):
        lhs_e[pl.ds(b, n, stride=vb), :] = u_ref[b, pl.ds(0, n, stride=2), :]
        lhs_o[pl.ds(b, n, stride=vb), :] = u_ref[b, pl.ds(1, n, stride=2), :]
    lhs = jnp.concatenate([lhs_e[...], lhs_o[...]], axis=1).astype(BF16)
    bu_ref[...] = jnp.dot(lhs, wb_ref[...], preferred_element_type=F32)
    lr = lam_ref[0]
    li = lam_ref[1]
    x_in = (st_ref[0], st_ref[1])
    if emit_y:
        in_row = pl.multiple_of(jnp.where(direction == 0, 0, n * vb), vb)
        xs_ref[pl.ds(in_row, vb), :] = jnp.concatenate(x_in, axis=1)

    def step(s, carry):
        xr, xi = carry
        t = jnp.where(direction == 0, s, n - 1 - s)
        b = bu_ref[pl.ds(pl.multiple_of(t * vb, vb), vb), :]
        nr = lr * xr - li * xi + b[:, :k]
        ni = lr * xi + li * xr + b[:, k:]
        if emit_y:
            out_row = pl.multiple_of(jnp.where(direction == 0, t + 1, t) * vb, vb)
            xs_ref[pl.ds(out_row, vb), :] = jnp.concatenate([nr, ni], axis=1)
        return nr, ni

    xr, xi = lax.fori_loop(0, n, step, x_in, unroll=8)
    st_ref[0] = xr
    st_ref[1] = xi
    if emit_y:
        base = pl.multiple_of(jnp.where(direction == 0, 0, vb), vb)
        xs = xs_ref[pl.ds(base, n * vb), :].astype(BF16)
        y = (jnp.dot(xs, wx_ref[...], preferred_element_type=F32)
             + jnp.dot(lhs, wu_ref[...], preferred_element_type=F32))
        ysc_e[...] = y[:, :lanes]
        ysc_o[...] = y[:, lanes:]
        for b in range(vb):
            y_ref[b, pl.ds(0, n, stride=2), :] = ysc_e[pl.ds(b, n, stride=vb), :]
            y_ref[b, pl.ds(1, n, stride=2), :] = ysc_o[pl.ds(b, n, stride=vb), :]

    @pl.when(ti == nt - 1)
    def _():
        xend_ref[0] = xr
        xend_ref[1] = xi


def s5_scan(u, wts, x0, vbatch, emit_y, tt=256):
    width = u.shape[1]
    vseq = u.shape[0] // vbatch
    wb, wx, wu, lam2 = wts['wb'], wts['wx'], wts['wu'], wts['lam2']
    nj = wb.shape[1]
    k = lam2.shape[-1]
    lanes = width // nj
    nt = vseq // tt
    n = tt // 2

    def tmap(d, t):
        return jnp.where(d == 0, t, nt - 1 - t)

    state_spec = pl.BlockSpec((None, None, 2, vbatch, k), lambda d, j, t: (d, j, 0, 0, 0))
    state_sds = jax.ShapeDtypeStruct((2, nj, 2, vbatch, k), F32)
    out_specs, out_shape = [state_spec], [state_sds]
    half = pltpu.VMEM((n * vbatch, lanes), F32)
    scratch = [half, half, pltpu.VMEM((n * vbatch, 2 * k), F32), pltpu.VMEM((2, vbatch, k), F32)]
    if emit_y:
        out_specs.insert(0, pl.BlockSpec((None, vbatch, tt, lanes), lambda d, j, t: (d, 0, tmap(d, t), j)))
        out_shape.insert(0, jax.ShapeDtypeStruct((2, vbatch, vseq, width), F32))
        scratch += [pltpu.VMEM(((n + 1) * vbatch, 2 * k), F32), half, half]
    outs = pl.pallas_call(
        functools.partial(_s5_kernel, tt=tt, emit_y=emit_y),
        grid=(2, nj, nt),
        in_specs=[
            pl.BlockSpec((vbatch, tt, lanes), lambda d, j, t: (0, tmap(d, t), j)),
            pl.BlockSpec((None, None, 2 * lanes, 2 * k), lambda d, j, t: (d, j, 0, 0)),
            pl.BlockSpec((None, None, 2 * k, 2 * lanes), lambda d, j, t: (d, j, 0, 0)),
            pl.BlockSpec((None, None, 2 * lanes, 2 * lanes), lambda d, j, t: (d, j, 0, 0)),
            state_spec, state_spec,
        ],
        out_specs=out_specs,
        out_shape=out_shape,
        scratch_shapes=scratch,
        compiler_params=_cparams(("parallel", "parallel", "arbitrary")),
        name="s5_scan" if emit_y else "s5_states",
    )(u.reshape(vbatch, vseq, width), wb, wx, wu, lam2, x0)
    if emit_y:
        return outs[0].reshape(2, vbatch * vseq, width), outs[1]
    return None, outs[0]


def s5_segment_carries(xend, lam_bar, seg_len, batch, nseg):
    _, nj, _, _, k = xend.shape
    lam_pow = (lam_bar ** seg_len).reshape(2, nj, 1, k)
    e = lax.complex(xend[:, :, 0], xend[:, :, 1]).reshape(2, nj, batch, nseg, k)
    fwd = [jnp.zeros((nj, batch, k), jnp.complex64)]
    for s in range(1, nseg):
        fwd.append(lam_pow[0] * fwd[-1] + e[0, :, :, s - 1])
    bwd = [jnp.zeros((nj, batch, k), jnp.complex64)]
    for s in range(nseg - 2, -1, -1):
        bwd.append(lam_pow[1] * bwd[-1] + e[1, :, :, s + 1])
    bwd = bwd[::-1]
    x_in = jnp.stack([jnp.stack(fwd, axis=2), jnp.stack(bwd, axis=2)])
    x_in = x_in.reshape(2, nj, batch * nseg, k)
    return jnp.stack([x_in.real, x_in.imag], axis=2).astype(F32)


def _gelu_tanh(x):
    return 0.5 * x * (1.0 + jnp.tanh(math.sqrt(2.0 / math.pi) * (x + 0.044715 * x * x * x)))


def _glu_kernel(yf_ref, yb_ref, u_ref, d_ref, wa_ref, wb_ref, o_ref):
    y = _gelu_tanh(yf_ref[...] + yb_ref[...] + d_ref[...] * u_ref[...]).astype(BF16)
    za = jnp.dot(y, wa_ref[...], preferred_element_type=F32)
    zb = jnp.dot(y, wb_ref[...], preferred_element_type=F32)
    o_ref[...] = (za * _sigmoid(zb)).astype(o_ref.dtype)


def ssm_glu(y, u, d_skip, w_glu, tm=512):
    n, width = u.shape
    wa = w_glu[:, :width]
    wb = w_glu[:, width:]
    const = lambda i: (0, 0)
    return pl.pallas_call(
        _glu_kernel,
        grid=(n // tm,),
        in_specs=[
            pl.BlockSpec((None, tm, width), lambda i: (0, i, 0)),
            pl.BlockSpec((None, tm, width), lambda i: (1, i, 0)),
            pl.BlockSpec((tm, width), lambda i: (i, 0)),
            pl.BlockSpec((1, width), const),
            pl.BlockSpec((width, width), const),
            pl.BlockSpec((width, width), const),
        ],
        out_specs=pl.BlockSpec((tm, width), lambda i: (i, 0)),
        out_shape=jax.ShapeDtypeStruct((n, width), BF16),
        compiler_params=_cparams(("parallel",)),
        name="ssm_glu",
    )(y, y, u, d_skip.reshape(1, width).astype(F32), wa, wb)


def _split_bf16(x):
    hi = x.astype(BF16)
    lo = (x - hi.astype(F32)).astype(BF16)
    return hi, lo


def _pack_bf16_pair(a, b):
    wa = pltpu.bitcast(a.astype(BF16).astype(F32), jnp.int32)
    wb = pltpu.bitcast(b.astype(BF16).astype(F32), jnp.int32)
    return wa | lax.shift_right_logical(wb, 16)


def _unpack_bf16_pair(w):
    hi = pltpu.bitcast(w & jnp.int32(-65536), F32).astype(BF16)
    lo = pltpu.bitcast(lax.shift_left(w, 16), F32).astype(BF16)
    return hi, lo


def _mixer_out_kernel(o0, o1, o2, l0, l1, l2, ob_ref, gate_ref, x_ref, wa_ref, wb_ref, wo_ref,
                      g_ref, wr_ref, x1_ref, h_ref, lg_ref):
    d_model = x_ref.shape[1]
    la, lb, lc = l0[...], l1[...], l2[...]
    m = jnp.maximum(jnp.maximum(la, lb), lc)
    ea, eb, ec = jnp.exp(la - m), jnp.exp(lb - m), jnp.exp(lc - m)
    oa = (ea * o0[...] + eb * o1[...] + ec * o2[...]) / (ea + eb + ec)
    pa = jnp.dot(oa.astype(BF16), wa_ref[...], preferred_element_type=F32)
    pb = jnp.dot(ob_ref[...], wb_ref[...], preferred_element_type=F32)
    gates = gate_ref[...].astype(F32)
    merged = gates[:, :d_model] * pa + gates[:, d_model:] * pb
    x1 = x_ref[...] + jnp.dot(merged.astype(BF16), wo_ref[...], preferred_element_type=F32)
    x1_ref[...] = x1
    h = x1 * lax.rsqrt(jnp.mean(x1 * x1, axis=-1, keepdims=True) + EPS) * g_ref[...]
    h_ref[...] = _pack_bf16_pair(h[:, :d_model // 2], h[:, d_model // 2:])
    hh, hl = _split_bf16(h)
    er = lg_ref.shape[1]
    both = jnp.dot(hh, wr_ref[...], preferred_element_type=F32)
    lg_ref[...] = both[:, :er] + both[:, er:] + jnp.dot(hl, wr_ref[:, :er], preferred_element_type=F32)


def mixer_out(o_list, lse_list, o_b, gates, x, w_br_a, w_br_b, w_out, g_ffn, wr_split, tm=256):
    n, d_model = x.shape
    row = lambda w: pl.BlockSpec((tm, w), lambda i: (i, 0))
    full = lambda a: pl.BlockSpec(a.shape, lambda i: (0,) * a.ndim, pipeline_mode=pl.Buffered(1))
    g2 = g_ffn.reshape(1, d_model).astype(F32)
    er = wr_split.shape[1] // 2
    return pl.pallas_call(
        _mixer_out_kernel,
        grid=(n // tm,),
        in_specs=[row(o.shape[1]) for o in o_list] + [row(l.shape[1]) for l in lse_list] + [
            row(o_b.shape[1]), row(gates.shape[1]), row(d_model),
            full(w_br_a), full(w_br_b), full(w_out), full(g2), full(wr_split)],
        out_specs=[row(d_model), row(d_model // 2), row(er)],
        out_shape=[jax.ShapeDtypeStruct((n, d_model), F32), jax.ShapeDtypeStruct((n, d_model // 2), jnp.int32),
                   jax.ShapeDtypeStruct((n, er), F32)],
        compiler_params=_cparams(("parallel",)),
        name="mixer_out",
    )(*o_list, *lse_list, o_b, gates, x, w_br_a, w_br_b, w_out, g2, wr_split)


def _expert_up_kernel(x_ref, wg_ref, wu_ref, o_ref, wg_s, wu_s):
    @pl.when(pl.program_id(2) == 0)
    def _():
        wg_s[...] = wg_ref[...].astype(BF16)
        wu_s[...] = wu_ref[...].astype(BF16)

    x_hi, x_lo = _unpack_bf16_pair(x_ref[...])
    half = x_hi.shape[1]

    def proj(w_s):
        return (jnp.dot(x_hi, w_s[:half, :], preferred_element_type=F32)
                + jnp.dot(x_lo, w_s[half:, :], preferred_element_type=F32))

    a = proj(wg_s)
    b = proj(wu_s)
    o_ref[...] = (a * _sigmoid(a) * b).astype(o_ref.dtype)


def expert_up(xe, w_gate, w_up, tm=1024, tf=512):
    e, cap, _ = xe.shape
    d_model = w_gate.shape[1]
    ff = w_gate.shape[2]
    tm = math.gcd(tm, cap)
    return pl.pallas_call(
        _expert_up_kernel,
        grid=(e, ff // tf, cap // tm),
        in_specs=[
            pl.BlockSpec((None, tm, d_model // 2), lambda ei, f, m: (ei, m, 0)),
            pl.BlockSpec((None, d_model, tf), lambda ei, f, m: (ei, 0, f)),
            pl.BlockSpec((None, d_model, tf), lambda ei, f, m: (ei, 0, f)),
        ],
        out_specs=pl.BlockSpec((None, tm, tf), lambda ei, f, m: (ei, m, f)),
        out_shape=jax.ShapeDtypeStruct((e, cap, ff), BF16),
        scratch_shapes=[pltpu.VMEM((d_model, tf), BF16), pltpu.VMEM((d_model, tf), BF16)],
        compiler_params=_cparams(("parallel", "parallel", "arbitrary")),
        name="expert_up",
    )(xe, w_gate, w_up)


def _expert_down_kernel(h_ref, w_ref, o_ref, w_s):
    @pl.when(pl.program_id(2) == 0)
    def _():
        w_s[...] = w_ref[...].astype(BF16)

    acc = jnp.dot(h_ref[...], w_s[...], preferred_element_type=F32)
    half = acc.shape[1] // 2
    o_ref[...] = _pack_bf16_pair(acc[:, :half], acc[:, half:])


def expert_down(hid, w_down, tm=512, tn=1024):
    e, cap, ff = hid.shape
    d_model = w_down.shape[2]
    tm = math.gcd(tm, cap)
    tn = min(tn, d_model)
    return pl.pallas_call(
        _expert_down_kernel,
        grid=(e, d_model // tn, cap // tm),
        in_specs=[
            pl.BlockSpec((None, tm, ff), lambda ei, n, m: (ei, m, 0)),
            pl.BlockSpec((None, ff, tn), lambda ei, n, m: (ei, 0, n)),
        ],
        out_specs=pl.BlockSpec((None, tm, tn // 2), lambda ei, n, m: (ei, m, n)),
        out_shape=jax.ShapeDtypeStruct((e, cap, d_model // 2), jnp.int32),
        scratch_shapes=[pltpu.VMEM((ff, tn), BF16)],
        compiler_params=_cparams(("parallel", "parallel", "arbitrary")),
        name="expert_down",
    )(hid, w_down)


SC_GATHER_CHUNK = 16


def sc_gather_rows(table, idx):
    n_idx = idx.shape[0]
    width = table.shape[1]
    info = plsc.get_sparse_core_info()
    n_workers = info.num_cores * info.num_subcores
    per_worker = n_idx // n_workers
    ch = SC_GATHER_CHUNK
    assert n_idx % n_workers == 0 and per_worker % (2 * ch) == 0
    mesh = plsc.VectorSubcoreMesh(core_axis_name="c", subcore_axis_name="s")

    @functools.partial(
        pl.kernel, mesh=mesh,
        out_type=jax.ShapeDtypeStruct((n_idx, width), table.dtype),
        scratch_types=[pltpu.VMEM((per_worker,), jnp.int32), pltpu.VMEM((ch, width), table.dtype),
                       pltpu.VMEM((ch, width), table.dtype), pltpu.SemaphoreType.DMA, pltpu.SemaphoreType.DMA],
        name="sc_gather_rows",
    )
    def gather(table_hbm, idx_hbm, out_hbm, idx_v, rows_a, rows_b, sem_a, sem_b):
        base = (lax.axis_index("s") * info.num_cores + lax.axis_index("c")) * per_worker
        pltpu.sync_copy(idx_hbm.at[pl.ds(base, per_worker)], idx_v)

        @pl.loop(0, per_worker // (2 * ch))
        def _(c):
            off = c * 2 * ch
            copy_a = pltpu.async_copy(table_hbm.at[idx_v.at[pl.ds(off, ch)]], rows_a, sem_a)
            copy_b = pltpu.async_copy(table_hbm.at[idx_v.at[pl.ds(off + ch, ch)]], rows_b, sem_b)
            copy_a.wait()
            pltpu.sync_copy(rows_a, out_hbm.at[pl.ds(base + off, ch)])
            copy_b.wait()
            pltpu.sync_copy(rows_b, out_hbm.at[pl.ds(base + off + ch, ch)])

    return gather(table, idx)


COMBINE_TOKENS = 256
COMBINE_ROWS = 512


def _combine_kernel(tile_ref, chunk_ref, first_ref, valid_ref, rows_ref, tok_ref, g_ref, o_ref, *, tn):
    p = pl.program_id(0)

    @pl.when(first_ref[p] == 1)
    def _():
        o_ref[...] = jnp.zeros_like(o_ref)

    @pl.when(valid_ref[p] == 1)
    def _():
        t, ch = o_ref.shape[0], tok_ref.shape[1]
        token = tile_ref[p] * t + lax.broadcasted_iota(jnp.int32, (t, ch), 0)
        weights = jnp.where(tok_ref[...] == token, g_ref[...], 0.0).astype(BF16)
        hi, lo = _unpack_bf16_pair(rows_ref[...])
        acc_hi = jnp.dot(weights, hi, preferred_element_type=F32)
        acc_lo = jnp.dot(weights, lo, preferred_element_type=F32)
        half = tn // 2
        for n in range(o_ref.shape[1] // tn):
            o_ref[:, n * tn:n * tn + half] += acc_hi[:, n * half:(n + 1) * half]
            o_ref[:, n * tn + half:(n + 1) * tn] += acc_lo[:, n * half:(n + 1) * half]


def combine_expert_outputs(rows, tok_sorted, gate_sorted, n_tok, tn):
    m, half_d = rows.shape
    t, ch = COMBINE_TOKENS, COMBINE_ROWS
    n_tiles, n_chunks = n_tok // t, m // ch
    n_pairs = n_tiles + n_chunks
    bounds = jnp.searchsorted(tok_sorted, jnp.arange(n_tiles + 1, dtype=jnp.int32) * t, side='left').astype(jnp.int32)
    start, end = bounds[:-1], bounds[1:]
    c_lo = jnp.minimum(start // ch, n_chunks - 1)
    c_hi = jnp.where(end > start, (end - 1) // ch, c_lo)
    count = c_hi - c_lo + 1
    offs = jnp.cumsum(count)
    p = jnp.arange(n_pairs, dtype=jnp.int32)
    tile = jnp.minimum(jnp.searchsorted(offs, p, side='right'), n_tiles - 1).astype(jnp.int32)
    first_p = (offs - count)[tile]
    valid = p < offs[-1]
    chunk = jnp.where(valid, c_lo[tile] + p - first_p, c_hi[-1]).astype(jnp.int32)
    first = (valid & (p == first_p)).astype(jnp.int32)
    grid_spec = pltpu.PrefetchScalarGridSpec(
        num_scalar_prefetch=4,
        grid=(n_pairs,),
        in_specs=[
            pl.BlockSpec((ch, half_d), lambda i, tl, ck, fr, vl: (ck[i], 0)),
            pl.BlockSpec((1, ch), lambda i, tl, ck, fr, vl: (0, ck[i])),
            pl.BlockSpec((1, ch), lambda i, tl, ck, fr, vl: (0, ck[i])),
        ],
        out_specs=pl.BlockSpec((t, 2 * half_d), lambda i, tl, ck, fr, vl: (tl[i], 0)),
    )
    return pl.pallas_call(
        functools.partial(_combine_kernel, tn=tn),
        grid_spec=grid_spec,
        out_shape=jax.ShapeDtypeStruct((n_tok, 2 * half_d), F32),
        compiler_params=_cparams(("arbitrary",)),
        name="combine_expert_outputs",
    )(tile, chunk, first, valid.astype(jnp.int32), rows, tok_sorted.reshape(1, m), gate_sorted.reshape(1, m).astype(F32))


def _final_kernel(x1_ref, moe_ref, p_ref, g_ref, wg_ref, wp_ref, o_ref):
    x2 = x1_ref[...] + moe_ref[...]
    h = (x2 * lax.rsqrt(jnp.mean(x2 * x2, axis=-1, keepdims=True) + EPS) * g_ref[...]).astype(BF16)
    gate = _sigmoid(jnp.dot(h, wg_ref[...], preferred_element_type=F32))
    proj = jnp.dot(p_ref[...].astype(BF16), wp_ref[...], preferred_element_type=F32)
    o_ref[...] = x2 + gate * proj


def final_embed(x1, moe, p, g_ple, w_gate, w_proj, tm=512):
    n, d_model = x1.shape
    row = lambda w: pl.BlockSpec((tm, w), lambda i: (i, 0))
    full = lambda a: pl.BlockSpec(a.shape, lambda i: (0,) * a.ndim, pipeline_mode=pl.Buffered(1))
    g2 = g_ple.reshape(1, d_model).astype(F32)
    return pl.pallas_call(
        _final_kernel,
        grid=(n // tm,),
        in_specs=[row(d_model), row(d_model), row(p.shape[1]), full(g2), full(w_gate), full(w_proj)],
        out_specs=row(d_model),
        out_shape=jax.ShapeDtypeStruct((n, d_model), F32),
        compiler_params=_cparams(("parallel",)),
        name="final_embed",
    )(x1, moe, p, g2, w_gate, w_proj)


def _prepare_weights(rel_bias, g_mix, w_in, q_gain, k_gain, ssm_a_re, ssm_a_im, ssm_log_step,
                     ssm_b_re, ssm_b_im, ssm_c_re, ssm_c_im, ssm_d, w_glu, w_br_a, w_br_b, w_out,
                     g_ffn, w_router, g_ple, w_ple_gate, w_ple_proj):
    n_heads = q_gain.shape[0]
    att_w = n_heads * HEAD_DIM
    hw = HEADS_PER_GROUP * HEAD_DIM
    ssm_w = ssm_d.shape[0]
    w_in_bf = w_in.astype(BF16)
    w_qkv, gains = [], []
    qg = q_gain.astype(F32) * (HEAD_DIM ** -0.5)
    kg = k_gain.astype(F32)
    for gi in range(len(ATT_PATTERNS)):
        cols = [w_in_bf[:, s * att_w + gi * hw: s * att_w + (gi + 1) * hw] for s in range(3)]
        w_qkv.append(jnp.concatenate(cols, axis=1))
        hs = slice(gi * HEADS_PER_GROUP, (gi + 1) * HEADS_PER_GROUP)
        gains.append(jnp.stack([qg[hs].reshape(1, hw), kg[hs].reshape(1, hw), jnp.ones((1, hw), F32)]))
    lam_bar, b_bar, c = ssm_discretise(ssm_a_re, ssm_a_im, ssm_log_step, ssm_b_re, ssm_b_im, ssm_c_re, ssm_c_im)
    wb, wx, wu, lam2 = ssm_block_weights(lam_bar, b_bar, c)
    n_exp = w_router.shape[1]
    wr = jnp.pad(w_router.astype(F32), ((0, 0), (0, 128 - n_exp)))
    wr_split = jnp.concatenate(_split_bf16(wr), axis=1)
    return dict(
        rel_bias=rel_bias, bias_tables={}, g_mix=g_mix, w_qkv=w_qkv, gains=gains,
        w_u=w_in_bf[:, 3 * att_w:3 * att_w + ssm_w], w_gates=w_in_bf[:, 3 * att_w + ssm_w:],
        lam_bar=lam_bar, wb=wb, wx=wx, wu=wu, lam2=lam2, ssm_d=ssm_d, w_glu=w_glu.astype(BF16),
        w_br_a=w_br_a.astype(BF16), w_br_b=w_br_b.astype(BF16), w_out=w_out.astype(BF16),
        g_ffn=g_ffn, wr_split=wr_split, n_exp=n_exp, g_ple=g_ple,
        w_ple_gate=w_ple_gate.astype(BF16), w_ple_proj=w_ple_proj.astype(BF16))


def _ssm_branch(hn, wts, batch, seq):
    nseg = max(1, 8 // batch)
    vbatch = batch * nseg
    vseq = seq // nseg
    u = plain_proj(hn, wts['w_u'], F32, False, "u_proj")
    nj, k = wts['lam2'].shape[1], wts['lam2'].shape[-1]
    x0 = jnp.zeros((2, nj, 2, vbatch, k), F32)
    if nseg > 1:
        _, xend = s5_scan(u, wts, x0, vbatch, emit_y=False)
        x0 = s5_segment_carries(xend, wts['lam_bar'], vseq, batch, nseg)
    y, _ = s5_scan(u, wts, x0, vbatch, emit_y=True)
    return ssm_glu(y, u, wts['ssm_d'], wts['w_glu'])


def _expert_choice(h_list, logits_list, n_exp, w_exp_gate, w_exp_up, w_exp_down):
    d_model = w_exp_down.shape[2]
    half_d = h_list[0].shape[1]
    caps = [CAPACITY_FACTOR * h.shape[0] // n_exp for h in h_list]
    cap_all = sum(caps)
    routes, picks, row0 = [], [], 0
    for h, logits, cap in zip(h_list, logits_list, caps):
        affinity = jax.nn.softmax(logits[:, :n_exp], axis=-1)
        gate, idx = lax.top_k(affinity.T, cap)
        routes.append((gate.reshape(-1), idx.reshape(-1)))
        picks.append(idx + row0)
        row0 += h.shape[0]
    xe = sc_gather_rows(jnp.concatenate(h_list, axis=0), jnp.concatenate(picks, axis=1).reshape(-1))
    hid = expert_up(xe.reshape(n_exp, cap_all, half_d), w_exp_gate, w_exp_up)
    tn = min(1024, d_model)
    ye = expert_down(hid, w_exp_down, tn=tn).reshape(n_exp * cap_all, half_d)
    outs, slot0 = [], 0
    for h, (gate_flat, idx_flat), cap in zip(h_list, routes, caps):
        order = jnp.argsort(idx_flat).astype(jnp.int32)
        rows = (order // cap) * cap_all + slot0 + order % cap
        outs.append(combine_expert_outputs(sc_gather_rows(ye, rows), idx_flat[order], gate_flat[order],
                                           h.shape[0], tn))
        slot0 += cap
    return outs


def _mixer(x, wts):
    batch, seq, d_model = x.shape
    n = batch * seq
    xf = x.reshape(n, d_model)
    dils = [dil for _, dil in ATT_PATTERNS]
    hn_list = rms_bf16(x, wts['g_mix'], dils)
    hn = hn_list[dils.index(1)]
    o_list, lse_list = [], []
    for gi, dil in enumerate(dils):
        qkv = qkv_proj(hn_list[gi], wts['w_qkv'][gi], wts['gains'][gi], f"qkv_proj_d{dil}")
        hs = slice(gi * HEADS_PER_GROUP, (gi + 1) * HEADS_PER_GROUP)
        bias_key = (gi,) + _attn_geometry(seq // dil)
        if bias_key not in wts['bias_tables']:
            wts['bias_tables'][bias_key] = attn_bias_tables(wts['rel_bias'][:, hs], dil, seq // dil)
        bias = wts['bias_tables'][bias_key]
        o, lse = dilated_attention(qkv, bias, batch, seq, dil)
        o_list.append(o)
        lse_list.append(lse)
    o_b = _ssm_branch(hn, wts, batch, seq)
    gates = plain_proj(hn, wts['w_gates'], BF16, True, "gates_proj")
    return mixer_out(o_list, lse_list, o_b, gates, xf, wts['w_br_a'], wts['w_br_b'],
                     wts['w_out'], wts['g_ffn'], wts['wr_split'])


def kernel(x_prompt, x_sample, p_prompt, p_sample, rel_bias, g_mix, w_in, q_gain, k_gain, ssm_a_re, ssm_a_im, ssm_log_step, ssm_b_re, ssm_b_im, ssm_c_re, ssm_c_im, ssm_d, w_glu, w_br_a, w_br_b, w_out, g_ffn, w_router, w_exp_gate, w_exp_up, w_exp_down, g_ple, w_ple_gate, w_ple_proj):
    assert g_mix.shape[0] == 1, "single-layer trunk"
    wts = _prepare_weights(rel_bias, g_mix[0], w_in[0], q_gain[0], k_gain[0], ssm_a_re[0], ssm_a_im[0],
                           ssm_log_step[0], ssm_b_re[0], ssm_b_im[0], ssm_c_re[0], ssm_c_im[0], ssm_d[0],
                           w_glu[0], w_br_a[0], w_br_b[0], w_out[0], g_ffn[0], w_router[0], g_ple[0],
                           w_ple_gate[0], w_ple_proj[0])
    xs, ps = (x_prompt, x_sample), (p_prompt[0], p_sample[0])
    mixed = [_mixer(x, wts) for x in xs]
    moe = _expert_choice([m[1] for m in mixed], [m[2] for m in mixed], wts['n_exp'],
                         w_exp_gate[0], w_exp_up[0], w_exp_down[0])
    outs = []
    for x, p, (x1, _, _), y in zip(xs, ps, mixed, moe):
        out = final_embed(x1, y, p.reshape(-1, p.shape[-1]), wts['g_ple'], wts['w_ple_gate'], wts['w_ple_proj'])
        outs.append(out.reshape(x.shape))
    return tuple(outs)
```

```python
import functools
import math

import jax
import jax.numpy as jnp
import numpy as np
from jax import lax
from jax.experimental import pallas as pl
from jax.experimental.pallas import tpu as pltpu
from jax.experimental.pallas import tpu_sc as plsc

F32 = jnp.float32
BF16 = jnp.bfloat16

LANES = 128
HEAD_DIM = 128
HEADS_PER_GROUP = 4
ATT_PATTERNS = ((128, 1), (512, 4), (2048, 16))
ATT_SIDE = 64
N_BUCKETS = 32
MAX_DISTANCE = 1024
CAPACITY_FACTOR = 2
EPS = 1e-6
NEG = -1e30
SSM_LANE_GROUPS = 8
V7X_VMEM_LIMIT = 56 * 1024 * 1024

assert all(w // (2 * d) == ATT_SIDE for w, d in ATT_PATTERNS)


def _cparams(semantics, vmem=V7X_VMEM_LIMIT):
    return pltpu.CompilerParams(dimension_semantics=("arbitrary",) * len(semantics), vmem_limit_bytes=vmem)


def _sigmoid(x):
    return 1.0 / (1.0 + jnp.exp(-x))


def _rms_kernel(x_ref, g_ref, *refs, dils):
    o_refs, y_ref = refs[:-1], refs[-1]
    x = x_ref[...]
    y = x * lax.rsqrt(jnp.mean(x * x, axis=-1, keepdims=True) + EPS) * g_ref[...]
    tm = x.shape[0]
    n_chunks = y_ref.shape[0]
    for c in range(n_chunks):
        y_ref[c] = y[:, c * LANES:(c + 1) * LANES]
    for o_ref, dil in zip(o_refs, dils):
        if dil == 1:
            o_ref[...] = y.astype(o_ref.dtype)
        else:
            for r in range(dil):
                rows = [y_ref[c, pl.ds(r, tm // dil, stride=dil), :] for c in range(n_chunks)]
                o_ref[r] = jnp.concatenate(rows, axis=1).astype(o_ref.dtype)


def rms_bf16(x, g, dils, tm=1024):
    batch, seq, d = x.shape
    out_specs, out_shape = [], []
    for dil in dils:
        if dil == 1:
            out_specs.append(pl.BlockSpec((None, tm, d), lambda b, i: (b, i, 0)))
            out_shape.append(jax.ShapeDtypeStruct((batch, seq, d), BF16))
        else:
            out_specs.append(pl.BlockSpec((None, dil, tm // dil, d), lambda b, i: (b, 0, i, 0)))
            out_shape.append(jax.ShapeDtypeStruct((batch, dil, seq // dil, d), BF16))
    outs = pl.pallas_call(
        functools.partial(_rms_kernel, dils=tuple(dils)),
        grid=(batch, seq // tm),
        in_specs=[pl.BlockSpec((None, tm, d), lambda b, i: (b, i, 0)), pl.BlockSpec((1, d), lambda b, i: (0, 0))],
        out_specs=out_specs,
        out_shape=out_shape,
        scratch_shapes=[pltpu.VMEM((d // LANES, tm, LANES), F32)],
        compiler_params=_cparams(("parallel", "parallel")),
        name="rms_bf16",
    )(x, g.reshape(1, d).astype(F32))
    return [o.reshape(batch * seq, d) for o in outs]


def _head_rms(acc, gain):
    outs = []
    for h in range(acc.shape[1] // HEAD_DIM):
        t = acc[:, h * HEAD_DIM:(h + 1) * HEAD_DIM]
        outs.append(t * lax.rsqrt(jnp.mean(t * t, axis=-1, keepdims=True) + EPS))
    return jnp.concatenate(outs, axis=1) * gain


def _qkv_kernel(x_ref, w_ref, g_ref, o_ref):
    x = x_ref[...]
    tn = o_ref.shape[1] // 3
    for j in range(3):
        acc = jnp.dot(x, w_ref[:, j * tn:(j + 1) * tn], preferred_element_type=F32)
        if j < 2:
            acc = _head_rms(acc, g_ref[j])
        o_ref[:, j * tn:(j + 1) * tn] = acc.astype(o_ref.dtype)


def qkv_proj(hn, w, gains, name, tm=1024):
    n, d_model = hn.shape
    tn = HEADS_PER_GROUP * HEAD_DIM
    return pl.pallas_call(
        _qkv_kernel,
        grid=(n // tm,),
        in_specs=[
            pl.BlockSpec((tm, d_model), lambda i: (i, 0)),
            pl.BlockSpec((d_model, 3 * tn), lambda i: (0, 0), pipeline_mode=pl.Buffered(1)),
            pl.BlockSpec((3, 1, tn), lambda i: (0, 0, 0), pipeline_mode=pl.Buffered(1)),
        ],
        out_specs=pl.BlockSpec((tm, 3 * tn), lambda i: (i, 0)),
        out_shape=jax.ShapeDtypeStruct((n, 3 * tn), BF16),
        compiler_params=_cparams(("parallel",)),
        name=name,
    )(hn, w, gains)


def _proj_kernel(x_ref, w_ref, o_ref, *, act):
    acc = jnp.dot(x_ref[...], w_ref[...], preferred_element_type=F32)
    if act:
        acc = _sigmoid(acc)
    o_ref[...] = acc.astype(o_ref.dtype)


def plain_proj(hn, w, out_dtype, act, name, tm=1024, tn=1024):
    n, d_model = hn.shape
    width = w.shape[1]
    tn = min(tn, width)
    return pl.pallas_call(
        functools.partial(_proj_kernel, act=act),
        grid=(width // tn, n // tm),
        in_specs=[
            pl.BlockSpec((tm, d_model), lambda j, i: (i, 0)),
            pl.BlockSpec((d_model, tn), lambda j, i: (0, j)),
        ],
        out_specs=pl.BlockSpec((tm, tn), lambda j, i: (i, j)),
        out_shape=jax.ShapeDtypeStruct((n, width), out_dtype),
        compiler_params=_cparams(("arbitrary", "parallel")),
        name=name,
    )(hn, w)


def _t5_buckets(rel):
    half = N_BUCKETS // 2
    max_exact = half // 2
    n = np.abs(rel)
    large = max_exact + (np.log(np.maximum(n, 1) / max_exact) / math.log(MAX_DISTANCE / max_exact)
                         * (half - max_exact)).astype(np.int32)
    large = np.minimum(large, half - 1)
    return ((rel > 0).astype(np.int32) * half + np.where(n < max_exact, n, large)).astype(np.int32)


def _attn_geometry(l):
    tqb = min(2 * ATT_SIDE, l)
    win = min(4 * ATT_SIDE, l)
    return tqb, win


def attn_bias_tables(rel_bias_g, dil, l):
    tqb, win = _attn_geometry(l)
    i = np.arange(tqb)[:, None]
    j = np.arange(win)[None, :]
    rel = np.stack([j - i - ATT_SIDE * shift for shift in range(3)])
    valid = np.abs(rel) <= ATT_SIDE
    onehot = (_t5_buckets(dil * rel)[..., None] == np.arange(N_BUCKETS)).astype(np.int8)
    b = jnp.einsum('sijb,bh->shij', jnp.asarray(onehot).astype(F32), rel_bias_g.astype(F32),
                   precision=lax.Precision.HIGHEST)
    return jnp.where(valid[:, None], b, NEG)


def _attn_kernel(q_ref, k_ref, v_ref, bias_ref, o_ref, lse_ref, *, l, dil, tql):
    tqb, win = _attn_geometry(l)
    qi = pl.program_id(2)
    for r in range(dil):
        for c in range(tql // tqb):
            q0 = qi * tql + c * tqb
            start = pl.multiple_of(jnp.clip(q0 - ATT_SIDE, 0, l - win), ATT_SIDE)
            shift = (q0 - start) // ATT_SIDE
            q = q_ref[r, c * tqb:(c + 1) * tqb, :]
            kw = k_ref[r, pl.ds(start, win), :]
            vw = v_ref[r, pl.ds(start, win), :]
            logits = lax.dot_general(q, kw, (((1,), (1,)), ((), ())), preferred_element_type=F32)
            logits = logits + bias_ref[shift]
            m = jnp.max(logits, axis=-1, keepdims=True)
            p = jnp.exp(logits - m)
            s = jnp.sum(p, axis=-1, keepdims=True)
            o = jnp.dot(p.astype(vw.dtype), vw, preferred_element_type=F32)
            if dil == 1:
                rows = pl.ds(c * tqb, tqb)
            else:
                rows = pl.ds(c * tqb * dil + r, tqb, stride=dil)
            o_ref[rows, :] = o / s
            lse_ref[rows, :] = jnp.broadcast_to(m + jnp.log(s), (tqb, HEAD_DIM))


def dilated_attention(qkv, bias, batch, seq, dil):
    l = seq // dil
    tqb, win = _attn_geometry(l)
    tql = min(l, max(tqb, 8 * tqb // dil))
    tok = tql * dil
    hw = HEADS_PER_GROUP * HEAD_DIM
    x = qkv.reshape(batch, dil, l, 3 * hw)
    out_sds = jax.ShapeDtypeStruct((batch, seq, hw), F32)
    out_spec = pl.BlockSpec((None, tok, HEAD_DIM), lambda b, h, i: (b, i, h))
    o, lse = pl.pallas_call(
        functools.partial(_attn_kernel, l=l, dil=dil, tql=tql),
        grid=(batch, HEADS_PER_GROUP, l // tql),
        in_specs=[
            pl.BlockSpec((None, dil, tql, HEAD_DIM), lambda b, h, i: (b, 0, i, h)),
            pl.BlockSpec((None, dil, l, HEAD_DIM), lambda b, h, i: (b, 0, 0, HEADS_PER_GROUP + h)),
            pl.BlockSpec((None, dil, l, HEAD_DIM), lambda b, h, i: (b, 0, 0, 2 * HEADS_PER_GROUP + h)),
            pl.BlockSpec((3, None, tqb, win), lambda b, h, i: (0, h, 0, 0)),
        ],
        out_specs=[out_spec, out_spec],
        out_shape=[out_sds, out_sds],
        compiler_params=_cparams(("parallel", "parallel", "parallel")),
        name=f"dilated_attn_d{dil}",
    )(x, x, x, bias)
    return o.reshape(batch * seq, hw), lse.reshape(batch * seq, hw)


def ssm_discretise(a_re, a_im, log_step, b_re, b_im, c_re, c_im):
    lam = lax.complex(a_re.astype(F32), a_im.astype(F32))
    step = jnp.exp(log_step.astype(F32))[..., None]
    lam_bar = jnp.exp(lam * step)
    b_bar = ((lam_bar - 1.0) / lam)[..., None] * lax.complex(b_re.astype(F32), b_im.astype(F32))
    c = lax.complex(c_re.astype(F32), c_im.astype(F32))
    return lam_bar, b_bar, c


def _block_diag(m):
    _, nj, lg, a, b = m.shape
    return jnp.einsum('djgab,gk->djgakb', m, jnp.eye(lg, dtype=F32)).reshape(2, nj, lg * a, lg * b)


def ssm_block_weights(lam_bar, b_bar, c):
    _, g, p, h = b_bar.shape
    lg = SSM_LANE_GROUPS
    nj = g // lg
    lam = lam_bar[..., None]
    ct = jnp.swapaxes(c, -1, -2)

    def grp(m):
        return m.reshape((2, nj, lg) + m.shape[2:])

    def state_in(bmat):
        t = grp(jnp.swapaxes(bmat, -1, -2))
        return jnp.concatenate([_block_diag(t.real), _block_diag(t.imag)], axis=-1)

    def state_out(cmat):
        t = grp(cmat)
        return jnp.concatenate([_block_diag(t.real), _block_diag(-t.imag)], axis=-2)

    def direct(k):
        m = jnp.einsum('dgpc,dgph->dghc', ct * lam ** k, b_bar).real
        return _block_diag(grp(m))

    first_in, second_in = state_in(lam * b_bar), state_in(b_bar)
    first_out, second_out = state_out(ct * lam), state_out(ct * lam ** 2)
    zero = jnp.zeros_like(direct(0))
    sel = jnp.array([0, 1]).reshape(2, 1, 1, 1)
    wb = jnp.where(sel == 0, jnp.concatenate([first_in, second_in], axis=-2),
                   jnp.concatenate([second_in, first_in], axis=-2))
    wx = jnp.where(sel == 0, jnp.concatenate([first_out, second_out], axis=-1),
                   jnp.concatenate([second_out, first_out], axis=-1))
    u_first = jnp.concatenate([direct(0), direct(1)], axis=-1)
    u_second = jnp.concatenate([zero, direct(0)], axis=-1)
    swap = lambda m: jnp.concatenate([m[..., 128:], m[..., :128]], axis=-1)
    wu = jnp.where(sel == 0, jnp.concatenate([u_first, u_second], axis=-2),
                   jnp.concatenate([swap(u_second), swap(u_first)], axis=-2))
    l2 = (lam_bar ** 2).reshape(2, nj, 1, 1, lg * p)
    lam2 = jnp.broadcast_to(jnp.concatenate([l2.real, l2.imag], axis=2), (2, nj, 2, 8, lg * p))
    return wb.astype(BF16), wx.astype(BF16), wu.astype(BF16), lam2.astype(F32)


def _s5_kernel(u_ref, wb_ref, wx_ref, wu_ref, lam_ref, x0_ref, *refs, tt, emit_y):
    if emit_y:
        y_ref, xend_ref, lhs_e, lhs_o, bu_ref, st_ref, xs_ref, ysc_e, ysc_o = refs
    else:
        xend_ref, lhs_e, lhs_o, bu_ref, st_ref = refs
    direction = pl.program_id(0)
    ti = pl.program_id(2)
    nt = pl.num_programs(2)
    vb, _, lanes = u_ref.shape
    k = lam_ref.shape[-1]
    n = tt // 2

    @pl.when(ti == 0)
    def _():
        st_ref[...] = x0_ref[...]

    for b in range(vb):
        lhs_e[pl.ds(b, n, stride=vb), :] = u_ref[b, pl.ds(0, n, stride=2), :]
        lhs_o[pl.ds(b, n, stride=vb), :] = u_ref[b, pl.ds(1, n, stride=2), :]
    lhs = jnp.concatenate([lhs_e[...], lhs_o[...]], axis=1).astype(BF16)
    bu_ref[...] = jnp.dot(lhs, wb_ref[...], preferred_element_type=F32)
    lr = lam_ref[0]
    li = lam_ref[1]
    x_in = (st_ref[0], st_ref[1])
    if emit_y:
        in_row = pl.multiple_of(jnp.where(direction == 0, 0, n * vb), vb)
        xs_ref[pl.ds(in_row, vb), :] = jnp.concatenate(x_in, axis=1)

    def step(s, carry):
        xr, xi = carry
        t = jnp.where(direction == 0, s, n - 1 - s)
        b = bu_ref[pl.ds(pl.multiple_of(t * vb, vb), vb), :]
        nr = lr * xr - li * xi + b[:, :k]
        ni = lr * xi + li * xr + b[:, k:]
        if emit_y:
            out_row = pl.multiple_of(jnp.where(direction == 0, t + 1, t) * vb, vb)
            xs_ref[pl.ds(out_row, vb), :] = jnp.concatenate([nr, ni], axis=1)
        return nr, ni

    xr, xi = lax.fori_loop(0, n, step, x_in, unroll=8)
    st_ref[0] = xr
    st_ref[1] = xi
    if emit_y:
        base = pl.multiple_of(jnp.where(direction == 0, 0, vb), vb)
        xs = xs_ref[pl.ds(base, n * vb), :].astype(BF16)
        y = (jnp.dot(xs, wx_ref[...], preferred_element_type=F32)
             + jnp.dot(lhs, wu_ref[...], preferred_element_type=F32))
        ysc_e[...] = y[:, :lanes]
        ysc_o[...] = y[:, lanes:]
        for b in range(vb):
            y_ref[b, pl.ds(0, n, stride=2), :] = ysc_e[pl.ds(b, n, stride=vb), :]
            y_ref[b, pl.ds(1, n, stride=2), :] = ysc_o[pl.ds(b, n, stride=vb), :]

    @pl.when(ti == nt - 1)
    def _():
        xend_ref[0] = xr
        xend_ref[1] = xi


def s5_scan(u, wts, x0, vbatch, emit_y, tt=256):
    width = u.shape[1]
    vseq = u.shape[0] // vbatch
    wb, wx, wu, lam2 = wts['wb'], wts['wx'], wts['wu'], wts['lam2']
    nj = wb.shape[1]
    k = lam2.shape[-1]
    lanes = width // nj
    nt = vseq // tt
    n = tt // 2

    def tmap(d, t):
        return jnp.where(d == 0, t, nt - 1 - t)

    state_spec = pl.BlockSpec((None, None, 2, vbatch, k), lambda d, j, t: (d, j, 0, 0, 0))
    state_sds = jax.ShapeDtypeStruct((2, nj, 2, vbatch, k), F32)
    out_specs, out_shape = [state_spec], [state_sds]
    half = pltpu.VMEM((n * vbatch, lanes), F32)
    scratch = [half, half, pltpu.VMEM((n * vbatch, 2 * k), F32), pltpu.VMEM((2, vbatch, k), F32)]
    if emit_y:
        out_specs.insert(0, pl.BlockSpec((None, vbatch, tt, lanes), lambda d, j, t: (d, 0, tmap(d, t), j)))
        out_shape.insert(0, jax.ShapeDtypeStruct((2, vbatch, vseq, width), F32))
        scratch += [pltpu.VMEM(((n + 1) * vbatch, 2 * k), F32), half, half]
    outs = pl.pallas_call(
        functools.partial(_s5_kernel, tt=tt, emit_y=emit_y),
        grid=(2, nj, nt),
        in_specs=[
            pl.BlockSpec((vbatch, tt, lanes), lambda d, j, t: (0, tmap(d, t), j)),
            pl.BlockSpec((None, None, 2 * lanes, 2 * k), lambda d, j, t: (d, j, 0, 0)),
            pl.BlockSpec((None, None, 2 * k, 2 * lanes), lambda d, j, t: (d, j, 0, 0)),
            pl.BlockSpec((None, None, 2 * lanes, 2 * lanes), lambda d, j, t: (d, j, 0, 0)),
            state_spec, state_spec,
        ],
        out_specs=out_specs,
        out_shape=out_shape,
        scratch_shapes=scratch,
        compiler_params=_cparams(("parallel", "parallel", "arbitrary")),
        name="s5_scan" if emit_y else "s5_states",
    )(u.reshape(vbatch, vseq, width), wb, wx, wu, lam2, x0)
    if emit_y:
        return outs[0].reshape(2, vbatch * vseq, width), outs[1]
    return None, outs[0]


def s5_segment_carries(xend, lam_bar, seg_len, batch, nseg):
    _, nj, _, _, k = xend.shape
    lam_pow = (lam_bar ** seg_len).reshape(2, nj, 1, k)
    e = lax.complex(xend[:, :, 0], xend[:, :, 1]).reshape(2, nj, batch, nseg, k)
    fwd = [jnp.zeros((nj, batch, k), jnp.complex64)]
    for s in range(1, nseg):
        fwd.append(lam_pow[0] * fwd[-1] + e[0, :, :, s - 1])
    bwd = [jnp.zeros((nj, batch, k), jnp.complex64)]
    for s in range(nseg - 2, -1, -1):
        bwd.append(lam_pow[1] * bwd[-1] + e[1, :, :, s + 1])
    bwd = bwd[::-1]
    x_in = jnp.stack([jnp.stack(fwd, axis=2), jnp.stack(bwd, axis=2)])
    x_in = x_in.reshape(2, nj, batch * nseg, k)
    return jnp.stack([x_in.real, x_in.imag], axis=2).astype(F32)


def _gelu_tanh(x):
    return 0.5 * x * (1.0 + jnp.tanh(math.sqrt(2.0 / math.pi) * (x + 0.044715 * x * x * x)))


def _glu_kernel(yf_ref, yb_ref, u_ref, d_ref, wa_ref, wb_ref, o_ref):
    y = _gelu_tanh(yf_ref[...] + yb_ref[...] + d_ref[...] * u_ref[...]).astype(BF16)
    za = jnp.dot(y, wa_ref[...], preferred_element_type=F32)
    zb = jnp.dot(y, wb_ref[...], preferred_element_type=F32)
    o_ref[...] = (za * _sigmoid(zb)).astype(o_ref.dtype)


def ssm_glu(y, u, d_skip, w_glu, tm=512):
    n, width = u.shape
    wa = w_glu[:, :width]
    wb = w_glu[:, width:]
    const = lambda i: (0, 0)
    return pl.pallas_call(
        _glu_kernel,
        grid=(n // tm,),
        in_specs=[
            pl.BlockSpec((None, tm, width), lambda i: (0, i, 0)),
            pl.BlockSpec((None, tm, width), lambda i: (1, i, 0)),
            pl.BlockSpec((tm, width), lambda i: (i, 0)),
            pl.BlockSpec((1, width), const),
            pl.BlockSpec((width, width), const),
            pl.BlockSpec((width, width), const),
        ],
        out_specs=pl.BlockSpec((tm, width), lambda i: (i, 0)),
        out_shape=jax.ShapeDtypeStruct((n, width), BF16),
        compiler_params=_cparams(("parallel",)),
        name="ssm_glu",
    )(y, y, u, d_skip.reshape(1, width).astype(F32), wa, wb)


def _split_bf16(x):
    hi = x.astype(BF16)
    lo = (x - hi.astype(F32)).astype(BF16)
    return hi, lo


def _pack_bf16_pair(a, b):
    wa = pltpu.bitcast(a.astype(BF16).astype(F32), jnp.int32)
    wb = pltpu.bitcast(b.astype(BF16).astype(F32), jnp.int32)
    return wa | lax.shift_right_logical(wb, 16)


def _unpack_bf16_pair(w):
    hi = pltpu.bitcast(w & jnp.int32(-65536), F32).astype(BF16)
    lo = pltpu.bitcast(lax.shift_left(w, 16), F32).astype(BF16)
    return hi, lo


def _mixer_out_kernel(o0, o1, o2, l0, l1, l2, ob_ref, gate_ref, x_ref, wa_ref, wb_ref, wo_ref,
                      g_ref, wr_ref, hbuf_ref, x1_ref, h_ref, lg_ref):
    del hbuf_ref
    d_model = x_ref.shape[1]
    la, lb, lc = l0[...], l1[...], l2[...]
    m = jnp.maximum(jnp.maximum(la, lb), lc)
    ea, eb, ec = jnp.exp(la - m), jnp.exp(lb - m), jnp.exp(lc - m)
    oa = (ea * o0[...] + eb * o1[...] + ec * o2[...]) / (ea + eb + ec)
    pa = jnp.dot(oa.astype(BF16), wa_ref[...], preferred_element_type=F32)
    pb = jnp.dot(ob_ref[...], wb_ref[...], preferred_element_type=F32)
    gates = gate_ref[...].astype(F32)
    merged = gates[:, :d_model] * pa + gates[:, d_model:] * pb
    x1 = x_ref[...] + jnp.dot(merged.astype(BF16), wo_ref[...], preferred_element_type=F32)
    x1_ref[...] = x1
    h = x1 * lax.rsqrt(jnp.mean(x1 * x1, axis=-1, keepdims=True) + EPS) * g_ref[...]
    h_ref[...] = _pack_bf16_pair(h[:, :d_model // 2], h[:, d_model // 2:])
    hh, hl = _split_bf16(h)
    er = lg_ref.shape[1]
    both = jnp.dot(hh, wr_ref[...], preferred_element_type=F32)
    lg_ref[...] = both[:, :er] + both[:, er:] + jnp.dot(hl, wr_ref[:, :er], preferred_element_type=F32)


def mixer_out(o_list, lse_list, o_b, gates, x, w_br_a, w_br_b, w_out, g_ffn, wr_split, h_buf, h_row0, tm=256):
    n, d_model = x.shape
    row = lambda w: pl.BlockSpec((tm, w), lambda i: (i, 0))
    full = lambda a: pl.BlockSpec(a.shape, lambda i: (0,) * a.ndim, pipeline_mode=pl.Buffered(1))
    g2 = g_ffn.reshape(1, d_model).astype(F32)
    er = wr_split.shape[1] // 2
    blk0 = h_row0 // tm
    in_specs = [row(o.shape[1]) for o in o_list] + [row(l.shape[1]) for l in lse_list] + [
        row(o_b.shape[1]), row(gates.shape[1]), row(d_model),
        full(w_br_a), full(w_br_b), full(w_out), full(g2), full(wr_split), pl.BlockSpec(memory_space=pl.ANY)]
    args = [*o_list, *lse_list, o_b, gates, x, w_br_a, w_br_b, w_out, g2, wr_split, h_buf]
    x1, h, logits = pl.pallas_call(
        _mixer_out_kernel,
        grid=(n // tm,),
        in_specs=in_specs,
        out_specs=[row(d_model), pl.BlockSpec((tm, d_model // 2), lambda i: (blk0 + i, 0)), row(er)],
        out_shape=[jax.ShapeDtypeStruct((n, d_model), F32), jax.ShapeDtypeStruct(h_buf.shape, h_buf.dtype),
                   jax.ShapeDtypeStruct((n, er), F32)],
        input_output_aliases={len(args) - 1: 1},
        compiler_params=_cparams(("parallel",)),
        name="mixer_out",
    )(*args)
    return x1, h, logits


def _expert_up_kernel(x_ref, wg_ref, wu_ref, o_ref, wg_s, wu_s):
    @pl.when(pl.program_id(2) == 0)
    def _():
        wg_s[...] = wg_ref[...].astype(BF16)
        wu_s[...] = wu_ref[...].astype(BF16)

    x_hi, x_lo = _unpack_bf16_pair(x_ref[...])
    half = x_hi.shape[1]

    def proj(w_s):
        return (jnp.dot(x_hi, w_s[:half, :], preferred_element_type=F32)
                + jnp.dot(x_lo, w_s[half:, :], preferred_element_type=F32))

    a = proj(wg_s)
    b = proj(wu_s)
    o_ref[...] = (a * _sigmoid(a) * b).astype(o_ref.dtype)


def _expert_up_into_kernel(x_ref, wg_ref, wu_ref, buf_ref, o_ref, wg_s, wu_s):
    del buf_ref
    _expert_up_kernel(x_ref, wg_ref, wu_ref, o_ref, wg_s, wu_s)


def expert_up(xe, w_gate, w_up, e0, hid=None, tm=1024, tf=512):
    n_e, cap, _ = xe.shape
    n_exp, d_model, ff = w_gate.shape
    tm = math.gcd(tm, cap)
    w_spec = pl.BlockSpec((None, d_model, tf), lambda ei, f, m: (e0 + ei, 0, f))
    in_specs = [pl.BlockSpec((None, tm, d_model // 2), lambda ei, f, m: (ei, m, 0)), w_spec, w_spec]
    args = [xe, w_gate, w_up]
    if hid is not None:
        in_specs.append(pl.BlockSpec(memory_space=pl.ANY))
        args.append(hid)
    return pl.pallas_call(
        _expert_up_kernel if hid is None else _expert_up_into_kernel,
        grid=(n_e, ff // tf, cap // tm),
        in_specs=in_specs,
        out_specs=pl.BlockSpec((None, tm, tf), lambda ei, f, m: (e0 + ei, m, f)),
        out_shape=jax.ShapeDtypeStruct((n_exp, cap, ff), BF16),
        scratch_shapes=[pltpu.VMEM((d_model, tf), BF16), pltpu.VMEM((d_model, tf), BF16)],
        input_output_aliases={} if hid is None else {3: 0},
        compiler_params=_cparams(("parallel", "parallel", "arbitrary")),
        name="expert_up",
    )(*args)


def _expert_down_kernel(h_ref, w_ref, o_ref, w_s):
    @pl.when(pl.program_id(2) == 0)
    def _():
        w_s[...] = w_ref[...].astype(BF16)

    acc = jnp.dot(h_ref[...], w_s[...], preferred_element_type=F32)
    half = acc.shape[1] // 2
    o_ref[...] = _pack_bf16_pair(acc[:, :half], acc[:, half:])


def expert_down(hid, w_down, tm=512, tn=1024):
    e, cap, ff = hid.shape
    d_model = w_down.shape[2]
    tm = math.gcd(tm, cap)
    tn = min(tn, d_model)
    return pl.pallas_call(
        _expert_down_kernel,
        grid=(e, d_model // tn, cap // tm),
        in_specs=[
            pl.BlockSpec((None, tm, ff), lambda ei, n, m: (ei, m, 0)),
            pl.BlockSpec((None, ff, tn), lambda ei, n, m: (ei, 0, n)),
        ],
        out_specs=pl.BlockSpec((None, tm, tn // 2), lambda ei, n, m: (ei, m, n)),
        out_shape=jax.ShapeDtypeStruct((e, cap, d_model // 2), jnp.int32),
        scratch_shapes=[pltpu.VMEM((ff, tn), BF16)],
        compiler_params=_cparams(("parallel", "parallel", "arbitrary")),
        name="expert_down",
    )(hid, w_down)


SC_GATHER_CHUNK = 16


def sc_gather_rows(table, idx):
    n_idx = idx.shape[0]
    width = table.shape[1]
    info = plsc.get_sparse_core_info()
    n_workers = info.num_cores * info.num_subcores
    per_worker = n_idx // n_workers
    ch = SC_GATHER_CHUNK
    assert n_idx % n_workers == 0 and per_worker % (2 * ch) == 0
    mesh = plsc.VectorSubcoreMesh(core_axis_name="c", subcore_axis_name="s")

    @functools.partial(
        pl.kernel, mesh=mesh,
        out_type=jax.ShapeDtypeStruct((n_idx, width), table.dtype),
        scratch_types=[pltpu.VMEM((per_worker,), jnp.int32), pltpu.VMEM((ch, width), table.dtype),
                       pltpu.VMEM((ch, width), table.dtype), pltpu.SemaphoreType.DMA, pltpu.SemaphoreType.DMA],
        name="sc_gather_rows",
    )
    def gather(table_hbm, idx_hbm, out_hbm, idx_v, rows_a, rows_b, sem_a, sem_b):
        base = (lax.axis_index("s") * info.num_cores + lax.axis_index("c")) * per_worker
        pltpu.sync_copy(idx_hbm.at[pl.ds(base, per_worker)], idx_v)

        @pl.loop(0, per_worker // (2 * ch))
        def _(c):
            off = c * 2 * ch
            copy_a = pltpu.async_copy(table_hbm.at[idx_v.at[pl.ds(off, ch)]], rows_a, sem_a)
            copy_b = pltpu.async_copy(table_hbm.at[idx_v.at[pl.ds(off + ch, ch)]], rows_b, sem_b)
            copy_a.wait()
            pltpu.sync_copy(rows_a, out_hbm.at[pl.ds(base + off, ch)])
            copy_b.wait()
            pltpu.sync_copy(rows_b, out_hbm.at[pl.ds(base + off + ch, ch)])

    return gather(table, idx)


COMBINE_TOKENS = 256
COMBINE_ROWS = 512


def _combine_kernel(tile_ref, chunk_ref, first_ref, valid_ref, rows_ref, tok_ref, g_ref, o_ref, *, tn):
    p = pl.program_id(0)

    @pl.when(first_ref[p] == 1)
    def _():
        o_ref[...] = jnp.zeros_like(o_ref)

    @pl.when(valid_ref[p] == 1)
    def _():
        t, ch = o_ref.shape[0], tok_ref.shape[1]
        token = tile_ref[p] * t + lax.broadcasted_iota(jnp.int32, (t, ch), 0)
        weights = jnp.where(tok_ref[...] == token, g_ref[...], 0.0).astype(BF16)
        hi, lo = _unpack_bf16_pair(rows_ref[...])
        acc_hi = jnp.dot(weights, hi, preferred_element_type=F32)
        acc_lo = jnp.dot(weights, lo, preferred_element_type=F32)
        half = tn // 2
        for n in range(o_ref.shape[1] // tn):
            o_ref[:, n * tn:n * tn + half] += acc_hi[:, n * half:(n + 1) * half]
            o_ref[:, n * tn + half:(n + 1) * tn] += acc_lo[:, n * half:(n + 1) * half]


def combine_expert_outputs(rows, tok_sorted, gate_sorted, n_tok, tn):
    m, half_d = rows.shape
    t, ch = COMBINE_TOKENS, COMBINE_ROWS
    n_tiles, n_chunks = n_tok // t, m // ch
    n_pairs = n_tiles + n_chunks
    bounds = jnp.searchsorted(tok_sorted, jnp.arange(n_tiles + 1, dtype=jnp.int32) * t, side='left').astype(jnp.int32)
    start, end = bounds[:-1], bounds[1:]
    c_lo = jnp.minimum(start // ch, n_chunks - 1)
    c_hi = jnp.where(end > start, (end - 1) // ch, c_lo)
    count = c_hi - c_lo + 1
    offs = jnp.cumsum(count)
    p = jnp.arange(n_pairs, dtype=jnp.int32)
    tile = jnp.minimum(jnp.searchsorted(offs, p, side='right'), n_tiles - 1).astype(jnp.int32)
    first_p = (offs - count)[tile]
    valid = p < offs[-1]
    chunk = jnp.where(valid, c_lo[tile] + p - first_p, c_hi[-1]).astype(jnp.int32)
    first = (valid & (p == first_p)).astype(jnp.int32)
    grid_spec = pltpu.PrefetchScalarGridSpec(
        num_scalar_prefetch=4,
        grid=(n_pairs,),
        in_specs=[
            pl.BlockSpec((ch, half_d), lambda i, tl, ck, fr, vl: (ck[i], 0)),
            pl.BlockSpec((1, ch), lambda i, tl, ck, fr, vl: (0, ck[i])),
            pl.BlockSpec((1, ch), lambda i, tl, ck, fr, vl: (0, ck[i])),
        ],
        out_specs=pl.BlockSpec((t, 2 * half_d), lambda i, tl, ck, fr, vl: (tl[i], 0)),
    )
    return pl.pallas_call(
        functools.partial(_combine_kernel, tn=tn),
        grid_spec=grid_spec,
        out_shape=jax.ShapeDtypeStruct((n_tok, 2 * half_d), F32),
        compiler_params=_cparams(("arbitrary",)),
        name="combine_expert_outputs",
    )(tile, chunk, first, valid.astype(jnp.int32), rows, tok_sorted.reshape(1, m), gate_sorted.reshape(1, m).astype(F32))


def _final_kernel(x1_ref, moe_ref, p_ref, g_ref, wg_ref, wp_ref, o_ref):
    x2 = x1_ref[...] + moe_ref[...]
    h = (x2 * lax.rsqrt(jnp.mean(x2 * x2, axis=-1, keepdims=True) + EPS) * g_ref[...]).astype(BF16)
    gate = _sigmoid(jnp.dot(h, wg_ref[...], preferred_element_type=F32))
    proj = jnp.dot(p_ref[...].astype(BF16), wp_ref[...], preferred_element_type=F32)
    o_ref[...] = x2 + gate * proj


def final_embed(x1, moe, p, g_ple, w_gate, w_proj, tm=512):
    n, d_model = x1.shape
    row = lambda w: pl.BlockSpec((tm, w), lambda i: (i, 0))
    full = lambda a: pl.BlockSpec(a.shape, lambda i: (0,) * a.ndim, pipeline_mode=pl.Buffered(1))
    g2 = g_ple.reshape(1, d_model).astype(F32)
    return pl.pallas_call(
        _final_kernel,
        grid=(n // tm,),
        in_specs=[row(d_model), row(d_model), row(p.shape[1]), full(g2), full(w_gate), full(w_proj)],
        out_specs=row(d_model),
        out_shape=jax.ShapeDtypeStruct((n, d_model), F32),
        compiler_params=_cparams(("parallel",)),
        name="final_embed",
    )(x1, moe, p, g2, w_gate, w_proj)


def _prepare_weights(rel_bias, g_mix, w_in, q_gain, k_gain, ssm_a_re, ssm_a_im, ssm_log_step,
                     ssm_b_re, ssm_b_im, ssm_c_re, ssm_c_im, ssm_d, w_glu, w_br_a, w_br_b, w_out,
                     g_ffn, w_router, g_ple, w_ple_gate, w_ple_proj):
    n_heads = q_gain.shape[0]
    att_w = n_heads * HEAD_DIM
    hw = HEADS_PER_GROUP * HEAD_DIM
    ssm_w = ssm_d.shape[0]
    w_in_bf = w_in.astype(BF16)
    w_qkv, gains = [], []
    qg = q_gain.astype(F32) * (HEAD_DIM ** -0.5)
    kg = k_gain.astype(F32)
    for gi in range(len(ATT_PATTERNS)):
        cols = [w_in_bf[:, s * att_w + gi * hw: s * att_w + (gi + 1) * hw] for s in range(3)]
        w_qkv.append(jnp.concatenate(cols, axis=1))
        hs = slice(gi * HEADS_PER_GROUP, (gi + 1) * HEADS_PER_GROUP)
        gains.append(jnp.stack([qg[hs].reshape(1, hw), kg[hs].reshape(1, hw), jnp.ones((1, hw), F32)]))
    lam_bar, b_bar, c = ssm_discretise(ssm_a_re, ssm_a_im, ssm_log_step, ssm_b_re, ssm_b_im, ssm_c_re, ssm_c_im)
    wb, wx, wu, lam2 = ssm_block_weights(lam_bar, b_bar, c)
    n_exp = w_router.shape[1]
    wr = jnp.pad(w_router.astype(F32), ((0, 0), (0, 128 - n_exp)))
    wr_split = jnp.concatenate(_split_bf16(wr), axis=1)
    return dict(
        rel_bias=rel_bias, bias_tables={}, g_mix=g_mix, w_qkv=w_qkv, gains=gains,
        w_u=w_in_bf[:, 3 * att_w:3 * att_w + ssm_w], w_gates=w_in_bf[:, 3 * att_w + ssm_w:],
        lam_bar=lam_bar, wb=wb, wx=wx, wu=wu, lam2=lam2, ssm_d=ssm_d, w_glu=w_glu.astype(BF16),
        w_br_a=w_br_a.astype(BF16), w_br_b=w_br_b.astype(BF16), w_out=w_out.astype(BF16),
        g_ffn=g_ffn, wr_split=wr_split, n_exp=n_exp, g_ple=g_ple,
        w_ple_gate=w_ple_gate.astype(BF16), w_ple_proj=w_ple_proj.astype(BF16))


def _ssm_branch(hn, wts, batch, seq):
    nseg = max(1, 8 // batch)
    vbatch = batch * nseg
    vseq = seq // nseg
    u = plain_proj(hn, wts['w_u'], F32, False, "u_proj")
    nj, k = wts['lam2'].shape[1], wts['lam2'].shape[-1]
    x0 = jnp.zeros((2, nj, 2, vbatch, k), F32)
    if nseg > 1:
        _, xend = s5_scan(u, wts, x0, vbatch, emit_y=False)
        x0 = s5_segment_carries(xend, wts['lam_bar'], vseq, batch, nseg)
    y, _ = s5_scan(u, wts, x0, vbatch, emit_y=True)
    return ssm_glu(y, u, wts['ssm_d'], wts['w_glu'])


MOE_LEAD_EXPERTS = 2


def _expert_choice(h_all, n_toks, logits_list, n_exp, w_exp_gate, w_exp_up, w_exp_down):
    d_model = w_exp_down.shape[2]
    half_d = h_all.shape[1]
    caps = [CAPACITY_FACTOR * n // n_exp for n in n_toks]
    cap_all = sum(caps)
    routes, picks, row0 = [], [], 0
    for n, logits, cap in zip(n_toks, logits_list, caps):
        affinity = jax.nn.softmax(logits[:, :n_exp], axis=-1)
        gate, idx = lax.top_k(affinity.T, cap)
        routes.append((gate.reshape(-1), idx.reshape(-1)))
        picks.append(idx + row0)
        row0 += n
    picks = jnp.concatenate(picks, axis=1)
    hid, lead = None, min(MOE_LEAD_EXPERTS, n_exp)
    for e0, e1 in ((0, lead), (lead, n_exp)):
        if e1 > e0:
            xe = sc_gather_rows(h_all, picks[e0:e1].reshape(-1)).reshape(e1 - e0, cap_all, half_d)
            hid = expert_up(xe, w_exp_gate, w_exp_up, e0, hid)
    tn = min(1024, d_model)
    ye = expert_down(hid, w_exp_down, tn=tn).reshape(n_exp * cap_all, half_d)
    outs, slot0 = [], 0
    for n, (gate_flat, idx_flat), cap in zip(n_toks, routes, caps):
        order = jnp.argsort(idx_flat).astype(jnp.int32)
        rows = (order // cap) * cap_all + slot0 + order % cap
        outs.append(combine_expert_outputs(sc_gather_rows(ye, rows), idx_flat[order], gate_flat[order], n, tn))
        slot0 += cap
    return outs


def _mixer(x, wts, h_buf, h_row0):
    batch, seq, d_model = x.shape
    n = batch * seq
    xf = x.reshape(n, d_model)
    dils = [dil for _, dil in ATT_PATTERNS]
    hn_list = rms_bf16(x, wts['g_mix'], dils)
    hn = hn_list[dils.index(1)]
    o_list, lse_list = [], []
    for gi, dil in enumerate(dils):
        qkv = qkv_proj(hn_list[gi], wts['w_qkv'][gi], wts['gains'][gi], f"qkv_proj_d{dil}")
        hs = slice(gi * HEADS_PER_GROUP, (gi + 1) * HEADS_PER_GROUP)
        bias_key = (gi,) + _attn_geometry(seq // dil)
        if bias_key not in wts['bias_tables']:
            wts['bias_tables'][bias_key] = attn_bias_tables(wts['rel_bias'][:, hs], dil, seq // dil)
        bias = wts['bias_tables'][bias_key]
        o, lse = dilated_attention(qkv, bias, batch, seq, dil)
        o_list.append(o)
        lse_list.append(lse)
    o_b = _ssm_branch(hn, wts, batch, seq)
    gates = plain_proj(hn, wts['w_gates'], BF16, True, "gates_proj")
    return mixer_out(o_list, lse_list, o_b, gates, xf, wts['w_br_a'], wts['w_br_b'],
                     wts['w_out'], wts['g_ffn'], wts['wr_split'], h_buf, h_row0)


def kernel(x_prompt, x_sample, p_prompt, p_sample, rel_bias, g_mix, w_in, q_gain, k_gain, ssm_a_re, ssm_a_im, ssm_log_step, ssm_b_re, ssm_b_im, ssm_c_re, ssm_c_im, ssm_d, w_glu, w_br_a, w_br_b, w_out, g_ffn, w_router, w_exp_gate, w_exp_up, w_exp_down, g_ple, w_ple_gate, w_ple_proj):
    assert g_mix.shape[0] == 1, "single-layer trunk"
    wts = _prepare_weights(rel_bias, g_mix[0], w_in[0], q_gain[0], k_gain[0], ssm_a_re[0], ssm_a_im[0],
                           ssm_log_step[0], ssm_b_re[0], ssm_b_im[0], ssm_c_re[0], ssm_c_im[0], ssm_d[0],
                           w_glu[0], w_br_a[0], w_br_b[0], w_out[0], g_ffn[0], w_router[0], g_ple[0],
                           w_ple_gate[0], w_ple_proj[0])
    xs, ps = (x_prompt, x_sample), (p_prompt[0], p_sample[0])
    n_toks = [x.shape[0] * x.shape[1] for x in xs]
    mixed, row0 = [], 0
    h_all = jnp.zeros((sum(n_toks), x_prompt.shape[-1] // 2), jnp.int32)
    for x, n in zip(xs, n_toks):
        x1, h_all, logits = _mixer(x, wts, h_all, row0)
        mixed.append((x1, logits))
        row0 += n
    moe = _expert_choice(h_all, n_toks, [m[1] for m in mixed], wts['n_exp'],
                         w_exp_gate[0], w_exp_up[0], w_exp_down[0])
    outs = []
    for x, p, (x1, _), y in zip(xs, ps, mixed, moe):
        out = final_embed(x1, y, p.reshape(-1, p.shape[-1]), wts['g_ple'], wts['w_ple_gate'], wts['w_ple_proj'])
        outs.append(out.reshape(x.shape))
    return tuple(outs)
```

```python
import functools
import math

import jax
import jax.numpy as jnp
import numpy as np
from jax import lax
from jax.experimental import pallas as pl
from jax.experimental.pallas import tpu as pltpu
from jax.experimental.pallas import tpu_sc as plsc

F32 = jnp.float32
BF16 = jnp.bfloat16

LANES = 128
HEAD_DIM = 128
HEADS_PER_GROUP = 4
ATT_PATTERNS = ((128, 1), (512, 4), (2048, 16))
ATT_SIDE = 64
N_BUCKETS = 32
MAX_DISTANCE = 1024
CAPACITY_FACTOR = 2
EPS = 1e-6
NEG = -1e30
SSM_LANE_GROUPS = 8
V7X_VMEM_LIMIT = 56 * 1024 * 1024

assert all(w // (2 * d) == ATT_SIDE for w, d in ATT_PATTERNS)


def _cparams(semantics, vmem=V7X_VMEM_LIMIT):
    return pltpu.CompilerParams(dimension_semantics=("arbitrary",) * len(semantics), vmem_limit_bytes=vmem)


def _sigmoid(x):
    return 1.0 / (1.0 + jnp.exp(-x))


def _rms_kernel(x_ref, g_ref, *refs, dils):
    o_refs, y_ref = refs[:-1], refs[-1]
    x = x_ref[...]
    y = x * lax.rsqrt(jnp.mean(x * x, axis=-1, keepdims=True) + EPS) * g_ref[...]
    tm = x.shape[0]
    n_chunks = y_ref.shape[0]
    for c in range(n_chunks):
        y_ref[c] = y[:, c * LANES:(c + 1) * LANES]
    for o_ref, dil in zip(o_refs, dils):
        if dil == 1:
            o_ref[...] = y.astype(o_ref.dtype)
        else:
            for r in range(dil):
                rows = [y_ref[c, pl.ds(r, tm // dil, stride=dil), :] for c in range(n_chunks)]
                o_ref[r] = jnp.concatenate(rows, axis=1).astype(o_ref.dtype)


def rms_bf16(x, g, dils, tm=1024):
    batch, seq, d = x.shape
    out_specs, out_shape = [], []
    for dil in dils:
        if dil == 1:
            out_specs.append(pl.BlockSpec((None, tm, d), lambda b, i: (b, i, 0)))
            out_shape.append(jax.ShapeDtypeStruct((batch, seq, d), BF16))
        else:
            out_specs.append(pl.BlockSpec((None, dil, tm // dil, d), lambda b, i: (b, 0, i, 0)))
            out_shape.append(jax.ShapeDtypeStruct((batch, dil, seq // dil, d), BF16))
    outs = pl.pallas_call(
        functools.partial(_rms_kernel, dils=tuple(dils)),
        grid=(batch, seq // tm),
        in_specs=[pl.BlockSpec((None, tm, d), lambda b, i: (b, i, 0)), pl.BlockSpec((1, d), lambda b, i: (0, 0))],
        out_specs=out_specs,
        out_shape=out_shape,
        scratch_shapes=[pltpu.VMEM((d // LANES, tm, LANES), F32)],
        compiler_params=_cparams(("parallel", "parallel")),
        name="rms_bf16",
    )(x, g.reshape(1, d).astype(F32))
    return [o.reshape(batch * seq, d) for o in outs]


def _head_rms(acc, gain):
    outs = []
    for h in range(acc.shape[1] // HEAD_DIM):
        t = acc[:, h * HEAD_DIM:(h + 1) * HEAD_DIM]
        outs.append(t * lax.rsqrt(jnp.mean(t * t, axis=-1, keepdims=True) + EPS))
    return jnp.concatenate(outs, axis=1) * gain


def _qkv_kernel(x_ref, w_ref, g_ref, o_ref):
    x = x_ref[...]
    tn = o_ref.shape[1] // 3
    for j in range(3):
        acc = jnp.dot(x, w_ref[:, j * tn:(j + 1) * tn], preferred_element_type=F32)
        if j < 2:
            acc = _head_rms(acc, g_ref[j])
        o_ref[:, j * tn:(j + 1) * tn] = acc.astype(o_ref.dtype)


def qkv_proj(hn, w, gains, name, tm=1024):
    n, d_model = hn.shape
    tn = HEADS_PER_GROUP * HEAD_DIM
    return pl.pallas_call(
        _qkv_kernel,
        grid=(n // tm,),
        in_specs=[
            pl.BlockSpec((tm, d_model), lambda i: (i, 0)),
            pl.BlockSpec((d_model, 3 * tn), lambda i: (0, 0), pipeline_mode=pl.Buffered(1)),
            pl.BlockSpec((3, 1, tn), lambda i: (0, 0, 0), pipeline_mode=pl.Buffered(1)),
        ],
        out_specs=pl.BlockSpec((tm, 3 * tn), lambda i: (i, 0)),
        out_shape=jax.ShapeDtypeStruct((n, 3 * tn), BF16),
        compiler_params=_cparams(("parallel",)),
        name=name,
    )(hn, w, gains)


def _proj_kernel(x_ref, w_ref, o_ref, *, act):
    acc = jnp.dot(x_ref[...], w_ref[...], preferred_element_type=F32)
    if act:
        acc = _sigmoid(acc)
    o_ref[...] = acc.astype(o_ref.dtype)


def plain_proj(hn, w, out_dtype, act, name, tm=1024, tn=1024):
    n, d_model = hn.shape
    width = w.shape[1]
    tn = min(tn, width)
    return pl.pallas_call(
        functools.partial(_proj_kernel, act=act),
        grid=(width // tn, n // tm),
        in_specs=[
            pl.BlockSpec((tm, d_model), lambda j, i: (i, 0)),
            pl.BlockSpec((d_model, tn), lambda j, i: (0, j)),
        ],
        out_specs=pl.BlockSpec((tm, tn), lambda j, i: (i, j)),
        out_shape=jax.ShapeDtypeStruct((n, width), out_dtype),
        compiler_params=_cparams(("arbitrary", "parallel")),
        name=name,
    )(hn, w)


def _t5_buckets(rel):
    half = N_BUCKETS // 2
    max_exact = half // 2
    n = np.abs(rel)
    large = max_exact + (np.log(np.maximum(n, 1) / max_exact) / math.log(MAX_DISTANCE / max_exact)
                         * (half - max_exact)).astype(np.int32)
    large = np.minimum(large, half - 1)
    return ((rel > 0).astype(np.int32) * half + np.where(n < max_exact, n, large)).astype(np.int32)


def _attn_geometry(l):
    tqb = min(2 * ATT_SIDE, l)
    win = min(4 * ATT_SIDE, l)
    return tqb, win


def attn_bias_tables(rel_bias_g, dil, l):
    tqb, win = _attn_geometry(l)
    i = np.arange(tqb)[:, None]
    j = np.arange(win)[None, :]
    rel = np.stack([j - i - ATT_SIDE * shift for shift in range(3)])
    valid = np.abs(rel) <= ATT_SIDE
    onehot = (_t5_buckets(dil * rel)[..., None] == np.arange(N_BUCKETS)).astype(np.int8)
    b = jnp.einsum('sijb,bh->shij', jnp.asarray(onehot).astype(F32), rel_bias_g.astype(F32),
                   precision=lax.Precision.HIGHEST)
    return jnp.where(valid[:, None], b, NEG)


def _attn_kernel(q_ref, k_ref, v_ref, bias_ref, o_ref, lse_ref, *, l, dil, tql):
    tqb, win = _attn_geometry(l)
    qi = pl.program_id(2)
    for r in range(dil):
        for c in range(tql // tqb):
            q0 = qi * tql + c * tqb
            start = pl.multiple_of(jnp.clip(q0 - ATT_SIDE, 0, l - win), ATT_SIDE)
            shift = (q0 - start) // ATT_SIDE
            q = q_ref[r, c * tqb:(c + 1) * tqb, :]
            kw = k_ref[r, pl.ds(start, win), :]
            vw = v_ref[r, pl.ds(start, win), :]
            logits = lax.dot_general(q, kw, (((1,), (1,)), ((), ())), preferred_element_type=F32)
            logits = logits + bias_ref[shift]
            m = jnp.max(logits, axis=-1, keepdims=True)
            p = jnp.exp(logits - m)
            s = jnp.sum(p, axis=-1, keepdims=True)
            o = jnp.dot(p.astype(vw.dtype), vw, preferred_element_type=F32)
            if dil == 1:
                rows = pl.ds(c * tqb, tqb)
            else:
                rows = pl.ds(c * tqb * dil + r, tqb, stride=dil)
            o_ref[rows, :] = o / s
            lse_ref[rows, :] = jnp.broadcast_to(m + jnp.log(s), (tqb, HEAD_DIM))


def dilated_attention(qkv, bias, batch, seq, dil):
    l = seq // dil
    tqb, win = _attn_geometry(l)
    tql = min(l, max(tqb, 8 * tqb // dil))
    tok = tql * dil
    hw = HEADS_PER_GROUP * HEAD_DIM
    x = qkv.reshape(batch, dil, l, 3 * hw)
    out_sds = jax.ShapeDtypeStruct((batch, seq, hw), F32)
    out_spec = pl.BlockSpec((None, tok, HEAD_DIM), lambda b, h, i: (b, i, h))
    o, lse = pl.pallas_call(
        functools.partial(_attn_kernel, l=l, dil=dil, tql=tql),
        grid=(batch, HEADS_PER_GROUP, l // tql),
        in_specs=[
            pl.BlockSpec((None, dil, tql, HEAD_DIM), lambda b, h, i: (b, 0, i, h)),
            pl.BlockSpec((None, dil, l, HEAD_DIM), lambda b, h, i: (b, 0, 0, HEADS_PER_GROUP + h)),
            pl.BlockSpec((None, dil, l, HEAD_DIM), lambda b, h, i: (b, 0, 0, 2 * HEADS_PER_GROUP + h)),
            pl.BlockSpec((3, None, tqb, win), lambda b, h, i: (0, h, 0, 0)),
        ],
        out_specs=[out_spec, out_spec],
        out_shape=[out_sds, out_sds],
        compiler_params=_cparams(("parallel", "parallel", "parallel")),
        name=f"dilated_attn_d{dil}",
    )(x, x, x, bias)
    return o.reshape(batch * seq, hw), lse.reshape(batch * seq, hw)


def ssm_discretise(a_re, a_im, log_step, b_re, b_im, c_re, c_im):
    lam = lax.complex(a_re.astype(F32), a_im.astype(F32))
    step = jnp.exp(log_step.astype(F32))[..., None]
    lam_bar = jnp.exp(lam * step)
    b_bar = ((lam_bar - 1.0) / lam)[..., None] * lax.complex(b_re.astype(F32), b_im.astype(F32))
    c = lax.complex(c_re.astype(F32), c_im.astype(F32))
    return lam_bar, b_bar, c


def _block_diag(m):
    _, nj, lg, a, b = m.shape
    return jnp.einsum('djgab,gk->djgakb', m, jnp.eye(lg, dtype=F32)).reshape(2, nj, lg * a, lg * b)


def ssm_block_weights(lam_bar, b_bar, c):
    _, g, p, h = b_bar.shape
    lg = SSM_LANE_GROUPS
    nj = g // lg
    lam = lam_bar[..., None]
    ct = jnp.swapaxes(c, -1, -2)

    def grp(m):
        return m.reshape((2, nj, lg) + m.shape[2:])

    def state_in(bmat):
        t = grp(jnp.swapaxes(bmat, -1, -2))
        return jnp.concatenate([_block_diag(t.real), _block_diag(t.imag)], axis=-1)

    def state_out(cmat):
        t = grp(cmat)
        return jnp.concatenate([_block_diag(t.real), _block_diag(-t.imag)], axis=-2)

    def direct(k):
        m = jnp.einsum('dgpc,dgph->dghc', ct * lam ** k, b_bar).real
        return _block_diag(grp(m))

    first_in, second_in = state_in(lam * b_bar), state_in(b_bar)
    first_out, second_out = state_out(ct * lam), state_out(ct * lam ** 2)
    zero = jnp.zeros_like(direct(0))
    sel = jnp.array([0, 1]).reshape(2, 1, 1, 1)
    wb = jnp.where(sel == 0, jnp.concatenate([first_in, second_in], axis=-2),
                   jnp.concatenate([second_in, first_in], axis=-2))
    wx = jnp.where(sel == 0, jnp.concatenate([first_out, second_out], axis=-1),
                   jnp.concatenate([second_out, first_out], axis=-1))
    u_first = jnp.concatenate([direct(0), direct(1)], axis=-1)
    u_second = jnp.concatenate([zero, direct(0)], axis=-1)
    swap = lambda m: jnp.concatenate([m[..., 128:], m[..., :128]], axis=-1)
    wu = jnp.where(sel == 0, jnp.concatenate([u_first, u_second], axis=-2),
                   jnp.concatenate([swap(u_second), swap(u_first)], axis=-2))
    l2 = (lam_bar ** 2).reshape(2, nj, 1, 1, lg * p)
    lam2 = jnp.broadcast_to(jnp.concatenate([l2.real, l2.imag], axis=2), (2, nj, 2, 8, lg * p))
    return wb.astype(BF16), wx.astype(BF16), wu.astype(BF16), lam2.astype(F32)


def _s5_kernel(u_ref, wb_ref, wx_ref, wu_ref, lam_ref, x0_ref, *refs, tt, emit_y):
    n_out = 2 if emit_y else 1
    outs, scratch = refs[:n_out], list(refs[n_out:])
    xend_ref = outs[-1]
    parts = []
    for _ in range(S5_PARTS):
        parts.append([scratch.pop(0) for _ in range(4 if emit_y else 3)])
    st_ref = scratch.pop(0)
    direction = pl.program_id(0)
    ti = pl.program_id(2)
    nt = pl.num_programs(2)
    vb, _, lanes = u_ref.shape
    k = lam_ref.shape[-1]
    part_t = tt // S5_PARTS
    n = part_t // 2

    @pl.when(ti == 0)
    def _():
        st_ref[...] = x0_ref[...]

    lr = lam_ref[0]
    li = lam_ref[1]
    state = (st_ref[0], st_ref[1])
    for order, part in enumerate(parts):
        lhs_e, lhs_o, bu_ref = part[:3]
        t0 = jnp.where(direction == 0, order, S5_PARTS - 1 - order) * part_t
        for b in range(vb):
            lhs_e[pl.ds(b, n, stride=vb), :] = u_ref[b, pl.ds(t0, n, stride=2), :]
            lhs_o[pl.ds(b, n, stride=vb), :] = u_ref[b, pl.ds(t0 + 1, n, stride=2), :]
        lhs = jnp.concatenate([lhs_e[...], lhs_o[...]], axis=1).astype(BF16)
        bu_ref[...] = jnp.dot(lhs, wb_ref[...], preferred_element_type=F32)
        if emit_y:
            xs_ref = part[3]
            in_row = pl.multiple_of(jnp.where(direction == 0, 0, n * vb), vb)
            xs_ref[pl.ds(in_row, vb), :] = jnp.concatenate(state, axis=1)

        def step(s, carry, part=part):
            xr, xi = carry
            t = jnp.where(direction == 0, s, n - 1 - s)
            b = part[2][pl.ds(pl.multiple_of(t * vb, vb), vb), :]
            nr = lr * xr - li * xi + b[:, :k]
            ni = lr * xi + li * xr + b[:, k:]
            if emit_y:
                out_row = pl.multiple_of(jnp.where(direction == 0, t + 1, t) * vb, vb)
                part[3][pl.ds(out_row, vb), :] = jnp.concatenate([nr, ni], axis=1)
            return nr, ni

        state = lax.fori_loop(0, n, step, state, unroll=True)
        if emit_y:
            y_ref, ysc_e, ysc_o = outs[0], scratch[0], scratch[1]
            base = pl.multiple_of(jnp.where(direction == 0, 0, vb), vb)
            xs = xs_ref[pl.ds(base, n * vb), :].astype(BF16)
            y = (jnp.dot(xs, wx_ref[...], preferred_element_type=F32)
                 + jnp.dot(lhs, wu_ref[...], preferred_element_type=F32))
            ysc_e[...] = y[:, :lanes]
            ysc_o[...] = y[:, lanes:]
            for b in range(vb):
                y_ref[b, pl.ds(t0, n, stride=2), :] = ysc_e[pl.ds(b, n, stride=vb), :]
                y_ref[b, pl.ds(t0 + 1, n, stride=2), :] = ysc_o[pl.ds(b, n, stride=vb), :]
    st_ref[0] = state[0]
    st_ref[1] = state[1]

    @pl.when(ti == nt - 1)
    def _():
        xend_ref[0] = state[0]
        xend_ref[1] = state[1]


S5_PARTS = 2


def s5_scan(u, wts, x0, vbatch, emit_y, tt=512):
    width = u.shape[1]
    vseq = u.shape[0] // vbatch
    wb, wx, wu, lam2 = wts['wb'], wts['wx'], wts['wu'], wts['lam2']
    nj = wb.shape[1]
    k = lam2.shape[-1]
    lanes = width // nj
    nt = vseq // tt
    n = tt // S5_PARTS // 2

    def tmap(d, t):
        return jnp.where(d == 0, t, nt - 1 - t)

    state_spec = pl.BlockSpec((None, None, 2, vbatch, k), lambda d, j, t: (d, j, 0, 0, 0))
    state_sds = jax.ShapeDtypeStruct((2, nj, 2, vbatch, k), F32)
    out_specs, out_shape = [state_spec], [state_sds]
    half = pltpu.VMEM((n * vbatch, lanes), F32)
    part = [half, half, pltpu.VMEM((n * vbatch, 2 * k), F32)]
    if emit_y:
        part.append(pltpu.VMEM(((n + 1) * vbatch, 2 * k), F32))
    scratch = part * S5_PARTS + [pltpu.VMEM((2, vbatch, k), F32)]
    if emit_y:
        out_specs.insert(0, pl.BlockSpec((None, vbatch, tt, lanes), lambda d, j, t: (d, 0, tmap(d, t), j)))
        out_shape.insert(0, jax.ShapeDtypeStruct((2, vbatch, vseq, width), F32))
        scratch += [half, half]
    outs = pl.pallas_call(
        functools.partial(_s5_kernel, tt=tt, emit_y=emit_y),
        grid=(2, nj, nt),
        in_specs=[
            pl.BlockSpec((vbatch, tt, lanes), lambda d, j, t: (0, tmap(d, t), j)),
            pl.BlockSpec((None, None, 2 * lanes, 2 * k), lambda d, j, t: (d, j, 0, 0)),
            pl.BlockSpec((None, None, 2 * k, 2 * lanes), lambda d, j, t: (d, j, 0, 0)),
            pl.BlockSpec((None, None, 2 * lanes, 2 * lanes), lambda d, j, t: (d, j, 0, 0)),
            state_spec, state_spec,
        ],
        out_specs=out_specs,
        out_shape=out_shape,
        scratch_shapes=scratch,
        compiler_params=_cparams(("parallel", "parallel", "arbitrary")),
        name="s5_scan" if emit_y else "s5_states",
    )(u.reshape(vbatch, vseq, width), wb, wx, wu, lam2, x0)
    if emit_y:
        return outs[0].reshape(2, vbatch * vseq, width), outs[1]
    return None, outs[0]


def s5_segment_carries(xend, lam_bar, seg_len, batch, nseg):
    _, nj, _, _, k = xend.shape
    lam_pow = (lam_bar ** seg_len).reshape(2, nj, 1, k)
    e = lax.complex(xend[:, :, 0], xend[:, :, 1]).reshape(2, nj, batch, nseg, k)
    fwd = [jnp.zeros((nj, batch, k), jnp.complex64)]
    for s in range(1, nseg):
        fwd.append(lam_pow[0] * fwd[-1] + e[0, :, :, s - 1])
    bwd = [jnp.zeros((nj, batch, k), jnp.complex64)]
    for s in range(nseg - 2, -1, -1):
        bwd.append(lam_pow[1] * bwd[-1] + e[1, :, :, s + 1])
    bwd = bwd[::-1]
    x_in = jnp.stack([jnp.stack(fwd, axis=2), jnp.stack(bwd, axis=2)])
    x_in = x_in.reshape(2, nj, batch * nseg, k)
    return jnp.stack([x_in.real, x_in.imag], axis=2).astype(F32)


def _gelu_tanh(x):
    return 0.5 * x * (1.0 + jnp.tanh(math.sqrt(2.0 / math.pi) * (x + 0.044715 * x * x * x)))


def _glu_kernel(yf_ref, yb_ref, u_ref, d_ref, wa_ref, wb_ref, o_ref):
    y = _gelu_tanh(yf_ref[...] + yb_ref[...] + d_ref[...] * u_ref[...]).astype(BF16)
    za = jnp.dot(y, wa_ref[...], preferred_element_type=F32)
    zb = jnp.dot(y, wb_ref[...], preferred_element_type=F32)
    o_ref[...] = (za * _sigmoid(zb)).astype(o_ref.dtype)


def ssm_glu(y, u, d_skip, w_glu, tm=512):
    n, width = u.shape
    wa = w_glu[:, :width]
    wb = w_glu[:, width:]
    const = lambda i: (0, 0)
    return pl.pallas_call(
        _glu_kernel,
        grid=(n // tm,),
        in_specs=[
            pl.BlockSpec((None, tm, width), lambda i: (0, i, 0)),
            pl.BlockSpec((None, tm, width), lambda i: (1, i, 0)),
            pl.BlockSpec((tm, width), lambda i: (i, 0)),
            pl.BlockSpec((1, width), const),
            pl.BlockSpec((width, width), const),
            pl.BlockSpec((width, width), const),
        ],
        out_specs=pl.BlockSpec((tm, width), lambda i: (i, 0)),
        out_shape=jax.ShapeDtypeStruct((n, width), BF16),
        compiler_params=_cparams(("parallel",)),
        name="ssm_glu",
    )(y, y, u, d_skip.reshape(1, width).astype(F32), wa, wb)


def _split_bf16(x):
    hi = x.astype(BF16)
    lo = (x - hi.astype(F32)).astype(BF16)
    return hi, lo


def _pack_bf16_pair(a, b):
    wa = pltpu.bitcast(a.astype(BF16).astype(F32), jnp.int32)
    wb = pltpu.bitcast(b.astype(BF16).astype(F32), jnp.int32)
    return wa | lax.shift_right_logical(wb, 16)


def _unpack_bf16_pair(w):
    hi = pltpu.bitcast(w & jnp.int32(-65536), F32).astype(BF16)
    lo = pltpu.bitcast(lax.shift_left(w, 16), F32).astype(BF16)
    return hi, lo


def _mixer_out_kernel(o0, o1, o2, l0, l1, l2, ob_ref, gate_ref, x_ref, wa_ref, wb_ref, wo_ref,
                      g_ref, wr_ref, hbuf_ref, x1_ref, h_ref, lg_ref):
    del hbuf_ref
    d_model = x_ref.shape[1]
    la, lb, lc = l0[...], l1[...], l2[...]
    m = jnp.maximum(jnp.maximum(la, lb), lc)
    ea, eb, ec = jnp.exp(la - m), jnp.exp(lb - m), jnp.exp(lc - m)
    oa = (ea * o0[...] + eb * o1[...] + ec * o2[...]) / (ea + eb + ec)
    pa = jnp.dot(oa.astype(BF16), wa_ref[...], preferred_element_type=F32)
    pb = jnp.dot(ob_ref[...], wb_ref[...], preferred_element_type=F32)
    gates = gate_ref[...].astype(F32)
    merged = gates[:, :d_model] * pa + gates[:, d_model:] * pb
    x1 = x_ref[...] + jnp.dot(merged.astype(BF16), wo_ref[...], preferred_element_type=F32)
    x1_ref[...] = x1
    h = x1 * lax.rsqrt(jnp.mean(x1 * x1, axis=-1, keepdims=True) + EPS) * g_ref[...]
    h_ref[...] = _pack_bf16_pair(h[:, :d_model // 2], h[:, d_model // 2:])
    hh, hl = _split_bf16(h)
    er = lg_ref.shape[1]
    both = jnp.dot(hh, wr_ref[...], preferred_element_type=F32)
    lg_ref[...] = both[:, :er] + both[:, er:] + jnp.dot(hl, wr_ref[:, :er], preferred_element_type=F32)


def mixer_out(o_list, lse_list, o_b, gates, x, w_br_a, w_br_b, w_out, g_ffn, wr_split, h_buf, h_row0, tm=256):
    n, d_model = x.shape
    row = lambda w: pl.BlockSpec((tm, w), lambda i: (i, 0))
    full = lambda a: pl.BlockSpec(a.shape, lambda i: (0,) * a.ndim, pipeline_mode=pl.Buffered(1))
    g2 = g_ffn.reshape(1, d_model).astype(F32)
    er = wr_split.shape[1] // 2
    blk0 = h_row0 // tm
    in_specs = [row(o.shape[1]) for o in o_list] + [row(l.shape[1]) for l in lse_list] + [
        row(o_b.shape[1]), row(gates.shape[1]), row(d_model),
        full(w_br_a), full(w_br_b), full(w_out), full(g2), full(wr_split), pl.BlockSpec(memory_space=pl.ANY)]
    args = [*o_list, *lse_list, o_b, gates, x, w_br_a, w_br_b, w_out, g2, wr_split, h_buf]
    x1, h, logits = pl.pallas_call(
        _mixer_out_kernel,
        grid=(n // tm,),
        in_specs=in_specs,
        out_specs=[row(d_model), pl.BlockSpec((tm, d_model // 2), lambda i: (blk0 + i, 0)), row(er)],
        out_shape=[jax.ShapeDtypeStruct((n, d_model), F32), jax.ShapeDtypeStruct(h_buf.shape, h_buf.dtype),
                   jax.ShapeDtypeStruct((n, er), F32)],
        input_output_aliases={len(args) - 1: 1},
        compiler_params=_cparams(("parallel",)),
        name="mixer_out",
    )(*args)
    return x1, h, logits


def _expert_up_kernel(x_ref, wg_ref, wu_ref, o_ref, wg_s, wu_s):
    @pl.when(pl.program_id(2) == 0)
    def _():
        wg_s[...] = wg_ref[...].astype(BF16)
        wu_s[...] = wu_ref[...].astype(BF16)

    x_hi, x_lo = _unpack_bf16_pair(x_ref[...])
    half = x_hi.shape[1]

    def proj(w_s):
        return (jnp.dot(x_hi, w_s[:half, :], preferred_element_type=F32)
                + jnp.dot(x_lo, w_s[half:, :], preferred_element_type=F32))

    a = proj(wg_s)
    b = proj(wu_s)
    o_ref[...] = (a * _sigmoid(a) * b).astype(o_ref.dtype)


def _expert_up_into_kernel(x_ref, wg_ref, wu_ref, buf_ref, o_ref, wg_s, wu_s):
    del buf_ref
    _expert_up_kernel(x_ref, wg_ref, wu_ref, o_ref, wg_s, wu_s)


def expert_up(xe, w_gate, w_up, e0, hid=None, tm=1024, tf=512):
    n_e, cap, _ = xe.shape
    n_exp, d_model, ff = w_gate.shape
    tm = math.gcd(tm, cap)
    w_spec = pl.BlockSpec((None, d_model, tf), lambda ei, f, m: (e0 + ei, 0, f))
    in_specs = [pl.BlockSpec((None, tm, d_model // 2), lambda ei, f, m: (ei, m, 0)), w_spec, w_spec]
    args = [xe, w_gate, w_up]
    if hid is not None:
        in_specs.append(pl.BlockSpec(memory_space=pl.ANY))
        args.append(hid)
    return pl.pallas_call(
        _expert_up_kernel if hid is None else _expert_up_into_kernel,
        grid=(n_e, ff // tf, cap // tm),
        in_specs=in_specs,
        out_specs=pl.BlockSpec((None, tm, tf), lambda ei, f, m: (e0 + ei, m, f)),
        out_shape=jax.ShapeDtypeStruct((n_exp, cap, ff), BF16),
        scratch_shapes=[pltpu.VMEM((d_model, tf), BF16), pltpu.VMEM((d_model, tf), BF16)],
        input_output_aliases={} if hid is None else {3: 0},
        compiler_params=_cparams(("parallel", "parallel", "arbitrary")),
        name="expert_up",
    )(*args)


def _expert_down_kernel(h_ref, w_ref, o_ref, w_s):
    @pl.when(pl.program_id(2) == 0)
    def _():
        w_s[...] = w_ref[...].astype(BF16)

    acc = jnp.dot(h_ref[...], w_s[...], preferred_element_type=F32)
    half = acc.shape[1] // 2
    o_ref[...] = _pack_bf16_pair(acc[:, :half], acc[:, half:])


def expert_down(hid, w_down, tm=512, tn=1024):
    e, cap, ff = hid.shape
    d_model = w_down.shape[2]
    tm = math.gcd(tm, cap)
    tn = min(tn, d_model)
    return pl.pallas_call(
        _expert_down_kernel,
        grid=(e, d_model // tn, cap // tm),
        in_specs=[
            pl.BlockSpec((None, tm, ff), lambda ei, n, m: (ei, m, 0)),
            pl.BlockSpec((None, ff, tn), lambda ei, n, m: (ei, 0, n)),
        ],
        out_specs=pl.BlockSpec((None, tm, tn // 2), lambda ei, n, m: (ei, m, n)),
        out_shape=jax.ShapeDtypeStruct((e, cap, d_model // 2), jnp.int32),
        scratch_shapes=[pltpu.VMEM((ff, tn), BF16)],
        compiler_params=_cparams(("parallel", "parallel", "arbitrary")),
        name="expert_down",
    )(hid, w_down)


SC_GATHER_CHUNK = 16


def sc_gather_rows(table, idx):
    n_idx = idx.shape[0]
    width = table.shape[1]
    info = plsc.get_sparse_core_info()
    n_workers = info.num_cores * info.num_subcores
    per_worker = n_idx // n_workers
    ch = SC_GATHER_CHUNK
    assert n_idx % n_workers == 0 and per_worker % (2 * ch) == 0
    mesh = plsc.VectorSubcoreMesh(core_axis_name="c", subcore_axis_name="s")

    @functools.partial(
        pl.kernel, mesh=mesh,
        out_type=jax.ShapeDtypeStruct((n_idx, width), table.dtype),
        scratch_types=[pltpu.VMEM((per_worker,), jnp.int32), pltpu.VMEM((ch, width), table.dtype),
                       pltpu.VMEM((ch, width), table.dtype), pltpu.SemaphoreType.DMA, pltpu.SemaphoreType.DMA],
        name="sc_gather_rows",
    )
    def gather(table_hbm, idx_hbm, out_hbm, idx_v, rows_a, rows_b, sem_a, sem_b):
        base = (lax.axis_index("s") * info.num_cores + lax.axis_index("c")) * per_worker
        pltpu.sync_copy(idx_hbm.at[pl.ds(base, per_worker)], idx_v)

        @pl.loop(0, per_worker // (2 * ch))
        def _(c):
            off = c * 2 * ch
            copy_a = pltpu.async_copy(table_hbm.at[idx_v.at[pl.ds(off, ch)]], rows_a, sem_a)
            copy_b = pltpu.async_copy(table_hbm.at[idx_v.at[pl.ds(off + ch, ch)]], rows_b, sem_b)
            copy_a.wait()
            pltpu.sync_copy(rows_a, out_hbm.at[pl.ds(base + off, ch)])
            copy_b.wait()
            pltpu.sync_copy(rows_b, out_hbm.at[pl.ds(base + off + ch, ch)])

    return gather(table, idx)


COMBINE_TOKENS = 512
COMBINE_ROWS = 512


def _combine_kernel(tile_ref, chunk_ref, first_ref, valid_ref, rows_ref, tok_ref, g_ref, o_ref, *, tn):
    p = pl.program_id(0)

    @pl.when(first_ref[p] == 1)
    def _():
        o_ref[...] = jnp.zeros_like(o_ref)

    @pl.when(valid_ref[p] == 1)
    def _():
        t, ch = o_ref.shape[0], tok_ref.shape[1]
        token = tile_ref[p] * t + lax.broadcasted_iota(jnp.int32, (t, ch), 0)
        weights = jnp.where(tok_ref[...] == token, g_ref[...], 0.0).astype(BF16)
        hi, lo = _unpack_bf16_pair(rows_ref[...])
        acc_hi = jnp.dot(weights, hi, preferred_element_type=F32)
        acc_lo = jnp.dot(weights, lo, preferred_element_type=F32)
        half = tn // 2
        for n in range(o_ref.shape[1] // tn):
            o_ref[:, n * tn:n * tn + half] += acc_hi[:, n * half:(n + 1) * half]
            o_ref[:, n * tn + half:(n + 1) * tn] += acc_lo[:, n * half:(n + 1) * half]


def combine_expert_outputs(rows, tok_sorted, gate_sorted, n_tok, tn):
    m, half_d = rows.shape
    t, ch = COMBINE_TOKENS, COMBINE_ROWS
    n_tiles, n_chunks = n_tok // t, m // ch
    n_pairs = n_tiles + n_chunks
    bounds = jnp.searchsorted(tok_sorted, jnp.arange(n_tiles + 1, dtype=jnp.int32) * t, side='left').astype(jnp.int32)
    start, end = bounds[:-1], bounds[1:]
    c_lo = jnp.minimum(start // ch, n_chunks - 1)
    c_hi = jnp.where(end > start, (end - 1) // ch, c_lo)
    count = c_hi - c_lo + 1
    offs = jnp.cumsum(count)
    p = jnp.arange(n_pairs, dtype=jnp.int32)
    tile = jnp.minimum(jnp.searchsorted(offs, p, side='right'), n_tiles - 1).astype(jnp.int32)
    first_p = (offs - count)[tile]
    valid = p < offs[-1]
    chunk = jnp.where(valid, c_lo[tile] + p - first_p, c_hi[-1]).astype(jnp.int32)
    first = (valid & (p == first_p)).astype(jnp.int32)
    grid_spec = pltpu.PrefetchScalarGridSpec(
        num_scalar_prefetch=4,
        grid=(n_pairs,),
        in_specs=[
            pl.BlockSpec((ch, half_d), lambda i, tl, ck, fr, vl: (ck[i], 0)),
            pl.BlockSpec((1, ch), lambda i, tl, ck, fr, vl: (0, ck[i])),
            pl.BlockSpec((1, ch), lambda i, tl, ck, fr, vl: (0, ck[i])),
        ],
        out_specs=pl.BlockSpec((t, 2 * half_d), lambda i, tl, ck, fr, vl: (tl[i], 0)),
    )
    return pl.pallas_call(
        functools.partial(_combine_kernel, tn=tn),
        grid_spec=grid_spec,
        out_shape=jax.ShapeDtypeStruct((n_tok, 2 * half_d), F32),
        compiler_params=_cparams(("arbitrary",)),
        name="combine_expert_outputs",
    )(tile, chunk, first, valid.astype(jnp.int32), rows, tok_sorted.reshape(1, m), gate_sorted.reshape(1, m).astype(F32))


def _final_kernel(x1_ref, moe_ref, p_ref, g_ref, wg_ref, wp_ref, o_ref):
    x2 = x1_ref[...] + moe_ref[...]
    h = (x2 * lax.rsqrt(jnp.mean(x2 * x2, axis=-1, keepdims=True) + EPS) * g_ref[...]).astype(BF16)
    gate = _sigmoid(jnp.dot(h, wg_ref[...], preferred_element_type=F32))
    proj = jnp.dot(p_ref[...].astype(BF16), wp_ref[...], preferred_element_type=F32)
    o_ref[...] = x2 + gate * proj


def final_embed(x1, moe, p, g_ple, w_gate, w_proj, tm=512):
    n, d_model = x1.shape
    row = lambda w: pl.BlockSpec((tm, w), lambda i: (i, 0))
    full = lambda a: pl.BlockSpec(a.shape, lambda i: (0,) * a.ndim, pipeline_mode=pl.Buffered(1))
    g2 = g_ple.reshape(1, d_model).astype(F32)
    return pl.pallas_call(
        _final_kernel,
        grid=(n // tm,),
        in_specs=[row(d_model), row(d_model), row(p.shape[1]), full(g2), full(w_gate), full(w_proj)],
        out_specs=row(d_model),
        out_shape=jax.ShapeDtypeStruct((n, d_model), F32),
        compiler_params=_cparams(("parallel",)),
        name="final_embed",
    )(x1, moe, p, g2, w_gate, w_proj)


def _prepare_weights(rel_bias, g_mix, w_in, q_gain, k_gain, ssm_a_re, ssm_a_im, ssm_log_step,
                     ssm_b_re, ssm_b_im, ssm_c_re, ssm_c_im, ssm_d, w_glu, w_br_a, w_br_b, w_out,
                     g_ffn, w_router, g_ple, w_ple_gate, w_ple_proj):
    n_heads = q_gain.shape[0]
    att_w = n_heads * HEAD_DIM
    hw = HEADS_PER_GROUP * HEAD_DIM
    ssm_w = ssm_d.shape[0]
    w_in_bf = w_in.astype(BF16)
    w_qkv, gains = [], []
    qg = q_gain.astype(F32) * (HEAD_DIM ** -0.5)
    kg = k_gain.astype(F32)
    for gi in range(len(ATT_PATTERNS)):
        cols = [w_in_bf[:, s * att_w + gi * hw: s * att_w + (gi + 1) * hw] for s in range(3)]
        w_qkv.append(jnp.concatenate(cols, axis=1))
        hs = slice(gi * HEADS_PER_GROUP, (gi + 1) * HEADS_PER_GROUP)
        gains.append(jnp.stack([qg[hs].reshape(1, hw), kg[hs].reshape(1, hw), jnp.ones((1, hw), F32)]))
    lam_bar, b_bar, c = ssm_discretise(ssm_a_re, ssm_a_im, ssm_log_step, ssm_b_re, ssm_b_im, ssm_c_re, ssm_c_im)
    wb, wx, wu, lam2 = ssm_block_weights(lam_bar, b_bar, c)
    n_exp = w_router.shape[1]
    wr = jnp.pad(w_router.astype(F32), ((0, 0), (0, 128 - n_exp)))
    wr_split = jnp.concatenate(_split_bf16(wr), axis=1)
    return dict(
        rel_bias=rel_bias, bias_tables={}, g_mix=g_mix, w_qkv=w_qkv, gains=gains,
        w_u=w_in_bf[:, 3 * att_w:3 * att_w + ssm_w], w_gates=w_in_bf[:, 3 * att_w + ssm_w:],
        lam_bar=lam_bar, wb=wb, wx=wx, wu=wu, lam2=lam2, ssm_d=ssm_d, w_glu=w_glu.astype(BF16),
        w_br_a=w_br_a.astype(BF16), w_br_b=w_br_b.astype(BF16), w_out=w_out.astype(BF16),
        g_ffn=g_ffn, wr_split=wr_split, n_exp=n_exp, g_ple=g_ple,
        w_ple_gate=w_ple_gate.astype(BF16), w_ple_proj=w_ple_proj.astype(BF16))


def _ssm_branch(hn, wts, batch, seq):
    nseg = max(1, 8 // batch)
    vbatch = batch * nseg
    vseq = seq // nseg
    u = plain_proj(hn, wts['w_u'], F32, False, "u_proj")
    nj, k = wts['lam2'].shape[1], wts['lam2'].shape[-1]
    x0 = jnp.zeros((2, nj, 2, vbatch, k), F32)
    if nseg > 1:
        _, xend = s5_scan(u, wts, x0, vbatch, emit_y=False)
        x0 = s5_segment_carries(xend, wts['lam_bar'], vseq, batch, nseg)
    y, _ = s5_scan(u, wts, x0, vbatch, emit_y=True)
    return ssm_glu(y, u, wts['ssm_d'], wts['w_glu'])


MOE_LEAD_EXPERTS = 2


def _expert_choice(h_all, n_toks, logits_list, n_exp, w_exp_gate, w_exp_up, w_exp_down):
    d_model = w_exp_down.shape[2]
    half_d = h_all.shape[1]
    caps = [CAPACITY_FACTOR * n // n_exp for n in n_toks]
    cap_all = sum(caps)
    routes, picks, row0 = [], [], 0
    for n, logits, cap in zip(n_toks, logits_list, caps):
        affinity = jax.nn.softmax(logits[:, :n_exp], axis=-1)
        gate, idx = lax.top_k(affinity.T, cap)
        routes.append((gate.reshape(-1), idx.reshape(-1)))
        picks.append(idx + row0)
        row0 += n
    picks = jnp.concatenate(picks, axis=1)
    hid, lead = None, min(MOE_LEAD_EXPERTS, n_exp)
    for e0, e1 in ((0, lead), (lead, n_exp)):
        if e1 > e0:
            xe = sc_gather_rows(h_all, picks[e0:e1].reshape(-1)).reshape(e1 - e0, cap_all, half_d)
            hid = expert_up(xe, w_exp_gate, w_exp_up, e0, hid)
    tn = min(1024, d_model)
    ye = expert_down(hid, w_exp_down, tn=tn).reshape(n_exp * cap_all, half_d)
    outs, slot0 = [], 0
    for n, (gate_flat, idx_flat), cap in zip(n_toks, routes, caps):
        order = jnp.argsort(idx_flat).astype(jnp.int32)
        rows = (order // cap) * cap_all + slot0 + order % cap
        outs.append(combine_expert_outputs(sc_gather_rows(ye, rows), idx_flat[order], gate_flat[order], n, tn))
        slot0 += cap
    return outs


def _mixer(x, wts, h_buf, h_row0):
    batch, seq, d_model = x.shape
    n = batch * seq
    xf = x.reshape(n, d_model)
    dils = [dil for _, dil in ATT_PATTERNS]
    hn_list = rms_bf16(x, wts['g_mix'], dils)
    hn = hn_list[dils.index(1)]
    o_list, lse_list = [], []
    for gi, dil in enumerate(dils):
        qkv = qkv_proj(hn_list[gi], wts['w_qkv'][gi], wts['gains'][gi], f"qkv_proj_d{dil}")
        hs = slice(gi * HEADS_PER_GROUP, (gi + 1) * HEADS_PER_GROUP)
        bias_key = (gi,) + _attn_geometry(seq // dil)
        if bias_key not in wts['bias_tables']:
            wts['bias_tables'][bias_key] = attn_bias_tables(wts['rel_bias'][:, hs], dil, seq // dil)
        bias = wts['bias_tables'][bias_key]
        o, lse = dilated_attention(qkv, bias, batch, seq, dil)
        o_list.append(o)
        lse_list.append(lse)
    o_b = _ssm_branch(hn, wts, batch, seq)
    gates = plain_proj(hn, wts['w_gates'], BF16, True, "gates_proj")
    return mixer_out(o_list, lse_list, o_b, gates, xf, wts['w_br_a'], wts['w_br_b'],
                     wts['w_out'], wts['g_ffn'], wts['wr_split'], h_buf, h_row0)


def kernel(x_prompt, x_sample, p_prompt, p_sample, rel_bias, g_mix, w_in, q_gain, k_gain, ssm_a_re, ssm_a_im, ssm_log_step, ssm_b_re, ssm_b_im, ssm_c_re, ssm_c_im, ssm_d, w_glu, w_br_a, w_br_b, w_out, g_ffn, w_router, w_exp_gate, w_exp_up, w_exp_down, g_ple, w_ple_gate, w_ple_proj):
    assert g_mix.shape[0] == 1, "single-layer trunk"
    wts = _prepare_weights(rel_bias, g_mix[0], w_in[0], q_gain[0], k_gain[0], ssm_a_re[0], ssm_a_im[0],
                           ssm_log_step[0], ssm_b_re[0], ssm_b_im[0], ssm_c_re[0], ssm_c_im[0], ssm_d[0],
                           w_glu[0], w_br_a[0], w_br_b[0], w_out[0], g_ffn[0], w_router[0], g_ple[0],
                           w_ple_gate[0], w_ple_proj[0])
    xs, ps = (x_prompt, x_sample), (p_prompt[0], p_sample[0])
    n_toks = [x.shape[0] * x.shape[1] for x in xs]
    mixed, row0 = [], 0
    h_all = jnp.zeros((sum(n_toks), x_prompt.shape[-1] // 2), jnp.int32)
    for x, n in zip(xs, n_toks):
        x1, h_all, logits = _mixer(x, wts, h_all, row0)
        mixed.append((x1, logits))
        row0 += n
    moe = _expert_choice(h_all, n_toks, [m[1] for m in mixed], wts['n_exp'],
                         w_exp_gate[0], w_exp_up[0], w_exp_down[0])
    outs = []
    for x, p, (x1, _), y in zip(xs, ps, mixed, moe):
        out = final_embed(x1, y, p.reshape(-1, p.shape[-1]), wts['g_ple'], wts['w_ple_gate'], wts['w_ple_proj'])
        outs.append(out.reshape(x.shape))
    return tuple(outs)
```

```python
import functools
import math

import jax
import jax.numpy as jnp
import numpy as np
from jax import lax
from jax.experimental import pallas as pl
from jax.experimental.pallas import tpu as pltpu
from jax.experimental.pallas import tpu_sc as plsc

F32 = jnp.float32
BF16 = jnp.bfloat16

LANES = 128
HEAD_DIM = 128
HEADS_PER_GROUP = 4
ATT_PATTERNS = ((128, 1), (512, 4), (2048, 16))
ATT_SIDE = 64
N_BUCKETS = 32
MAX_DISTANCE = 1024
CAPACITY_FACTOR = 2
EPS = 1e-6
NEG = -1e30
SSM_LANE_GROUPS = 8
V7X_VMEM_LIMIT = 56 * 1024 * 1024

assert all(w // (2 * d) == ATT_SIDE for w, d in ATT_PATTERNS)


def _cparams(semantics, vmem=V7X_VMEM_LIMIT):
    return pltpu.CompilerParams(dimension_semantics=("arbitrary",) * len(semantics), vmem_limit_bytes=vmem)


def _sigmoid(x):
    return 1.0 / (1.0 + jnp.exp(-x))


def _rms_kernel(x_ref, g_ref, *refs, dils):
    o_refs, y_ref = refs[:-1], refs[-1]
    x = x_ref[...]
    y = x * lax.rsqrt(jnp.mean(x * x, axis=-1, keepdims=True) + EPS) * g_ref[...]
    tm = x.shape[0]
    n_chunks = y_ref.shape[0]
    for c in range(n_chunks):
        y_ref[c] = y[:, c * LANES:(c + 1) * LANES]
    for o_ref, dil in zip(o_refs, dils):
        if dil == 1:
            o_ref[...] = y.astype(o_ref.dtype)
        else:
            for r in range(dil):
                rows = [y_ref[c, pl.ds(r, tm // dil, stride=dil), :] for c in range(n_chunks)]
                o_ref[r] = jnp.concatenate(rows, axis=1).astype(o_ref.dtype)


def rms_bf16(x, g, dils, tm=1024):
    batch, seq, d = x.shape
    out_specs, out_shape = [], []
    for dil in dils:
        if dil == 1:
            out_specs.append(pl.BlockSpec((None, tm, d), lambda b, i: (b, i, 0)))
            out_shape.append(jax.ShapeDtypeStruct((batch, seq, d), BF16))
        else:
            out_specs.append(pl.BlockSpec((None, dil, tm // dil, d), lambda b, i: (b, 0, i, 0)))
            out_shape.append(jax.ShapeDtypeStruct((batch, dil, seq // dil, d), BF16))
    outs = pl.pallas_call(
        functools.partial(_rms_kernel, dils=tuple(dils)),
        grid=(batch, seq // tm),
        in_specs=[pl.BlockSpec((None, tm, d), lambda b, i: (b, i, 0)), pl.BlockSpec((1, d), lambda b, i: (0, 0))],
        out_specs=out_specs,
        out_shape=out_shape,
        scratch_shapes=[pltpu.VMEM((d // LANES, tm, LANES), F32)],
        compiler_params=_cparams(("parallel", "parallel")),
        name="rms_bf16",
    )(x, g.reshape(1, d).astype(F32))
    return [o.reshape(batch * seq, d) for o in outs]


def _head_rms(acc, gain):
    outs = []
    for h in range(acc.shape[1] // HEAD_DIM):
        t = acc[:, h * HEAD_DIM:(h + 1) * HEAD_DIM]
        outs.append(t * lax.rsqrt(jnp.mean(t * t, axis=-1, keepdims=True) + EPS))
    return jnp.concatenate(outs, axis=1) * gain


def _qkv_kernel(x_ref, w_ref, g_ref, o_ref):
    x = x_ref[...]
    tn = o_ref.shape[1] // 3
    for j in range(3):
        acc = jnp.dot(x, w_ref[:, j * tn:(j + 1) * tn], preferred_element_type=F32)
        if j < 2:
            acc = _head_rms(acc, g_ref[j])
        o_ref[:, j * tn:(j + 1) * tn] = acc.astype(o_ref.dtype)


def qkv_proj(hn, w, gains, name, tm=1024):
    n, d_model = hn.shape
    tn = HEADS_PER_GROUP * HEAD_DIM
    return pl.pallas_call(
        _qkv_kernel,
        grid=(n // tm,),
        in_specs=[
            pl.BlockSpec((tm, d_model), lambda i: (i, 0)),
            pl.BlockSpec((d_model, 3 * tn), lambda i: (0, 0), pipeline_mode=pl.Buffered(1)),
            pl.BlockSpec((3, 1, tn), lambda i: (0, 0, 0), pipeline_mode=pl.Buffered(1)),
        ],
        out_specs=pl.BlockSpec((tm, 3 * tn), lambda i: (i, 0)),
        out_shape=jax.ShapeDtypeStruct((n, 3 * tn), BF16),
        compiler_params=_cparams(("parallel",)),
        name=name,
    )(hn, w, gains)


def _proj_kernel(x_ref, w_ref, o_ref, *, act):
    acc = jnp.dot(x_ref[...], w_ref[...], preferred_element_type=F32)
    if act:
        acc = _sigmoid(acc)
    o_ref[...] = acc.astype(o_ref.dtype)


def plain_proj(hn, w, out_dtype, act, name, tm=1024, tn=1024):
    n, d_model = hn.shape
    width = w.shape[1]
    tn = min(tn, width)
    return pl.pallas_call(
        functools.partial(_proj_kernel, act=act),
        grid=(width // tn, n // tm),
        in_specs=[
            pl.BlockSpec((tm, d_model), lambda j, i: (i, 0)),
            pl.BlockSpec((d_model, tn), lambda j, i: (0, j)),
        ],
        out_specs=pl.BlockSpec((tm, tn), lambda j, i: (i, j)),
        out_shape=jax.ShapeDtypeStruct((n, width), out_dtype),
        compiler_params=_cparams(("arbitrary", "parallel")),
        name=name,
    )(hn, w)


def _t5_buckets(rel):
    half = N_BUCKETS // 2
    max_exact = half // 2
    n = np.abs(rel)
    large = max_exact + (np.log(np.maximum(n, 1) / max_exact) / math.log(MAX_DISTANCE / max_exact)
                         * (half - max_exact)).astype(np.int32)
    large = np.minimum(large, half - 1)
    return ((rel > 0).astype(np.int32) * half + np.where(n < max_exact, n, large)).astype(np.int32)


def _attn_geometry(l):
    tqb = min(2 * ATT_SIDE, l)
    win = min(4 * ATT_SIDE, l)
    return tqb, win


def attn_bias_tables(rel_bias_g, dil, l):
    tqb, win = _attn_geometry(l)
    i = np.arange(tqb)[:, None]
    j = np.arange(win)[None, :]
    rel = np.stack([j - i - ATT_SIDE * shift for shift in range(3)])
    valid = np.abs(rel) <= ATT_SIDE
    onehot = (_t5_buckets(dil * rel)[..., None] == np.arange(N_BUCKETS)).astype(np.int8)
    b = jnp.einsum('sijb,bh->shij', jnp.asarray(onehot).astype(F32), rel_bias_g.astype(F32),
                   precision=lax.Precision.HIGHEST)
    return jnp.where(valid[:, None], b, NEG)


def _attn_kernel(q_ref, k_ref, v_ref, bias_ref, o_ref, lse_ref, *, l, dil, tql):
    tqb, win = _attn_geometry(l)
    qi = pl.program_id(2)
    for r in range(dil):
        for c in range(tql // tqb):
            q0 = qi * tql + c * tqb
            start = pl.multiple_of(jnp.clip(q0 - ATT_SIDE, 0, l - win), ATT_SIDE)
            shift = (q0 - start) // ATT_SIDE
            q = q_ref[r, c * tqb:(c + 1) * tqb, :]
            kw = k_ref[r, pl.ds(start, win), :]
            vw = v_ref[r, pl.ds(start, win), :]
            logits = lax.dot_general(q, kw, (((1,), (1,)), ((), ())), preferred_element_type=F32)
            logits = logits + bias_ref[shift]
            m = jnp.max(logits, axis=-1, keepdims=True)
            p = jnp.exp(logits - m)
            s = jnp.sum(p, axis=-1, keepdims=True)
            o = jnp.dot(p.astype(vw.dtype), vw, preferred_element_type=F32)
            if dil == 1:
                rows = pl.ds(c * tqb, tqb)
            else:
                rows = pl.ds(c * tqb * dil + r, tqb, stride=dil)
            o_ref[rows, :] = o / s
            lse_ref[rows, :] = jnp.broadcast_to(m + jnp.log(s), (tqb, HEAD_DIM))


def dilated_attention(qkv, bias, batch, seq, dil):
    l = seq // dil
    tqb, win = _attn_geometry(l)
    tql = min(l, max(tqb, 8 * tqb // dil))
    tok = tql * dil
    hw = HEADS_PER_GROUP * HEAD_DIM
    x = qkv.reshape(batch, dil, l, 3 * hw)
    out_sds = jax.ShapeDtypeStruct((batch, seq, hw), F32)
    out_spec = pl.BlockSpec((None, tok, HEAD_DIM), lambda b, h, i: (b, i, h))
    o, lse = pl.pallas_call(
        functools.partial(_attn_kernel, l=l, dil=dil, tql=tql),
        grid=(batch, HEADS_PER_GROUP, l // tql),
        in_specs=[
            pl.BlockSpec((None, dil, tql, HEAD_DIM), lambda b, h, i: (b, 0, i, h)),
            pl.BlockSpec((None, dil, l, HEAD_DIM), lambda b, h, i: (b, 0, 0, HEADS_PER_GROUP + h)),
            pl.BlockSpec((None, dil, l, HEAD_DIM), lambda b, h, i: (b, 0, 0, 2 * HEADS_PER_GROUP + h)),
            pl.BlockSpec((3, None, tqb, win), lambda b, h, i: (0, h, 0, 0)),
        ],
        out_specs=[out_spec, out_spec],
        out_shape=[out_sds, out_sds],
        compiler_params=_cparams(("parallel", "parallel", "parallel")),
        name=f"dilated_attn_d{dil}",
    )(x, x, x, bias)
    return o.reshape(batch * seq, hw), lse.reshape(batch * seq, hw)


def ssm_discretise(a_re, a_im, log_step, b_re, b_im, c_re, c_im):
    lam = lax.complex(a_re.astype(F32), a_im.astype(F32))
    step = jnp.exp(log_step.astype(F32))[..., None]
    lam_bar = jnp.exp(lam * step)
    b_bar = ((lam_bar - 1.0) / lam)[..., None] * lax.complex(b_re.astype(F32), b_im.astype(F32))
    c = lax.complex(c_re.astype(F32), c_im.astype(F32))
    return lam_bar, b_bar, c


def _block_diag(m):
    _, nj, lg, a, b = m.shape
    return jnp.einsum('djgab,gk->djgakb', m, jnp.eye(lg, dtype=F32)).reshape(2, nj, lg * a, lg * b)


def ssm_block_weights(lam_bar, b_bar, c):
    _, g, p, h = b_bar.shape
    lg = SSM_LANE_GROUPS
    nj = g // lg
    lam = lam_bar[..., None]
    ct = jnp.swapaxes(c, -1, -2)

    def grp(m):
        return m.reshape((2, nj, lg) + m.shape[2:])

    def state_in(bmat):
        t = grp(jnp.swapaxes(bmat, -1, -2))
        return jnp.concatenate([_block_diag(t.real), _block_diag(t.imag)], axis=-1)

    def state_out(cmat):
        t = grp(cmat)
        return jnp.concatenate([_block_diag(t.real), _block_diag(-t.imag)], axis=-2)

    def direct(k):
        m = jnp.einsum('dgpc,dgph->dghc', ct * lam ** k, b_bar).real
        return _block_diag(grp(m))

    first_in, second_in = state_in(lam * b_bar), state_in(b_bar)
    first_out, second_out = state_out(ct * lam), state_out(ct * lam ** 2)
    zero = jnp.zeros_like(direct(0))
    sel = jnp.array([0, 1]).reshape(2, 1, 1, 1)
    wb = jnp.where(sel == 0, jnp.concatenate([first_in, second_in], axis=-2),
                   jnp.concatenate([second_in, first_in], axis=-2))
    wx = jnp.where(sel == 0, jnp.concatenate([first_out, second_out], axis=-1),
                   jnp.concatenate([second_out, first_out], axis=-1))
    u_first = jnp.concatenate([direct(0), direct(1)], axis=-1)
    u_second = jnp.concatenate([zero, direct(0)], axis=-1)
    swap = lambda m: jnp.concatenate([m[..., 128:], m[..., :128]], axis=-1)
    wu = jnp.where(sel == 0, jnp.concatenate([u_first, u_second], axis=-2),
                   jnp.concatenate([swap(u_second), swap(u_first)], axis=-2))
    l2 = (lam_bar ** 2).reshape(2, nj, 1, 1, lg * p)
    lam2 = jnp.broadcast_to(jnp.concatenate([l2.real, l2.imag], axis=2), (2, nj, 2, 8, lg * p))
    return wb.astype(BF16), wx.astype(BF16), wu.astype(BF16), lam2.astype(F32)


def _s5_kernel(u_ref, wb_ref, wx_ref, wu_ref, lam_ref, x0_ref, *refs, tt, emit_y):
    n_out = 2 if emit_y else 1
    outs, scratch = refs[:n_out], list(refs[n_out:])
    xend_ref = outs[-1]
    parts = []
    for _ in range(S5_PARTS):
        parts.append([scratch.pop(0) for _ in range(4 if emit_y else 3)])
    st_ref = scratch.pop(0)
    direction = pl.program_id(0)
    ti = pl.program_id(2)
    nt = pl.num_programs(2)
    vb, _, lanes = u_ref.shape
    k = lam_ref.shape[-1]
    part_t = tt // S5_PARTS
    n = part_t // 2

    @pl.when(ti == 0)
    def _():
        st_ref[...] = x0_ref[...]

    lr = lam_ref[0]
    li = lam_ref[1]
    state = (st_ref[0], st_ref[1])
    for order, part in enumerate(parts):
        lhs_e, lhs_o, bu_ref = part[:3]
        t0 = jnp.where(direction == 0, order, S5_PARTS - 1 - order) * part_t
        for b in range(vb):
            lhs_e[pl.ds(b, n, stride=vb), :] = u_ref[b, pl.ds(t0, n, stride=2), :]
            lhs_o[pl.ds(b, n, stride=vb), :] = u_ref[b, pl.ds(t0 + 1, n, stride=2), :]
        lhs = jnp.concatenate([lhs_e[...], lhs_o[...]], axis=1).astype(BF16)
        bu_ref[...] = jnp.dot(lhs, wb_ref[...], preferred_element_type=F32)
        if emit_y:
            xs_ref = part[3]
            in_row = pl.multiple_of(jnp.where(direction == 0, 0, n * vb), vb)
            xs_ref[pl.ds(in_row, vb), :] = jnp.concatenate(state, axis=1)

        def step(s, carry, part=part):
            xr, xi = carry
            t = jnp.where(direction == 0, s, n - 1 - s)
            b = part[2][pl.ds(pl.multiple_of(t * vb, vb), vb), :]
            nr = lr * xr - li * xi + b[:, :k]
            ni = lr * xi + li * xr + b[:, k:]
            if emit_y:
                out_row = pl.multiple_of(jnp.where(direction == 0, t + 1, t) * vb, vb)
                part[3][pl.ds(out_row, vb), :] = jnp.concatenate([nr, ni], axis=1)
            return nr, ni

        state = lax.fori_loop(0, n, step, state, unroll=True)
        if emit_y:
            y_ref, ysc_e, ysc_o = outs[0], scratch[0], scratch[1]
            base = pl.multiple_of(jnp.where(direction == 0, 0, vb), vb)
            xs = xs_ref[pl.ds(base, n * vb), :].astype(BF16)
            y = (jnp.dot(xs, wx_ref[...], preferred_element_type=F32)
                 + jnp.dot(lhs, wu_ref[...], preferred_element_type=F32))
            ysc_e[...] = y[:, :lanes]
            ysc_o[...] = y[:, lanes:]
            for b in range(vb):
                y_ref[b, pl.ds(t0, n, stride=2), :] = ysc_e[pl.ds(b, n, stride=vb), :]
                y_ref[b, pl.ds(t0 + 1, n, stride=2), :] = ysc_o[pl.ds(b, n, stride=vb), :]
    st_ref[0] = state[0]
    st_ref[1] = state[1]

    @pl.when(ti == nt - 1)
    def _():
        xend_ref[0] = state[0]
        xend_ref[1] = state[1]


S5_PARTS = 2


def s5_scan(u, wts, x0, vbatch, emit_y, tt=512):
    width = u.shape[1]
    vseq = u.shape[0] // vbatch
    wb, wx, wu, lam2 = wts['wb'], wts['wx'], wts['wu'], wts['lam2']
    nj = wb.shape[1]
    k = lam2.shape[-1]
    lanes = width // nj
    nt = vseq // tt
    n = tt // S5_PARTS // 2

    def tmap(d, t):
        return jnp.where(d == 0, t, nt - 1 - t)

    state_spec = pl.BlockSpec((None, None, 2, vbatch, k), lambda d, j, t: (d, j, 0, 0, 0))
    state_sds = jax.ShapeDtypeStruct((2, nj, 2, vbatch, k), F32)
    out_specs, out_shape = [state_spec], [state_sds]
    half = pltpu.VMEM((n * vbatch, lanes), F32)
    part = [half, half, pltpu.VMEM((n * vbatch, 2 * k), F32)]
    if emit_y:
        part.append(pltpu.VMEM(((n + 1) * vbatch, 2 * k), F32))
    scratch = part * S5_PARTS + [pltpu.VMEM((2, vbatch, k), F32)]
    if emit_y:
        out_specs.insert(0, pl.BlockSpec((None, vbatch, tt, lanes), lambda d, j, t: (d, 0, tmap(d, t), j)))
        out_shape.insert(0, jax.ShapeDtypeStruct((2, vbatch, vseq, width), F32))
        scratch += [half, half]
    outs = pl.pallas_call(
        functools.partial(_s5_kernel, tt=tt, emit_y=emit_y),
        grid=(2, nj, nt),
        in_specs=[
            pl.BlockSpec((vbatch, tt, lanes), lambda d, j, t: (0, tmap(d, t), j)),
            pl.BlockSpec((None, None, 2 * lanes, 2 * k), lambda d, j, t: (d, j, 0, 0)),
            pl.BlockSpec((None, None, 2 * k, 2 * lanes), lambda d, j, t: (d, j, 0, 0)),
            pl.BlockSpec((None, None, 2 * lanes, 2 * lanes), lambda d, j, t: (d, j, 0, 0)),
            state_spec, state_spec,
        ],
        out_specs=out_specs,
        out_shape=out_shape,
        scratch_shapes=scratch,
        compiler_params=_cparams(("parallel", "parallel", "arbitrary")),
        name="s5_scan" if emit_y else "s5_states",
    )(u.reshape(vbatch, vseq, width), wb, wx, wu, lam2, x0)
    if emit_y:
        return outs[0].reshape(2, vbatch * vseq, width), outs[1]
    return None, outs[0]


def s5_segment_carries(xend, lam_bar, seg_len, batch, nseg):
    _, nj, _, _, k = xend.shape
    lam_pow = (lam_bar ** seg_len).reshape(2, nj, 1, k)
    e = lax.complex(xend[:, :, 0], xend[:, :, 1]).reshape(2, nj, batch, nseg, k)
    fwd = [jnp.zeros((nj, batch, k), jnp.complex64)]
    for s in range(1, nseg):
        fwd.append(lam_pow[0] * fwd[-1] + e[0, :, :, s - 1])
    bwd = [jnp.zeros((nj, batch, k), jnp.complex64)]
    for s in range(nseg - 2, -1, -1):
        bwd.append(lam_pow[1] * bwd[-1] + e[1, :, :, s + 1])
    bwd = bwd[::-1]
    x_in = jnp.stack([jnp.stack(fwd, axis=2), jnp.stack(bwd, axis=2)])
    x_in = x_in.reshape(2, nj, batch * nseg, k)
    return jnp.stack([x_in.real, x_in.imag], axis=2).astype(F32)


def _gelu_tanh(x):
    return 0.5 * x * (1.0 + jnp.tanh(math.sqrt(2.0 / math.pi) * (x + 0.044715 * x * x * x)))


def _glu_kernel(yf_ref, yb_ref, u_ref, d_ref, wa_ref, wb_ref, o_ref):
    y = _gelu_tanh(yf_ref[...] + yb_ref[...] + d_ref[...] * u_ref[...]).astype(BF16)
    za = jnp.dot(y, wa_ref[...], preferred_element_type=F32)
    zb = jnp.dot(y, wb_ref[...], preferred_element_type=F32)
    o_ref[...] = (za * _sigmoid(zb)).astype(o_ref.dtype)


def ssm_glu(y, u, d_skip, w_glu, tm=512):
    n, width = u.shape
    wa = w_glu[:, :width]
    wb = w_glu[:, width:]
    const = lambda i: (0, 0)
    return pl.pallas_call(
        _glu_kernel,
        grid=(n // tm,),
        in_specs=[
            pl.BlockSpec((None, tm, width), lambda i: (0, i, 0)),
            pl.BlockSpec((None, tm, width), lambda i: (1, i, 0)),
            pl.BlockSpec((tm, width), lambda i: (i, 0)),
            pl.BlockSpec((1, width), const),
            pl.BlockSpec((width, width), const),
            pl.BlockSpec((width, width), const),
        ],
        out_specs=pl.BlockSpec((tm, width), lambda i: (i, 0)),
        out_shape=jax.ShapeDtypeStruct((n, width), BF16),
        compiler_params=_cparams(("parallel",)),
        name="ssm_glu",
    )(y, y, u, d_skip.reshape(1, width).astype(F32), wa, wb)


def _split_bf16(x):
    hi = x.astype(BF16)
    lo = (x - hi.astype(F32)).astype(BF16)
    return hi, lo


def _pack_bf16_pair(a, b):
    wa = pltpu.bitcast(a.astype(BF16).astype(F32), jnp.int32)
    wb = pltpu.bitcast(b.astype(BF16).astype(F32), jnp.int32)
    return wa | lax.shift_right_logical(wb, 16)


def _unpack_bf16_pair(w):
    hi = pltpu.bitcast(w & jnp.int32(-65536), F32).astype(BF16)
    lo = pltpu.bitcast(lax.shift_left(w, 16), F32).astype(BF16)
    return hi, lo


def _mixer_out_kernel(o0, o1, o2, l0, l1, l2, ob_ref, gate_ref, x_ref, wa_ref, wb_ref, wo_ref,
                      g_ref, wr_ref, hbuf_ref, x1_ref, h_ref, lg_ref):
    del hbuf_ref
    d_model = x_ref.shape[1]
    la, lb, lc = l0[...], l1[...], l2[...]
    m = jnp.maximum(jnp.maximum(la, lb), lc)
    ea, eb, ec = jnp.exp(la - m), jnp.exp(lb - m), jnp.exp(lc - m)
    oa = (ea * o0[...] + eb * o1[...] + ec * o2[...]) / (ea + eb + ec)
    pa = jnp.dot(oa.astype(BF16), wa_ref[...], preferred_element_type=F32)
    pb = jnp.dot(ob_ref[...], wb_ref[...], preferred_element_type=F32)
    gates = gate_ref[...].astype(F32)
    merged = gates[:, :d_model] * pa + gates[:, d_model:] * pb
    x1 = x_ref[...] + jnp.dot(merged.astype(BF16), wo_ref[...], preferred_element_type=F32)
    x1_ref[...] = x1
    h = x1 * lax.rsqrt(jnp.mean(x1 * x1, axis=-1, keepdims=True) + EPS) * g_ref[...]
    h_ref[...] = _pack_bf16_pair(h[:, :d_model // 2], h[:, d_model // 2:])
    hh, hl = _split_bf16(h)
    er = lg_ref.shape[1]
    both = jnp.dot(hh, wr_ref[...], preferred_element_type=F32)
    lg_ref[...] = both[:, :er] + both[:, er:] + jnp.dot(hl, wr_ref[:, :er], preferred_element_type=F32)


def mixer_out(o_list, lse_list, o_b, gates, x, w_br_a, w_br_b, w_out, g_ffn, wr_split, h_buf, h_row0, tm=256):
    n, d_model = x.shape
    row = lambda w: pl.BlockSpec((tm, w), lambda i: (i, 0))
    full = lambda a: pl.BlockSpec(a.shape, lambda i: (0,) * a.ndim, pipeline_mode=pl.Buffered(1))
    g2 = g_ffn.reshape(1, d_model).astype(F32)
    er = wr_split.shape[1] // 2
    blk0 = h_row0 // tm
    in_specs = [row(o.shape[1]) for o in o_list] + [row(l.shape[1]) for l in lse_list] + [
        row(o_b.shape[1]), row(gates.shape[1]), row(d_model),
        full(w_br_a), full(w_br_b), full(w_out), full(g2), full(wr_split), pl.BlockSpec(memory_space=pl.ANY)]
    args = [*o_list, *lse_list, o_b, gates, x, w_br_a, w_br_b, w_out, g2, wr_split, h_buf]
    x1, h, logits = pl.pallas_call(
        _mixer_out_kernel,
        grid=(n // tm,),
        in_specs=in_specs,
        out_specs=[row(d_model), pl.BlockSpec((tm, d_model // 2), lambda i: (blk0 + i, 0)), row(er)],
        out_shape=[jax.ShapeDtypeStruct((n, d_model), F32), jax.ShapeDtypeStruct(h_buf.shape, h_buf.dtype),
                   jax.ShapeDtypeStruct((n, er), F32)],
        input_output_aliases={len(args) - 1: 1},
        compiler_params=_cparams(("parallel",)),
        name="mixer_out",
    )(*args)
    return x1, h, logits


def _expert_up_kernel(x_ref, wg_ref, wu_ref, o_ref, wg_s, wu_s):
    @pl.when(pl.program_id(2) == 0)
    def _():
        wg_s[...] = wg_ref[...].astype(BF16)
        wu_s[...] = wu_ref[...].astype(BF16)

    x_hi, x_lo = _unpack_bf16_pair(x_ref[...])
    half = x_hi.shape[1]

    def proj(w_s):
        return (jnp.dot(x_hi, w_s[:half, :], preferred_element_type=F32)
                + jnp.dot(x_lo, w_s[half:, :], preferred_element_type=F32))

    a = proj(wg_s)
    b = proj(wu_s)
    o_ref[...] = (a * _sigmoid(a) * b).astype(o_ref.dtype)


def _expert_up_into_kernel(x_ref, wg_ref, wu_ref, buf_ref, o_ref, wg_s, wu_s):
    del buf_ref
    _expert_up_kernel(x_ref, wg_ref, wu_ref, o_ref, wg_s, wu_s)


def expert_up(xe, w_gate, w_up, e0, hid=None, tm=1024, tf=512):
    n_e, cap, _ = xe.shape
    n_exp, d_model, ff = w_gate.shape
    tm = math.gcd(tm, cap)
    w_spec = pl.BlockSpec((None, d_model, tf), lambda ei, f, m: (e0 + ei, 0, f))
    in_specs = [pl.BlockSpec((None, tm, d_model // 2), lambda ei, f, m: (ei, m, 0)), w_spec, w_spec]
    args = [xe, w_gate, w_up]
    if hid is not None:
        in_specs.append(pl.BlockSpec(memory_space=pl.ANY))
        args.append(hid)
    return pl.pallas_call(
        _expert_up_kernel if hid is None else _expert_up_into_kernel,
        grid=(n_e, ff // tf, cap // tm),
        in_specs=in_specs,
        out_specs=pl.BlockSpec((None, tm, tf), lambda ei, f, m: (e0 + ei, m, f)),
        out_shape=jax.ShapeDtypeStruct((n_exp, cap, ff), BF16),
        scratch_shapes=[pltpu.VMEM((d_model, tf), BF16), pltpu.VMEM((d_model, tf), BF16)],
        input_output_aliases={} if hid is None else {3: 0},
        compiler_params=_cparams(("parallel", "parallel", "arbitrary")),
        name="expert_up",
    )(*args)


def _expert_down_kernel(h_ref, w_ref, o_ref, w_s):
    @pl.when(pl.program_id(2) == 0)
    def _():
        w_s[...] = w_ref[...].astype(BF16)

    acc = jnp.dot(h_ref[...], w_s[...], preferred_element_type=F32)
    half = acc.shape[1] // 2
    o_ref[...] = _pack_bf16_pair(acc[:, :half], acc[:, half:])


def expert_down(hid, w_down, tm=512, tn=1024):
    e, cap, ff = hid.shape
    d_model = w_down.shape[2]
    tm = math.gcd(tm, cap)
    tn = min(tn, d_model)
    return pl.pallas_call(
        _expert_down_kernel,
        grid=(e, d_model // tn, cap // tm),
        in_specs=[
            pl.BlockSpec((None, tm, ff), lambda ei, n, m: (ei, m, 0)),
            pl.BlockSpec((None, ff, tn), lambda ei, n, m: (ei, 0, n)),
        ],
        out_specs=pl.BlockSpec((None, tm, tn // 2), lambda ei, n, m: (ei, m, n)),
        out_shape=jax.ShapeDtypeStruct((e, cap, d_model // 2), jnp.int32),
        scratch_shapes=[pltpu.VMEM((ff, tn), BF16)],
        compiler_params=_cparams(("parallel", "parallel", "arbitrary")),
        name="expert_down",
    )(hid, w_down)


ROUTER_ROWS = 512


def _router_kernel(lg_ref, o_ref, aff_ref, *, n_exp, cap):
    n, lanes = lg_ref.shape
    t = ROUTER_ROWS
    nt = n // t
    real = lax.broadcasted_iota(jnp.int32, (1, lanes), 1) < n_exp

    def rows(i):
        return pl.ds(pl.multiple_of(i * t, t), t)

    def softmax_tile(i, carry):
        x = jnp.where(real, lg_ref[rows(i), :], -jnp.inf)
        e = jnp.exp(x - jnp.max(x, axis=-1, keepdims=True))
        aff_ref[rows(i), :] = e / jnp.sum(e, axis=-1, keepdims=True)
        return carry

    lax.fori_loop(0, nt, softmax_tile, 0)

    def count(pred):
        def body(i, acc):
            hit = pred(pltpu.bitcast(aff_ref[rows(i), :], jnp.int32)).astype(jnp.int32)
            return acc + jnp.sum(hit.reshape(t // 8, 8, lanes), axis=0)
        acc = lax.fori_loop(0, nt, body, jnp.zeros((8, lanes), jnp.int32))
        return jnp.sum(acc, axis=0, keepdims=True)

    thr = jnp.zeros((1, lanes), jnp.int32)
    for bit in range(30, -1, -1):
        cand = thr | jnp.int32(1 << bit)
        thr = jnp.where(count(lambda k, cand=cand: k >= cand) >= cap, cand, thr)
    ties_needed = (cap - count(lambda k: k > thr)).astype(F32)

    r_iota = lax.broadcasted_iota(jnp.int32, (t, t), 0)
    c_iota = lax.broadcasted_iota(jnp.int32, (t, t), 1)
    prefix = (c_iota <= r_iota).astype(BF16)

    def select_tile(i, ties_before):
        aff = aff_ref[rows(i), :]
        key = pltpu.bitcast(aff, jnp.int32)
        tie = (key == thr).astype(BF16)
        tie_rank = ties_before + jnp.dot(prefix, tie, preferred_element_type=F32)
        chosen = (key > thr) | ((key == thr) & (tie_rank <= ties_needed))
        chosen = jnp.where(real & chosen, 1.0, 0.0)
        o_ref[rows(i), :] = jnp.where(real, aff, 0.0) + pltpu.roll(chosen, n_exp, 1)
        return tie_rank[t - 1:t, :]

    lax.fori_loop(0, nt, select_tile, jnp.zeros((1, lanes), F32))


def route_expert_choice(logits, n_exp, cap):
    n, lanes = logits.shape
    whole = pl.BlockSpec(memory_space=pltpu.VMEM)
    routed = pl.pallas_call(
        functools.partial(_router_kernel, n_exp=n_exp, cap=cap),
        in_specs=[whole],
        out_specs=whole,
        out_shape=jax.ShapeDtypeStruct((n, lanes), F32),
        scratch_shapes=[pltpu.VMEM((n, lanes), F32)],
        compiler_params=pltpu.CompilerParams(vmem_limit_bytes=V7X_VMEM_LIMIT),
        name="route_expert_choice",
    )(logits)
    chosen = routed[:, n_exp:2 * n_exp].T.reshape(-1) > 0.5
    pos = jnp.nonzero(chosen, size=n_exp * cap, fill_value=0)[0].astype(jnp.int32)
    gate = routed[:, :n_exp].T.reshape(-1)[pos]
    return gate, pos % n


SC_GATHER_CHUNK = 16


def sc_gather_rows(table, idx):
    n_idx = idx.shape[0]
    width = table.shape[1]
    info = plsc.get_sparse_core_info()
    n_workers = info.num_cores * info.num_subcores
    per_worker = n_idx // n_workers
    ch = SC_GATHER_CHUNK
    assert n_idx % n_workers == 0 and per_worker % (2 * ch) == 0
    mesh = plsc.VectorSubcoreMesh(core_axis_name="c", subcore_axis_name="s")

    @functools.partial(
        pl.kernel, mesh=mesh,
        out_type=jax.ShapeDtypeStruct((n_idx, width), table.dtype),
        scratch_types=[pltpu.VMEM((per_worker,), jnp.int32), pltpu.VMEM((ch, width), table.dtype),
                       pltpu.VMEM((ch, width), table.dtype), pltpu.SemaphoreType.DMA, pltpu.SemaphoreType.DMA],
        name="sc_gather_rows",
    )
    def gather(table_hbm, idx_hbm, out_hbm, idx_v, rows_a, rows_b, sem_a, sem_b):
        base = (lax.axis_index("s") * info.num_cores + lax.axis_index("c")) * per_worker
        pltpu.sync_copy(idx_hbm.at[pl.ds(base, per_worker)], idx_v)

        @pl.loop(0, per_worker // (2 * ch))
        def _(c):
            off = c * 2 * ch
            copy_a = pltpu.async_copy(table_hbm.at[idx_v.at[pl.ds(off, ch)]], rows_a, sem_a)
            copy_b = pltpu.async_copy(table_hbm.at[idx_v.at[pl.ds(off + ch, ch)]], rows_b, sem_b)
            copy_a.wait()
            pltpu.sync_copy(rows_a, out_hbm.at[pl.ds(base + off, ch)])
            copy_b.wait()
            pltpu.sync_copy(rows_b, out_hbm.at[pl.ds(base + off + ch, ch)])

    return gather(table, idx)


COMBINE_TOKENS = 512
COMBINE_ROWS = 512


def _combine_kernel(tile_ref, chunk_ref, first_ref, valid_ref, rows_ref, tok_ref, g_ref, o_ref, *, tn):
    p = pl.program_id(0)

    @pl.when(first_ref[p] == 1)
    def _():
        o_ref[...] = jnp.zeros_like(o_ref)

    @pl.when(valid_ref[p] == 1)
    def _():
        t, ch = o_ref.shape[0], tok_ref.shape[1]
        token = tile_ref[p] * t + lax.broadcasted_iota(jnp.int32, (t, ch), 0)
        weights = jnp.where(tok_ref[...] == token, g_ref[...], 0.0).astype(BF16)
        hi, lo = _unpack_bf16_pair(rows_ref[...])
        acc_hi = jnp.dot(weights, hi, preferred_element_type=F32)
        acc_lo = jnp.dot(weights, lo, preferred_element_type=F32)
        half = tn // 2
        for n in range(o_ref.shape[1] // tn):
            o_ref[:, n * tn:n * tn + half] += acc_hi[:, n * half:(n + 1) * half]
            o_ref[:, n * tn + half:(n + 1) * tn] += acc_lo[:, n * half:(n + 1) * half]


def combine_expert_outputs(rows, tok_sorted, gate_sorted, n_tok, tn):
    m, half_d = rows.shape
    t, ch = COMBINE_TOKENS, COMBINE_ROWS
    n_tiles, n_chunks = n_tok // t, m // ch
    n_pairs = n_tiles + n_chunks
    bounds = jnp.searchsorted(tok_sorted, jnp.arange(n_tiles + 1, dtype=jnp.int32) * t, side='left').astype(jnp.int32)
    start, end = bounds[:-1], bounds[1:]
    c_lo = jnp.minimum(start // ch, n_chunks - 1)
    c_hi = jnp.where(end > start, (end - 1) // ch, c_lo)
    count = c_hi - c_lo + 1
    offs = jnp.cumsum(count)
    p = jnp.arange(n_pairs, dtype=jnp.int32)
    tile = jnp.minimum(jnp.searchsorted(offs, p, side='right'), n_tiles - 1).astype(jnp.int32)
    first_p = (offs - count)[tile]
    valid = p < offs[-1]
    chunk = jnp.where(valid, c_lo[tile] + p - first_p, c_hi[-1]).astype(jnp.int32)
    first = (valid & (p == first_p)).astype(jnp.int32)
    grid_spec = pltpu.PrefetchScalarGridSpec(
        num_scalar_prefetch=4,
        grid=(n_pairs,),
        in_specs=[
            pl.BlockSpec((ch, half_d), lambda i, tl, ck, fr, vl: (ck[i], 0)),
            pl.BlockSpec((1, ch), lambda i, tl, ck, fr, vl: (0, ck[i])),
            pl.BlockSpec((1, ch), lambda i, tl, ck, fr, vl: (0, ck[i])),
        ],
        out_specs=pl.BlockSpec((t, 2 * half_d), lambda i, tl, ck, fr, vl: (tl[i], 0)),
    )
    return pl.pallas_call(
        functools.partial(_combine_kernel, tn=tn),
        grid_spec=grid_spec,
        out_shape=jax.ShapeDtypeStruct((n_tok, 2 * half_d), F32),
        compiler_params=_cparams(("arbitrary",)),
        name="combine_expert_outputs",
    )(tile, chunk, first, valid.astype(jnp.int32), rows, tok_sorted.reshape(1, m), gate_sorted.reshape(1, m).astype(F32))


def _final_kernel(x1_ref, moe_ref, p_ref, g_ref, wg_ref, wp_ref, o_ref):
    x2 = x1_ref[...] + moe_ref[...]
    h = (x2 * lax.rsqrt(jnp.mean(x2 * x2, axis=-1, keepdims=True) + EPS) * g_ref[...]).astype(BF16)
    gate = _sigmoid(jnp.dot(h, wg_ref[...], preferred_element_type=F32))
    proj = jnp.dot(p_ref[...].astype(BF16), wp_ref[...], preferred_element_type=F32)
    o_ref[...] = x2 + gate * proj


def final_embed(x1, moe, p, g_ple, w_gate, w_proj, tm=512):
    n, d_model = x1.shape
    row = lambda w: pl.BlockSpec((tm, w), lambda i: (i, 0))
    full = lambda a: pl.BlockSpec(a.shape, lambda i: (0,) * a.ndim, pipeline_mode=pl.Buffered(1))
    g2 = g_ple.reshape(1, d_model).astype(F32)
    return pl.pallas_call(
        _final_kernel,
        grid=(n // tm,),
        in_specs=[row(d_model), row(d_model), row(p.shape[1]), full(g2), full(w_gate), full(w_proj)],
        out_specs=row(d_model),
        out_shape=jax.ShapeDtypeStruct((n, d_model), F32),
        compiler_params=_cparams(("parallel",)),
        name="final_embed",
    )(x1, moe, p, g2, w_gate, w_proj)


def _prepare_weights(rel_bias, g_mix, w_in, q_gain, k_gain, ssm_a_re, ssm_a_im, ssm_log_step,
                     ssm_b_re, ssm_b_im, ssm_c_re, ssm_c_im, ssm_d, w_glu, w_br_a, w_br_b, w_out,
                     g_ffn, w_router, g_ple, w_ple_gate, w_ple_proj):
    n_heads = q_gain.shape[0]
    att_w = n_heads * HEAD_DIM
    hw = HEADS_PER_GROUP * HEAD_DIM
    ssm_w = ssm_d.shape[0]
    w_in_bf = w_in.astype(BF16)
    w_qkv, gains = [], []
    qg = q_gain.astype(F32) * (HEAD_DIM ** -0.5)
    kg = k_gain.astype(F32)
    for gi in range(len(ATT_PATTERNS)):
        cols = [w_in_bf[:, s * att_w + gi * hw: s * att_w + (gi + 1) * hw] for s in range(3)]
        w_qkv.append(jnp.concatenate(cols, axis=1))
        hs = slice(gi * HEADS_PER_GROUP, (gi + 1) * HEADS_PER_GROUP)
        gains.append(jnp.stack([qg[hs].reshape(1, hw), kg[hs].reshape(1, hw), jnp.ones((1, hw), F32)]))
    lam_bar, b_bar, c = ssm_discretise(ssm_a_re, ssm_a_im, ssm_log_step, ssm_b_re, ssm_b_im, ssm_c_re, ssm_c_im)
    wb, wx, wu, lam2 = ssm_block_weights(lam_bar, b_bar, c)
    n_exp = w_router.shape[1]
    wr = jnp.pad(w_router.astype(F32), ((0, 0), (0, 128 - n_exp)))
    wr_split = jnp.concatenate(_split_bf16(wr), axis=1)
    return dict(
        rel_bias=rel_bias, bias_tables={}, g_mix=g_mix, w_qkv=w_qkv, gains=gains,
        w_u=w_in_bf[:, 3 * att_w:3 * att_w + ssm_w], w_gates=w_in_bf[:, 3 * att_w + ssm_w:],
        lam_bar=lam_bar, wb=wb, wx=wx, wu=wu, lam2=lam2, ssm_d=ssm_d, w_glu=w_glu.astype(BF16),
        w_br_a=w_br_a.astype(BF16), w_br_b=w_br_b.astype(BF16), w_out=w_out.astype(BF16),
        g_ffn=g_ffn, wr_split=wr_split, n_exp=n_exp, g_ple=g_ple,
        w_ple_gate=w_ple_gate.astype(BF16), w_ple_proj=w_ple_proj.astype(BF16))


def _ssm_branch(hn, wts, batch, seq):
    nseg = max(1, 8 // batch)
    vbatch = batch * nseg
    vseq = seq // nseg
    u = plain_proj(hn, wts['w_u'], F32, False, "u_proj")
    nj, k = wts['lam2'].shape[1], wts['lam2'].shape[-1]
    x0 = jnp.zeros((2, nj, 2, vbatch, k), F32)
    if nseg > 1:
        _, xend = s5_scan(u, wts, x0, vbatch, emit_y=False)
        x0 = s5_segment_carries(xend, wts['lam_bar'], vseq, batch, nseg)
    y, _ = s5_scan(u, wts, x0, vbatch, emit_y=True)
    return ssm_glu(y, u, wts['ssm_d'], wts['w_glu'])


MOE_LEAD_EXPERTS = 2


def _expert_choice(h_all, n_toks, logits_list, n_exp, w_exp_gate, w_exp_up, w_exp_down):
    d_model = w_exp_down.shape[2]
    half_d = h_all.shape[1]
    caps = [CAPACITY_FACTOR * n // n_exp for n in n_toks]
    cap_all = sum(caps)
    routes, picks, row0 = [], [], 0
    for n, logits, cap in zip(n_toks, logits_list, caps):
        gate_flat, idx_flat = route_expert_choice(logits, n_exp, cap)
        routes.append((gate_flat, idx_flat))
        picks.append(idx_flat.reshape(n_exp, cap) + row0)
        row0 += n
    picks = jnp.concatenate(picks, axis=1)
    hid, lead = None, min(MOE_LEAD_EXPERTS, n_exp)
    for e0, e1 in ((0, lead), (lead, n_exp)):
        if e1 > e0:
            xe = sc_gather_rows(h_all, picks[e0:e1].reshape(-1)).reshape(e1 - e0, cap_all, half_d)
            hid = expert_up(xe, w_exp_gate, w_exp_up, e0, hid)
    tn = min(1024, d_model)
    ye = expert_down(hid, w_exp_down, tn=tn).reshape(n_exp * cap_all, half_d)
    outs, slot0 = [], 0
    for n, (gate_flat, idx_flat), cap in zip(n_toks, routes, caps):
        order = jnp.argsort(idx_flat).astype(jnp.int32)
        rows = (order // cap) * cap_all + slot0 + order % cap
        outs.append(combine_expert_outputs(sc_gather_rows(ye, rows), idx_flat[order], gate_flat[order], n, tn))
        slot0 += cap
    return outs


def _mixer(x, wts, h_buf, h_row0):
    batch, seq, d_model = x.shape
    n = batch * seq
    xf = x.reshape(n, d_model)
    dils = [dil for _, dil in ATT_PATTERNS]
    hn_list = rms_bf16(x, wts['g_mix'], dils)
    hn = hn_list[dils.index(1)]
    o_list, lse_list = [], []
    for gi, dil in enumerate(dils):
        qkv = qkv_proj(hn_list[gi], wts['w_qkv'][gi], wts['gains'][gi], f"qkv_proj_d{dil}")
        hs = slice(gi * HEADS_PER_GROUP, (gi + 1) * HEADS_PER_GROUP)
        bias_key = (gi,) + _attn_geometry(seq // dil)
        if bias_key not in wts['bias_tables']:
            wts['bias_tables'][bias_key] = attn_bias_tables(wts['rel_bias'][:, hs], dil, seq // dil)
        bias = wts['bias_tables'][bias_key]
        o, lse = dilated_attention(qkv, bias, batch, seq, dil)
        o_list.append(o)
        lse_list.append(lse)
    o_b = _ssm_branch(hn, wts, batch, seq)
    gates = plain_proj(hn, wts['w_gates'], BF16, True, "gates_proj")
    return mixer_out(o_list, lse_list, o_b, gates, xf, wts['w_br_a'], wts['w_br_b'],
                     wts['w_out'], wts['g_ffn'], wts['wr_split'], h_buf, h_row0)


def kernel(x_prompt, x_sample, p_prompt, p_sample, rel_bias, g_mix, w_in, q_gain, k_gain, ssm_a_re, ssm_a_im, ssm_log_step, ssm_b_re, ssm_b_im, ssm_c_re, ssm_c_im, ssm_d, w_glu, w_br_a, w_br_b, w_out, g_ffn, w_router, w_exp_gate, w_exp_up, w_exp_down, g_ple, w_ple_gate, w_ple_proj):
    assert g_mix.shape[0] == 1, "single-layer trunk"
    wts = _prepare_weights(rel_bias, g_mix[0], w_in[0], q_gain[0], k_gain[0], ssm_a_re[0], ssm_a_im[0],
                           ssm_log_step[0], ssm_b_re[0], ssm_b_im[0], ssm_c_re[0], ssm_c_im[0], ssm_d[0],
                           w_glu[0], w_br_a[0], w_br_b[0], w_out[0], g_ffn[0], w_router[0], g_ple[0],
                           w_ple_gate[0], w_ple_proj[0])
    xs, ps = (x_prompt, x_sample), (p_prompt[0], p_sample[0])
    n_toks = [x.shape[0] * x.shape[1] for x in xs]
    mixed, row0 = [], 0
    h_all = jnp.zeros((sum(n_toks), x_prompt.shape[-1] // 2), jnp.int32)
    for x, n in zip(xs, n_toks):
        x1, h_all, logits = _mixer(x, wts, h_all, row0)
        mixed.append((x1, logits))
        row0 += n
    moe = _expert_choice(h_all, n_toks, [m[1] for m in mixed], wts['n_exp'],
                         w_exp_gate[0], w_exp_up[0], w_exp_down[0])
    outs = []
    for x, p, (x1, _), y in zip(xs, ps, mixed, moe):
        out = final_embed(x1, y, p.reshape(-1, p.shape[-1]), wts['g_ple'], wts['w_ple_gate'], wts['w_ple_proj'])
        outs.append(out.reshape(x.shape))
    return tuple(outs)
```

```python
import functools
import math

import jax
import jax.numpy as jnp
import numpy as np
from jax import lax
from jax.experimental import pallas as pl
from jax.experimental.pallas import tpu as pltpu
from jax.experimental.pallas import tpu_sc as plsc

F32 = jnp.float32
BF16 = jnp.bfloat16

LANES = 128
HEAD_DIM = 128
HEADS_PER_GROUP = 4
ATT_PATTERNS = ((128, 1), (512, 4), (2048, 16))
ATT_SIDE = 64
N_BUCKETS = 32
MAX_DISTANCE = 1024
CAPACITY_FACTOR = 2
EPS = 1e-6
NEG = -1e30
SSM_LANE_GROUPS = 8
V7X_VMEM_LIMIT = 56 * 1024 * 1024

assert all(w // (2 * d) == ATT_SIDE for w, d in ATT_PATTERNS)


def _cparams(semantics, vmem=V7X_VMEM_LIMIT):
    return pltpu.CompilerParams(dimension_semantics=("arbitrary",) * len(semantics), vmem_limit_bytes=vmem)


def _sigmoid(x):
    return 1.0 / (1.0 + jnp.exp(-x))


def _rms_kernel(x_ref, g_ref, *refs, dils):
    o_refs, y_ref = refs[:-1], refs[-1]
    x = x_ref[...]
    y = x * lax.rsqrt(jnp.mean(x * x, axis=-1, keepdims=True) + EPS) * g_ref[...]
    tm = x.shape[0]
    n_chunks = y_ref.shape[0]
    for c in range(n_chunks):
        y_ref[c] = y[:, c * LANES:(c + 1) * LANES]
    for o_ref, dil in zip(o_refs, dils):
        if dil == 1:
            o_ref[...] = y.astype(o_ref.dtype)
        else:
            for r in range(dil):
                rows = [y_ref[c, pl.ds(r, tm // dil, stride=dil), :] for c in range(n_chunks)]
                o_ref[r] = jnp.concatenate(rows, axis=1).astype(o_ref.dtype)


def rms_bf16(x, g, dils, tm=1024):
    batch, seq, d = x.shape
    out_specs, out_shape = [], []
    for dil in dils:
        if dil == 1:
            out_specs.append(pl.BlockSpec((None, tm, d), lambda b, i: (b, i, 0)))
            out_shape.append(jax.ShapeDtypeStruct((batch, seq, d), BF16))
        else:
            out_specs.append(pl.BlockSpec((None, dil, tm // dil, d), lambda b, i: (b, 0, i, 0)))
            out_shape.append(jax.ShapeDtypeStruct((batch, dil, seq // dil, d), BF16))
    outs = pl.pallas_call(
        functools.partial(_rms_kernel, dils=tuple(dils)),
        grid=(batch, seq // tm),
        in_specs=[pl.BlockSpec((None, tm, d), lambda b, i: (b, i, 0)), pl.BlockSpec((1, d), lambda b, i: (0, 0))],
        out_specs=out_specs,
        out_shape=out_shape,
        scratch_shapes=[pltpu.VMEM((d // LANES, tm, LANES), F32)],
        compiler_params=_cparams(("parallel", "parallel")),
        name="rms_bf16",
    )(x, g.reshape(1, d).astype(F32))
    return [o.reshape(batch * seq, d) for o in outs]


def _head_rms(acc, gain):
    outs = []
    for h in range(acc.shape[1] // HEAD_DIM):
        t = acc[:, h * HEAD_DIM:(h + 1) * HEAD_DIM]
        outs.append(t * lax.rsqrt(jnp.mean(t * t, axis=-1, keepdims=True) + EPS))
    return jnp.concatenate(outs, axis=1) * gain


def _qkv_kernel(x_ref, w_ref, g_ref, o_ref):
    x = x_ref[...]
    tn = o_ref.shape[1] // 3
    for j in range(3):
        acc = jnp.dot(x, w_ref[:, j * tn:(j + 1) * tn], preferred_element_type=F32)
        if j < 2:
            acc = _head_rms(acc, g_ref[j])
        o_ref[:, j * tn:(j + 1) * tn] = acc.astype(o_ref.dtype)


def qkv_proj(hn, w, gains, name, tm=1024):
    n, d_model = hn.shape
    tn = HEADS_PER_GROUP * HEAD_DIM
    return pl.pallas_call(
        _qkv_kernel,
        grid=(n // tm,),
        in_specs=[
            pl.BlockSpec((tm, d_model), lambda i: (i, 0)),
            pl.BlockSpec((d_model, 3 * tn), lambda i: (0, 0), pipeline_mode=pl.Buffered(1)),
            pl.BlockSpec((3, 1, tn), lambda i: (0, 0, 0), pipeline_mode=pl.Buffered(1)),
        ],
        out_specs=pl.BlockSpec((tm, 3 * tn), lambda i: (i, 0)),
        out_shape=jax.ShapeDtypeStruct((n, 3 * tn), BF16),
        compiler_params=_cparams(("parallel",)),
        name=name,
    )(hn, w, gains)


def _proj_kernel(x_ref, w_ref, o_ref, *, act):
    acc = jnp.dot(x_ref[...], w_ref[...], preferred_element_type=F32)
    if act:
        acc = _sigmoid(acc)
    o_ref[...] = acc.astype(o_ref.dtype)


def plain_proj(hn, w, out_dtype, act, name, tm=1024, tn=1024):
    n, d_model = hn.shape
    width = w.shape[1]
    tn = min(tn, width)
    return pl.pallas_call(
        functools.partial(_proj_kernel, act=act),
        grid=(width // tn, n // tm),
        in_specs=[
            pl.BlockSpec((tm, d_model), lambda j, i: (i, 0)),
            pl.BlockSpec((d_model, tn), lambda j, i: (0, j)),
        ],
        out_specs=pl.BlockSpec((tm, tn), lambda j, i: (i, j)),
        out_shape=jax.ShapeDtypeStruct((n, width), out_dtype),
        compiler_params=_cparams(("arbitrary", "parallel")),
        name=name,
    )(hn, w)


def _t5_buckets(rel):
    half = N_BUCKETS // 2
    max_exact = half // 2
    n = np.abs(rel)
    large = max_exact + (np.log(np.maximum(n, 1) / max_exact) / math.log(MAX_DISTANCE / max_exact)
                         * (half - max_exact)).astype(np.int32)
    large = np.minimum(large, half - 1)
    return ((rel > 0).astype(np.int32) * half + np.where(n < max_exact, n, large)).astype(np.int32)


def _attn_geometry(l):
    tqb = min(2 * ATT_SIDE, l)
    win = min(4 * ATT_SIDE, l)
    return tqb, win


def attn_bias_tables(rel_bias_g, dil, l):
    tqb, win = _attn_geometry(l)
    i = np.arange(tqb)[:, None]
    j = np.arange(win)[None, :]
    rel = np.stack([j - i - ATT_SIDE * shift for shift in range(3)])
    valid = np.abs(rel) <= ATT_SIDE
    onehot = (_t5_buckets(dil * rel)[..., None] == np.arange(N_BUCKETS)).astype(np.int8)
    b = jnp.einsum('sijb,bh->shij', jnp.asarray(onehot).astype(F32), rel_bias_g.astype(F32),
                   precision=lax.Precision.HIGHEST)
    return jnp.where(valid[:, None], b, NEG)


def _attn_kernel(q_ref, k_ref, v_ref, bias_ref, o_ref, lse_ref, *, l, dil, tql):
    tqb, win = _attn_geometry(l)
    qi = pl.program_id(2)
    for r in range(dil):
        for c in range(tql // tqb):
            q0 = qi * tql + c * tqb
            start = pl.multiple_of(jnp.clip(q0 - ATT_SIDE, 0, l - win), ATT_SIDE)
            shift = (q0 - start) // ATT_SIDE
            q = q_ref[r, c * tqb:(c + 1) * tqb, :]
            kw = k_ref[r, pl.ds(start, win), :]
            vw = v_ref[r, pl.ds(start, win), :]
            logits = lax.dot_general(q, kw, (((1,), (1,)), ((), ())), preferred_element_type=F32)
            logits = logits + bias_ref[shift]
            m = jnp.max(logits, axis=-1, keepdims=True)
            p = jnp.exp(logits - m)
            s = jnp.sum(p, axis=-1, keepdims=True)
            o = jnp.dot(p.astype(vw.dtype), vw, preferred_element_type=F32)
            if dil == 1:
                rows = pl.ds(c * tqb, tqb)
            else:
                rows = pl.ds(c * tqb * dil + r, tqb, stride=dil)
            o_ref[rows, :] = o / s
            lse_ref[rows, :] = jnp.broadcast_to(m + jnp.log(s), (tqb, HEAD_DIM))


def dilated_attention(qkv, bias, batch, seq, dil):
    l = seq // dil
    tqb, win = _attn_geometry(l)
    tql = min(l, max(tqb, 8 * tqb // dil))
    tok = tql * dil
    hw = HEADS_PER_GROUP * HEAD_DIM
    x = qkv.reshape(batch, dil, l, 3 * hw)
    out_sds = jax.ShapeDtypeStruct((batch, seq, hw), F32)
    out_spec = pl.BlockSpec((None, tok, HEAD_DIM), lambda b, h, i: (b, i, h))
    o, lse = pl.pallas_call(
        functools.partial(_attn_kernel, l=l, dil=dil, tql=tql),
        grid=(batch, HEADS_PER_GROUP, l // tql),
        in_specs=[
            pl.BlockSpec((None, dil, tql, HEAD_DIM), lambda b, h, i: (b, 0, i, h)),
            pl.BlockSpec((None, dil, l, HEAD_DIM), lambda b, h, i: (b, 0, 0, HEADS_PER_GROUP + h)),
            pl.BlockSpec((None, dil, l, HEAD_DIM), lambda b, h, i: (b, 0, 0, 2 * HEADS_PER_GROUP + h)),
            pl.BlockSpec((3, None, tqb, win), lambda b, h, i: (0, h, 0, 0)),
        ],
        out_specs=[out_spec, out_spec],
        out_shape=[out_sds, out_sds],
        compiler_params=_cparams(("parallel", "parallel", "parallel")),
        name=f"dilated_attn_d{dil}",
    )(x, x, x, bias)
    return o.reshape(batch * seq, hw), lse.reshape(batch * seq, hw)


def ssm_discretise(a_re, a_im, log_step, b_re, b_im, c_re, c_im):
    lam = lax.complex(a_re.astype(F32), a_im.astype(F32))
    step = jnp.exp(log_step.astype(F32))[..., None]
    lam_bar = jnp.exp(lam * step)
    b_bar = ((lam_bar - 1.0) / lam)[..., None] * lax.complex(b_re.astype(F32), b_im.astype(F32))
    c = lax.complex(c_re.astype(F32), c_im.astype(F32))
    return lam_bar, b_bar, c


def _block_diag(m):
    _, nj, lg, a, b = m.shape
    return jnp.einsum('djgab,gk->djgakb', m, jnp.eye(lg, dtype=F32)).reshape(2, nj, lg * a, lg * b)


def ssm_block_weights(lam_bar, b_bar, c):
    _, g, p, h = b_bar.shape
    lg = SSM_LANE_GROUPS
    nj = g // lg
    lam = lam_bar[..., None]
    ct = jnp.swapaxes(c, -1, -2)

    def grp(m):
        return m.reshape((2, nj, lg) + m.shape[2:])

    def state_in(bmat):
        t = grp(jnp.swapaxes(bmat, -1, -2))
        return jnp.concatenate([_block_diag(t.real), _block_diag(t.imag)], axis=-1)

    def state_out(cmat):
        t = grp(cmat)
        return jnp.concatenate([_block_diag(t.real), _block_diag(-t.imag)], axis=-2)

    def direct(k):
        m = jnp.einsum('dgpc,dgph->dghc', ct * lam ** k, b_bar).real
        return _block_diag(grp(m))

    first_in, second_in = state_in(lam * b_bar), state_in(b_bar)
    first_out, second_out = state_out(ct * lam), state_out(ct * lam ** 2)
    zero = jnp.zeros_like(direct(0))
    sel = jnp.array([0, 1]).reshape(2, 1, 1, 1)
    wb = jnp.where(sel == 0, jnp.concatenate([first_in, second_in], axis=-2),
                   jnp.concatenate([second_in, first_in], axis=-2))
    wx = jnp.where(sel == 0, jnp.concatenate([first_out, second_out], axis=-1),
                   jnp.concatenate([second_out, first_out], axis=-1))
    u_first = jnp.concatenate([direct(0), direct(1)], axis=-1)
    u_second = jnp.concatenate([zero, direct(0)], axis=-1)
    swap = lambda m: jnp.concatenate([m[..., 128:], m[..., :128]], axis=-1)
    wu = jnp.where(sel == 0, jnp.concatenate([u_first, u_second], axis=-2),
                   jnp.concatenate([swap(u_second), swap(u_first)], axis=-2))
    l2 = (lam_bar ** 2).reshape(2, nj, 1, 1, lg * p)
    lam2 = jnp.broadcast_to(jnp.concatenate([l2.real, l2.imag], axis=2), (2, nj, 2, 8, lg * p))
    return wb.astype(BF16), wx.astype(BF16), wu.astype(BF16), lam2.astype(F32)


def _s5_kernel(u_ref, wb_ref, wx_ref, wu_ref, lam_ref, x0_ref, *refs, tt, emit_y):
    n_out = 2 if emit_y else 1
    outs, scratch = refs[:n_out], list(refs[n_out:])
    xend_ref = outs[-1]
    parts = []
    for _ in range(S5_PARTS):
        parts.append([scratch.pop(0) for _ in range(4 if emit_y else 3)])
    st_ref = scratch.pop(0)
    direction = pl.program_id(0)
    ti = pl.program_id(2)
    nt = pl.num_programs(2)
    vb, _, lanes = u_ref.shape
    k = lam_ref.shape[-1]
    part_t = tt // S5_PARTS
    n = part_t // 2

    @pl.when(ti == 0)
    def _():
        st_ref[...] = x0_ref[...]

    lr = lam_ref[0]
    li = lam_ref[1]
    state = (st_ref[0], st_ref[1])
    for order, part in enumerate(parts):
        lhs_e, lhs_o, bu_ref = part[:3]
        t0 = jnp.where(direction == 0, order, S5_PARTS - 1 - order) * part_t
        for b in range(vb):
            lhs_e[pl.ds(b, n, stride=vb), :] = u_ref[b, pl.ds(t0, n, stride=2), :]
            lhs_o[pl.ds(b, n, stride=vb), :] = u_ref[b, pl.ds(t0 + 1, n, stride=2), :]
        lhs = jnp.concatenate([lhs_e[...], lhs_o[...]], axis=1).astype(BF16)
        bu_ref[...] = jnp.dot(lhs, wb_ref[...], preferred_element_type=F32)
        if emit_y:
            xs_ref = part[3]
            in_row = pl.multiple_of(jnp.where(direction == 0, 0, n * vb), vb)
            xs_ref[pl.ds(in_row, vb), :] = jnp.concatenate(state, axis=1)

        def step(s, carry, part=part):
            xr, xi = carry
            t = jnp.where(direction == 0, s, n - 1 - s)
            b = part[2][pl.ds(pl.multiple_of(t * vb, vb), vb), :]
            nr = lr * xr - li * xi + b[:, :k]
            ni = lr * xi + li * xr + b[:, k:]
            if emit_y:
                out_row = pl.multiple_of(jnp.where(direction == 0, t + 1, t) * vb, vb)
                part[3][pl.ds(out_row, vb), :] = jnp.concatenate([nr, ni], axis=1)
            return nr, ni

        state = lax.fori_loop(0, n, step, state, unroll=True)
        if emit_y:
            y_ref, ysc_e, ysc_o = outs[0], scratch[0], scratch[1]
            base = pl.multiple_of(jnp.where(direction == 0, 0, vb), vb)
            xs = xs_ref[pl.ds(base, n * vb), :].astype(BF16)
            y = (jnp.dot(xs, wx_ref[...], preferred_element_type=F32)
                 + jnp.dot(lhs, wu_ref[...], preferred_element_type=F32))
            ysc_e[...] = y[:, :lanes]
            ysc_o[...] = y[:, lanes:]
            for b in range(vb):
                y_ref[b, pl.ds(t0, n, stride=2), :] = ysc_e[pl.ds(b, n, stride=vb), :]
                y_ref[b, pl.ds(t0 + 1, n, stride=2), :] = ysc_o[pl.ds(b, n, stride=vb), :]
    st_ref[0] = state[0]
    st_ref[1] = state[1]

    @pl.when(ti == nt - 1)
    def _():
        xend_ref[0] = state[0]
        xend_ref[1] = state[1]


S5_PARTS = 2


def s5_scan(u, wts, x0, vbatch, emit_y, tt=512):
    width = u.shape[1]
    vseq = u.shape[0] // vbatch
    wb, wx, wu, lam2 = wts['wb'], wts['wx'], wts['wu'], wts['lam2']
    nj = wb.shape[1]
    k = lam2.shape[-1]
    lanes = width // nj
    nt = vseq // tt
    n = tt // S5_PARTS // 2

    def tmap(d, t):
        return jnp.where(d == 0, t, nt - 1 - t)

    state_spec = pl.BlockSpec((None, None, 2, vbatch, k), lambda d, j, t: (d, j, 0, 0, 0))
    state_sds = jax.ShapeDtypeStruct((2, nj, 2, vbatch, k), F32)
    out_specs, out_shape = [state_spec], [state_sds]
    half = pltpu.VMEM((n * vbatch, lanes), F32)
    part = [half, half, pltpu.VMEM((n * vbatch, 2 * k), F32)]
    if emit_y:
        part.append(pltpu.VMEM(((n + 1) * vbatch, 2 * k), F32))
    scratch = part * S5_PARTS + [pltpu.VMEM((2, vbatch, k), F32)]
    if emit_y:
        out_specs.insert(0, pl.BlockSpec((None, vbatch, tt, lanes), lambda d, j, t: (d, 0, tmap(d, t), j)))
        out_shape.insert(0, jax.ShapeDtypeStruct((2, vbatch, vseq, width), F32))
        scratch += [half, half]
    outs = pl.pallas_call(
        functools.partial(_s5_kernel, tt=tt, emit_y=emit_y),
        grid=(2, nj, nt),
        in_specs=[
            pl.BlockSpec((vbatch, tt, lanes), lambda d, j, t: (0, tmap(d, t), j)),
            pl.BlockSpec((None, None, 2 * lanes, 2 * k), lambda d, j, t: (d, j, 0, 0)),
            pl.BlockSpec((None, None, 2 * k, 2 * lanes), lambda d, j, t: (d, j, 0, 0)),
            pl.BlockSpec((None, None, 2 * lanes, 2 * lanes), lambda d, j, t: (d, j, 0, 0)),
            state_spec, state_spec,
        ],
        out_specs=out_specs,
        out_shape=out_shape,
        scratch_shapes=scratch,
        compiler_params=_cparams(("parallel", "parallel", "arbitrary")),
        name="s5_scan" if emit_y else "s5_states",
    )(u.reshape(vbatch, vseq, width), wb, wx, wu, lam2, x0)
    if emit_y:
        return outs[0].reshape(2, vbatch * vseq, width), outs[1]
    return None, outs[0]


def s5_segment_carries(xend, lam_bar, seg_len, batch, nseg):
    _, nj, _, _, k = xend.shape
    lam_pow = (lam_bar ** seg_len).reshape(2, nj, 1, k)
    e = lax.complex(xend[:, :, 0], xend[:, :, 1]).reshape(2, nj, batch, nseg, k)
    fwd = [jnp.zeros((nj, batch, k), jnp.complex64)]
    for s in range(1, nseg):
        fwd.append(lam_pow[0] * fwd[-1] + e[0, :, :, s - 1])
    bwd = [jnp.zeros((nj, batch, k), jnp.complex64)]
    for s in range(nseg - 2, -1, -1):
        bwd.append(lam_pow[1] * bwd[-1] + e[1, :, :, s + 1])
    bwd = bwd[::-1]
    x_in = jnp.stack([jnp.stack(fwd, axis=2), jnp.stack(bwd, axis=2)])
    x_in = x_in.reshape(2, nj, batch * nseg, k)
    return jnp.stack([x_in.real, x_in.imag], axis=2).astype(F32)


def _gelu_tanh(x):
    return 0.5 * x * (1.0 + jnp.tanh(math.sqrt(2.0 / math.pi) * (x + 0.044715 * x * x * x)))


def _glu_kernel(yf_ref, yb_ref, u_ref, d_ref, wa_ref, wb_ref, o_ref):
    y = _gelu_tanh(yf_ref[...] + yb_ref[...] + d_ref[...] * u_ref[...]).astype(BF16)
    za = jnp.dot(y, wa_ref[...], preferred_element_type=F32)
    zb = jnp.dot(y, wb_ref[...], preferred_element_type=F32)
    o_ref[...] = (za * _sigmoid(zb)).astype(o_ref.dtype)


def ssm_glu(y, u, d_skip, w_glu, tm=512):
    n, width = u.shape
    wa = w_glu[:, :width]
    wb = w_glu[:, width:]
    const = lambda i: (0, 0)
    return pl.pallas_call(
        _glu_kernel,
        grid=(n // tm,),
        in_specs=[
            pl.BlockSpec((None, tm, width), lambda i: (0, i, 0)),
            pl.BlockSpec((None, tm, width), lambda i: (1, i, 0)),
            pl.BlockSpec((tm, width), lambda i: (i, 0)),
            pl.BlockSpec((1, width), const),
            pl.BlockSpec((width, width), const),
            pl.BlockSpec((width, width), const),
        ],
        out_specs=pl.BlockSpec((tm, width), lambda i: (i, 0)),
        out_shape=jax.ShapeDtypeStruct((n, width), BF16),
        compiler_params=_cparams(("parallel",)),
        name="ssm_glu",
    )(y, y, u, d_skip.reshape(1, width).astype(F32), wa, wb)


def _split_bf16(x):
    hi = x.astype(BF16)
    lo = (x - hi.astype(F32)).astype(BF16)
    return hi, lo


def _pack_bf16_pair(a, b):
    wa = pltpu.bitcast(a.astype(BF16).astype(F32), jnp.int32)
    wb = pltpu.bitcast(b.astype(BF16).astype(F32), jnp.int32)
    return wa | lax.shift_right_logical(wb, 16)


def _unpack_bf16_pair(w):
    hi = pltpu.bitcast(w & jnp.int32(-65536), F32).astype(BF16)
    lo = pltpu.bitcast(lax.shift_left(w, 16), F32).astype(BF16)
    return hi, lo


def _mixer_out_kernel(o0, o1, o2, l0, l1, l2, ob_ref, gate_ref, x_ref, wa_ref, wb_ref, wo_ref,
                      g_ref, wr_ref, hbuf_ref, x1_ref, h_ref, lg_ref):
    del hbuf_ref
    d_model = x_ref.shape[1]
    la, lb, lc = l0[...], l1[...], l2[...]
    m = jnp.maximum(jnp.maximum(la, lb), lc)
    ea, eb, ec = jnp.exp(la - m), jnp.exp(lb - m), jnp.exp(lc - m)
    oa = (ea * o0[...] + eb * o1[...] + ec * o2[...]) / (ea + eb + ec)
    pa = jnp.dot(oa.astype(BF16), wa_ref[...], preferred_element_type=F32)
    pb = jnp.dot(ob_ref[...], wb_ref[...], preferred_element_type=F32)
    gates = gate_ref[...].astype(F32)
    merged = gates[:, :d_model] * pa + gates[:, d_model:] * pb
    x1 = x_ref[...] + jnp.dot(merged.astype(BF16), wo_ref[...], preferred_element_type=F32)
    x1_ref[...] = x1
    h = x1 * lax.rsqrt(jnp.mean(x1 * x1, axis=-1, keepdims=True) + EPS) * g_ref[...]
    h_ref[...] = _pack_bf16_pair(h[:, :d_model // 2], h[:, d_model // 2:])
    hh, hl = _split_bf16(h)
    er = lg_ref.shape[1]
    both = jnp.dot(hh, wr_ref[...], preferred_element_type=F32)
    lg_ref[...] = both[:, :er] + both[:, er:] + jnp.dot(hl, wr_ref[:, :er], preferred_element_type=F32)


def mixer_out(o_list, lse_list, o_b, gates, x, w_br_a, w_br_b, w_out, g_ffn, wr_split, h_buf, h_row0, tm=256):
    n, d_model = x.shape
    row = lambda w: pl.BlockSpec((tm, w), lambda i: (i, 0))
    full = lambda a: pl.BlockSpec(a.shape, lambda i: (0,) * a.ndim, pipeline_mode=pl.Buffered(1))
    g2 = g_ffn.reshape(1, d_model).astype(F32)
    er = wr_split.shape[1] // 2
    blk0 = h_row0 // tm
    in_specs = [row(o.shape[1]) for o in o_list] + [row(l.shape[1]) for l in lse_list] + [
        row(o_b.shape[1]), row(gates.shape[1]), row(d_model),
        full(w_br_a), full(w_br_b), full(w_out), full(g2), full(wr_split), pl.BlockSpec(memory_space=pl.ANY)]
    args = [*o_list, *lse_list, o_b, gates, x, w_br_a, w_br_b, w_out, g2, wr_split, h_buf]
    x1, h, logits = pl.pallas_call(
        _mixer_out_kernel,
        grid=(n // tm,),
        in_specs=in_specs,
        out_specs=[row(d_model), pl.BlockSpec((tm, d_model // 2), lambda i: (blk0 + i, 0)), row(er)],
        out_shape=[jax.ShapeDtypeStruct((n, d_model), F32), jax.ShapeDtypeStruct(h_buf.shape, h_buf.dtype),
                   jax.ShapeDtypeStruct((n, er), F32)],
        input_output_aliases={len(args) - 1: 1},
        compiler_params=_cparams(("parallel",)),
        name="mixer_out",
    )(*args)
    return x1, h, logits


def _expert_up_kernel(x_ref, wg_ref, wu_ref, o_ref, wg_s, wu_s):
    @pl.when(pl.program_id(2) == 0)
    def _():
        wg_s[...] = wg_ref[...].astype(BF16)
        wu_s[...] = wu_ref[...].astype(BF16)

    x_hi, x_lo = _unpack_bf16_pair(x_ref[...])
    half = x_hi.shape[1]

    def proj(w_s):
        return (jnp.dot(x_hi, w_s[:half, :], preferred_element_type=F32)
                + jnp.dot(x_lo, w_s[half:, :], preferred_element_type=F32))

    a = proj(wg_s)
    b = proj(wu_s)
    o_ref[...] = (a * _sigmoid(a) * b).astype(o_ref.dtype)


def _expert_up_into_kernel(x_ref, wg_ref, wu_ref, buf_ref, o_ref, wg_s, wu_s):
    del buf_ref
    _expert_up_kernel(x_ref, wg_ref, wu_ref, o_ref, wg_s, wu_s)


def expert_up(xe, w_gate, w_up, e0, hid=None, tm=1024, tf=512):
    n_e, cap, _ = xe.shape
    n_exp, d_model, ff = w_gate.shape
    tm = math.gcd(tm, cap)
    w_spec = pl.BlockSpec((None, d_model, tf), lambda ei, f, m: (e0 + ei, 0, f))
    in_specs = [pl.BlockSpec((None, tm, d_model // 2), lambda ei, f, m: (ei, m, 0)), w_spec, w_spec]
    args = [xe, w_gate, w_up]
    if hid is not None:
        in_specs.append(pl.BlockSpec(memory_space=pl.ANY))
        args.append(hid)
    return pl.pallas_call(
        _expert_up_kernel if hid is None else _expert_up_into_kernel,
        grid=(n_e, ff // tf, cap // tm),
        in_specs=in_specs,
        out_specs=pl.BlockSpec((None, tm, tf), lambda ei, f, m: (e0 + ei, m, f)),
        out_shape=jax.ShapeDtypeStruct((n_exp, cap, ff), BF16),
        scratch_shapes=[pltpu.VMEM((d_model, tf), BF16), pltpu.VMEM((d_model, tf), BF16)],
        input_output_aliases={} if hid is None else {3: 0},
        compiler_params=_cparams(("parallel", "parallel", "arbitrary")),
        name="expert_up",
    )(*args)


def _expert_down_kernel(h_ref, w_ref, o_ref, w_s):
    @pl.when(pl.program_id(2) == 0)
    def _():
        w_s[...] = w_ref[...].astype(BF16)

    acc = jnp.dot(h_ref[...], w_s[...], preferred_element_type=F32)
    half = acc.shape[1] // 2
    o_ref[...] = _pack_bf16_pair(acc[:, :half], acc[:, half:])


def expert_down(hid, w_down, tm=512, tn=1024):
    e, cap, ff = hid.shape
    d_model = w_down.shape[2]
    tm = math.gcd(tm, cap)
    tn = min(tn, d_model)
    return pl.pallas_call(
        _expert_down_kernel,
        grid=(e, d_model // tn, cap // tm),
        in_specs=[
            pl.BlockSpec((None, tm, ff), lambda ei, n, m: (ei, m, 0)),
            pl.BlockSpec((None, ff, tn), lambda ei, n, m: (ei, 0, n)),
        ],
        out_specs=pl.BlockSpec((None, tm, tn // 2), lambda ei, n, m: (ei, m, n)),
        out_shape=jax.ShapeDtypeStruct((e, cap, d_model // 2), jnp.int32),
        scratch_shapes=[pltpu.VMEM((ff, tn), BF16)],
        compiler_params=_cparams(("parallel", "parallel", "arbitrary")),
        name="expert_down",
    )(hid, w_down)


ROUTER_ROWS = 512


def _router_kernel(lg_ref, o_ref, aff_ref, *, n_exp, cap):
    n, lanes = lg_ref.shape
    t = ROUTER_ROWS
    nt = n // t
    real = lax.broadcasted_iota(jnp.int32, (1, lanes), 1) < n_exp

    def rows(i):
        return pl.ds(pl.multiple_of(i * t, t), t)

    def softmax_tile(i, carry):
        x = jnp.where(real, lg_ref[rows(i), :], -jnp.inf)
        e = jnp.exp(x - jnp.max(x, axis=-1, keepdims=True))
        aff_ref[rows(i), :] = e / jnp.sum(e, axis=-1, keepdims=True)
        return carry

    lax.fori_loop(0, nt, softmax_tile, 0)

    def count(pred):
        def body(i, acc):
            hit = pred(pltpu.bitcast(aff_ref[rows(i), :], jnp.int32)).astype(jnp.int32)
            return acc + jnp.sum(hit.reshape(t // 8, 8, lanes), axis=0)
        acc = lax.fori_loop(0, nt, body, jnp.zeros((8, lanes), jnp.int32))
        return jnp.sum(acc, axis=0, keepdims=True)

    thr = jnp.zeros((1, lanes), jnp.int32)
    for bit in range(30, -1, -1):
        cand = thr | jnp.int32(1 << bit)
        thr = jnp.where(count(lambda k, cand=cand: k >= cand) >= cap, cand, thr)
    ties_needed = (cap - count(lambda k: k > thr)).astype(F32)

    r_iota = lax.broadcasted_iota(jnp.int32, (t, t), 0)
    c_iota = lax.broadcasted_iota(jnp.int32, (t, t), 1)
    prefix = (c_iota <= r_iota).astype(BF16)

    def select_tile(i, ties_before):
        aff = aff_ref[rows(i), :]
        key = pltpu.bitcast(aff, jnp.int32)
        tie = (key == thr).astype(BF16)
        tie_rank = ties_before + jnp.dot(prefix, tie, preferred_element_type=F32)
        chosen = (key > thr) | ((key == thr) & (tie_rank <= ties_needed))
        chosen = jnp.where(real & chosen, 1.0, 0.0)
        o_ref[rows(i), :] = jnp.where(real, aff, 0.0) + pltpu.roll(chosen, n_exp, 1)
        return tie_rank[t - 1:t, :]

    lax.fori_loop(0, nt, select_tile, jnp.zeros((1, lanes), F32))


def route_expert_choice(logits, n_exp, cap):
    n, lanes = logits.shape
    whole = pl.BlockSpec(memory_space=pltpu.VMEM)
    routed = pl.pallas_call(
        functools.partial(_router_kernel, n_exp=n_exp, cap=cap),
        in_specs=[whole],
        out_specs=whole,
        out_shape=jax.ShapeDtypeStruct((n, lanes), F32),
        scratch_shapes=[pltpu.VMEM((n, lanes), F32)],
        compiler_params=pltpu.CompilerParams(vmem_limit_bytes=V7X_VMEM_LIMIT),
        name="route_expert_choice",
    )(logits)
    chosen = routed[:, n_exp:2 * n_exp].T.reshape(-1) > 0.5
    pos = jnp.nonzero(chosen, size=n_exp * cap, fill_value=0)[0].astype(jnp.int32)
    gate = routed[:, :n_exp].T.reshape(-1)[pos]
    return gate, pos % n


SC_GATHER_CHUNK = 16


def sc_gather_rows(table, idx):
    n_idx = idx.shape[0]
    width = table.shape[1]
    info = plsc.get_sparse_core_info()
    n_workers = info.num_cores * info.num_subcores
    per_worker = n_idx // n_workers
    ch = SC_GATHER_CHUNK
    assert n_idx % n_workers == 0 and per_worker % (2 * ch) == 0
    mesh = plsc.VectorSubcoreMesh(core_axis_name="c", subcore_axis_name="s")

    @functools.partial(
        pl.kernel, mesh=mesh,
        out_type=jax.ShapeDtypeStruct((n_idx, width), table.dtype),
        scratch_types=[pltpu.VMEM((per_worker,), jnp.int32), pltpu.VMEM((ch, width), table.dtype),
                       pltpu.VMEM((ch, width), table.dtype), pltpu.SemaphoreType.DMA, pltpu.SemaphoreType.DMA],
        name="sc_gather_rows",
    )
    def gather(table_hbm, idx_hbm, out_hbm, idx_v, rows_a, rows_b, sem_a, sem_b):
        base = (lax.axis_index("s") * info.num_cores + lax.axis_index("c")) * per_worker
        pltpu.sync_copy(idx_hbm.at[pl.ds(base, per_worker)], idx_v)

        @pl.loop(0, per_worker // (2 * ch))
        def _(c):
            off = c * 2 * ch
            copy_a = pltpu.async_copy(table_hbm.at[idx_v.at[pl.ds(off, ch)]], rows_a, sem_a)
            copy_b = pltpu.async_copy(table_hbm.at[idx_v.at[pl.ds(off + ch, ch)]], rows_b, sem_b)
            copy_a.wait()
            pltpu.sync_copy(rows_a, out_hbm.at[pl.ds(base + off, ch)])
            copy_b.wait()
            pltpu.sync_copy(rows_b, out_hbm.at[pl.ds(base + off + ch, ch)])

    return gather(table, idx)


COMBINE_TOKENS = 512
COMBINE_ROWS = 512


def _combine_kernel(tile_ref, chunk_ref, first_ref, valid_ref, rows_ref, tok_ref, g_ref, o_ref, *, tn):
    p = pl.program_id(0)

    @pl.when(first_ref[p] == 1)
    def _():
        o_ref[...] = jnp.zeros_like(o_ref)

    @pl.when(valid_ref[p] == 1)
    def _():
        t, ch = o_ref.shape[0], tok_ref.shape[1]
        token = tile_ref[p] * t + lax.broadcasted_iota(jnp.int32, (t, ch), 0)
        weights = jnp.where(tok_ref[...] == token, g_ref[...], 0.0).astype(BF16)
        hi, lo = _unpack_bf16_pair(rows_ref[...])
        acc_hi = jnp.dot(weights, hi, preferred_element_type=F32)
        acc_lo = jnp.dot(weights, lo, preferred_element_type=F32)
        half = tn // 2
        for n in range(o_ref.shape[1] // tn):
            o_ref[:, n * tn:n * tn + half] += acc_hi[:, n * half:(n + 1) * half]
            o_ref[:, n * tn + half:(n + 1) * tn] += acc_lo[:, n * half:(n + 1) * half]


def combine_expert_outputs(rows, tok_sorted, gate_sorted, n_tok, tn):
    m, half_d = rows.shape
    t, ch = COMBINE_TOKENS, COMBINE_ROWS
    n_tiles, n_chunks = n_tok // t, m // ch
    n_pairs = n_tiles + n_chunks
    bounds = jnp.searchsorted(tok_sorted, jnp.arange(n_tiles + 1, dtype=jnp.int32) * t, side='left').astype(jnp.int32)
    start, end = bounds[:-1], bounds[1:]
    c_lo = jnp.minimum(start // ch, n_chunks - 1)
    c_hi = jnp.where(end > start, (end - 1) // ch, c_lo)
    count = c_hi - c_lo + 1
    offs = jnp.cumsum(count)
    p = jnp.arange(n_pairs, dtype=jnp.int32)
    tile = jnp.minimum(jnp.searchsorted(offs, p, side='right'), n_tiles - 1).astype(jnp.int32)
    first_p = (offs - count)[tile]
    valid = p < offs[-1]
    chunk = jnp.where(valid, c_lo[tile] + p - first_p, c_hi[-1]).astype(jnp.int32)
    first = (valid & (p == first_p)).astype(jnp.int32)
    grid_spec = pltpu.PrefetchScalarGridSpec(
        num_scalar_prefetch=4,
        grid=(n_pairs,),
        in_specs=[
            pl.BlockSpec((ch, half_d), lambda i, tl, ck, fr, vl: (ck[i], 0)),
            pl.BlockSpec((1, ch), lambda i, tl, ck, fr, vl: (0, ck[i])),
            pl.BlockSpec((1, ch), lambda i, tl, ck, fr, vl: (0, ck[i])),
        ],
        out_specs=pl.BlockSpec((t, 2 * half_d), lambda i, tl, ck, fr, vl: (tl[i], 0)),
    )
    return pl.pallas_call(
        functools.partial(_combine_kernel, tn=tn),
        grid_spec=grid_spec,
        out_shape=jax.ShapeDtypeStruct((n_tok, 2 * half_d), F32),
        compiler_params=_cparams(("arbitrary",)),
        name="combine_expert_outputs",
    )(tile, chunk, first, valid.astype(jnp.int32), rows, tok_sorted.reshape(1, m), gate_sorted.reshape(1, m).astype(F32))


def _final_kernel(x1_ref, moe_ref, p_ref, g_ref, wg_ref, wp_ref, o_ref):
    x2 = x1_ref[...] + moe_ref[...]
    h = (x2 * lax.rsqrt(jnp.mean(x2 * x2, axis=-1, keepdims=True) + EPS) * g_ref[...]).astype(BF16)
    gate = _sigmoid(jnp.dot(h, wg_ref[...], preferred_element_type=F32))
    proj = jnp.dot(p_ref[...].astype(BF16), wp_ref[...], preferred_element_type=F32)
    o_ref[...] = x2 + gate * proj


def final_embed(x1, moe, p, g_ple, w_gate, w_proj, tm=512):
    n, d_model = x1.shape
    row = lambda w: pl.BlockSpec((tm, w), lambda i: (i, 0))
    full = lambda a: pl.BlockSpec(a.shape, lambda i: (0,) * a.ndim, pipeline_mode=pl.Buffered(1))
    g2 = g_ple.reshape(1, d_model).astype(F32)
    return pl.pallas_call(
        _final_kernel,
        grid=(n // tm,),
        in_specs=[row(d_model), row(d_model), row(p.shape[1]), full(g2), full(w_gate), full(w_proj)],
        out_specs=row(d_model),
        out_shape=jax.ShapeDtypeStruct((n, d_model), F32),
        compiler_params=_cparams(("parallel",)),
        name="final_embed",
    )(x1, moe, p, g2, w_gate, w_proj)


def _prepare_weights(rel_bias, g_mix, w_in, q_gain, k_gain, ssm_a_re, ssm_a_im, ssm_log_step,
                     ssm_b_re, ssm_b_im, ssm_c_re, ssm_c_im, ssm_d, w_glu, w_br_a, w_br_b, w_out,
                     g_ffn, w_router, g_ple, w_ple_gate, w_ple_proj):
    n_heads = q_gain.shape[0]
    att_w = n_heads * HEAD_DIM
    hw = HEADS_PER_GROUP * HEAD_DIM
    ssm_w = ssm_d.shape[0]
    w_in_bf = w_in.astype(BF16)
    w_qkv, gains = [], []
    qg = q_gain.astype(F32) * (HEAD_DIM ** -0.5)
    kg = k_gain.astype(F32)
    for gi in range(len(ATT_PATTERNS)):
        cols = [w_in_bf[:, s * att_w + gi * hw: s * att_w + (gi + 1) * hw] for s in range(3)]
        w_qkv.append(jnp.concatenate(cols, axis=1))
        hs = slice(gi * HEADS_PER_GROUP, (gi + 1) * HEADS_PER_GROUP)
        gains.append(jnp.stack([qg[hs].reshape(1, hw), kg[hs].reshape(1, hw), jnp.ones((1, hw), F32)]))
    lam_bar, b_bar, c = ssm_discretise(ssm_a_re, ssm_a_im, ssm_log_step, ssm_b_re, ssm_b_im, ssm_c_re, ssm_c_im)
    wb, wx, wu, lam2 = ssm_block_weights(lam_bar, b_bar, c)
    n_exp = w_router.shape[1]
    wr = jnp.pad(w_router.astype(F32), ((0, 0), (0, 128 - n_exp)))
    wr_split = jnp.concatenate(_split_bf16(wr), axis=1)
    return dict(
        rel_bias=rel_bias, bias_tables={}, g_mix=g_mix, w_qkv=w_qkv, gains=gains,
        w_u=w_in_bf[:, 3 * att_w:3 * att_w + ssm_w], w_gates=w_in_bf[:, 3 * att_w + ssm_w:],
        lam_bar=lam_bar, wb=wb, wx=wx, wu=wu, lam2=lam2, ssm_d=ssm_d, w_glu=w_glu.astype(BF16),
        w_br_a=w_br_a.astype(BF16), w_br_b=w_br_b.astype(BF16), w_out=w_out.astype(BF16),
        g_ffn=g_ffn, wr_split=wr_split, n_exp=n_exp, g_ple=g_ple,
        w_ple_gate=w_ple_gate.astype(BF16), w_ple_proj=w_ple_proj.astype(BF16))


def _ssm_branch(hn, wts, batch, seq):
    nseg = max(1, 8 // batch)
    vbatch = batch * nseg
    vseq = seq // nseg
    u = plain_proj(hn, wts['w_u'], F32, False, "u_proj")
    nj, k = wts['lam2'].shape[1], wts['lam2'].shape[-1]
    x0 = jnp.zeros((2, nj, 2, vbatch, k), F32)
    if nseg > 1:
        _, xend = s5_scan(u, wts, x0, vbatch, emit_y=False)
        x0 = s5_segment_carries(xend, wts['lam_bar'], vseq, batch, nseg)
    y, _ = s5_scan(u, wts, x0, vbatch, emit_y=True)
    return ssm_glu(y, u, wts['ssm_d'], wts['w_glu'])


MOE_LEAD_EXPERTS = 1


def _expert_choice(h_all, n_toks, logits_list, n_exp, w_exp_gate, w_exp_up, w_exp_down):
    d_model = w_exp_down.shape[2]
    half_d = h_all.shape[1]
    caps = [CAPACITY_FACTOR * n // n_exp for n in n_toks]
    cap_all = sum(caps)
    routes, picks, row0 = [], [], 0
    for n, logits, cap in zip(n_toks, logits_list, caps):
        gate_flat, idx_flat = route_expert_choice(logits, n_exp, cap)
        routes.append((gate_flat, idx_flat))
        picks.append(idx_flat.reshape(n_exp, cap) + row0)
        row0 += n
    picks = jnp.concatenate(picks, axis=1)
    hid, lead = None, min(MOE_LEAD_EXPERTS, n_exp)
    for e0, e1 in ((0, lead), (lead, n_exp)):
        if e1 > e0:
            xe = sc_gather_rows(h_all, picks[e0:e1].reshape(-1)).reshape(e1 - e0, cap_all, half_d)
            hid = expert_up(xe, w_exp_gate, w_exp_up, e0, hid)
    tn = min(1024, d_model)
    ye = expert_down(hid, w_exp_down, tn=tn).reshape(n_exp * cap_all, half_d)
    outs, slot0 = [], 0
    for n, (gate_flat, idx_flat), cap in zip(n_toks, routes, caps):
        order = jnp.argsort(idx_flat).astype(jnp.int32)
        rows = (order // cap) * cap_all + slot0 + order % cap
        outs.append(combine_expert_outputs(sc_gather_rows(ye, rows), idx_flat[order], gate_flat[order], n, tn))
        slot0 += cap
    return outs


def _mixer(x, wts, h_buf, h_row0):
    batch, seq, d_model = x.shape
    n = batch * seq
    xf = x.reshape(n, d_model)
    dils = [dil for _, dil in ATT_PATTERNS]
    hn_list = rms_bf16(x, wts['g_mix'], dils)
    hn = hn_list[dils.index(1)]
    o_list, lse_list = [], []
    for gi, dil in enumerate(dils):
        qkv = qkv_proj(hn_list[gi], wts['w_qkv'][gi], wts['gains'][gi], f"qkv_proj_d{dil}")
        hs = slice(gi * HEADS_PER_GROUP, (gi + 1) * HEADS_PER_GROUP)
        bias_key = (gi,) + _attn_geometry(seq // dil)
        if bias_key not in wts['bias_tables']:
            wts['bias_tables'][bias_key] = attn_bias_tables(wts['rel_bias'][:, hs], dil, seq // dil)
        bias = wts['bias_tables'][bias_key]
        o, lse = dilated_attention(qkv, bias, batch, seq, dil)
        o_list.append(o)
        lse_list.append(lse)
    o_b = _ssm_branch(hn, wts, batch, seq)
    gates = plain_proj(hn, wts['w_gates'], BF16, True, "gates_proj")
    return mixer_out(o_list, lse_list, o_b, gates, xf, wts['w_br_a'], wts['w_br_b'],
                     wts['w_out'], wts['g_ffn'], wts['wr_split'], h_buf, h_row0)


def kernel(x_prompt, x_sample, p_prompt, p_sample, rel_bias, g_mix, w_in, q_gain, k_gain, ssm_a_re, ssm_a_im, ssm_log_step, ssm_b_re, ssm_b_im, ssm_c_re, ssm_c_im, ssm_d, w_glu, w_br_a, w_br_b, w_out, g_ffn, w_router, w_exp_gate, w_exp_up, w_exp_down, g_ple, w_ple_gate, w_ple_proj):
    assert g_mix.shape[0] == 1, "single-layer trunk"
    wts = _prepare_weights(rel_bias, g_mix[0], w_in[0], q_gain[0], k_gain[0], ssm_a_re[0], ssm_a_im[0],
                           ssm_log_step[0], ssm_b_re[0], ssm_b_im[0], ssm_c_re[0], ssm_c_im[0], ssm_d[0],
                           w_glu[0], w_br_a[0], w_br_b[0], w_out[0], g_ffn[0], w_router[0], g_ple[0],
                           w_ple_gate[0], w_ple_proj[0])
    xs, ps = (x_prompt, x_sample), (p_prompt[0], p_sample[0])
    n_toks = [x.shape[0] * x.shape[1] for x in xs]
    mixed, row0 = [], 0
    h_all = jnp.zeros((sum(n_toks), x_prompt.shape[-1] // 2), jnp.int32)
    for x, n in zip(xs, n_toks):
        x1, h_all, logits = _mixer(x, wts, h_all, row0)
        mixed.append((x1, logits))
        row0 += n
    moe = _expert_choice(h_all, n_toks, [m[1] for m in mixed], wts['n_exp'],
                         w_exp_gate[0], w_exp_up[0], w_exp_down[0])
    outs = []
    for x, p, (x1, _), y in zip(xs, ps, mixed, moe):
        out = final_embed(x1, y, p.reshape(-1, p.shape[-1]), wts['g_ple'], wts['w_ple_gate'], wts['w_ple_proj'])
        outs.append(out.reshape(x.shape))
    return tuple(outs)
```

```python
import functools
import math

import jax
import jax.numpy as jnp
import numpy as np
from jax import lax
from jax.experimental import pallas as pl
from jax.experimental.pallas import tpu as pltpu
from jax.experimental.pallas import tpu_sc as plsc

F32 = jnp.float32
BF16 = jnp.bfloat16

LANES = 128
HEAD_DIM = 128
HEADS_PER_GROUP = 4
ATT_PATTERNS = ((128, 1), (512, 4), (2048, 16))
ATT_SIDE = 64
N_BUCKETS = 32
MAX_DISTANCE = 1024
CAPACITY_FACTOR = 2
EPS = 1e-6
NEG = -1e30
SSM_LANE_GROUPS = 8
V7X_VMEM_LIMIT = 56 * 1024 * 1024

assert all(w // (2 * d) == ATT_SIDE for w, d in ATT_PATTERNS)


def _cparams(semantics, vmem=V7X_VMEM_LIMIT):
    return pltpu.CompilerParams(dimension_semantics=("arbitrary",) * len(semantics), vmem_limit_bytes=vmem)


def _sigmoid(x):
    return 1.0 / (1.0 + jnp.exp(-x))


def _rms_kernel(x_ref, g_ref, *refs, dils):
    o_refs, y_ref = refs[:-1], refs[-1]
    x = x_ref[...]
    y = x * lax.rsqrt(jnp.mean(x * x, axis=-1, keepdims=True) + EPS) * g_ref[...]
    tm = x.shape[0]
    n_chunks = y_ref.shape[0]
    for c in range(n_chunks):
        y_ref[c] = y[:, c * LANES:(c + 1) * LANES]
    for o_ref, dil in zip(o_refs, dils):
        if dil == 1:
            o_ref[...] = y.astype(o_ref.dtype)
        else:
            for r in range(dil):
                rows = [y_ref[c, pl.ds(r, tm // dil, stride=dil), :] for c in range(n_chunks)]
                o_ref[r] = jnp.concatenate(rows, axis=1).astype(o_ref.dtype)


def rms_bf16(x, g, dils, tm=1024):
    batch, seq, d = x.shape
    out_specs, out_shape = [], []
    for dil in dils:
        if dil == 1:
            out_specs.append(pl.BlockSpec((None, tm, d), lambda b, i: (b, i, 0)))
            out_shape.append(jax.ShapeDtypeStruct((batch, seq, d), BF16))
        else:
            out_specs.append(pl.BlockSpec((None, dil, tm // dil, d), lambda b, i: (b, 0, i, 0)))
            out_shape.append(jax.ShapeDtypeStruct((batch, dil, seq // dil, d), BF16))
    outs = pl.pallas_call(
        functools.partial(_rms_kernel, dils=tuple(dils)),
        grid=(batch, seq // tm),
        in_specs=[pl.BlockSpec((None, tm, d), lambda b, i: (b, i, 0)), pl.BlockSpec((1, d), lambda b, i: (0, 0))],
        out_specs=out_specs,
        out_shape=out_shape,
        scratch_shapes=[pltpu.VMEM((d // LANES, tm, LANES), F32)],
        compiler_params=_cparams(("parallel", "parallel")),
        name="rms_bf16",
    )(x, g.reshape(1, d).astype(F32))
    return [o.reshape(batch * seq, d) for o in outs]


def _head_rms(acc, gain):
    outs = []
    for h in range(acc.shape[1] // HEAD_DIM):
        t = acc[:, h * HEAD_DIM:(h + 1) * HEAD_DIM]
        outs.append(t * lax.rsqrt(jnp.mean(t * t, axis=-1, keepdims=True) + EPS))
    return jnp.concatenate(outs, axis=1) * gain


def _qkv_kernel(x_ref, w_ref, g_ref, o_ref):
    x = x_ref[...]
    tn = o_ref.shape[1] // 3
    for j in range(3):
        acc = jnp.dot(x, w_ref[:, j * tn:(j + 1) * tn], preferred_element_type=F32)
        if j < 2:
            acc = _head_rms(acc, g_ref[j])
        o_ref[:, j * tn:(j + 1) * tn] = acc.astype(o_ref.dtype)


def qkv_proj(hn, w, gains, name, tm=1024):
    n, d_model = hn.shape
    tn = HEADS_PER_GROUP * HEAD_DIM
    return pl.pallas_call(
        _qkv_kernel,
        grid=(n // tm,),
        in_specs=[
            pl.BlockSpec((tm, d_model), lambda i: (i, 0)),
            pl.BlockSpec((d_model, 3 * tn), lambda i: (0, 0), pipeline_mode=pl.Buffered(1)),
            pl.BlockSpec((3, 1, tn), lambda i: (0, 0, 0), pipeline_mode=pl.Buffered(1)),
        ],
        out_specs=pl.BlockSpec((tm, 3 * tn), lambda i: (i, 0)),
        out_shape=jax.ShapeDtypeStruct((n, 3 * tn), BF16),
        compiler_params=_cparams(("parallel",)),
        name=name,
    )(hn, w, gains)


def _proj_kernel(x_ref, w_ref, o_ref, *, act):
    acc = jnp.dot(x_ref[...], w_ref[...], preferred_element_type=F32)
    if act:
        acc = _sigmoid(acc)
    o_ref[...] = acc.astype(o_ref.dtype)


def plain_proj(hn, w, out_dtype, act, name, tm=1024, tn=1024):
    n, d_model = hn.shape
    width = w.shape[1]
    tn = min(tn, width)
    return pl.pallas_call(
        functools.partial(_proj_kernel, act=act),
        grid=(width // tn, n // tm),
        in_specs=[
            pl.BlockSpec((tm, d_model), lambda j, i: (i, 0)),
            pl.BlockSpec((d_model, tn), lambda j, i: (0, j)),
        ],
        out_specs=pl.BlockSpec((tm, tn), lambda j, i: (i, j)),
        out_shape=jax.ShapeDtypeStruct((n, width), out_dtype),
        compiler_params=_cparams(("arbitrary", "parallel")),
        name=name,
    )(hn, w)


def _t5_buckets(rel):
    half = N_BUCKETS // 2
    max_exact = half // 2
    n = np.abs(rel)
    large = max_exact + (np.log(np.maximum(n, 1) / max_exact) / math.log(MAX_DISTANCE / max_exact)
                         * (half - max_exact)).astype(np.int32)
    large = np.minimum(large, half - 1)
    return ((rel > 0).astype(np.int32) * half + np.where(n < max_exact, n, large)).astype(np.int32)


def _attn_geometry(l):
    tqb = min(2 * ATT_SIDE, l)
    win = min(4 * ATT_SIDE, l)
    return tqb, win


def attn_bias_tables(rel_bias_g, dil, l):
    tqb, win = _attn_geometry(l)
    i = np.arange(tqb)[:, None]
    j = np.arange(win)[None, :]
    rel = np.stack([j - i - ATT_SIDE * shift for shift in range(3)])
    valid = np.abs(rel) <= ATT_SIDE
    onehot = (_t5_buckets(dil * rel)[..., None] == np.arange(N_BUCKETS)).astype(np.int8)
    b = jnp.einsum('sijb,bh->shij', jnp.asarray(onehot).astype(F32), rel_bias_g.astype(F32),
                   precision=lax.Precision.HIGHEST)
    return jnp.where(valid[:, None], b, NEG)


def _attn_kernel(q_ref, k_ref, v_ref, bias_ref, o_ref, lse_ref, *, l, dil, tql):
    tqb, win = _attn_geometry(l)
    qi = pl.program_id(2)
    for r in range(dil):
        for c in range(tql // tqb):
            q0 = qi * tql + c * tqb
            start = pl.multiple_of(jnp.clip(q0 - ATT_SIDE, 0, l - win), ATT_SIDE)
            shift = (q0 - start) // ATT_SIDE
            q = q_ref[r, c * tqb:(c + 1) * tqb, :]
            kw = k_ref[r, pl.ds(start, win), :]
            vw = v_ref[r, pl.ds(start, win), :]
            logits = lax.dot_general(q, kw, (((1,), (1,)), ((), ())), preferred_element_type=F32)
            logits = logits + bias_ref[shift]
            m = jnp.max(logits, axis=-1, keepdims=True)
            p = jnp.exp(logits - m)
            s = jnp.sum(p, axis=-1, keepdims=True)
            o = jnp.dot(p.astype(vw.dtype), vw, preferred_element_type=F32)
            if dil == 1:
                rows = pl.ds(c * tqb, tqb)
            else:
                rows = pl.ds(c * tqb * dil + r, tqb, stride=dil)
            o_ref[rows, :] = o / s
            lse_ref[rows, :] = jnp.broadcast_to(m + jnp.log(s), (tqb, HEAD_DIM))


def dilated_attention(qkv, bias, batch, seq, dil):
    l = seq // dil
    tqb, win = _attn_geometry(l)
    tql = min(l, max(tqb, 8 * tqb // dil))
    tok = tql * dil
    hw = HEADS_PER_GROUP * HEAD_DIM
    x = qkv.reshape(batch, dil, l, 3 * hw)
    out_sds = jax.ShapeDtypeStruct((batch, seq, hw), F32)
    out_spec = pl.BlockSpec((None, tok, HEAD_DIM), lambda b, h, i: (b, i, h))
    o, lse = pl.pallas_call(
        functools.partial(_attn_kernel, l=l, dil=dil, tql=tql),
        grid=(batch, HEADS_PER_GROUP, l // tql),
        in_specs=[
            pl.BlockSpec((None, dil, tql, HEAD_DIM), lambda b, h, i: (b, 0, i, h)),
            pl.BlockSpec((None, dil, l, HEAD_DIM), lambda b, h, i: (b, 0, 0, HEADS_PER_GROUP + h)),
            pl.BlockSpec((None, dil, l, HEAD_DIM), lambda b, h, i: (b, 0, 0, 2 * HEADS_PER_GROUP + h)),
            pl.BlockSpec((3, None, tqb, win), lambda b, h, i: (0, h, 0, 0)),
        ],
        out_specs=[out_spec, out_spec],
        out_shape=[out_sds, out_sds],
        compiler_params=_cparams(("parallel", "parallel", "parallel")),
        name=f"dilated_attn_d{dil}",
    )(x, x, x, bias)
    return o.reshape(batch * seq, hw), lse.reshape(batch * seq, hw)


def ssm_discretise(a_re, a_im, log_step, b_re, b_im, c_re, c_im):
    lam = lax.complex(a_re.astype(F32), a_im.astype(F32))
    step = jnp.exp(log_step.astype(F32))[..., None]
    lam_bar = jnp.exp(lam * step)
    b_bar = ((lam_bar - 1.0) / lam)[..., None] * lax.complex(b_re.astype(F32), b_im.astype(F32))
    c = lax.complex(c_re.astype(F32), c_im.astype(F32))
    return lam_bar, b_bar, c


def _block_diag(m):
    _, nj, lg, a, b = m.shape
    return jnp.einsum('djgab,gk->djgakb', m, jnp.eye(lg, dtype=F32)).reshape(2, nj, lg * a, lg * b)


def ssm_block_weights(lam_bar, b_bar, c):
    _, g, p, h = b_bar.shape
    lg = SSM_LANE_GROUPS
    nj = g // lg
    lam = lam_bar[..., None]
    ct = jnp.swapaxes(c, -1, -2)

    def grp(m):
        return m.reshape((2, nj, lg) + m.shape[2:])

    def state_in(bmat):
        t = grp(jnp.swapaxes(bmat, -1, -2))
        return jnp.concatenate([_block_diag(t.real), _block_diag(t.imag)], axis=-1)

    def state_out(cmat):
        t = grp(cmat)
        return jnp.concatenate([_block_diag(t.real), _block_diag(-t.imag)], axis=-2)

    def direct(k):
        m = jnp.einsum('dgpc,dgph->dghc', ct * lam ** k, b_bar).real
        return _block_diag(grp(m))

    first_in, second_in = state_in(lam * b_bar), state_in(b_bar)
    first_out, second_out = state_out(ct * lam), state_out(ct * lam ** 2)
    zero = jnp.zeros_like(direct(0))
    sel = jnp.array([0, 1]).reshape(2, 1, 1, 1)
    wb = jnp.where(sel == 0, jnp.concatenate([first_in, second_in], axis=-2),
                   jnp.concatenate([second_in, first_in], axis=-2))
    wx = jnp.where(sel == 0, jnp.concatenate([first_out, second_out], axis=-1),
                   jnp.concatenate([second_out, first_out], axis=-1))
    u_first = jnp.concatenate([direct(0), direct(1)], axis=-1)
    u_second = jnp.concatenate([zero, direct(0)], axis=-1)
    swap = lambda m: jnp.concatenate([m[..., 128:], m[..., :128]], axis=-1)
    wu = jnp.where(sel == 0, jnp.concatenate([u_first, u_second], axis=-2),
                   jnp.concatenate([swap(u_second), swap(u_first)], axis=-2))
    l2 = (lam_bar ** 2).reshape(2, nj, 1, 1, lg * p)
    lam2 = jnp.broadcast_to(jnp.concatenate([l2.real, l2.imag], axis=2), (2, nj, 2, 8, lg * p))
    return wb.astype(BF16), wx.astype(BF16), wu.astype(BF16), lam2.astype(F32)


def _s5_kernel(u_ref, wb_ref, wx_ref, wu_ref, lam_ref, x0_ref, *refs, tt, emit_y):
    n_out = 2 if emit_y else 1
    outs, scratch = refs[:n_out], list(refs[n_out:])
    xend_ref = outs[-1]
    parts = []
    for _ in range(S5_PARTS):
        parts.append([scratch.pop(0) for _ in range(4 if emit_y else 3)])
    st_ref = scratch.pop(0)
    direction = pl.program_id(0)
    ti = pl.program_id(2)
    nt = pl.num_programs(2)
    vb, _, lanes = u_ref.shape
    k = lam_ref.shape[-1]
    part_t = tt // S5_PARTS
    n = part_t // 2

    @pl.when(ti == 0)
    def _():
        st_ref[...] = x0_ref[...]

    lr = lam_ref[0]
    li = lam_ref[1]
    state = (st_ref[0], st_ref[1])
    for order, part in enumerate(parts):
        lhs_e, lhs_o, bu_ref = part[:3]
        t0 = jnp.where(direction == 0, order, S5_PARTS - 1 - order) * part_t
        for b in range(vb):
            lhs_e[pl.ds(b, n, stride=vb), :] = u_ref[b, pl.ds(t0, n, stride=2), :]
            lhs_o[pl.ds(b, n, stride=vb), :] = u_ref[b, pl.ds(t0 + 1, n, stride=2), :]
        lhs = jnp.concatenate([lhs_e[...], lhs_o[...]], axis=1).astype(BF16)
        bu_ref[...] = jnp.dot(lhs, wb_ref[...], preferred_element_type=F32)
        if emit_y:
            xs_ref = part[3]
            in_row = pl.multiple_of(jnp.where(direction == 0, 0, n * vb), vb)
            xs_ref[pl.ds(in_row, vb), :] = jnp.concatenate(state, axis=1)

        def step(s, carry, part=part):
            xr, xi = carry
            t = jnp.where(direction == 0, s, n - 1 - s)
            b = part[2][pl.ds(pl.multiple_of(t * vb, vb), vb), :]
            nr = lr * xr - li * xi + b[:, :k]
            ni = lr * xi + li * xr + b[:, k:]
            if emit_y:
                out_row = pl.multiple_of(jnp.where(direction == 0, t + 1, t) * vb, vb)
                part[3][pl.ds(out_row, vb), :] = jnp.concatenate([nr, ni], axis=1)
            return nr, ni

        state = lax.fori_loop(0, n, step, state, unroll=True)
        if emit_y:
            y_ref, ysc_e, ysc_o = outs[0], scratch[0], scratch[1]
            base = pl.multiple_of(jnp.where(direction == 0, 0, vb), vb)
            xs = xs_ref[pl.ds(base, n * vb), :].astype(BF16)
            y = (jnp.dot(xs, wx_ref[...], preferred_element_type=F32)
                 + jnp.dot(lhs, wu_ref[...], preferred_element_type=F32))
            ysc_e[...] = y[:, :lanes]
            ysc_o[...] = y[:, lanes:]
            for b in range(vb):
                y_ref[b, pl.ds(t0, n, stride=2), :] = ysc_e[pl.ds(b, n, stride=vb), :]
                y_ref[b, pl.ds(t0 + 1, n, stride=2), :] = ysc_o[pl.ds(b, n, stride=vb), :]
    st_ref[0] = state[0]
    st_ref[1] = state[1]

    @pl.when(ti == nt - 1)
    def _():
        xend_ref[0] = state[0]
        xend_ref[1] = state[1]


S5_PARTS = 2


def s5_scan(u, wts, x0, vbatch, emit_y, tt=512):
    width = u.shape[1]
    vseq = u.shape[0] // vbatch
    wb, wx, wu, lam2 = wts['wb'], wts['wx'], wts['wu'], wts['lam2']
    nj = wb.shape[1]
    k = lam2.shape[-1]
    lanes = width // nj
    nt = vseq // tt
    n = tt // S5_PARTS // 2

    def tmap(d, t):
        return jnp.where(d == 0, t, nt - 1 - t)

    state_spec = pl.BlockSpec((None, None, 2, vbatch, k), lambda d, j, t: (d, j, 0, 0, 0))
    state_sds = jax.ShapeDtypeStruct((2, nj, 2, vbatch, k), F32)
    out_specs, out_shape = [state_spec], [state_sds]
    half = pltpu.VMEM((n * vbatch, lanes), F32)
    part = [half, half, pltpu.VMEM((n * vbatch, 2 * k), F32)]
    if emit_y:
        part.append(pltpu.VMEM(((n + 1) * vbatch, 2 * k), F32))
    scratch = part * S5_PARTS + [pltpu.VMEM((2, vbatch, k), F32)]
    if emit_y:
        out_specs.insert(0, pl.BlockSpec((None, vbatch, tt, lanes), lambda d, j, t: (d, 0, tmap(d, t), j)))
        out_shape.insert(0, jax.ShapeDtypeStruct((2, vbatch, vseq, width), F32))
        scratch += [half, half]
    outs = pl.pallas_call(
        functools.partial(_s5_kernel, tt=tt, emit_y=emit_y),
        grid=(2, nj, nt),
        in_specs=[
            pl.BlockSpec((vbatch, tt, lanes), lambda d, j, t: (0, tmap(d, t), j)),
            pl.BlockSpec((None, None, 2 * lanes, 2 * k), lambda d, j, t: (d, j, 0, 0)),
            pl.BlockSpec((None, None, 2 * k, 2 * lanes), lambda d, j, t: (d, j, 0, 0)),
            pl.BlockSpec((None, None, 2 * lanes, 2 * lanes), lambda d, j, t: (d, j, 0, 0)),
            state_spec, state_spec,
        ],
        out_specs=out_specs,
        out_shape=out_shape,
        scratch_shapes=scratch,
        compiler_params=_cparams(("parallel", "parallel", "arbitrary")),
        name="s5_scan" if emit_y else "s5_states",
    )(u.reshape(vbatch, vseq, width), wb, wx, wu, lam2, x0)
    if emit_y:
        return outs[0].reshape(2, vbatch * vseq, width), outs[1]
    return None, outs[0]


def s5_segment_carries(xend, lam_bar, seg_len, batch, nseg):
    _, nj, _, _, k = xend.shape
    lam_pow = (lam_bar ** seg_len).reshape(2, nj, 1, k)
    e = lax.complex(xend[:, :, 0], xend[:, :, 1]).reshape(2, nj, batch, nseg, k)
    fwd = [jnp.zeros((nj, batch, k), jnp.complex64)]
    for s in range(1, nseg):
        fwd.append(lam_pow[0] * fwd[-1] + e[0, :, :, s - 1])
    bwd = [jnp.zeros((nj, batch, k), jnp.complex64)]
    for s in range(nseg - 2, -1, -1):
        bwd.append(lam_pow[1] * bwd[-1] + e[1, :, :, s + 1])
    bwd = bwd[::-1]
    x_in = jnp.stack([jnp.stack(fwd, axis=2), jnp.stack(bwd, axis=2)])
    x_in = x_in.reshape(2, nj, batch * nseg, k)
    return jnp.stack([x_in.real, x_in.imag], axis=2).astype(F32)


def _gelu_tanh(x):
    return 0.5 * x * (1.0 + jnp.tanh(math.sqrt(2.0 / math.pi) * (x + 0.044715 * x * x * x)))


def _glu_kernel(yf_ref, yb_ref, u_ref, d_ref, wa_ref, wb_ref, o_ref):
    y = _gelu_tanh(yf_ref[...] + yb_ref[...] + d_ref[...] * u_ref[...]).astype(BF16)
    za = jnp.dot(y, wa_ref[...], preferred_element_type=F32)
    zb = jnp.dot(y, wb_ref[...], preferred_element_type=F32)
    o_ref[...] = (za * _sigmoid(zb)).astype(o_ref.dtype)


def ssm_glu(y, u, d_skip, w_glu, tm=512):
    n, width = u.shape
    wa = w_glu[:, :width]
    wb = w_glu[:, width:]
    const = lambda i: (0, 0)
    return pl.pallas_call(
        _glu_kernel,
        grid=(n // tm,),
        in_specs=[
            pl.BlockSpec((None, tm, width), lambda i: (0, i, 0)),
            pl.BlockSpec((None, tm, width), lambda i: (1, i, 0)),
            pl.BlockSpec((tm, width), lambda i: (i, 0)),
            pl.BlockSpec((1, width), const),
            pl.BlockSpec((width, width), const),
            pl.BlockSpec((width, width), const),
        ],
        out_specs=pl.BlockSpec((tm, width), lambda i: (i, 0)),
        out_shape=jax.ShapeDtypeStruct((n, width), BF16),
        compiler_params=_cparams(("parallel",)),
        name="ssm_glu",
    )(y, y, u, d_skip.reshape(1, width).astype(F32), wa, wb)


def _split_bf16(x):
    hi = x.astype(BF16)
    lo = (x - hi.astype(F32)).astype(BF16)
    return hi, lo


def _pack_bf16_pair(a, b):
    wa = pltpu.bitcast(a.astype(BF16).astype(F32), jnp.int32)
    wb = pltpu.bitcast(b.astype(BF16).astype(F32), jnp.int32)
    return wa | lax.shift_right_logical(wb, 16)


def _unpack_bf16_pair(w):
    hi = pltpu.bitcast(w & jnp.int32(-65536), F32).astype(BF16)
    lo = pltpu.bitcast(lax.shift_left(w, 16), F32).astype(BF16)
    return hi, lo


def _mixer_out_kernel(o0, o1, o2, l0, l1, l2, ob_ref, gate_ref, x_ref, wa_ref, wb_ref, wo_ref,
                      g_ref, wr_ref, hbuf_ref, x1_ref, h_ref, lg_ref):
    del hbuf_ref
    d_model = x_ref.shape[1]
    la, lb, lc = l0[...], l1[...], l2[...]
    m = jnp.maximum(jnp.maximum(la, lb), lc)
    ea, eb, ec = jnp.exp(la - m), jnp.exp(lb - m), jnp.exp(lc - m)
    oa = (ea * o0[...] + eb * o1[...] + ec * o2[...]) / (ea + eb + ec)
    pa = jnp.dot(oa.astype(BF16), wa_ref[...], preferred_element_type=F32)
    pb = jnp.dot(ob_ref[...], wb_ref[...], preferred_element_type=F32)
    gates = gate_ref[...].astype(F32)
    merged = gates[:, :d_model] * pa + gates[:, d_model:] * pb
    x1 = x_ref[...] + jnp.dot(merged.astype(BF16), wo_ref[...], preferred_element_type=F32)
    x1_ref[...] = x1
    h = x1 * lax.rsqrt(jnp.mean(x1 * x1, axis=-1, keepdims=True) + EPS) * g_ref[...]
    h_ref[...] = _pack_bf16_pair(h[:, :d_model // 2], h[:, d_model // 2:])
    hh, hl = _split_bf16(h)
    er = lg_ref.shape[1]
    both = jnp.dot(hh, wr_ref[...], preferred_element_type=F32)
    lg_ref[...] = both[:, :er] + both[:, er:] + jnp.dot(hl, wr_ref[:, :er], preferred_element_type=F32)


def mixer_out(o_list, lse_list, o_b, gates, x, w_br_a, w_br_b, w_out, g_ffn, wr_split, h_buf, h_row0, tm=256):
    n, d_model = x.shape
    row = lambda w: pl.BlockSpec((tm, w), lambda i: (i, 0))
    full = lambda a: pl.BlockSpec(a.shape, lambda i: (0,) * a.ndim, pipeline_mode=pl.Buffered(1))
    g2 = g_ffn.reshape(1, d_model).astype(F32)
    er = wr_split.shape[1] // 2
    blk0 = h_row0 // tm
    in_specs = [row(o.shape[1]) for o in o_list] + [row(l.shape[1]) for l in lse_list] + [
        row(o_b.shape[1]), row(gates.shape[1]), row(d_model),
        full(w_br_a), full(w_br_b), full(w_out), full(g2), full(wr_split), pl.BlockSpec(memory_space=pl.ANY)]
    args = [*o_list, *lse_list, o_b, gates, x, w_br_a, w_br_b, w_out, g2, wr_split, h_buf]
    x1, h, logits = pl.pallas_call(
        _mixer_out_kernel,
        grid=(n // tm,),
        in_specs=in_specs,
        out_specs=[row(d_model), pl.BlockSpec((tm, d_model // 2), lambda i: (blk0 + i, 0)), row(er)],
        out_shape=[jax.ShapeDtypeStruct((n, d_model), F32), jax.ShapeDtypeStruct(h_buf.shape, h_buf.dtype),
                   jax.ShapeDtypeStruct((n, er), F32)],
        input_output_aliases={len(args) - 1: 1},
        compiler_params=_cparams(("parallel",)),
        name="mixer_out",
    )(*args)
    return x1, h, logits


def _expert_up_kernel(x_ref, wg_ref, wu_ref, o_ref, wg_s, wu_s):
    @pl.when(pl.program_id(2) == 0)
    def _():
        wg_s[...] = wg_ref[...].astype(BF16)
        wu_s[...] = wu_ref[...].astype(BF16)

    x_hi, x_lo = _unpack_bf16_pair(x_ref[...])
    half = x_hi.shape[1]

    def proj(w_s):
        return (jnp.dot(x_hi, w_s[:half, :], preferred_element_type=F32)
                + jnp.dot(x_lo, w_s[half:, :], preferred_element_type=F32))

    a = proj(wg_s)
    b = proj(wu_s)
    o_ref[...] = (a * _sigmoid(a) * b).astype(o_ref.dtype)


def _expert_up_into_kernel(x_ref, wg_ref, wu_ref, buf_ref, o_ref, wg_s, wu_s):
    del buf_ref
    _expert_up_kernel(x_ref, wg_ref, wu_ref, o_ref, wg_s, wu_s)


def expert_up(xe, w_gate, w_up, e0, hid=None, tm=1024, tf=512):
    n_e, cap, _ = xe.shape
    n_exp, d_model, ff = w_gate.shape
    tm = math.gcd(tm, cap)
    w_spec = pl.BlockSpec((None, d_model, tf), lambda ei, f, m: (e0 + ei, 0, f))
    in_specs = [pl.BlockSpec((None, tm, d_model // 2), lambda ei, f, m: (ei, m, 0)), w_spec, w_spec]
    args = [xe, w_gate, w_up]
    if hid is not None:
        in_specs.append(pl.BlockSpec(memory_space=pl.ANY))
        args.append(hid)
    return pl.pallas_call(
        _expert_up_kernel if hid is None else _expert_up_into_kernel,
        grid=(n_e, ff // tf, cap // tm),
        in_specs=in_specs,
        out_specs=pl.BlockSpec((None, tm, tf), lambda ei, f, m: (e0 + ei, m, f)),
        out_shape=jax.ShapeDtypeStruct((n_exp, cap, ff), BF16),
        scratch_shapes=[pltpu.VMEM((d_model, tf), BF16), pltpu.VMEM((d_model, tf), BF16)],
        input_output_aliases={} if hid is None else {3: 0},
        compiler_params=_cparams(("parallel", "parallel", "arbitrary")),
        name="expert_up",
    )(*args)


def _expert_down_kernel(h_ref, w_ref, o_ref, w_s):
    @pl.when(pl.program_id(2) == 0)
    def _():
        w_s[...] = w_ref[...].astype(BF16)

    acc = jnp.dot(h_ref[...], w_s[...], preferred_element_type=F32)
    half = acc.shape[1] // 2
    o_ref[...] = _pack_bf16_pair(acc[:, :half], acc[:, half:])


def expert_down(hid, w_down, tm=512, tn=1024):
    e, cap, ff = hid.shape
    d_model = w_down.shape[2]
    tm = math.gcd(tm, cap)
    tn = min(tn, d_model)
    return pl.pallas_call(
        _expert_down_kernel,
        grid=(e, d_model // tn, cap // tm),
        in_specs=[
            pl.BlockSpec((None, tm, ff), lambda ei, n, m: (ei, m, 0)),
            pl.BlockSpec((None, ff, tn), lambda ei, n, m: (ei, 0, n)),
        ],
        out_specs=pl.BlockSpec((None, tm, tn // 2), lambda ei, n, m: (ei, m, n)),
        out_shape=jax.ShapeDtypeStruct((e, cap, d_model // 2), jnp.int32),
        scratch_shapes=[pltpu.VMEM((ff, tn), BF16)],
        compiler_params=_cparams(("parallel", "parallel", "arbitrary")),
        name="expert_down",
    )(hid, w_down)


ROUTER_ROWS = 512


def _router_kernel(lg_ref, o_ref, aff_ref, *, n_exp, cap):
    n, lanes = lg_ref.shape
    t = ROUTER_ROWS
    nt = n // t
    real = lax.broadcasted_iota(jnp.int32, (1, lanes), 1) < n_exp

    def rows(i):
        return pl.ds(pl.multiple_of(i * t, t), t)

    def softmax_tile(i, carry):
        x = jnp.where(real, lg_ref[rows(i), :], -jnp.inf)
        e = jnp.exp(x - jnp.max(x, axis=-1, keepdims=True))
        aff_ref[rows(i), :] = e / jnp.sum(e, axis=-1, keepdims=True)
        return carry

    lax.fori_loop(0, nt, softmax_tile, 0)

    def count(pred):
        def body(i, acc):
            hit = pred(pltpu.bitcast(aff_ref[rows(i), :], jnp.int32)).astype(jnp.int32)
            return acc + jnp.sum(hit.reshape(t // 8, 8, lanes), axis=0)
        acc = lax.fori_loop(0, nt, body, jnp.zeros((8, lanes), jnp.int32))
        return jnp.sum(acc, axis=0, keepdims=True)

    thr = jnp.zeros((1, lanes), jnp.int32)
    for bit in range(30, -1, -1):
        cand = thr | jnp.int32(1 << bit)
        thr = jnp.where(count(lambda k, cand=cand: k >= cand) >= cap, cand, thr)
    ties_needed = (cap - count(lambda k: k > thr)).astype(F32)

    r_iota = lax.broadcasted_iota(jnp.int32, (t, t), 0)
    c_iota = lax.broadcasted_iota(jnp.int32, (t, t), 1)
    prefix = (c_iota <= r_iota).astype(BF16)

    def select_tile(i, ties_before):
        aff = aff_ref[rows(i), :]
        key = pltpu.bitcast(aff, jnp.int32)
        tie = (key == thr).astype(BF16)
        tie_rank = ties_before + jnp.dot(prefix, tie, preferred_element_type=F32)
        chosen = (key > thr) | ((key == thr) & (tie_rank <= ties_needed))
        chosen = jnp.where(real & chosen, 1.0, 0.0)
        o_ref[rows(i), :] = jnp.where(real, aff, 0.0) + pltpu.roll(chosen, n_exp, 1)
        return tie_rank[t - 1:t, :]

    lax.fori_loop(0, nt, select_tile, jnp.zeros((1, lanes), F32))


def route_expert_choice(logits, n_exp, cap):
    n, lanes = logits.shape
    whole = pl.BlockSpec(memory_space=pltpu.VMEM)
    routed = pl.pallas_call(
        functools.partial(_router_kernel, n_exp=n_exp, cap=cap),
        in_specs=[whole],
        out_specs=whole,
        out_shape=jax.ShapeDtypeStruct((n, lanes), F32),
        scratch_shapes=[pltpu.VMEM((n, lanes), F32)],
        compiler_params=pltpu.CompilerParams(vmem_limit_bytes=V7X_VMEM_LIMIT),
        name="route_expert_choice",
    )(logits)
    chosen = routed[:, n_exp:2 * n_exp].T.reshape(-1) > 0.5
    pos = jnp.nonzero(chosen, size=n_exp * cap, fill_value=0)[0].astype(jnp.int32)
    gate = routed[:, :n_exp].T.reshape(-1)[pos]
    return gate, pos % n


SC_GATHER_CHUNK = 16


def sc_gather_rows(table, idx):
    n_idx = idx.shape[0]
    width = table.shape[1]
    info = plsc.get_sparse_core_info()
    n_workers = info.num_cores * info.num_subcores
    per_worker = n_idx // n_workers
    ch = SC_GATHER_CHUNK
    assert n_idx % n_workers == 0 and per_worker % (2 * ch) == 0
    mesh = plsc.VectorSubcoreMesh(core_axis_name="c", subcore_axis_name="s")

    @functools.partial(
        pl.kernel, mesh=mesh,
        out_type=jax.ShapeDtypeStruct((n_idx, width), table.dtype),
        scratch_types=[pltpu.VMEM((per_worker,), jnp.int32), pltpu.VMEM((ch, width), table.dtype),
                       pltpu.VMEM((ch, width), table.dtype), pltpu.SemaphoreType.DMA, pltpu.SemaphoreType.DMA],
        name="sc_gather_rows",
    )
    def gather(table_hbm, idx_hbm, out_hbm, idx_v, rows_a, rows_b, sem_a, sem_b):
        base = (lax.axis_index("s") * info.num_cores + lax.axis_index("c")) * per_worker
        pltpu.sync_copy(idx_hbm.at[pl.ds(base, per_worker)], idx_v)

        @pl.loop(0, per_worker // (2 * ch))
        def _(c):
            off = c * 2 * ch
            copy_a = pltpu.async_copy(table_hbm.at[idx_v.at[pl.ds(off, ch)]], rows_a, sem_a)
            copy_b = pltpu.async_copy(table_hbm.at[idx_v.at[pl.ds(off + ch, ch)]], rows_b, sem_b)
            copy_a.wait()
            pltpu.sync_copy(rows_a, out_hbm.at[pl.ds(base + off, ch)])
            copy_b.wait()
            pltpu.sync_copy(rows_b, out_hbm.at[pl.ds(base + off + ch, ch)])

    return gather(table, idx)


COMBINE_TOKENS = 512
COMBINE_ROWS = 512


def _combine_kernel(tile_ref, chunk_ref, first_ref, valid_ref, rows_ref, tok_ref, g_ref, o_ref, *, tn):
    p = pl.program_id(0)

    @pl.when(first_ref[p] == 1)
    def _():
        o_ref[...] = jnp.zeros_like(o_ref)

    @pl.when(valid_ref[p] == 1)
    def _():
        t, ch = o_ref.shape[0], tok_ref.shape[1]
        token = tile_ref[p] * t + lax.broadcasted_iota(jnp.int32, (t, ch), 0)
        weights = jnp.where(tok_ref[...] == token, g_ref[...], 0.0).astype(BF16)
        hi, lo = _unpack_bf16_pair(rows_ref[...])
        acc_hi = jnp.dot(weights, hi, preferred_element_type=F32)
        acc_lo = jnp.dot(weights, lo, preferred_element_type=F32)
        half = tn // 2
        for n in range(o_ref.shape[1] // tn):
            o_ref[:, n * tn:n * tn + half] += acc_hi[:, n * half:(n + 1) * half]
            o_ref[:, n * tn + half:(n + 1) * tn] += acc_lo[:, n * half:(n + 1) * half]


def combine_expert_outputs(rows, tok_sorted, gate_sorted, n_tok, tn):
    m, half_d = rows.shape
    t, ch = COMBINE_TOKENS, COMBINE_ROWS
    n_tiles, n_chunks = n_tok // t, m // ch
    n_pairs = n_tiles + n_chunks
    edges = jnp.arange(n_tiles + 1, dtype=jnp.int32) * t
    bounds = jnp.sum(tok_sorted[None, :] < edges[:, None], axis=1, dtype=jnp.int32)
    start, end = bounds[:-1], bounds[1:]
    c_lo = jnp.minimum(start // ch, n_chunks - 1)
    c_hi = jnp.where(end > start, (end - 1) // ch, c_lo)
    count = c_hi - c_lo + 1
    offs = jnp.cumsum(count)
    p = jnp.arange(n_pairs, dtype=jnp.int32)
    tile = jnp.minimum(jnp.sum(offs[None, :] <= p[:, None], axis=1, dtype=jnp.int32), n_tiles - 1)
    first_p = (offs - count)[tile]
    valid = p < offs[-1]
    chunk = jnp.where(valid, c_lo[tile] + p - first_p, c_hi[-1]).astype(jnp.int32)
    first = (valid & (p == first_p)).astype(jnp.int32)
    grid_spec = pltpu.PrefetchScalarGridSpec(
        num_scalar_prefetch=4,
        grid=(n_pairs,),
        in_specs=[
            pl.BlockSpec((ch, half_d), lambda i, tl, ck, fr, vl: (ck[i], 0)),
            pl.BlockSpec((1, ch), lambda i, tl, ck, fr, vl: (0, ck[i])),
            pl.BlockSpec((1, ch), lambda i, tl, ck, fr, vl: (0, ck[i])),
        ],
        out_specs=pl.BlockSpec((t, 2 * half_d), lambda i, tl, ck, fr, vl: (tl[i], 0)),
    )
    return pl.pallas_call(
        functools.partial(_combine_kernel, tn=tn),
        grid_spec=grid_spec,
        out_shape=jax.ShapeDtypeStruct((n_tok, 2 * half_d), F32),
        compiler_params=_cparams(("arbitrary",)),
        name="combine_expert_outputs",
    )(tile, chunk, first, valid.astype(jnp.int32), rows, tok_sorted.reshape(1, m), gate_sorted.reshape(1, m).astype(F32))


def _final_kernel(x1_ref, moe_ref, p_ref, g_ref, wg_ref, wp_ref, o_ref):
    x2 = x1_ref[...] + moe_ref[...]
    h = (x2 * lax.rsqrt(jnp.mean(x2 * x2, axis=-1, keepdims=True) + EPS) * g_ref[...]).astype(BF16)
    gate = _sigmoid(jnp.dot(h, wg_ref[...], preferred_element_type=F32))
    proj = jnp.dot(p_ref[...].astype(BF16), wp_ref[...], preferred_element_type=F32)
    o_ref[...] = x2 + gate * proj


def final_embed(x1, moe, p, g_ple, w_gate, w_proj, tm=512):
    n, d_model = x1.shape
    row = lambda w: pl.BlockSpec((tm, w), lambda i: (i, 0))
    full = lambda a: pl.BlockSpec(a.shape, lambda i: (0,) * a.ndim, pipeline_mode=pl.Buffered(1))
    g2 = g_ple.reshape(1, d_model).astype(F32)
    return pl.pallas_call(
        _final_kernel,
        grid=(n // tm,),
        in_specs=[row(d_model), row(d_model), row(p.shape[1]), full(g2), full(w_gate), full(w_proj)],
        out_specs=row(d_model),
        out_shape=jax.ShapeDtypeStruct((n, d_model), F32),
        compiler_params=_cparams(("parallel",)),
        name="final_embed",
    )(x1, moe, p, g2, w_gate, w_proj)


def _prepare_weights(rel_bias, g_mix, w_in, q_gain, k_gain, ssm_a_re, ssm_a_im, ssm_log_step,
                     ssm_b_re, ssm_b_im, ssm_c_re, ssm_c_im, ssm_d, w_glu, w_br_a, w_br_b, w_out,
                     g_ffn, w_router, g_ple, w_ple_gate, w_ple_proj):
    n_heads = q_gain.shape[0]
    att_w = n_heads * HEAD_DIM
    hw = HEADS_PER_GROUP * HEAD_DIM
    ssm_w = ssm_d.shape[0]
    w_in_bf = w_in.astype(BF16)
    w_qkv, gains = [], []
    qg = q_gain.astype(F32) * (HEAD_DIM ** -0.5)
    kg = k_gain.astype(F32)
    for gi in range(len(ATT_PATTERNS)):
        cols = [w_in_bf[:, s * att_w + gi * hw: s * att_w + (gi + 1) * hw] for s in range(3)]
        w_qkv.append(jnp.concatenate(cols, axis=1))
        hs = slice(gi * HEADS_PER_GROUP, (gi + 1) * HEADS_PER_GROUP)
        gains.append(jnp.stack([qg[hs].reshape(1, hw), kg[hs].reshape(1, hw), jnp.ones((1, hw), F32)]))
    lam_bar, b_bar, c = ssm_discretise(ssm_a_re, ssm_a_im, ssm_log_step, ssm_b_re, ssm_b_im, ssm_c_re, ssm_c_im)
    wb, wx, wu, lam2 = ssm_block_weights(lam_bar, b_bar, c)
    n_exp = w_router.shape[1]
    wr = jnp.pad(w_router.astype(F32), ((0, 0), (0, 128 - n_exp)))
    wr_split = jnp.concatenate(_split_bf16(wr), axis=1)
    return dict(
        rel_bias=rel_bias, bias_tables={}, g_mix=g_mix, w_qkv=w_qkv, gains=gains,
        w_u=w_in_bf[:, 3 * att_w:3 * att_w + ssm_w], w_gates=w_in_bf[:, 3 * att_w + ssm_w:],
        lam_bar=lam_bar, wb=wb, wx=wx, wu=wu, lam2=lam2, ssm_d=ssm_d, w_glu=w_glu.astype(BF16),
        w_br_a=w_br_a.astype(BF16), w_br_b=w_br_b.astype(BF16), w_out=w_out.astype(BF16),
        g_ffn=g_ffn, wr_split=wr_split, n_exp=n_exp, g_ple=g_ple,
        w_ple_gate=w_ple_gate.astype(BF16), w_ple_proj=w_ple_proj.astype(BF16))


def _ssm_branch(hn, wts, batch, seq):
    nseg = max(1, 8 // batch)
    vbatch = batch * nseg
    vseq = seq // nseg
    u = plain_proj(hn, wts['w_u'], F32, False, "u_proj")
    nj, k = wts['lam2'].shape[1], wts['lam2'].shape[-1]
    x0 = jnp.zeros((2, nj, 2, vbatch, k), F32)
    if nseg > 1:
        _, xend = s5_scan(u, wts, x0, vbatch, emit_y=False)
        x0 = s5_segment_carries(xend, wts['lam_bar'], vseq, batch, nseg)
    y, _ = s5_scan(u, wts, x0, vbatch, emit_y=True)
    return ssm_glu(y, u, wts['ssm_d'], wts['w_glu'])


MOE_LEAD_EXPERTS = 1


def _expert_choice(h_all, n_toks, logits_list, n_exp, w_exp_gate, w_exp_up, w_exp_down):
    d_model = w_exp_down.shape[2]
    half_d = h_all.shape[1]
    caps = [CAPACITY_FACTOR * n // n_exp for n in n_toks]
    cap_all = sum(caps)
    routes, picks, row0 = [], [], 0
    for n, logits, cap in zip(n_toks, logits_list, caps):
        gate_flat, idx_flat = route_expert_choice(logits, n_exp, cap)
        routes.append((gate_flat, idx_flat))
        picks.append(idx_flat.reshape(n_exp, cap) + row0)
        row0 += n
    picks = jnp.concatenate(picks, axis=1)
    hid, lead = None, min(MOE_LEAD_EXPERTS, n_exp)
    for e0, e1 in ((0, lead), (lead, n_exp)):
        if e1 > e0:
            xe = sc_gather_rows(h_all, picks[e0:e1].reshape(-1)).reshape(e1 - e0, cap_all, half_d)
            hid = expert_up(xe, w_exp_gate, w_exp_up, e0, hid)
    tn = min(1024, d_model)
    ye = expert_down(hid, w_exp_down, tn=tn).reshape(n_exp * cap_all, half_d)
    outs, slot0 = [], 0
    for n, (gate_flat, idx_flat), cap in zip(n_toks, routes, caps):
        order = jnp.argsort(idx_flat).astype(jnp.int32)
        rows = (order // cap) * cap_all + slot0 + order % cap
        outs.append(combine_expert_outputs(sc_gather_rows(ye, rows), idx_flat[order], gate_flat[order], n, tn))
        slot0 += cap
    return outs


def _mixer(x, wts, h_buf, h_row0):
    batch, seq, d_model = x.shape
    n = batch * seq
    xf = x.reshape(n, d_model)
    dils = [dil for _, dil in ATT_PATTERNS]
    hn_list = rms_bf16(x, wts['g_mix'], dils)
    hn = hn_list[dils.index(1)]
    o_list, lse_list = [], []
    for gi, dil in enumerate(dils):
        qkv = qkv_proj(hn_list[gi], wts['w_qkv'][gi], wts['gains'][gi], f"qkv_proj_d{dil}")
        hs = slice(gi * HEADS_PER_GROUP, (gi + 1) * HEADS_PER_GROUP)
        bias_key = (gi,) + _attn_geometry(seq // dil)
        if bias_key not in wts['bias_tables']:
            wts['bias_tables'][bias_key] = attn_bias_tables(wts['rel_bias'][:, hs], dil, seq // dil)
        bias = wts['bias_tables'][bias_key]
        o, lse = dilated_attention(qkv, bias, batch, seq, dil)
        o_list.append(o)
        lse_list.append(lse)
    o_b = _ssm_branch(hn, wts, batch, seq)
    gates = plain_proj(hn, wts['w_gates'], BF16, True, "gates_proj")
    return mixer_out(o_list, lse_list, o_b, gates, xf, wts['w_br_a'], wts['w_br_b'],
                     wts['w_out'], wts['g_ffn'], wts['wr_split'], h_buf, h_row0)


def kernel(x_prompt, x_sample, p_prompt, p_sample, rel_bias, g_mix, w_in, q_gain, k_gain, ssm_a_re, ssm_a_im, ssm_log_step, ssm_b_re, ssm_b_im, ssm_c_re, ssm_c_im, ssm_d, w_glu, w_br_a, w_br_b, w_out, g_ffn, w_router, w_exp_gate, w_exp_up, w_exp_down, g_ple, w_ple_gate, w_ple_proj):
    assert g_mix.shape[0] == 1, "single-layer trunk"
    wts = _prepare_weights(rel_bias, g_mix[0], w_in[0], q_gain[0], k_gain[0], ssm_a_re[0], ssm_a_im[0],
                           ssm_log_step[0], ssm_b_re[0], ssm_b_im[0], ssm_c_re[0], ssm_c_im[0], ssm_d[0],
                           w_glu[0], w_br_a[0], w_br_b[0], w_out[0], g_ffn[0], w_router[0], g_ple[0],
                           w_ple_gate[0], w_ple_proj[0])
    xs, ps = (x_prompt, x_sample), (p_prompt[0], p_sample[0])
    n_toks = [x.shape[0] * x.shape[1] for x in xs]
    mixed, row0 = [], 0
    h_all = jnp.zeros((sum(n_toks), x_prompt.shape[-1] // 2), jnp.int32)
    for x, n in zip(xs, n_toks):
        x1, h_all, logits = _mixer(x, wts, h_all, row0)
        mixed.append((x1, logits))
        row0 += n
    moe = _expert_choice(h_all, n_toks, [m[1] for m in mixed], wts['n_exp'],
                         w_exp_gate[0], w_exp_up[0], w_exp_down[0])
    outs = []
    for x, p, (x1, _), y in zip(xs, ps, mixed, moe):
        out = final_embed(x1, y, p.reshape(-1, p.shape[-1]), wts['g_ple'], wts['w_ple_gate'], wts['w_ple_proj'])
        outs.append(out.reshape(x.shape))
    return tuple(outs)
```

```python
import functools
import math

import jax
import jax.numpy as jnp
import numpy as np
from jax import lax
from jax.experimental import pallas as pl
from jax.experimental.pallas import tpu as pltpu
from jax.experimental.pallas import tpu_sc as plsc

F32 = jnp.float32
BF16 = jnp.bfloat16

LANES = 128
HEAD_DIM = 128
HEADS_PER_GROUP = 4
ATT_PATTERNS = ((128, 1), (512, 4), (2048, 16))
ATT_SIDE = 64
N_BUCKETS = 32
MAX_DISTANCE = 1024
CAPACITY_FACTOR = 2
EPS = 1e-6
NEG = -1e30
SSM_LANE_GROUPS = 8
V7X_VMEM_LIMIT = 56 * 1024 * 1024

assert all(w // (2 * d) == ATT_SIDE for w, d in ATT_PATTERNS)


def _cparams(semantics, vmem=V7X_VMEM_LIMIT):
    return pltpu.CompilerParams(dimension_semantics=("arbitrary",) * len(semantics), vmem_limit_bytes=vmem)


def _sigmoid(x):
    return 1.0 / (1.0 + jnp.exp(-x))


def _rms_kernel(x_ref, g_ref, *refs, dils):
    o_refs, y_ref, yb_ref = refs[:-2], refs[-2], refs[-1]
    x = x_ref[...]
    y = x * lax.rsqrt(jnp.mean(x * x, axis=-1, keepdims=True) + EPS) * g_ref[...]
    tm = x.shape[0]
    n_chunks = y_ref.shape[0]
    base = RMS_BASE_DIL
    for c in range(n_chunks):
        y_ref[c] = y[:, c * LANES:(c + 1) * LANES]
    for o_ref, dil in sorted(zip(o_refs, dils), key=lambda od: od[1]):
        if dil == 1:
            o_ref[...] = y.astype(o_ref.dtype)
        elif dil == base:
            for r in range(dil):
                rows = [y_ref[c, pl.ds(r, tm // dil, stride=dil), :] for c in range(n_chunks)]
                for c in range(n_chunks):
                    yb_ref[c, r * (tm // dil):(r + 1) * (tm // dil), :] = rows[c]
                o_ref[r] = jnp.concatenate(rows, axis=1).astype(o_ref.dtype)
        else:
            assert base in dils and dil == base * base
            for r in range(dil):
                start = (r % base) * (tm // base) + r // base
                rows = [yb_ref[c, pl.ds(start, tm // dil, stride=base), :] for c in range(n_chunks)]
                o_ref[r] = jnp.concatenate(rows, axis=1).astype(o_ref.dtype)


RMS_BASE_DIL = 4


def rms_bf16(x, g, dils, tm=512):
    batch, seq, d = x.shape
    out_specs, out_shape = [], []
    for dil in dils:
        if dil == 1:
            out_specs.append(pl.BlockSpec((None, tm, d), lambda b, i: (b, i, 0)))
            out_shape.append(jax.ShapeDtypeStruct((batch, seq, d), BF16))
        else:
            out_specs.append(pl.BlockSpec((None, dil, tm // dil, d), lambda b, i: (b, 0, i, 0)))
            out_shape.append(jax.ShapeDtypeStruct((batch, dil, seq // dil, d), BF16))
    outs = pl.pallas_call(
        functools.partial(_rms_kernel, dils=tuple(dils)),
        grid=(batch, seq // tm),
        in_specs=[pl.BlockSpec((None, tm, d), lambda b, i: (b, i, 0)), pl.BlockSpec((1, d), lambda b, i: (0, 0))],
        out_specs=out_specs,
        out_shape=out_shape,
        scratch_shapes=[pltpu.VMEM((d // LANES, tm, LANES), F32)] * 2,
        compiler_params=_cparams(("parallel", "parallel")),
        name="rms_bf16",
    )(x, g.reshape(1, d).astype(F32))
    return [o.reshape(batch * seq, d) for o in outs]


def _head_rms(acc, gain):
    outs = []
    for h in range(acc.shape[1] // HEAD_DIM):
        t = acc[:, h * HEAD_DIM:(h + 1) * HEAD_DIM]
        outs.append(t * lax.rsqrt(jnp.mean(t * t, axis=-1, keepdims=True) + EPS))
    return jnp.concatenate(outs, axis=1) * gain


def _qkv_kernel(x_ref, w_ref, g_ref, o_ref):
    x = x_ref[...]
    tn = o_ref.shape[1] // 3
    for j in range(3):
        acc = jnp.dot(x, w_ref[:, j * tn:(j + 1) * tn], preferred_element_type=F32)
        if j < 2:
            acc = _head_rms(acc, g_ref[j])
        o_ref[:, j * tn:(j + 1) * tn] = acc.astype(o_ref.dtype)


def qkv_proj(hn, w, gains, name, tm=1024):
    n, d_model = hn.shape
    tn = HEADS_PER_GROUP * HEAD_DIM
    return pl.pallas_call(
        _qkv_kernel,
        grid=(n // tm,),
        in_specs=[
            pl.BlockSpec((tm, d_model), lambda i: (i, 0)),
            pl.BlockSpec((d_model, 3 * tn), lambda i: (0, 0), pipeline_mode=pl.Buffered(1)),
            pl.BlockSpec((3, 1, tn), lambda i: (0, 0, 0), pipeline_mode=pl.Buffered(1)),
        ],
        out_specs=pl.BlockSpec((tm, 3 * tn), lambda i: (i, 0)),
        out_shape=jax.ShapeDtypeStruct((n, 3 * tn), BF16),
        compiler_params=_cparams(("parallel",)),
        name=name,
    )(hn, w, gains)


def _proj_kernel(x_ref, w_ref, o_ref, *, act):
    acc = jnp.dot(x_ref[...], w_ref[...], preferred_element_type=F32)
    if act:
        acc = _sigmoid(acc)
    o_ref[...] = acc.astype(o_ref.dtype)


def plain_proj(hn, w, out_dtype, act, name, tm=1024, tn=1024):
    n, d_model = hn.shape
    width = w.shape[1]
    tn = min(tn, width)
    return pl.pallas_call(
        functools.partial(_proj_kernel, act=act),
        grid=(width // tn, n // tm),
        in_specs=[
            pl.BlockSpec((tm, d_model), lambda j, i: (i, 0)),
            pl.BlockSpec((d_model, tn), lambda j, i: (0, j)),
        ],
        out_specs=pl.BlockSpec((tm, tn), lambda j, i: (i, j)),
        out_shape=jax.ShapeDtypeStruct((n, width), out_dtype),
        compiler_params=_cparams(("arbitrary", "parallel")),
        name=name,
    )(hn, w)


def _t5_buckets(rel):
    half = N_BUCKETS // 2
    max_exact = half // 2
    n = np.abs(rel)
    large = max_exact + (np.log(np.maximum(n, 1) / max_exact) / math.log(MAX_DISTANCE / max_exact)
                         * (half - max_exact)).astype(np.int32)
    large = np.minimum(large, half - 1)
    return ((rel > 0).astype(np.int32) * half + np.where(n < max_exact, n, large)).astype(np.int32)


def _attn_geometry(l):
    tqb = min(2 * ATT_SIDE, l)
    win = min(4 * ATT_SIDE, l)
    return tqb, win


def attn_bias_tables(rel_bias_g, dil, l):
    tqb, win = _attn_geometry(l)
    i = np.arange(tqb)[:, None]
    j = np.arange(win)[None, :]
    rel = np.stack([j - i - ATT_SIDE * shift for shift in range(3)])
    valid = np.abs(rel) <= ATT_SIDE
    onehot = (_t5_buckets(dil * rel)[..., None] == np.arange(N_BUCKETS)).astype(np.int8)
    b = jnp.einsum('sijb,bh->shij', jnp.asarray(onehot).astype(F32), rel_bias_g.astype(F32),
                   precision=lax.Precision.HIGHEST)
    return jnp.where(valid[:, None], b, NEG)


def _attn_kernel(q_ref, k_ref, v_ref, bias_ref, o_ref, lse_ref, *stage, l, dil, tql):
    tqb, win = _attn_geometry(l)
    qi = pl.program_id(2)
    base = RMS_BASE_DIL
    tok = tql * dil
    out_o, out_lse = stage if stage else (o_ref, lse_ref)
    for r in range(dil):
        for c in range(tql // tqb):
            q0 = qi * tql + c * tqb
            start = pl.multiple_of(jnp.clip(q0 - ATT_SIDE, 0, l - win), ATT_SIDE)
            shift = (q0 - start) // ATT_SIDE
            q = q_ref[r, c * tqb:(c + 1) * tqb, :]
            kw = k_ref[r, pl.ds(start, win), :]
            vw = v_ref[r, pl.ds(start, win), :]
            logits = lax.dot_general(q, kw, (((1,), (1,)), ((), ())), preferred_element_type=F32)
            logits = logits + bias_ref[shift]
            m = jnp.max(logits, axis=-1, keepdims=True)
            p = jnp.exp(logits - m)
            s = jnp.sum(p, axis=-1, keepdims=True)
            o = jnp.dot(p.astype(vw.dtype), vw, preferred_element_type=F32)
            if stage:
                rows = pl.ds((r % base) * (tok // base) + c * tqb * base + r // base, tqb, stride=base)
            elif dil == 1:
                rows = pl.ds(c * tqb, tqb)
            else:
                rows = pl.ds(c * tqb * dil + r, tqb, stride=dil)
            out_o[rows, :] = o / s
            out_lse[rows, :] = jnp.broadcast_to(m + jnp.log(s), (tqb, HEAD_DIM))
    if stage:
        for rb in range(base):
            part = pl.ds(rb * (tok // base), tok // base)
            o_ref[pl.ds(rb, tok // base, stride=base), :] = out_o[part, :]
            lse_ref[pl.ds(rb, tok // base, stride=base), :] = out_lse[part, :]


def dilated_attention(qkv, bias, batch, seq, dil):
    l = seq // dil
    tqb, win = _attn_geometry(l)
    tql = min(l, max(tqb, 8 * tqb // dil))
    tok = tql * dil
    hw = HEADS_PER_GROUP * HEAD_DIM
    x = qkv.reshape(batch, dil, l, 3 * hw)
    out_sds = jax.ShapeDtypeStruct((batch, seq, hw), F32)
    out_spec = pl.BlockSpec((None, tok, HEAD_DIM), lambda b, h, i: (b, i, h))
    o, lse = pl.pallas_call(
        functools.partial(_attn_kernel, l=l, dil=dil, tql=tql),
        grid=(batch, HEADS_PER_GROUP, l // tql),
        in_specs=[
            pl.BlockSpec((None, dil, tql, HEAD_DIM), lambda b, h, i: (b, 0, i, h)),
            pl.BlockSpec((None, dil, l, HEAD_DIM), lambda b, h, i: (b, 0, 0, HEADS_PER_GROUP + h)),
            pl.BlockSpec((None, dil, l, HEAD_DIM), lambda b, h, i: (b, 0, 0, 2 * HEADS_PER_GROUP + h)),
            pl.BlockSpec((3, None, tqb, win), lambda b, h, i: (0, h, 0, 0)),
        ],
        out_specs=[out_spec, out_spec],
        out_shape=[out_sds, out_sds],
        scratch_shapes=[pltpu.VMEM((tok, HEAD_DIM), F32)] * 2 if dil == RMS_BASE_DIL ** 2 else [],
        compiler_params=_cparams(("parallel", "parallel", "parallel")),
        name=f"dilated_attn_d{dil}",
    )(x, x, x, bias)
    return o.reshape(batch * seq, hw), lse.reshape(batch * seq, hw)


def ssm_discretise(a_re, a_im, log_step, b_re, b_im, c_re, c_im):
    lam = lax.complex(a_re.astype(F32), a_im.astype(F32))
    step = jnp.exp(log_step.astype(F32))[..., None]
    lam_bar = jnp.exp(lam * step)
    b_bar = ((lam_bar - 1.0) / lam)[..., None] * lax.complex(b_re.astype(F32), b_im.astype(F32))
    c = lax.complex(c_re.astype(F32), c_im.astype(F32))
    return lam_bar, b_bar, c


def _block_diag(m):
    _, nj, lg, a, b = m.shape
    return jnp.einsum('djgab,gk->djgakb', m, jnp.eye(lg, dtype=F32)).reshape(2, nj, lg * a, lg * b)


def ssm_block_weights(lam_bar, b_bar, c):
    _, g, p, h = b_bar.shape
    lg = SSM_LANE_GROUPS
    nj = g // lg
    lam = lam_bar[..., None]
    ct = jnp.swapaxes(c, -1, -2)

    def grp(m):
        return m.reshape((2, nj, lg) + m.shape[2:])

    def state_in(bmat):
        t = grp(jnp.swapaxes(bmat, -1, -2))
        return jnp.concatenate([_block_diag(t.real), _block_diag(t.imag)], axis=-1)

    def state_out(cmat):
        t = grp(cmat)
        return jnp.concatenate([_block_diag(t.real), _block_diag(-t.imag)], axis=-2)

    def direct(k):
        m = jnp.einsum('dgpc,dgph->dghc', ct * lam ** k, b_bar).real
        return _block_diag(grp(m))

    first_in, second_in = state_in(lam * b_bar), state_in(b_bar)
    first_out, second_out = state_out(ct * lam), state_out(ct * lam ** 2)
    zero = jnp.zeros_like(direct(0))
    sel = jnp.array([0, 1]).reshape(2, 1, 1, 1)
    wb = jnp.where(sel == 0, jnp.concatenate([first_in, second_in], axis=-2),
                   jnp.concatenate([second_in, first_in], axis=-2))
    wx = jnp.where(sel == 0, jnp.concatenate([first_out, second_out], axis=-1),
                   jnp.concatenate([second_out, first_out], axis=-1))
    u_first = jnp.concatenate([direct(0), direct(1)], axis=-1)
    u_second = jnp.concatenate([zero, direct(0)], axis=-1)
    swap = lambda m: jnp.concatenate([m[..., 128:], m[..., :128]], axis=-1)
    wu = jnp.where(sel == 0, jnp.concatenate([u_first, u_second], axis=-2),
                   jnp.concatenate([swap(u_second), swap(u_first)], axis=-2))
    l2 = (lam_bar ** 2).reshape(2, nj, 1, 1, lg * p)
    lam2 = jnp.broadcast_to(jnp.concatenate([l2.real, l2.imag], axis=2), (2, nj, 2, 8, lg * p))
    return wb.astype(BF16), wx.astype(BF16), wu.astype(BF16), lam2.astype(F32)


def _s5_kernel(u_ref, wb_ref, wx_ref, wu_ref, lam_ref, x0_ref, *refs, tt, emit_y):
    n_out = 2 if emit_y else 1
    outs, scratch = refs[:n_out], list(refs[n_out:])
    xend_ref = outs[-1]
    parts = []
    for _ in range(S5_PARTS):
        parts.append([scratch.pop(0) for _ in range(4 if emit_y else 3)])
    st_ref = scratch.pop(0)
    direction = pl.program_id(0)
    ti = pl.program_id(2)
    nt = pl.num_programs(2)
    vb, _, lanes = u_ref.shape
    k = lam_ref.shape[-1]
    part_t = tt // S5_PARTS
    n = part_t // 2

    @pl.when(ti == 0)
    def _():
        st_ref[...] = x0_ref[...]

    lr = lam_ref[0]
    li = lam_ref[1]
    state = (st_ref[0], st_ref[1])
    for order, part in enumerate(parts):
        lhs_e, lhs_o, bu_ref = part[:3]
        t0 = jnp.where(direction == 0, order, S5_PARTS - 1 - order) * part_t
        for b in range(vb):
            lhs_e[pl.ds(b, n, stride=vb), :] = u_ref[b, pl.ds(t0, n, stride=2), :]
            lhs_o[pl.ds(b, n, stride=vb), :] = u_ref[b, pl.ds(t0 + 1, n, stride=2), :]
        lhs = jnp.concatenate([lhs_e[...], lhs_o[...]], axis=1).astype(BF16)
        bu_ref[...] = jnp.dot(lhs, wb_ref[...], preferred_element_type=F32)
        if emit_y:
            xs_ref = part[3]
            in_row = pl.multiple_of(jnp.where(direction == 0, 0, n * vb), vb)
            xs_ref[pl.ds(in_row, vb), :] = jnp.concatenate(state, axis=1)

        def step(s, carry, part=part):
            xr, xi = carry
            t = jnp.where(direction == 0, s, n - 1 - s)
            b = part[2][pl.ds(pl.multiple_of(t * vb, vb), vb), :]
            nr = lr * xr - li * xi + b[:, :k]
            ni = lr * xi + li * xr + b[:, k:]
            if emit_y:
                out_row = pl.multiple_of(jnp.where(direction == 0, t + 1, t) * vb, vb)
                part[3][pl.ds(out_row, vb), :] = jnp.concatenate([nr, ni], axis=1)
            return nr, ni

        state = lax.fori_loop(0, n, step, state, unroll=True)
        if emit_y:
            y_ref, ysc_e, ysc_o = outs[0], scratch[0], scratch[1]
            base = pl.multiple_of(jnp.where(direction == 0, 0, vb), vb)
            xs = xs_ref[pl.ds(base, n * vb), :].astype(BF16)
            y = (jnp.dot(xs, wx_ref[...], preferred_element_type=F32)
                 + jnp.dot(lhs, wu_ref[...], preferred_element_type=F32))
            ysc_e[...] = y[:, :lanes]
            ysc_o[...] = y[:, lanes:]
            for b in range(vb):
                y_ref[b, pl.ds(t0, n, stride=2), :] = ysc_e[pl.ds(b, n, stride=vb), :]
                y_ref[b, pl.ds(t0 + 1, n, stride=2), :] = ysc_o[pl.ds(b, n, stride=vb), :]
    st_ref[0] = state[0]
    st_ref[1] = state[1]

    @pl.when(ti == nt - 1)
    def _():
        xend_ref[0] = state[0]
        xend_ref[1] = state[1]


S5_PARTS = 2


def s5_scan(u, wts, x0, vbatch, emit_y, tt=512):
    width = u.shape[1]
    vseq = u.shape[0] // vbatch
    wb, wx, wu, lam2 = wts['wb'], wts['wx'], wts['wu'], wts['lam2']
    nj = wb.shape[1]
    k = lam2.shape[-1]
    lanes = width // nj
    nt = vseq // tt
    n = tt // S5_PARTS // 2

    def tmap(d, t):
        return jnp.where(d == 0, t, nt - 1 - t)

    state_spec = pl.BlockSpec((None, None, 2, vbatch, k), lambda d, j, t: (d, j, 0, 0, 0))
    state_sds = jax.ShapeDtypeStruct((2, nj, 2, vbatch, k), F32)
    out_specs, out_shape = [state_spec], [state_sds]
    half = pltpu.VMEM((n * vbatch, lanes), F32)
    part = [half, half, pltpu.VMEM((n * vbatch, 2 * k), F32)]
    if emit_y:
        part.append(pltpu.VMEM(((n + 1) * vbatch, 2 * k), F32))
    scratch = part * S5_PARTS + [pltpu.VMEM((2, vbatch, k), F32)]
    if emit_y:
        out_specs.insert(0, pl.BlockSpec((None, vbatch, tt, lanes), lambda d, j, t: (d, 0, tmap(d, t), j)))
        out_shape.insert(0, jax.ShapeDtypeStruct((2, vbatch, vseq, width), F32))
        scratch += [half, half]
    outs = pl.pallas_call(
        functools.partial(_s5_kernel, tt=tt, emit_y=emit_y),
        grid=(2, nj, nt),
        in_specs=[
            pl.BlockSpec((vbatch, tt, lanes), lambda d, j, t: (0, tmap(d, t), j)),
            pl.BlockSpec((None, None, 2 * lanes, 2 * k), lambda d, j, t: (d, j, 0, 0)),
            pl.BlockSpec((None, None, 2 * k, 2 * lanes), lambda d, j, t: (d, j, 0, 0)),
            pl.BlockSpec((None, None, 2 * lanes, 2 * lanes), lambda d, j, t: (d, j, 0, 0)),
            state_spec, state_spec,
        ],
        out_specs=out_specs,
        out_shape=out_shape,
        scratch_shapes=scratch,
        compiler_params=_cparams(("parallel", "parallel", "arbitrary")),
        name="s5_scan" if emit_y else "s5_states",
    )(u.reshape(vbatch, vseq, width), wb, wx, wu, lam2, x0)
    if emit_y:
        return outs[0].reshape(2, vbatch * vseq, width), outs[1]
    return None, outs[0]


def s5_segment_carries(xend, lam_bar, seg_len, batch, nseg):
    _, nj, _, _, k = xend.shape
    lam_pow = (lam_bar ** seg_len).reshape(2, nj, 1, k)
    e = lax.complex(xend[:, :, 0], xend[:, :, 1]).reshape(2, nj, batch, nseg, k)
    fwd = [jnp.zeros((nj, batch, k), jnp.complex64)]
    for s in range(1, nseg):
        fwd.append(lam_pow[0] * fwd[-1] + e[0, :, :, s - 1])
    bwd = [jnp.zeros((nj, batch, k), jnp.complex64)]
    for s in range(nseg - 2, -1, -1):
        bwd.append(lam_pow[1] * bwd[-1] + e[1, :, :, s + 1])
    bwd = bwd[::-1]
    x_in = jnp.stack([jnp.stack(fwd, axis=2), jnp.stack(bwd, axis=2)])
    x_in = x_in.reshape(2, nj, batch * nseg, k)
    return jnp.stack([x_in.real, x_in.imag], axis=2).astype(F32)


def _gelu_tanh(x):
    return 0.5 * x * (1.0 + jnp.tanh(math.sqrt(2.0 / math.pi) * (x + 0.044715 * x * x * x)))


def _glu_kernel(yf_ref, yb_ref, u_ref, d_ref, wa_ref, wb_ref, o_ref):
    y = _gelu_tanh(yf_ref[...] + yb_ref[...] + d_ref[...] * u_ref[...]).astype(BF16)
    za = jnp.dot(y, wa_ref[...], preferred_element_type=F32)
    zb = jnp.dot(y, wb_ref[...], preferred_element_type=F32)
    o_ref[...] = (za * _sigmoid(zb)).astype(o_ref.dtype)


def ssm_glu(y, u, d_skip, w_glu, tm=512):
    n, width = u.shape
    wa = w_glu[:, :width]
    wb = w_glu[:, width:]
    const = lambda i: (0, 0)
    return pl.pallas_call(
        _glu_kernel,
        grid=(n // tm,),
        in_specs=[
            pl.BlockSpec((None, tm, width), lambda i: (0, i, 0)),
            pl.BlockSpec((None, tm, width), lambda i: (1, i, 0)),
            pl.BlockSpec((tm, width), lambda i: (i, 0)),
            pl.BlockSpec((1, width), const),
            pl.BlockSpec((width, width), const),
            pl.BlockSpec((width, width), const),
        ],
        out_specs=pl.BlockSpec((tm, width), lambda i: (i, 0)),
        out_shape=jax.ShapeDtypeStruct((n, width), BF16),
        compiler_params=_cparams(("parallel",)),
        name="ssm_glu",
    )(y, y, u, d_skip.reshape(1, width).astype(F32), wa, wb)


def _split_bf16(x):
    hi = x.astype(BF16)
    lo = (x - hi.astype(F32)).astype(BF16)
    return hi, lo


def _pack_bf16_pair(a, b):
    wa = pltpu.bitcast(a.astype(BF16).astype(F32), jnp.int32)
    wb = pltpu.bitcast(b.astype(BF16).astype(F32), jnp.int32)
    return wa | lax.shift_right_logical(wb, 16)


def _unpack_bf16_pair(w):
    hi = pltpu.bitcast(w & jnp.int32(-65536), F32).astype(BF16)
    lo = pltpu.bitcast(lax.shift_left(w, 16), F32).astype(BF16)
    return hi, lo


def _mixer_out_kernel(o0, o1, o2, l0, l1, l2, ob_ref, gate_ref, x_ref, wa_ref, wb_ref, wo_ref,
                      g_ref, wr_ref, hbuf_ref, x1_ref, h_ref, lg_ref):
    del hbuf_ref
    d_model = x_ref.shape[1]
    la, lb, lc = l0[...], l1[...], l2[...]
    m = jnp.maximum(jnp.maximum(la, lb), lc)
    ea, eb, ec = jnp.exp(la - m), jnp.exp(lb - m), jnp.exp(lc - m)
    oa = (ea * o0[...] + eb * o1[...] + ec * o2[...]) / (ea + eb + ec)
    pa = jnp.dot(oa.astype(BF16), wa_ref[...], preferred_element_type=F32)
    pb = jnp.dot(ob_ref[...], wb_ref[...], preferred_element_type=F32)
    gates = gate_ref[...].astype(F32)
    merged = gates[:, :d_model] * pa + gates[:, d_model:] * pb
    x1 = x_ref[...] + jnp.dot(merged.astype(BF16), wo_ref[...], preferred_element_type=F32)
    x1_ref[...] = x1
    h = x1 * lax.rsqrt(jnp.mean(x1 * x1, axis=-1, keepdims=True) + EPS) * g_ref[...]
    h_ref[...] = _pack_bf16_pair(h[:, :d_model // 2], h[:, d_model // 2:])
    hh, hl = _split_bf16(h)
    er = lg_ref.shape[1]
    both = jnp.dot(hh, wr_ref[...], preferred_element_type=F32)
    lg_ref[...] = both[:, :er] + both[:, er:] + jnp.dot(hl, wr_ref[:, :er], preferred_element_type=F32)


def mixer_out(o_list, lse_list, o_b, gates, x, w_br_a, w_br_b, w_out, g_ffn, wr_split, h_buf, h_row0, tm=256):
    n, d_model = x.shape
    row = lambda w: pl.BlockSpec((tm, w), lambda i: (i, 0))
    full = lambda a: pl.BlockSpec(a.shape, lambda i: (0,) * a.ndim, pipeline_mode=pl.Buffered(1))
    g2 = g_ffn.reshape(1, d_model).astype(F32)
    er = wr_split.shape[1] // 2
    blk0 = h_row0 // tm
    in_specs = [row(o.shape[1]) for o in o_list] + [row(l.shape[1]) for l in lse_list] + [
        row(o_b.shape[1]), row(gates.shape[1]), row(d_model),
        full(w_br_a), full(w_br_b), full(w_out), full(g2), full(wr_split), pl.BlockSpec(memory_space=pl.ANY)]
    args = [*o_list, *lse_list, o_b, gates, x, w_br_a, w_br_b, w_out, g2, wr_split, h_buf]
    x1, h, logits = pl.pallas_call(
        _mixer_out_kernel,
        grid=(n // tm,),
        in_specs=in_specs,
        out_specs=[row(d_model), pl.BlockSpec((tm, d_model // 2), lambda i: (blk0 + i, 0)), row(er)],
        out_shape=[jax.ShapeDtypeStruct((n, d_model), F32), jax.ShapeDtypeStruct(h_buf.shape, h_buf.dtype),
                   jax.ShapeDtypeStruct((n, er), F32)],
        input_output_aliases={len(args) - 1: 1},
        compiler_params=_cparams(("parallel",)),
        name="mixer_out",
    )(*args)
    return x1, h, logits


def _expert_up_kernel(x_ref, wg_ref, wu_ref, o_ref, wg_s, wu_s):
    @pl.when(pl.program_id(2) == 0)
    def _():
        wg_s[...] = wg_ref[...].astype(BF16)
        wu_s[...] = wu_ref[...].astype(BF16)

    x_hi, x_lo = _unpack_bf16_pair(x_ref[...])
    half = x_hi.shape[1]

    def proj(w_s):
        return (jnp.dot(x_hi, w_s[:half, :], preferred_element_type=F32)
                + jnp.dot(x_lo, w_s[half:, :], preferred_element_type=F32))

    a = proj(wg_s)
    b = proj(wu_s)
    o_ref[...] = (a * _sigmoid(a) * b).astype(o_ref.dtype)


def _expert_up_into_kernel(x_ref, wg_ref, wu_ref, buf_ref, o_ref, wg_s, wu_s):
    del buf_ref
    _expert_up_kernel(x_ref, wg_ref, wu_ref, o_ref, wg_s, wu_s)


def expert_up(xe, w_gate, w_up, e0, hid=None, tm=1024, tf=512):
    n_e, cap, _ = xe.shape
    n_exp, d_model, ff = w_gate.shape
    tm = math.gcd(tm, cap)
    w_spec = pl.BlockSpec((None, d_model, tf), lambda ei, f, m: (e0 + ei, 0, f))
    in_specs = [pl.BlockSpec((None, tm, d_model // 2), lambda ei, f, m: (ei, m, 0)), w_spec, w_spec]
    args = [xe, w_gate, w_up]
    if hid is not None:
        in_specs.append(pl.BlockSpec(memory_space=pl.ANY))
        args.append(hid)
    return pl.pallas_call(
        _expert_up_kernel if hid is None else _expert_up_into_kernel,
        grid=(n_e, ff // tf, cap // tm),
        in_specs=in_specs,
        out_specs=pl.BlockSpec((None, tm, tf), lambda ei, f, m: (e0 + ei, m, f)),
        out_shape=jax.ShapeDtypeStruct((n_exp, cap, ff), BF16),
        scratch_shapes=[pltpu.VMEM((d_model, tf), BF16), pltpu.VMEM((d_model, tf), BF16)],
        input_output_aliases={} if hid is None else {3: 0},
        compiler_params=_cparams(("parallel", "parallel", "arbitrary")),
        name="expert_up",
    )(*args)


def _expert_down_kernel(h_ref, w_ref, o_ref, w_s):
    @pl.when(pl.program_id(2) == 0)
    def _():
        w_s[...] = w_ref[...].astype(BF16)

    acc = jnp.dot(h_ref[...], w_s[...], preferred_element_type=F32)
    half = acc.shape[1] // 2
    o_ref[...] = _pack_bf16_pair(acc[:, :half], acc[:, half:])


def expert_down(hid, w_down, tm=512, tn=1024):
    e, cap, ff = hid.shape
    d_model = w_down.shape[2]
    tm = math.gcd(tm, cap)
    tn = min(tn, d_model)
    return pl.pallas_call(
        _expert_down_kernel,
        grid=(e, d_model // tn, cap // tm),
        in_specs=[
            pl.BlockSpec((None, tm, ff), lambda ei, n, m: (ei, m, 0)),
            pl.BlockSpec((None, ff, tn), lambda ei, n, m: (ei, 0, n)),
        ],
        out_specs=pl.BlockSpec((None, tm, tn // 2), lambda ei, n, m: (ei, m, n)),
        out_shape=jax.ShapeDtypeStruct((e, cap, d_model // 2), jnp.int32),
        scratch_shapes=[pltpu.VMEM((ff, tn), BF16)],
        compiler_params=_cparams(("parallel", "parallel", "arbitrary")),
        name="expert_down",
    )(hid, w_down)


ROUTER_ROWS = 512


def _router_kernel(lg_ref, o_ref, aff_ref, *, n_exp, cap):
    n, lanes = lg_ref.shape
    t = ROUTER_ROWS
    nt = n // t
    real = lax.broadcasted_iota(jnp.int32, (1, lanes), 1) < n_exp

    def rows(i):
        return pl.ds(pl.multiple_of(i * t, t), t)

    def softmax_tile(i, carry):
        x = jnp.where(real, lg_ref[rows(i), :], -jnp.inf)
        e = jnp.exp(x - jnp.max(x, axis=-1, keepdims=True))
        aff_ref[rows(i), :] = e / jnp.sum(e, axis=-1, keepdims=True)
        return carry

    lax.fori_loop(0, nt, softmax_tile, 0)

    def count(pred):
        def body(i, acc):
            hit = pred(pltpu.bitcast(aff_ref[rows(i), :], jnp.int32)).astype(jnp.int32)
            return acc + jnp.sum(hit.reshape(t // 8, 8, lanes), axis=0)
        acc = lax.fori_loop(0, nt, body, jnp.zeros((8, lanes), jnp.int32))
        return jnp.sum(acc, axis=0, keepdims=True)

    thr = jnp.zeros((1, lanes), jnp.int32)
    for bit in range(30, -1, -1):
        cand = thr | jnp.int32(1 << bit)
        thr = jnp.where(count(lambda k, cand=cand: k >= cand) >= cap, cand, thr)
    ties_needed = (cap - count(lambda k: k > thr)).astype(F32)

    r_iota = lax.broadcasted_iota(jnp.int32, (t, t), 0)
    c_iota = lax.broadcasted_iota(jnp.int32, (t, t), 1)
    prefix = (c_iota <= r_iota).astype(BF16)

    def select_tile(i, ties_before):
        aff = aff_ref[rows(i), :]
        key = pltpu.bitcast(aff, jnp.int32)
        tie = (key == thr).astype(BF16)
        tie_rank = ties_before + jnp.dot(prefix, tie, preferred_element_type=F32)
        chosen = (key > thr) | ((key == thr) & (tie_rank <= ties_needed))
        chosen = jnp.where(real & chosen, 1.0, 0.0)
        o_ref[rows(i), :] = jnp.where(real, aff, 0.0) + pltpu.roll(chosen, n_exp, 1)
        return tie_rank[t - 1:t, :]

    lax.fori_loop(0, nt, select_tile, jnp.zeros((1, lanes), F32))


def route_expert_choice(logits, n_exp, cap):
    n, lanes = logits.shape
    whole = pl.BlockSpec(memory_space=pltpu.VMEM)
    routed = pl.pallas_call(
        functools.partial(_router_kernel, n_exp=n_exp, cap=cap),
        in_specs=[whole],
        out_specs=whole,
        out_shape=jax.ShapeDtypeStruct((n, lanes), F32),
        scratch_shapes=[pltpu.VMEM((n, lanes), F32)],
        compiler_params=pltpu.CompilerParams(vmem_limit_bytes=V7X_VMEM_LIMIT),
        name="route_expert_choice",
    )(logits)
    chosen = routed[:, n_exp:2 * n_exp].T.reshape(-1) > 0.5
    pos = jnp.nonzero(chosen, size=n_exp * cap, fill_value=0)[0].astype(jnp.int32)
    gate = routed[:, :n_exp].T.reshape(-1)[pos]
    return gate, pos % n


SC_GATHER_CHUNK = 16


def sc_gather_rows(table, idx):
    n_idx = idx.shape[0]
    width = table.shape[1]
    info = plsc.get_sparse_core_info()
    n_workers = info.num_cores * info.num_subcores
    per_worker = n_idx // n_workers
    ch = SC_GATHER_CHUNK
    assert n_idx % n_workers == 0 and per_worker % (2 * ch) == 0
    mesh = plsc.VectorSubcoreMesh(core_axis_name="c", subcore_axis_name="s")

    @functools.partial(
        pl.kernel, mesh=mesh,
        out_type=jax.ShapeDtypeStruct((n_idx, width), table.dtype),
        scratch_types=[pltpu.VMEM((per_worker,), jnp.int32), pltpu.VMEM((ch, width), table.dtype),
                       pltpu.VMEM((ch, width), table.dtype), pltpu.SemaphoreType.DMA, pltpu.SemaphoreType.DMA],
        name="sc_gather_rows",
    )
    def gather(table_hbm, idx_hbm, out_hbm, idx_v, rows_a, rows_b, sem_a, sem_b):
        base = (lax.axis_index("s") * info.num_cores + lax.axis_index("c")) * per_worker
        pltpu.sync_copy(idx_hbm.at[pl.ds(base, per_worker)], idx_v)

        @pl.loop(0, per_worker // (2 * ch))
        def _(c):
            off = c * 2 * ch
            copy_a = pltpu.async_copy(table_hbm.at[idx_v.at[pl.ds(off, ch)]], rows_a, sem_a)
            copy_b = pltpu.async_copy(table_hbm.at[idx_v.at[pl.ds(off + ch, ch)]], rows_b, sem_b)
            copy_a.wait()
            pltpu.sync_copy(rows_a, out_hbm.at[pl.ds(base + off, ch)])
            copy_b.wait()
            pltpu.sync_copy(rows_b, out_hbm.at[pl.ds(base + off + ch, ch)])

    return gather(table, idx)


COMBINE_TOKENS = 512
COMBINE_ROWS = 512


def _combine_kernel(tile_ref, chunk_ref, first_ref, valid_ref, rows_ref, tok_ref, g_ref, o_ref, *, tn):
    p = pl.program_id(0)

    @pl.when(first_ref[p] == 1)
    def _():
        o_ref[...] = jnp.zeros_like(o_ref)

    @pl.when(valid_ref[p] == 1)
    def _():
        t, ch = o_ref.shape[0], tok_ref.shape[1]
        token = tile_ref[p] * t + lax.broadcasted_iota(jnp.int32, (t, ch), 0)
        weights = jnp.where(tok_ref[...] == token, g_ref[...], 0.0).astype(BF16)
        hi, lo = _unpack_bf16_pair(rows_ref[...])
        acc_hi = jnp.dot(weights, hi, preferred_element_type=F32)
        acc_lo = jnp.dot(weights, lo, preferred_element_type=F32)
        half = tn // 2
        for n in range(o_ref.shape[1] // tn):
            o_ref[:, n * tn:n * tn + half] += acc_hi[:, n * half:(n + 1) * half]
            o_ref[:, n * tn + half:(n + 1) * tn] += acc_lo[:, n * half:(n + 1) * half]


def combine_expert_outputs(rows, tok_sorted, gate_sorted, n_tok, tn):
    m, half_d = rows.shape
    t, ch = COMBINE_TOKENS, COMBINE_ROWS
    n_tiles, n_chunks = n_tok // t, m // ch
    n_pairs = n_tiles + n_chunks
    edges = jnp.arange(n_tiles + 1, dtype=jnp.int32) * t
    bounds = jnp.sum(tok_sorted[None, :] < edges[:, None], axis=1, dtype=jnp.int32)
    start, end = bounds[:-1], bounds[1:]
    c_lo = jnp.minimum(start // ch, n_chunks - 1)
    c_hi = jnp.where(end > start, (end - 1) // ch, c_lo)
    count = c_hi - c_lo + 1
    offs = jnp.cumsum(count)
    p = jnp.arange(n_pairs, dtype=jnp.int32)
    tile = jnp.minimum(jnp.sum(offs[None, :] <= p[:, None], axis=1, dtype=jnp.int32), n_tiles - 1)
    first_p = (offs - count)[tile]
    valid = p < offs[-1]
    chunk = jnp.where(valid, c_lo[tile] + p - first_p, c_hi[-1]).astype(jnp.int32)
    first = (valid & (p == first_p)).astype(jnp.int32)
    grid_spec = pltpu.PrefetchScalarGridSpec(
        num_scalar_prefetch=4,
        grid=(n_pairs,),
        in_specs=[
            pl.BlockSpec((ch, half_d), lambda i, tl, ck, fr, vl: (ck[i], 0)),
            pl.BlockSpec((1, ch), lambda i, tl, ck, fr, vl: (0, ck[i])),
            pl.BlockSpec((1, ch), lambda i, tl, ck, fr, vl: (0, ck[i])),
        ],
        out_specs=pl.BlockSpec((t, 2 * half_d), lambda i, tl, ck, fr, vl: (tl[i], 0)),
    )
    return pl.pallas_call(
        functools.partial(_combine_kernel, tn=tn),
        grid_spec=grid_spec,
        out_shape=jax.ShapeDtypeStruct((n_tok, 2 * half_d), F32),
        compiler_params=_cparams(("arbitrary",)),
        name="combine_expert_outputs",
    )(tile, chunk, first, valid.astype(jnp.int32), rows, tok_sorted.reshape(1, m), gate_sorted.reshape(1, m).astype(F32))


def _final_kernel(x1_ref, moe_ref, p_ref, g_ref, wg_ref, wp_ref, o_ref):
    x2 = x1_ref[...] + moe_ref[...]
    h = (x2 * lax.rsqrt(jnp.mean(x2 * x2, axis=-1, keepdims=True) + EPS) * g_ref[...]).astype(BF16)
    gate = _sigmoid(jnp.dot(h, wg_ref[...], preferred_element_type=F32))
    proj = jnp.dot(p_ref[...].astype(BF16), wp_ref[...], preferred_element_type=F32)
    o_ref[...] = x2 + gate * proj


def final_embed(x1, moe, p, g_ple, w_gate, w_proj, tm=512):
    n, d_model = x1.shape
    row = lambda w: pl.BlockSpec((tm, w), lambda i: (i, 0))
    full = lambda a: pl.BlockSpec(a.shape, lambda i: (0,) * a.ndim, pipeline_mode=pl.Buffered(1))
    g2 = g_ple.reshape(1, d_model).astype(F32)
    return pl.pallas_call(
        _final_kernel,
        grid=(n // tm,),
        in_specs=[row(d_model), row(d_model), row(p.shape[1]), full(g2), full(w_gate), full(w_proj)],
        out_specs=row(d_model),
        out_shape=jax.ShapeDtypeStruct((n, d_model), F32),
        compiler_params=_cparams(("parallel",)),
        name="final_embed",
    )(x1, moe, p, g2, w_gate, w_proj)


def _prepare_weights(rel_bias, g_mix, w_in, q_gain, k_gain, ssm_a_re, ssm_a_im, ssm_log_step,
                     ssm_b_re, ssm_b_im, ssm_c_re, ssm_c_im, ssm_d, w_glu, w_br_a, w_br_b, w_out,
                     g_ffn, w_router, g_ple, w_ple_gate, w_ple_proj):
    n_heads = q_gain.shape[0]
    att_w = n_heads * HEAD_DIM
    hw = HEADS_PER_GROUP * HEAD_DIM
    ssm_w = ssm_d.shape[0]
    w_in_bf = w_in.astype(BF16)
    w_qkv, gains = [], []
    qg = q_gain.astype(F32) * (HEAD_DIM ** -0.5)
    kg = k_gain.astype(F32)
    for gi in range(len(ATT_PATTERNS)):
        cols = [w_in_bf[:, s * att_w + gi * hw: s * att_w + (gi + 1) * hw] for s in range(3)]
        w_qkv.append(jnp.concatenate(cols, axis=1))
        hs = slice(gi * HEADS_PER_GROUP, (gi + 1) * HEADS_PER_GROUP)
        gains.append(jnp.stack([qg[hs].reshape(1, hw), kg[hs].reshape(1, hw), jnp.ones((1, hw), F32)]))
    lam_bar, b_bar, c = ssm_discretise(ssm_a_re, ssm_a_im, ssm_log_step, ssm_b_re, ssm_b_im, ssm_c_re, ssm_c_im)
    wb, wx, wu, lam2 = ssm_block_weights(lam_bar, b_bar, c)
    n_exp = w_router.shape[1]
    wr = jnp.pad(w_router.astype(F32), ((0, 0), (0, 128 - n_exp)))
    wr_split = jnp.concatenate(_split_bf16(wr), axis=1)
    return dict(
        rel_bias=rel_bias, bias_tables={}, g_mix=g_mix, w_qkv=w_qkv, gains=gains,
        w_u=w_in_bf[:, 3 * att_w:3 * att_w + ssm_w], w_gates=w_in_bf[:, 3 * att_w + ssm_w:],
        lam_bar=lam_bar, wb=wb, wx=wx, wu=wu, lam2=lam2, ssm_d=ssm_d, w_glu=w_glu.astype(BF16),
        w_br_a=w_br_a.astype(BF16), w_br_b=w_br_b.astype(BF16), w_out=w_out.astype(BF16),
        g_ffn=g_ffn, wr_split=wr_split, n_exp=n_exp, g_ple=g_ple,
        w_ple_gate=w_ple_gate.astype(BF16), w_ple_proj=w_ple_proj.astype(BF16))


def _ssm_branch(hn, wts, batch, seq):
    nseg = max(1, 8 // batch)
    vbatch = batch * nseg
    vseq = seq // nseg
    u = plain_proj(hn, wts['w_u'], F32, False, "u_proj")
    nj, k = wts['lam2'].shape[1], wts['lam2'].shape[-1]
    x0 = jnp.zeros((2, nj, 2, vbatch, k), F32)
    if nseg > 1:
        _, xend = s5_scan(u, wts, x0, vbatch, emit_y=False)
        x0 = s5_segment_carries(xend, wts['lam_bar'], vseq, batch, nseg)
    y, _ = s5_scan(u, wts, x0, vbatch, emit_y=True)
    return ssm_glu(y, u, wts['ssm_d'], wts['w_glu'])


MOE_LEAD_EXPERTS = 1


def _expert_choice(h_all, n_toks, logits_list, n_exp, w_exp_gate, w_exp_up, w_exp_down):
    d_model = w_exp_down.shape[2]
    half_d = h_all.shape[1]
    caps = [CAPACITY_FACTOR * n // n_exp for n in n_toks]
    cap_all = sum(caps)
    routes, picks, row0 = [], [], 0
    for n, logits, cap in zip(n_toks, logits_list, caps):
        gate_flat, idx_flat = route_expert_choice(logits, n_exp, cap)
        routes.append((gate_flat, idx_flat))
        picks.append(idx_flat.reshape(n_exp, cap) + row0)
        row0 += n
    picks = jnp.concatenate(picks, axis=1)
    hid, lead = None, min(MOE_LEAD_EXPERTS, n_exp)
    for e0, e1 in ((0, lead), (lead, n_exp)):
        if e1 > e0:
            xe = sc_gather_rows(h_all, picks[e0:e1].reshape(-1)).reshape(e1 - e0, cap_all, half_d)
            hid = expert_up(xe, w_exp_gate, w_exp_up, e0, hid)
    tn = min(1024, d_model)
    ye = expert_down(hid, w_exp_down, tn=tn).reshape(n_exp * cap_all, half_d)
    outs, slot0 = [], 0
    for n, (gate_flat, idx_flat), cap in zip(n_toks, routes, caps):
        order = jnp.argsort(idx_flat).astype(jnp.int32)
        rows = (order // cap) * cap_all + slot0 + order % cap
        outs.append(combine_expert_outputs(sc_gather_rows(ye, rows), idx_flat[order], gate_flat[order], n, tn))
        slot0 += cap
    return outs


def _mixer(x, wts, h_buf, h_row0):
    batch, seq, d_model = x.shape
    n = batch * seq
    xf = x.reshape(n, d_model)
    dils = [dil for _, dil in ATT_PATTERNS]
    hn_list = rms_bf16(x, wts['g_mix'], dils)
    hn = hn_list[dils.index(1)]
    o_list, lse_list = [], []
    for gi, dil in enumerate(dils):
        qkv = qkv_proj(hn_list[gi], wts['w_qkv'][gi], wts['gains'][gi], f"qkv_proj_d{dil}")
        hs = slice(gi * HEADS_PER_GROUP, (gi + 1) * HEADS_PER_GROUP)
        bias_key = (gi,) + _attn_geometry(seq // dil)
        if bias_key not in wts['bias_tables']:
            wts['bias_tables'][bias_key] = attn_bias_tables(wts['rel_bias'][:, hs], dil, seq // dil)
        bias = wts['bias_tables'][bias_key]
        o, lse = dilated_attention(qkv, bias, batch, seq, dil)
        o_list.append(o)
        lse_list.append(lse)
    o_b = _ssm_branch(hn, wts, batch, seq)
    gates = plain_proj(hn, wts['w_gates'], BF16, True, "gates_proj")
    return mixer_out(o_list, lse_list, o_b, gates, xf, wts['w_br_a'], wts['w_br_b'],
                     wts['w_out'], wts['g_ffn'], wts['wr_split'], h_buf, h_row0)


def kernel(x_prompt, x_sample, p_prompt, p_sample, rel_bias, g_mix, w_in, q_gain, k_gain, ssm_a_re, ssm_a_im, ssm_log_step, ssm_b_re, ssm_b_im, ssm_c_re, ssm_c_im, ssm_d, w_glu, w_br_a, w_br_b, w_out, g_ffn, w_router, w_exp_gate, w_exp_up, w_exp_down, g_ple, w_ple_gate, w_ple_proj):
    assert g_mix.shape[0] == 1, "single-layer trunk"
    wts = _prepare_weights(rel_bias, g_mix[0], w_in[0], q_gain[0], k_gain[0], ssm_a_re[0], ssm_a_im[0],
                           ssm_log_step[0], ssm_b_re[0], ssm_b_im[0], ssm_c_re[0], ssm_c_im[0], ssm_d[0],
                           w_glu[0], w_br_a[0], w_br_b[0], w_out[0], g_ffn[0], w_router[0], g_ple[0],
                           w_ple_gate[0], w_ple_proj[0])
    xs, ps = (x_prompt, x_sample), (p_prompt[0], p_sample[0])
    n_toks = [x.shape[0] * x.shape[1] for x in xs]
    mixed, row0 = [], 0
    h_all = jnp.zeros((sum(n_toks), x_prompt.shape[-1] // 2), jnp.int32)
    for x, n in zip(xs, n_toks):
        x1, h_all, logits = _mixer(x, wts, h_all, row0)
        mixed.append((x1, logits))
        row0 += n
    moe = _expert_choice(h_all, n_toks, [m[1] for m in mixed], wts['n_exp'],
                         w_exp_gate[0], w_exp_up[0], w_exp_down[0])
    outs = []
    for x, p, (x1, _), y in zip(xs, ps, mixed, moe):
        out = final_embed(x1, y, p.reshape(-1, p.shape[-1]), wts['g_ple'], wts['w_ple_gate'], wts['w_ple_proj'])
        outs.append(out.reshape(x.shape))
    return tuple(outs)
```

```python
import functools
import math

import jax
import jax.numpy as jnp
import numpy as np
from jax import lax
from jax.experimental import pallas as pl
from jax.experimental.pallas import tpu as pltpu
from jax.experimental.pallas import tpu_sc as plsc

F32 = jnp.float32
BF16 = jnp.bfloat16

LANES = 128
HEAD_DIM = 128
HEADS_PER_GROUP = 4
ATT_PATTERNS = ((128, 1), (512, 4), (2048, 16))
ATT_SIDE = 64
N_BUCKETS = 32
MAX_DISTANCE = 1024
CAPACITY_FACTOR = 2
EPS = 1e-6
NEG = -1e30
SSM_LANE_GROUPS = 8
V7X_VMEM_LIMIT = 56 * 1024 * 1024

assert all(w // (2 * d) == ATT_SIDE for w, d in ATT_PATTERNS)


def _cparams(semantics, vmem=V7X_VMEM_LIMIT):
    return pltpu.CompilerParams(dimension_semantics=("arbitrary",) * len(semantics), vmem_limit_bytes=vmem)


def _sigmoid(x):
    return 1.0 / (1.0 + jnp.exp(-x))


def _rms_kernel(x_ref, g_ref, *refs, dils):
    o_refs, y_ref, yb_ref = refs[:-2], refs[-2], refs[-1]
    x = x_ref[...]
    y = x * lax.rsqrt(jnp.mean(x * x, axis=-1, keepdims=True) + EPS) * g_ref[...]
    tm = x.shape[0]
    n_chunks = y_ref.shape[0]
    base = RMS_BASE_DIL
    for c in range(n_chunks):
        y_ref[c] = y[:, c * LANES:(c + 1) * LANES]
    for o_ref, dil in sorted(zip(o_refs, dils), key=lambda od: od[1]):
        if dil == 1:
            o_ref[...] = y.astype(o_ref.dtype)
        elif dil == base:
            for r in range(dil):
                rows = [y_ref[c, pl.ds(r, tm // dil, stride=dil), :] for c in range(n_chunks)]
                for c in range(n_chunks):
                    yb_ref[c, r * (tm // dil):(r + 1) * (tm // dil), :] = rows[c]
                o_ref[r] = jnp.concatenate(rows, axis=1).astype(o_ref.dtype)
        else:
            assert base in dils and dil == base * base
            for r in range(dil):
                start = (r % base) * (tm // base) + r // base
                rows = [yb_ref[c, pl.ds(start, tm // dil, stride=base), :] for c in range(n_chunks)]
                o_ref[r] = jnp.concatenate(rows, axis=1).astype(o_ref.dtype)


RMS_BASE_DIL = 4


def rms_bf16(x, g, dils, tm=512):
    batch, seq, d = x.shape
    out_specs, out_shape = [], []
    for dil in dils:
        if dil == 1:
            out_specs.append(pl.BlockSpec((None, tm, d), lambda b, i: (b, i, 0)))
            out_shape.append(jax.ShapeDtypeStruct((batch, seq, d), BF16))
        else:
            out_specs.append(pl.BlockSpec((None, dil, tm // dil, d), lambda b, i: (b, 0, i, 0)))
            out_shape.append(jax.ShapeDtypeStruct((batch, dil, seq // dil, d), BF16))
    outs = pl.pallas_call(
        functools.partial(_rms_kernel, dils=tuple(dils)),
        grid=(batch, seq // tm),
        in_specs=[pl.BlockSpec((None, tm, d), lambda b, i: (b, i, 0)), pl.BlockSpec((1, d), lambda b, i: (0, 0))],
        out_specs=out_specs,
        out_shape=out_shape,
        scratch_shapes=[pltpu.VMEM((d // LANES, tm, LANES), F32)] * 2,
        compiler_params=_cparams(("parallel", "parallel")),
        name="rms_bf16",
    )(x, g.reshape(1, d).astype(F32))
    return [o.reshape(batch * seq, d) for o in outs]


def _head_rms(acc, gain):
    outs = []
    for h in range(acc.shape[1] // HEAD_DIM):
        t = acc[:, h * HEAD_DIM:(h + 1) * HEAD_DIM]
        outs.append(t * lax.rsqrt(jnp.mean(t * t, axis=-1, keepdims=True) + EPS))
    return jnp.concatenate(outs, axis=1) * gain


def _qkv_kernel(x_ref, w_ref, g_ref, o_ref):
    x = x_ref[...]
    tn = o_ref.shape[1] // 3
    for j in range(3):
        acc = jnp.dot(x, w_ref[:, j * tn:(j + 1) * tn], preferred_element_type=F32)
        if j < 2:
            acc = _head_rms(acc, g_ref[j])
        o_ref[:, j * tn:(j + 1) * tn] = acc.astype(o_ref.dtype)


def qkv_proj(hn, w, gains, name, tm=1024):
    n, d_model = hn.shape
    tn = HEADS_PER_GROUP * HEAD_DIM
    return pl.pallas_call(
        _qkv_kernel,
        grid=(n // tm,),
        in_specs=[
            pl.BlockSpec((tm, d_model), lambda i: (i, 0)),
            pl.BlockSpec((d_model, 3 * tn), lambda i: (0, 0), pipeline_mode=pl.Buffered(1)),
            pl.BlockSpec((3, 1, tn), lambda i: (0, 0, 0), pipeline_mode=pl.Buffered(1)),
        ],
        out_specs=pl.BlockSpec((tm, 3 * tn), lambda i: (i, 0)),
        out_shape=jax.ShapeDtypeStruct((n, 3 * tn), BF16),
        compiler_params=_cparams(("parallel",)),
        name=name,
    )(hn, w, gains)


def _proj_kernel(x_ref, w_ref, o_ref, *, act):
    acc = jnp.dot(x_ref[...], w_ref[...], preferred_element_type=F32)
    if act:
        acc = _sigmoid(acc)
    o_ref[...] = acc.astype(o_ref.dtype)


def plain_proj(hn, w, out_dtype, act, name, tm=1024, tn=1024):
    n, d_model = hn.shape
    width = w.shape[1]
    tn = min(tn, width)
    return pl.pallas_call(
        functools.partial(_proj_kernel, act=act),
        grid=(width // tn, n // tm),
        in_specs=[
            pl.BlockSpec((tm, d_model), lambda j, i: (i, 0)),
            pl.BlockSpec((d_model, tn), lambda j, i: (0, j)),
        ],
        out_specs=pl.BlockSpec((tm, tn), lambda j, i: (i, j)),
        out_shape=jax.ShapeDtypeStruct((n, width), out_dtype),
        compiler_params=_cparams(("arbitrary", "parallel")),
        name=name,
    )(hn, w)


def _t5_buckets(rel):
    half = N_BUCKETS // 2
    max_exact = half // 2
    n = np.abs(rel)
    large = max_exact + (np.log(np.maximum(n, 1) / max_exact) / math.log(MAX_DISTANCE / max_exact)
                         * (half - max_exact)).astype(np.int32)
    large = np.minimum(large, half - 1)
    return ((rel > 0).astype(np.int32) * half + np.where(n < max_exact, n, large)).astype(np.int32)


def _attn_geometry(l):
    tqb = min(2 * ATT_SIDE, l)
    win = min(4 * ATT_SIDE, l)
    return tqb, win


def attn_bias_tables(rel_bias_g, dil, l):
    tqb, win = _attn_geometry(l)
    i = np.arange(tqb)[:, None]
    j = np.arange(win)[None, :]
    rel = np.stack([j - i - ATT_SIDE * shift for shift in range(3)])
    valid = np.abs(rel) <= ATT_SIDE
    onehot = (_t5_buckets(dil * rel)[..., None] == np.arange(N_BUCKETS)).astype(np.int8)
    b = jnp.einsum('sijb,bh->shij', jnp.asarray(onehot).astype(F32), rel_bias_g.astype(F32),
                   precision=lax.Precision.HIGHEST)
    return jnp.where(valid[:, None], b, NEG)


def _attn_kernel(q_ref, k_ref, v_ref, bias_ref, o_ref, lse_ref, *stage, l, dil, tql):
    tqb, win = _attn_geometry(l)
    single_step = tql == l
    qi = 0 if single_step else pl.program_id(2)
    base = RMS_BASE_DIL
    tok = tql * dil
    out_o, out_lse = stage if stage else (o_ref, lse_ref)
    group = 2 if (l == win == tqb and dil % 2 == 0) else 1
    if group == 2:
        b0 = bias_ref[0]
        neg = jnp.full_like(b0, NEG)
        pair_bias = jnp.concatenate([jnp.concatenate([b0, neg], axis=1), jnp.concatenate([neg, b0], axis=1)], axis=0)
    for r in range(0, dil, group):
        for c in range(tql // tqb):
            q0 = qi * tql + c * tqb
            if single_step:
                start = min(max(q0 - ATT_SIDE, 0), l - win)
            else:
                start = pl.multiple_of(jnp.clip(q0 - ATT_SIDE, 0, l - win), ATT_SIDE)
            shift = (q0 - start) // ATT_SIDE
            if group == 2:
                q = q_ref[r:r + 2].reshape(2 * tqb, HEAD_DIM)
                kw = k_ref[r:r + 2].reshape(2 * win, HEAD_DIM)
                vw = v_ref[r:r + 2].reshape(2 * win, HEAD_DIM)
                bias = pair_bias
            else:
                q = q_ref[r, c * tqb:(c + 1) * tqb, :]
                kw = k_ref[r, pl.ds(start, win), :]
                vw = v_ref[r, pl.ds(start, win), :]
                bias = bias_ref[shift]
            logits = lax.dot_general(q, kw, (((1,), (1,)), ((), ())), preferred_element_type=F32)
            logits = logits + bias
            m = jnp.max(logits, axis=-1, keepdims=True)
            p = jnp.exp(logits - m)
            s = jnp.sum(p, axis=-1, keepdims=True)
            o = jnp.dot(p.astype(vw.dtype), vw, preferred_element_type=F32) / s
            lse = m + jnp.log(s)
            for g in range(group):
                rr = r + g
                if stage:
                    rows = pl.ds((rr % base) * (tok // base) + c * tqb * base + rr // base, tqb, stride=base)
                elif dil == 1:
                    rows = pl.ds(c * tqb, tqb)
                else:
                    rows = pl.ds(c * tqb * dil + rr, tqb, stride=dil)
                out_o[rows, :] = o[g * tqb:(g + 1) * tqb, :]
                out_lse[rows, :] = jnp.broadcast_to(lse[g * tqb:(g + 1) * tqb, :], (tqb, HEAD_DIM))
    if stage:
        for rb in range(base):
            part = pl.ds(rb * (tok // base), tok // base)
            o_ref[pl.ds(rb, tok // base, stride=base), :] = out_o[part, :]
            lse_ref[pl.ds(rb, tok // base, stride=base), :] = out_lse[part, :]


def dilated_attention(qkv, bias, batch, seq, dil):
    l = seq // dil
    tqb, win = _attn_geometry(l)
    tql = min(l, max(tqb, 8 * tqb // dil))
    tok = tql * dil
    hw = HEADS_PER_GROUP * HEAD_DIM
    x = qkv.reshape(batch, dil, l, 3 * hw)
    out_sds = jax.ShapeDtypeStruct((batch, seq, hw), F32)
    out_spec = pl.BlockSpec((None, tok, HEAD_DIM), lambda b, h, i: (b, i, h))
    o, lse = pl.pallas_call(
        functools.partial(_attn_kernel, l=l, dil=dil, tql=tql),
        grid=(batch, HEADS_PER_GROUP, l // tql),
        in_specs=[
            pl.BlockSpec((None, dil, tql, HEAD_DIM), lambda b, h, i: (b, 0, i, h)),
            pl.BlockSpec((None, dil, l, HEAD_DIM), lambda b, h, i: (b, 0, 0, HEADS_PER_GROUP + h)),
            pl.BlockSpec((None, dil, l, HEAD_DIM), lambda b, h, i: (b, 0, 0, 2 * HEADS_PER_GROUP + h)),
            pl.BlockSpec((3, None, tqb, win), lambda b, h, i: (0, h, 0, 0)),
        ],
        out_specs=[out_spec, out_spec],
        out_shape=[out_sds, out_sds],
        scratch_shapes=[pltpu.VMEM((tok, HEAD_DIM), F32)] * 2 if dil == RMS_BASE_DIL ** 2 else [],
        compiler_params=_cparams(("parallel", "parallel", "parallel")),
        name=f"dilated_attn_d{dil}",
    )(x, x, x, bias)
    return o.reshape(batch * seq, hw), lse.reshape(batch * seq, hw)


def ssm_discretise(a_re, a_im, log_step, b_re, b_im, c_re, c_im):
    lam = lax.complex(a_re.astype(F32), a_im.astype(F32))
    step = jnp.exp(log_step.astype(F32))[..., None]
    lam_bar = jnp.exp(lam * step)
    b_bar = ((lam_bar - 1.0) / lam)[..., None] * lax.complex(b_re.astype(F32), b_im.astype(F32))
    c = lax.complex(c_re.astype(F32), c_im.astype(F32))
    return lam_bar, b_bar, c


def _block_diag(m):
    _, nj, lg, a, b = m.shape
    return jnp.einsum('djgab,gk->djgakb', m, jnp.eye(lg, dtype=F32)).reshape(2, nj, lg * a, lg * b)


def ssm_block_weights(lam_bar, b_bar, c):
    _, g, p, h = b_bar.shape
    lg = SSM_LANE_GROUPS
    nj = g // lg
    lam = lam_bar[..., None]
    ct = jnp.swapaxes(c, -1, -2)

    def grp(m):
        return m.reshape((2, nj, lg) + m.shape[2:])

    def state_in(bmat):
        t = grp(jnp.swapaxes(bmat, -1, -2))
        return jnp.concatenate([_block_diag(t.real), _block_diag(t.imag)], axis=-1)

    def state_out(cmat):
        t = grp(cmat)
        return jnp.concatenate([_block_diag(t.real), _block_diag(-t.imag)], axis=-2)

    def direct(k):
        m = jnp.einsum('dgpc,dgph->dghc', ct * lam ** k, b_bar).real
        return _block_diag(grp(m))

    first_in, second_in = state_in(lam * b_bar), state_in(b_bar)
    first_out, second_out = state_out(ct * lam), state_out(ct * lam ** 2)
    zero = jnp.zeros_like(direct(0))
    sel = jnp.array([0, 1]).reshape(2, 1, 1, 1)
    wb = jnp.where(sel == 0, jnp.concatenate([first_in, second_in], axis=-2),
                   jnp.concatenate([second_in, first_in], axis=-2))
    wx = jnp.where(sel == 0, jnp.concatenate([first_out, second_out], axis=-1),
                   jnp.concatenate([second_out, first_out], axis=-1))
    u_first = jnp.concatenate([direct(0), direct(1)], axis=-1)
    u_second = jnp.concatenate([zero, direct(0)], axis=-1)
    swap = lambda m: jnp.concatenate([m[..., 128:], m[..., :128]], axis=-1)
    wu = jnp.where(sel == 0, jnp.concatenate([u_first, u_second], axis=-2),
                   jnp.concatenate([swap(u_second), swap(u_first)], axis=-2))
    l2 = (lam_bar ** 2).reshape(2, nj, 1, 1, lg * p)
    lam2 = jnp.broadcast_to(jnp.concatenate([l2.real, l2.imag], axis=2), (2, nj, 2, 8, lg * p))
    return wb.astype(BF16), wx.astype(BF16), wu.astype(BF16), lam2.astype(F32)


def _s5_kernel(u_ref, wb_ref, wx_ref, wu_ref, lam_ref, x0_ref, *refs, tt, emit_y):
    n_out = 2 if emit_y else 1
    outs, scratch = refs[:n_out], list(refs[n_out:])
    xend_ref = outs[-1]
    parts = []
    for _ in range(S5_PARTS):
        parts.append([scratch.pop(0) for _ in range(4 if emit_y else 3)])
    st_ref = scratch.pop(0)
    direction = pl.program_id(0)
    ti = pl.program_id(2)
    nt = pl.num_programs(2)
    vb, _, lanes = u_ref.shape
    k = lam_ref.shape[-1]
    part_t = tt // S5_PARTS
    n = part_t // 2

    @pl.when(ti == 0)
    def _():
        st_ref[...] = x0_ref[...]

    lr = lam_ref[0]
    li = lam_ref[1]
    state = (st_ref[0], st_ref[1])
    for order, part in enumerate(parts):
        lhs_e, lhs_o, bu_ref = part[:3]
        t0 = jnp.where(direction == 0, order, S5_PARTS - 1 - order) * part_t
        for b in range(vb):
            lhs_e[pl.ds(b, n, stride=vb), :] = u_ref[b, pl.ds(t0, n, stride=2), :]
            lhs_o[pl.ds(b, n, stride=vb), :] = u_ref[b, pl.ds(t0 + 1, n, stride=2), :]
        lhs = jnp.concatenate([lhs_e[...], lhs_o[...]], axis=1).astype(BF16)
        bu_ref[...] = jnp.dot(lhs, wb_ref[...], preferred_element_type=F32)
        if emit_y:
            xs_ref = part[3]
            in_row = pl.multiple_of(jnp.where(direction == 0, 0, n * vb), vb)
            xs_ref[pl.ds(in_row, vb), :] = jnp.concatenate(state, axis=1)

        def step(s, carry, part=part):
            xr, xi = carry
            t = jnp.where(direction == 0, s, n - 1 - s)
            b = part[2][pl.ds(pl.multiple_of(t * vb, vb), vb), :]
            nr = lr * xr - li * xi + b[:, :k]
            ni = lr * xi + li * xr + b[:, k:]
            if emit_y:
                out_row = pl.multiple_of(jnp.where(direction == 0, t + 1, t) * vb, vb)
                part[3][pl.ds(out_row, vb), :] = jnp.concatenate([nr, ni], axis=1)
            return nr, ni

        state = lax.fori_loop(0, n, step, state, unroll=True)
        if emit_y:
            y_ref, ysc_e, ysc_o = outs[0], scratch[0], scratch[1]
            base = pl.multiple_of(jnp.where(direction == 0, 0, vb), vb)
            xs = xs_ref[pl.ds(base, n * vb), :].astype(BF16)
            y = (jnp.dot(xs, wx_ref[...], preferred_element_type=F32)
                 + jnp.dot(lhs, wu_ref[...], preferred_element_type=F32))
            ysc_e[...] = y[:, :lanes]
            ysc_o[...] = y[:, lanes:]
            for b in range(vb):
                y_ref[b, pl.ds(t0, n, stride=2), :] = ysc_e[pl.ds(b, n, stride=vb), :]
                y_ref[b, pl.ds(t0 + 1, n, stride=2), :] = ysc_o[pl.ds(b, n, stride=vb), :]
    st_ref[0] = state[0]
    st_ref[1] = state[1]

    @pl.when(ti == nt - 1)
    def _():
        xend_ref[0] = state[0]
        xend_ref[1] = state[1]


S5_PARTS = 2


def s5_scan(u, wts, x0, vbatch, emit_y, tt=512):
    width = u.shape[1]
    vseq = u.shape[0] // vbatch
    wb, wx, wu, lam2 = wts['wb'], wts['wx'], wts['wu'], wts['lam2']
    nj = wb.shape[1]
    k = lam2.shape[-1]
    lanes = width // nj
    nt = vseq // tt
    n = tt // S5_PARTS // 2

    def tmap(d, t):
        return jnp.where(d == 0, t, nt - 1 - t)

    state_spec = pl.BlockSpec((None, None, 2, vbatch, k), lambda d, j, t: (d, j, 0, 0, 0))
    state_sds = jax.ShapeDtypeStruct((2, nj, 2, vbatch, k), F32)
    out_specs, out_shape = [state_spec], [state_sds]
    half = pltpu.VMEM((n * vbatch, lanes), F32)
    part = [half, half, pltpu.VMEM((n * vbatch, 2 * k), F32)]
    if emit_y:
        part.append(pltpu.VMEM(((n + 1) * vbatch, 2 * k), F32))
    scratch = part * S5_PARTS + [pltpu.VMEM((2, vbatch, k), F32)]
    if emit_y:
        out_specs.insert(0, pl.BlockSpec((None, vbatch, tt, lanes), lambda d, j, t: (d, 0, tmap(d, t), j)))
        out_shape.insert(0, jax.ShapeDtypeStruct((2, vbatch, vseq, width), F32))
        scratch += [half, half]
    outs = pl.pallas_call(
        functools.partial(_s5_kernel, tt=tt, emit_y=emit_y),
        grid=(2, nj, nt),
        in_specs=[
            pl.BlockSpec((vbatch, tt, lanes), lambda d, j, t: (0, tmap(d, t), j)),
            pl.BlockSpec((None, None, 2 * lanes, 2 * k), lambda d, j, t: (d, j, 0, 0)),
            pl.BlockSpec((None, None, 2 * k, 2 * lanes), lambda d, j, t: (d, j, 0, 0)),
            pl.BlockSpec((None, None, 2 * lanes, 2 * lanes), lambda d, j, t: (d, j, 0, 0)),
            state_spec, state_spec,
        ],
        out_specs=out_specs,
        out_shape=out_shape,
        scratch_shapes=scratch,
        compiler_params=_cparams(("parallel", "parallel", "arbitrary")),
        name="s5_scan" if emit_y else "s5_states",
    )(u.reshape(vbatch, vseq, width), wb, wx, wu, lam2, x0)
    if emit_y:
        return outs[0].reshape(2, vbatch * vseq, width), outs[1]
    return None, outs[0]


def s5_segment_carries(xend, lam_bar, seg_len, batch, nseg):
    _, nj, _, _, k = xend.shape
    lam_pow = (lam_bar ** seg_len).reshape(2, nj, 1, k)
    e = lax.complex(xend[:, :, 0], xend[:, :, 1]).reshape(2, nj, batch, nseg, k)
    fwd = [jnp.zeros((nj, batch, k), jnp.complex64)]
    for s in range(1, nseg):
        fwd.append(lam_pow[0] * fwd[-1] + e[0, :, :, s - 1])
    bwd = [jnp.zeros((nj, batch, k), jnp.complex64)]
    for s in range(nseg - 2, -1, -1):
        bwd.append(lam_pow[1] * bwd[-1] + e[1, :, :, s + 1])
    bwd = bwd[::-1]
    x_in = jnp.stack([jnp.stack(fwd, axis=2), jnp.stack(bwd, axis=2)])
    x_in = x_in.reshape(2, nj, batch * nseg, k)
    return jnp.stack([x_in.real, x_in.imag], axis=2).astype(F32)


def _gelu_tanh(x):
    return 0.5 * x * (1.0 + jnp.tanh(math.sqrt(2.0 / math.pi) * (x + 0.044715 * x * x * x)))


def _glu_kernel(yf_ref, yb_ref, u_ref, d_ref, wa_ref, wb_ref, o_ref):
    y = _gelu_tanh(yf_ref[...] + yb_ref[...] + d_ref[...] * u_ref[...]).astype(BF16)
    za = jnp.dot(y, wa_ref[...], preferred_element_type=F32)
    zb = jnp.dot(y, wb_ref[...], preferred_element_type=F32)
    o_ref[...] = (za * _sigmoid(zb)).astype(o_ref.dtype)


def ssm_glu(y, u, d_skip, w_glu, tm=512):
    n, width = u.shape
    wa = w_glu[:, :width]
    wb = w_glu[:, width:]
    const = lambda i: (0, 0)
    return pl.pallas_call(
        _glu_kernel,
        grid=(n // tm,),
        in_specs=[
            pl.BlockSpec((None, tm, width), lambda i: (0, i, 0)),
            pl.BlockSpec((None, tm, width), lambda i: (1, i, 0)),
            pl.BlockSpec((tm, width), lambda i: (i, 0)),
            pl.BlockSpec((1, width), const),
            pl.BlockSpec((width, width), const),
            pl.BlockSpec((width, width), const),
        ],
        out_specs=pl.BlockSpec((tm, width), lambda i: (i, 0)),
        out_shape=jax.ShapeDtypeStruct((n, width), BF16),
        compiler_params=_cparams(("parallel",)),
        name="ssm_glu",
    )(y, y, u, d_skip.reshape(1, width).astype(F32), wa, wb)


def _split_bf16(x):
    hi = x.astype(BF16)
    lo = (x - hi.astype(F32)).astype(BF16)
    return hi, lo


def _pack_bf16_pair(a, b):
    wa = pltpu.bitcast(a.astype(BF16).astype(F32), jnp.int32)
    wb = pltpu.bitcast(b.astype(BF16).astype(F32), jnp.int32)
    return wa | lax.shift_right_logical(wb, 16)


def _unpack_bf16_pair(w):
    hi = pltpu.bitcast(w & jnp.int32(-65536), F32).astype(BF16)
    lo = pltpu.bitcast(lax.shift_left(w, 16), F32).astype(BF16)
    return hi, lo


def _mixer_out_kernel(o0, o1, o2, l0, l1, l2, ob_ref, gate_ref, x_ref, wa_ref, wb_ref, wo_ref,
                      g_ref, wr_ref, hbuf_ref, x1_ref, h_ref, lg_ref):
    del hbuf_ref
    d_model = x_ref.shape[1]
    la, lb, lc = l0[...], l1[...], l2[...]
    m = jnp.maximum(jnp.maximum(la, lb), lc)
    ea, eb, ec = jnp.exp(la - m), jnp.exp(lb - m), jnp.exp(lc - m)
    oa = (ea * o0[...] + eb * o1[...] + ec * o2[...]) / (ea + eb + ec)
    pa = jnp.dot(oa.astype(BF16), wa_ref[...], preferred_element_type=F32)
    pb = jnp.dot(ob_ref[...], wb_ref[...], preferred_element_type=F32)
    gates = gate_ref[...].astype(F32)
    merged = gates[:, :d_model] * pa + gates[:, d_model:] * pb
    x1 = x_ref[...] + jnp.dot(merged.astype(BF16), wo_ref[...], preferred_element_type=F32)
    x1_ref[...] = x1
    h = x1 * lax.rsqrt(jnp.mean(x1 * x1, axis=-1, keepdims=True) + EPS) * g_ref[...]
    h_ref[...] = _pack_bf16_pair(h[:, :d_model // 2], h[:, d_model // 2:])
    hh, hl = _split_bf16(h)
    er = lg_ref.shape[1]
    both = jnp.dot(hh, wr_ref[...], preferred_element_type=F32)
    lg_ref[...] = both[:, :er] + both[:, er:] + jnp.dot(hl, wr_ref[:, :er], preferred_element_type=F32)


def mixer_out(o_list, lse_list, o_b, gates, x, w_br_a, w_br_b, w_out, g_ffn, wr_split, h_buf, h_row0, tm=256):
    n, d_model = x.shape
    row = lambda w: pl.BlockSpec((tm, w), lambda i: (i, 0))
    full = lambda a: pl.BlockSpec(a.shape, lambda i: (0,) * a.ndim, pipeline_mode=pl.Buffered(1))
    g2 = g_ffn.reshape(1, d_model).astype(F32)
    er = wr_split.shape[1] // 2
    blk0 = h_row0 // tm
    in_specs = [row(o.shape[1]) for o in o_list] + [row(l.shape[1]) for l in lse_list] + [
        row(o_b.shape[1]), row(gates.shape[1]), row(d_model),
        full(w_br_a), full(w_br_b), full(w_out), full(g2), full(wr_split), pl.BlockSpec(memory_space=pl.ANY)]
    args = [*o_list, *lse_list, o_b, gates, x, w_br_a, w_br_b, w_out, g2, wr_split, h_buf]
    x1, h, logits = pl.pallas_call(
        _mixer_out_kernel,
        grid=(n // tm,),
        in_specs=in_specs,
        out_specs=[row(d_model), pl.BlockSpec((tm, d_model // 2), lambda i: (blk0 + i, 0)), row(er)],
        out_shape=[jax.ShapeDtypeStruct((n, d_model), F32), jax.ShapeDtypeStruct(h_buf.shape, h_buf.dtype),
                   jax.ShapeDtypeStruct((n, er), F32)],
        input_output_aliases={len(args) - 1: 1},
        compiler_params=_cparams(("parallel",)),
        name="mixer_out",
    )(*args)
    return x1, h, logits


def _expert_up_kernel(x_ref, wg_ref, wu_ref, o_ref, wg_s, wu_s):
    @pl.when(pl.program_id(2) == 0)
    def _():
        wg_s[...] = wg_ref[...].astype(BF16)
        wu_s[...] = wu_ref[...].astype(BF16)

    x_hi, x_lo = _unpack_bf16_pair(x_ref[...])
    half = x_hi.shape[1]

    def proj(w_s):
        return (jnp.dot(x_hi, w_s[:half, :], preferred_element_type=F32)
                + jnp.dot(x_lo, w_s[half:, :], preferred_element_type=F32))

    a = proj(wg_s)
    b = proj(wu_s)
    o_ref[...] = (a * _sigmoid(a) * b).astype(o_ref.dtype)


def _expert_up_into_kernel(x_ref, wg_ref, wu_ref, buf_ref, o_ref, wg_s, wu_s):
    del buf_ref
    _expert_up_kernel(x_ref, wg_ref, wu_ref, o_ref, wg_s, wu_s)


def expert_up(xe, w_gate, w_up, e0, hid=None, tm=1024, tf=512):
    n_e, cap, _ = xe.shape
    n_exp, d_model, ff = w_gate.shape
    tm = math.gcd(tm, cap)
    w_spec = pl.BlockSpec((None, d_model, tf), lambda ei, f, m: (e0 + ei, 0, f))
    in_specs = [pl.BlockSpec((None, tm, d_model // 2), lambda ei, f, m: (ei, m, 0)), w_spec, w_spec]
    args = [xe, w_gate, w_up]
    if hid is not None:
        in_specs.append(pl.BlockSpec(memory_space=pl.ANY))
        args.append(hid)
    return pl.pallas_call(
        _expert_up_kernel if hid is None else _expert_up_into_kernel,
        grid=(n_e, ff // tf, cap // tm),
        in_specs=in_specs,
        out_specs=pl.BlockSpec((None, tm, tf), lambda ei, f, m: (e0 + ei, m, f)),
        out_shape=jax.ShapeDtypeStruct((n_exp, cap, ff), BF16),
        scratch_shapes=[pltpu.VMEM((d_model, tf), BF16), pltpu.VMEM((d_model, tf), BF16)],
        input_output_aliases={} if hid is None else {3: 0},
        compiler_params=_cparams(("parallel", "parallel", "arbitrary")),
        name="expert_up",
    )(*args)


def _expert_down_kernel(h_ref, w_ref, o_ref, w_s):
    @pl.when(pl.program_id(2) == 0)
    def _():
        w_s[...] = w_ref[...].astype(BF16)

    acc = jnp.dot(h_ref[...], w_s[...], preferred_element_type=F32)
    half = acc.shape[1] // 2
    o_ref[...] = _pack_bf16_pair(acc[:, :half], acc[:, half:])


def expert_down(hid, w_down, tm=512, tn=1024):
    e, cap, ff = hid.shape
    d_model = w_down.shape[2]
    tm = math.gcd(tm, cap)
    tn = min(tn, d_model)
    return pl.pallas_call(
        _expert_down_kernel,
        grid=(e, d_model // tn, cap // tm),
        in_specs=[
            pl.BlockSpec((None, tm, ff), lambda ei, n, m: (ei, m, 0)),
            pl.BlockSpec((None, ff, tn), lambda ei, n, m: (ei, 0, n)),
        ],
        out_specs=pl.BlockSpec((None, tm, tn // 2), lambda ei, n, m: (ei, m, n)),
        out_shape=jax.ShapeDtypeStruct((e, cap, d_model // 2), jnp.int32),
        scratch_shapes=[pltpu.VMEM((ff, tn), BF16)],
        compiler_params=_cparams(("parallel", "parallel", "arbitrary")),
        name="expert_down",
    )(hid, w_down)


ROUTER_ROWS = 512


def _router_kernel(lg_ref, o_ref, aff_ref, *, n_exp, cap):
    n, lanes = lg_ref.shape
    t = ROUTER_ROWS
    nt = n // t
    real = lax.broadcasted_iota(jnp.int32, (1, lanes), 1) < n_exp

    def rows(i):
        return pl.ds(pl.multiple_of(i * t, t), t)

    def softmax_tile(i, carry):
        x = jnp.where(real, lg_ref[rows(i), :], -jnp.inf)
        e = jnp.exp(x - jnp.max(x, axis=-1, keepdims=True))
        aff_ref[rows(i), :] = e / jnp.sum(e, axis=-1, keepdims=True)
        return carry

    lax.fori_loop(0, nt, softmax_tile, 0)

    def count(pred):
        def body(i, acc):
            hit = pred(pltpu.bitcast(aff_ref[rows(i), :], jnp.int32)).astype(jnp.int32)
            return acc + jnp.sum(hit.reshape(t // 8, 8, lanes), axis=0)
        acc = lax.fori_loop(0, nt, body, jnp.zeros((8, lanes), jnp.int32))
        return jnp.sum(acc, axis=0, keepdims=True)

    thr = jnp.zeros((1, lanes), jnp.int32)
    for bit in range(30, -1, -1):
        cand = thr | jnp.int32(1 << bit)
        thr = jnp.where(count(lambda k, cand=cand: k >= cand) >= cap, cand, thr)
    ties_needed = (cap - count(lambda k: k > thr)).astype(F32)

    r_iota = lax.broadcasted_iota(jnp.int32, (t, t), 0)
    c_iota = lax.broadcasted_iota(jnp.int32, (t, t), 1)
    prefix = (c_iota <= r_iota).astype(BF16)

    def select_tile(i, ties_before):
        aff = aff_ref[rows(i), :]
        key = pltpu.bitcast(aff, jnp.int32)
        tie = (key == thr).astype(BF16)
        tie_rank = ties_before + jnp.dot(prefix, tie, preferred_element_type=F32)
        chosen = (key > thr) | ((key == thr) & (tie_rank <= ties_needed))
        chosen = jnp.where(real & chosen, 1.0, 0.0)
        o_ref[rows(i), :] = jnp.where(real, aff, 0.0) + pltpu.roll(chosen, n_exp, 1)
        return tie_rank[t - 1:t, :]

    lax.fori_loop(0, nt, select_tile, jnp.zeros((1, lanes), F32))


def route_expert_choice(logits, n_exp, cap):
    n, lanes = logits.shape
    whole = pl.BlockSpec(memory_space=pltpu.VMEM)
    routed = pl.pallas_call(
        functools.partial(_router_kernel, n_exp=n_exp, cap=cap),
        in_specs=[whole],
        out_specs=whole,
        out_shape=jax.ShapeDtypeStruct((n, lanes), F32),
        scratch_shapes=[pltpu.VMEM((n, lanes), F32)],
        compiler_params=pltpu.CompilerParams(vmem_limit_bytes=V7X_VMEM_LIMIT),
        name="route_expert_choice",
    )(logits)
    chosen = routed[:, n_exp:2 * n_exp].T.reshape(-1) > 0.5
    pos = jnp.nonzero(chosen, size=n_exp * cap, fill_value=0)[0].astype(jnp.int32)
    gate = routed[:, :n_exp].T.reshape(-1)[pos]
    return gate, pos % n


SC_GATHER_CHUNK = 16


def sc_gather_rows(table, idx):
    n_idx = idx.shape[0]
    width = table.shape[1]
    info = plsc.get_sparse_core_info()
    n_workers = info.num_cores * info.num_subcores
    per_worker = n_idx // n_workers
    ch = SC_GATHER_CHUNK
    assert n_idx % n_workers == 0 and per_worker % (2 * ch) == 0
    mesh = plsc.VectorSubcoreMesh(core_axis_name="c", subcore_axis_name="s")

    @functools.partial(
        pl.kernel, mesh=mesh,
        out_type=jax.ShapeDtypeStruct((n_idx, width), table.dtype),
        scratch_types=[pltpu.VMEM((per_worker,), jnp.int32), pltpu.VMEM((ch, width), table.dtype),
                       pltpu.VMEM((ch, width), table.dtype), pltpu.SemaphoreType.DMA, pltpu.SemaphoreType.DMA],
        name="sc_gather_rows",
    )
    def gather(table_hbm, idx_hbm, out_hbm, idx_v, rows_a, rows_b, sem_a, sem_b):
        base = (lax.axis_index("s") * info.num_cores + lax.axis_index("c")) * per_worker
        pltpu.sync_copy(idx_hbm.at[pl.ds(base, per_worker)], idx_v)

        @pl.loop(0, per_worker // (2 * ch))
        def _(c):
            off = c * 2 * ch
            copy_a = pltpu.async_copy(table_hbm.at[idx_v.at[pl.ds(off, ch)]], rows_a, sem_a)
            copy_b = pltpu.async_copy(table_hbm.at[idx_v.at[pl.ds(off + ch, ch)]], rows_b, sem_b)
            copy_a.wait()
            pltpu.sync_copy(rows_a, out_hbm.at[pl.ds(base + off, ch)])
            copy_b.wait()
            pltpu.sync_copy(rows_b, out_hbm.at[pl.ds(base + off + ch, ch)])

    return gather(table, idx)


COMBINE_TOKENS = 512
COMBINE_ROWS = 512


def _combine_kernel(tile_ref, chunk_ref, first_ref, valid_ref, rows_ref, tok_ref, g_ref, o_ref, *, tn):
    p = pl.program_id(0)

    @pl.when(first_ref[p] == 1)
    def _():
        o_ref[...] = jnp.zeros_like(o_ref)

    @pl.when(valid_ref[p] == 1)
    def _():
        t, ch = o_ref.shape[0], tok_ref.shape[1]
        token = tile_ref[p] * t + lax.broadcasted_iota(jnp.int32, (t, ch), 0)
        weights = jnp.where(tok_ref[...] == token, g_ref[...], 0.0).astype(BF16)
        hi, lo = _unpack_bf16_pair(rows_ref[...])
        acc_hi = jnp.dot(weights, hi, preferred_element_type=F32)
        acc_lo = jnp.dot(weights, lo, preferred_element_type=F32)
        half = tn // 2
        for n in range(o_ref.shape[1] // tn):
            o_ref[:, n * tn:n * tn + half] += acc_hi[:, n * half:(n + 1) * half]
            o_ref[:, n * tn + half:(n + 1) * tn] += acc_lo[:, n * half:(n + 1) * half]


def combine_expert_outputs(rows, tok_sorted, gate_sorted, n_tok, tn):
    m, half_d = rows.shape
    t, ch = COMBINE_TOKENS, COMBINE_ROWS
    n_tiles, n_chunks = n_tok // t, m // ch
    n_pairs = n_tiles + n_chunks
    edges = jnp.arange(n_tiles + 1, dtype=jnp.int32) * t
    bounds = jnp.sum(tok_sorted[None, :] < edges[:, None], axis=1, dtype=jnp.int32)
    start, end = bounds[:-1], bounds[1:]
    c_lo = jnp.minimum(start // ch, n_chunks - 1)
    c_hi = jnp.where(end > start, (end - 1) // ch, c_lo)
    count = c_hi - c_lo + 1
    offs = jnp.cumsum(count)
    p = jnp.arange(n_pairs, dtype=jnp.int32)
    tile = jnp.minimum(jnp.sum(offs[None, :] <= p[:, None], axis=1, dtype=jnp.int32), n_tiles - 1)
    first_p = (offs - count)[tile]
    valid = p < offs[-1]
    chunk = jnp.where(valid, c_lo[tile] + p - first_p, c_hi[-1]).astype(jnp.int32)
    first = (valid & (p == first_p)).astype(jnp.int32)
    grid_spec = pltpu.PrefetchScalarGridSpec(
        num_scalar_prefetch=4,
        grid=(n_pairs,),
        in_specs=[
            pl.BlockSpec((ch, half_d), lambda i, tl, ck, fr, vl: (ck[i], 0)),
            pl.BlockSpec((1, ch), lambda i, tl, ck, fr, vl: (0, ck[i])),
            pl.BlockSpec((1, ch), lambda i, tl, ck, fr, vl: (0, ck[i])),
        ],
        out_specs=pl.BlockSpec((t, 2 * half_d), lambda i, tl, ck, fr, vl: (tl[i], 0)),
    )
    return pl.pallas_call(
        functools.partial(_combine_kernel, tn=tn),
        grid_spec=grid_spec,
        out_shape=jax.ShapeDtypeStruct((n_tok, 2 * half_d), F32),
        compiler_params=_cparams(("arbitrary",)),
        name="combine_expert_outputs",
    )(tile, chunk, first, valid.astype(jnp.int32), rows, tok_sorted.reshape(1, m), gate_sorted.reshape(1, m).astype(F32))


def _final_kernel(x1_ref, moe_ref, p_ref, g_ref, wg_ref, wp_ref, o_ref):
    x2 = x1_ref[...] + moe_ref[...]
    h = (x2 * lax.rsqrt(jnp.mean(x2 * x2, axis=-1, keepdims=True) + EPS) * g_ref[...]).astype(BF16)
    gate = _sigmoid(jnp.dot(h, wg_ref[...], preferred_element_type=F32))
    proj = jnp.dot(p_ref[...].astype(BF16), wp_ref[...], preferred_element_type=F32)
    o_ref[...] = x2 + gate * proj


def final_embed(x1, moe, p, g_ple, w_gate, w_proj, tm=512):
    n, d_model = x1.shape
    row = lambda w: pl.BlockSpec((tm, w), lambda i: (i, 0))
    full = lambda a: pl.BlockSpec(a.shape, lambda i: (0,) * a.ndim, pipeline_mode=pl.Buffered(1))
    g2 = g_ple.reshape(1, d_model).astype(F32)
    return pl.pallas_call(
        _final_kernel,
        grid=(n // tm,),
        in_specs=[row(d_model), row(d_model), row(p.shape[1]), full(g2), full(w_gate), full(w_proj)],
        out_specs=row(d_model),
        out_shape=jax.ShapeDtypeStruct((n, d_model), F32),
        compiler_params=_cparams(("parallel",)),
        name="final_embed",
    )(x1, moe, p, g2, w_gate, w_proj)


def _prepare_weights(rel_bias, g_mix, w_in, q_gain, k_gain, ssm_a_re, ssm_a_im, ssm_log_step,
                     ssm_b_re, ssm_b_im, ssm_c_re, ssm_c_im, ssm_d, w_glu, w_br_a, w_br_b, w_out,
                     g_ffn, w_router, g_ple, w_ple_gate, w_ple_proj):
    n_heads = q_gain.shape[0]
    att_w = n_heads * HEAD_DIM
    hw = HEADS_PER_GROUP * HEAD_DIM
    ssm_w = ssm_d.shape[0]
    w_in_bf = w_in.astype(BF16)
    w_qkv, gains = [], []
    qg = q_gain.astype(F32) * (HEAD_DIM ** -0.5)
    kg = k_gain.astype(F32)
    for gi in range(len(ATT_PATTERNS)):
        cols = [w_in_bf[:, s * att_w + gi * hw: s * att_w + (gi + 1) * hw] for s in range(3)]
        w_qkv.append(jnp.concatenate(cols, axis=1))
        hs = slice(gi * HEADS_PER_GROUP, (gi + 1) * HEADS_PER_GROUP)
        gains.append(jnp.stack([qg[hs].reshape(1, hw), kg[hs].reshape(1, hw), jnp.ones((1, hw), F32)]))
    lam_bar, b_bar, c = ssm_discretise(ssm_a_re, ssm_a_im, ssm_log_step, ssm_b_re, ssm_b_im, ssm_c_re, ssm_c_im)
    wb, wx, wu, lam2 = ssm_block_weights(lam_bar, b_bar, c)
    n_exp = w_router.shape[1]
    wr = jnp.pad(w_router.astype(F32), ((0, 0), (0, 128 - n_exp)))
    wr_split = jnp.concatenate(_split_bf16(wr), axis=1)
    return dict(
        rel_bias=rel_bias, bias_tables={}, g_mix=g_mix, w_qkv=w_qkv, gains=gains,
        w_u=w_in_bf[:, 3 * att_w:3 * att_w + ssm_w], w_gates=w_in_bf[:, 3 * att_w + ssm_w:],
        lam_bar=lam_bar, wb=wb, wx=wx, wu=wu, lam2=lam2, ssm_d=ssm_d, w_glu=w_glu.astype(BF16),
        w_br_a=w_br_a.astype(BF16), w_br_b=w_br_b.astype(BF16), w_out=w_out.astype(BF16),
        g_ffn=g_ffn, wr_split=wr_split, n_exp=n_exp, g_ple=g_ple,
        w_ple_gate=w_ple_gate.astype(BF16), w_ple_proj=w_ple_proj.astype(BF16))


def _ssm_branch(hn, wts, batch, seq):
    nseg = max(1, 8 // batch)
    vbatch = batch * nseg
    vseq = seq // nseg
    u = plain_proj(hn, wts['w_u'], F32, False, "u_proj")
    nj, k = wts['lam2'].shape[1], wts['lam2'].shape[-1]
    x0 = jnp.zeros((2, nj, 2, vbatch, k), F32)
    if nseg > 1:
        _, xend = s5_scan(u, wts, x0, vbatch, emit_y=False)
        x0 = s5_segment_carries(xend, wts['lam_bar'], vseq, batch, nseg)
    y, _ = s5_scan(u, wts, x0, vbatch, emit_y=True)
    return ssm_glu(y, u, wts['ssm_d'], wts['w_glu'])


MOE_LEAD_EXPERTS = 1


def _expert_choice(h_all, n_toks, logits_list, n_exp, w_exp_gate, w_exp_up, w_exp_down):
    d_model = w_exp_down.shape[2]
    half_d = h_all.shape[1]
    caps = [CAPACITY_FACTOR * n // n_exp for n in n_toks]
    cap_all = sum(caps)
    routes, picks, row0 = [], [], 0
    for n, logits, cap in zip(n_toks, logits_list, caps):
        gate_flat, idx_flat = route_expert_choice(logits, n_exp, cap)
        routes.append((gate_flat, idx_flat))
        picks.append(idx_flat.reshape(n_exp, cap) + row0)
        row0 += n
    picks = jnp.concatenate(picks, axis=1)
    hid, lead = None, min(MOE_LEAD_EXPERTS, n_exp)
    for e0, e1 in ((0, lead), (lead, n_exp)):
        if e1 > e0:
            xe = sc_gather_rows(h_all, picks[e0:e1].reshape(-1)).reshape(e1 - e0, cap_all, half_d)
            hid = expert_up(xe, w_exp_gate, w_exp_up, e0, hid)
    tn = min(1024, d_model)
    ye = expert_down(hid, w_exp_down, tn=tn).reshape(n_exp * cap_all, half_d)
    outs, slot0 = [], 0
    for n, (gate_flat, idx_flat), cap in zip(n_toks, routes, caps):
        order = jnp.argsort(idx_flat).astype(jnp.int32)
        rows = (order // cap) * cap_all + slot0 + order % cap
        outs.append(combine_expert_outputs(sc_gather_rows(ye, rows), idx_flat[order], gate_flat[order], n, tn))
        slot0 += cap
    return outs


def _mixer(x, wts, h_buf, h_row0):
    batch, seq, d_model = x.shape
    n = batch * seq
    xf = x.reshape(n, d_model)
    dils = [dil for _, dil in ATT_PATTERNS]
    hn_list = rms_bf16(x, wts['g_mix'], dils)
    hn = hn_list[dils.index(1)]
    o_list, lse_list = [], []
    for gi, dil in enumerate(dils):
        qkv = qkv_proj(hn_list[gi], wts['w_qkv'][gi], wts['gains'][gi], f"qkv_proj_d{dil}")
        hs = slice(gi * HEADS_PER_GROUP, (gi + 1) * HEADS_PER_GROUP)
        bias_key = (gi,) + _attn_geometry(seq // dil)
        if bias_key not in wts['bias_tables']:
            wts['bias_tables'][bias_key] = attn_bias_tables(wts['rel_bias'][:, hs], dil, seq // dil)
        bias = wts['bias_tables'][bias_key]
        o, lse = dilated_attention(qkv, bias, batch, seq, dil)
        o_list.append(o)
        lse_list.append(lse)
    o_b = _ssm_branch(hn, wts, batch, seq)
    gates = plain_proj(hn, wts['w_gates'], BF16, True, "gates_proj")
    return mixer_out(o_list, lse_list, o_b, gates, xf, wts['w_br_a'], wts['w_br_b'],
                     wts['w_out'], wts['g_ffn'], wts['wr_split'], h_buf, h_row0)


def kernel(x_prompt, x_sample, p_prompt, p_sample, rel_bias, g_mix, w_in, q_gain, k_gain, ssm_a_re, ssm_a_im, ssm_log_step, ssm_b_re, ssm_b_im, ssm_c_re, ssm_c_im, ssm_d, w_glu, w_br_a, w_br_b, w_out, g_ffn, w_router, w_exp_gate, w_exp_up, w_exp_down, g_ple, w_ple_gate, w_ple_proj):
    assert g_mix.shape[0] == 1, "single-layer trunk"
    wts = _prepare_weights(rel_bias, g_mix[0], w_in[0], q_gain[0], k_gain[0], ssm_a_re[0], ssm_a_im[0],
                           ssm_log_step[0], ssm_b_re[0], ssm_b_im[0], ssm_c_re[0], ssm_c_im[0], ssm_d[0],
                           w_glu[0], w_br_a[0], w_br_b[0], w_out[0], g_ffn[0], w_router[0], g_ple[0],
                           w_ple_gate[0], w_ple_proj[0])
    xs, ps = (x_prompt, x_sample), (p_prompt[0], p_sample[0])
    n_toks = [x.shape[0] * x.shape[1] for x in xs]
    mixed, row0 = [], 0
    h_all = jnp.zeros((sum(n_toks), x_prompt.shape[-1] // 2), jnp.int32)
    for x, n in zip(xs, n_toks):
        x1, h_all, logits = _mixer(x, wts, h_all, row0)
        mixed.append((x1, logits))
        row0 += n
    moe = _expert_choice(h_all, n_toks, [m[1] for m in mixed], wts['n_exp'],
                         w_exp_gate[0], w_exp_up[0], w_exp_down[0])
    outs = []
    for x, p, (x1, _), y in zip(xs, ps, mixed, moe):
        out = final_embed(x1, y, p.reshape(-1, p.shape[-1]), wts['g_ple'], wts['w_ple_gate'], wts['w_ple_proj'])
        outs.append(out.reshape(x.shape))
    return tuple(outs)
```

```python
import functools
import math

import jax
import jax.numpy as jnp
import numpy as np
from jax import lax
from jax.experimental import pallas as pl
from jax.experimental.pallas import tpu as pltpu
from jax.experimental.pallas import tpu_sc as plsc

F32 = jnp.float32
BF16 = jnp.bfloat16

LANES = 128
HEAD_DIM = 128
HEADS_PER_GROUP = 4
ATT_PATTERNS = ((128, 1), (512, 4), (2048, 16))
ATT_SIDE = 64
N_BUCKETS = 32
MAX_DISTANCE = 1024
CAPACITY_FACTOR = 2
EPS = 1e-6
NEG = -1e30
SSM_LANE_GROUPS = 8
V7X_VMEM_LIMIT = 56 * 1024 * 1024

assert all(w // (2 * d) == ATT_SIDE for w, d in ATT_PATTERNS)


def _cparams(semantics, vmem=V7X_VMEM_LIMIT):
    return pltpu.CompilerParams(dimension_semantics=("arbitrary",) * len(semantics), vmem_limit_bytes=vmem)


def _sigmoid(x):
    return 1.0 / (1.0 + jnp.exp(-x))


def _rms_kernel(x_ref, g_ref, *refs, dils):
    o_refs, y_ref, yb_ref = refs[:-2], refs[-2], refs[-1]
    x = x_ref[...]
    y = x * lax.rsqrt(jnp.mean(x * x, axis=-1, keepdims=True) + EPS) * g_ref[...]
    tm = x.shape[0]
    n_chunks = y_ref.shape[0]
    base = RMS_BASE_DIL
    for c in range(n_chunks):
        y_ref[c] = y[:, c * LANES:(c + 1) * LANES]
    for o_ref, dil in sorted(zip(o_refs, dils), key=lambda od: od[1]):
        if dil == 1:
            o_ref[...] = y.astype(o_ref.dtype)
        elif dil == base:
            for r in range(dil):
                rows = [y_ref[c, pl.ds(r, tm // dil, stride=dil), :] for c in range(n_chunks)]
                for c in range(n_chunks):
                    yb_ref[c, r * (tm // dil):(r + 1) * (tm // dil), :] = rows[c]
                o_ref[r] = jnp.concatenate(rows, axis=1).astype(o_ref.dtype)
        else:
            assert base in dils and dil == base * base
            for r in range(dil):
                start = (r % base) * (tm // base) + r // base
                rows = [yb_ref[c, pl.ds(start, tm // dil, stride=base), :] for c in range(n_chunks)]
                o_ref[r] = jnp.concatenate(rows, axis=1).astype(o_ref.dtype)


RMS_BASE_DIL = 4


def rms_bf16(x, g, dils, tm=512):
    batch, seq, d = x.shape
    out_specs, out_shape = [], []
    for dil in dils:
        if dil == 1:
            out_specs.append(pl.BlockSpec((None, tm, d), lambda b, i: (b, i, 0)))
            out_shape.append(jax.ShapeDtypeStruct((batch, seq, d), BF16))
        else:
            out_specs.append(pl.BlockSpec((None, dil, tm // dil, d), lambda b, i: (b, 0, i, 0)))
            out_shape.append(jax.ShapeDtypeStruct((batch, dil, seq // dil, d), BF16))
    outs = pl.pallas_call(
        functools.partial(_rms_kernel, dils=tuple(dils)),
        grid=(batch, seq // tm),
        in_specs=[pl.BlockSpec((None, tm, d), lambda b, i: (b, i, 0)), pl.BlockSpec((1, d), lambda b, i: (0, 0))],
        out_specs=out_specs,
        out_shape=out_shape,
        scratch_shapes=[pltpu.VMEM((d // LANES, tm, LANES), F32)] * 2,
        compiler_params=_cparams(("parallel", "parallel")),
        name="rms_bf16",
    )(x, g.reshape(1, d).astype(F32))
    return [o.reshape(batch * seq, d) for o in outs]


def _head_rms(acc, gain):
    outs = []
    for h in range(acc.shape[1] // HEAD_DIM):
        t = acc[:, h * HEAD_DIM:(h + 1) * HEAD_DIM]
        outs.append(t * lax.rsqrt(jnp.mean(t * t, axis=-1, keepdims=True) + EPS))
    return jnp.concatenate(outs, axis=1) * gain


def _qkv_kernel(x_ref, w_ref, g_ref, o_ref):
    x = x_ref[...]
    tn = o_ref.shape[1] // 3
    for j in range(3):
        acc = jnp.dot(x, w_ref[:, j * tn:(j + 1) * tn], preferred_element_type=F32)
        if j < 2:
            acc = _head_rms(acc, g_ref[j])
        o_ref[:, j * tn:(j + 1) * tn] = acc.astype(o_ref.dtype)


def qkv_proj(hn, w, gains, name, tm=1024):
    n, d_model = hn.shape
    tn = HEADS_PER_GROUP * HEAD_DIM
    return pl.pallas_call(
        _qkv_kernel,
        grid=(n // tm,),
        in_specs=[
            pl.BlockSpec((tm, d_model), lambda i: (i, 0)),
            pl.BlockSpec((d_model, 3 * tn), lambda i: (0, 0), pipeline_mode=pl.Buffered(1)),
            pl.BlockSpec((3, 1, tn), lambda i: (0, 0, 0), pipeline_mode=pl.Buffered(1)),
        ],
        out_specs=pl.BlockSpec((tm, 3 * tn), lambda i: (i, 0)),
        out_shape=jax.ShapeDtypeStruct((n, 3 * tn), BF16),
        compiler_params=_cparams(("parallel",)),
        name=name,
    )(hn, w, gains)


def _proj_kernel(x_ref, w_ref, o_ref, *, act):
    acc = jnp.dot(x_ref[...], w_ref[...], preferred_element_type=F32)
    if act:
        acc = _sigmoid(acc)
    o_ref[...] = acc.astype(o_ref.dtype)


def plain_proj(hn, w, out_dtype, act, name, tm=1024, tn=1024):
    n, d_model = hn.shape
    width = w.shape[1]
    tn = min(tn, width)
    return pl.pallas_call(
        functools.partial(_proj_kernel, act=act),
        grid=(width // tn, n // tm),
        in_specs=[
            pl.BlockSpec((tm, d_model), lambda j, i: (i, 0)),
            pl.BlockSpec((d_model, tn), lambda j, i: (0, j)),
        ],
        out_specs=pl.BlockSpec((tm, tn), lambda j, i: (i, j)),
        out_shape=jax.ShapeDtypeStruct((n, width), out_dtype),
        compiler_params=_cparams(("arbitrary", "parallel")),
        name=name,
    )(hn, w)


def _t5_buckets(rel):
    half = N_BUCKETS // 2
    max_exact = half // 2
    n = np.abs(rel)
    large = max_exact + (np.log(np.maximum(n, 1) / max_exact) / math.log(MAX_DISTANCE / max_exact)
                         * (half - max_exact)).astype(np.int32)
    large = np.minimum(large, half - 1)
    return ((rel > 0).astype(np.int32) * half + np.where(n < max_exact, n, large)).astype(np.int32)


def _attn_geometry(l):
    tqb = min(2 * ATT_SIDE, l)
    win = min(4 * ATT_SIDE, l)
    return tqb, win


def attn_bias_tables(rel_bias_g, dil, l):
    tqb, win = _attn_geometry(l)
    i = np.arange(tqb)[:, None]
    j = np.arange(win)[None, :]
    rel = np.stack([j - i - ATT_SIDE * shift for shift in range(3)])
    valid = np.abs(rel) <= ATT_SIDE
    onehot = (_t5_buckets(dil * rel)[..., None] == np.arange(N_BUCKETS)).astype(np.int8)
    b = jnp.einsum('sijb,bh->shij', jnp.asarray(onehot).astype(F32), rel_bias_g.astype(F32),
                   precision=lax.Precision.HIGHEST)
    return jnp.where(valid[:, None], b, NEG)


def _attn_kernel(q_ref, k_ref, v_ref, bias_ref, o_ref, lse_ref, *stage, l, dil, tql):
    tqb, win = _attn_geometry(l)
    single_step = tql == l
    qi = 0 if single_step else pl.program_id(2)
    base = RMS_BASE_DIL
    tok = tql * dil
    out_o, out_lse = stage if stage else (o_ref, lse_ref)
    group = 2 if (l == win == tqb and dil % 2 == 0) else 1
    if group == 2:
        b0 = bias_ref[0]
        neg = jnp.full_like(b0, NEG)
        pair_bias = jnp.concatenate([jnp.concatenate([b0, neg], axis=1), jnp.concatenate([neg, b0], axis=1)], axis=0)
    for r in range(0, dil, group):
        for c in range(tql // tqb):
            q0 = qi * tql + c * tqb
            if single_step:
                start = min(max(q0 - ATT_SIDE, 0), l - win)
            else:
                start = pl.multiple_of(jnp.clip(q0 - ATT_SIDE, 0, l - win), ATT_SIDE)
            shift = (q0 - start) // ATT_SIDE
            if group == 2:
                q = q_ref[r:r + 2].reshape(2 * tqb, HEAD_DIM)
                kw = k_ref[r:r + 2].reshape(2 * win, HEAD_DIM)
                vw = v_ref[r:r + 2].reshape(2 * win, HEAD_DIM)
                bias = pair_bias
            else:
                q = q_ref[r, c * tqb:(c + 1) * tqb, :]
                kw = k_ref[r, pl.ds(start, win), :]
                vw = v_ref[r, pl.ds(start, win), :]
                bias = bias_ref[shift]
            logits = lax.dot_general(q, kw, (((1,), (1,)), ((), ())), preferred_element_type=F32)
            logits = logits + bias
            m = jnp.max(logits, axis=-1, keepdims=True)
            p = jnp.exp(logits - m)
            s = jnp.sum(p, axis=-1, keepdims=True)
            o = jnp.dot(p.astype(vw.dtype), vw, preferred_element_type=F32) / s
            lse = m + jnp.log(s)
            for g in range(group):
                rr = r + g
                if stage:
                    rows = pl.ds((rr % base) * (tok // base) + c * tqb * base + rr // base, tqb, stride=base)
                elif dil == 1:
                    rows = pl.ds(c * tqb, tqb)
                else:
                    rows = pl.ds(c * tqb * dil + rr, tqb, stride=dil)
                out_o[rows, :] = o[g * tqb:(g + 1) * tqb, :]
                out_lse[rows, :] = jnp.broadcast_to(lse[g * tqb:(g + 1) * tqb, :], (tqb, HEAD_DIM))
    if stage:
        for rb in range(base):
            part = pl.ds(rb * (tok // base), tok // base)
            o_ref[pl.ds(rb, tok // base, stride=base), :] = out_o[part, :]
            lse_ref[pl.ds(rb, tok // base, stride=base), :] = out_lse[part, :]


def dilated_attention(qkv, bias, batch, seq, dil):
    l = seq // dil
    tqb, win = _attn_geometry(l)
    tql = min(l, max(tqb, 32 * tqb // dil))
    tok = tql * dil
    hw = HEADS_PER_GROUP * HEAD_DIM
    x = qkv.reshape(batch, dil, l, 3 * hw)
    out_sds = jax.ShapeDtypeStruct((batch, seq, hw), F32)
    out_spec = pl.BlockSpec((None, tok, HEAD_DIM), lambda b, h, i: (b, i, h))
    o, lse = pl.pallas_call(
        functools.partial(_attn_kernel, l=l, dil=dil, tql=tql),
        grid=(batch, HEADS_PER_GROUP, l // tql),
        in_specs=[
            pl.BlockSpec((None, dil, tql, HEAD_DIM), lambda b, h, i: (b, 0, i, h)),
            pl.BlockSpec((None, dil, l, HEAD_DIM), lambda b, h, i: (b, 0, 0, HEADS_PER_GROUP + h)),
            pl.BlockSpec((None, dil, l, HEAD_DIM), lambda b, h, i: (b, 0, 0, 2 * HEADS_PER_GROUP + h)),
            pl.BlockSpec((3, None, tqb, win), lambda b, h, i: (0, h, 0, 0)),
        ],
        out_specs=[out_spec, out_spec],
        out_shape=[out_sds, out_sds],
        scratch_shapes=[pltpu.VMEM((tok, HEAD_DIM), F32)] * 2 if dil == RMS_BASE_DIL ** 2 else [],
        compiler_params=_cparams(("parallel", "parallel", "parallel")),
        name=f"dilated_attn_d{dil}",
    )(x, x, x, bias)
    return o.reshape(batch * seq, hw), lse.reshape(batch * seq, hw)


def ssm_discretise(a_re, a_im, log_step, b_re, b_im, c_re, c_im):
    lam = lax.complex(a_re.astype(F32), a_im.astype(F32))
    step = jnp.exp(log_step.astype(F32))[..., None]
    lam_bar = jnp.exp(lam * step)
    b_bar = ((lam_bar - 1.0) / lam)[..., None] * lax.complex(b_re.astype(F32), b_im.astype(F32))
    c = lax.complex(c_re.astype(F32), c_im.astype(F32))
    return lam_bar, b_bar, c


def _block_diag(m):
    _, nj, lg, a, b = m.shape
    return jnp.einsum('djgab,gk->djgakb', m, jnp.eye(lg, dtype=F32)).reshape(2, nj, lg * a, lg * b)


def ssm_block_weights(lam_bar, b_bar, c):
    _, g, p, h = b_bar.shape
    lg = SSM_LANE_GROUPS
    nj = g // lg
    lam = lam_bar[..., None]
    ct = jnp.swapaxes(c, -1, -2)

    def grp(m):
        return m.reshape((2, nj, lg) + m.shape[2:])

    def state_in(bmat):
        t = grp(jnp.swapaxes(bmat, -1, -2))
        return jnp.concatenate([_block_diag(t.real), _block_diag(t.imag)], axis=-1)

    def state_out(cmat):
        t = grp(cmat)
        return jnp.concatenate([_block_diag(t.real), _block_diag(-t.imag)], axis=-2)

    def direct(k):
        m = jnp.einsum('dgpc,dgph->dghc', ct * lam ** k, b_bar).real
        return _block_diag(grp(m))

    first_in, second_in = state_in(lam * b_bar), state_in(b_bar)
    first_out, second_out = state_out(ct * lam), state_out(ct * lam ** 2)
    zero = jnp.zeros_like(direct(0))
    sel = jnp.array([0, 1]).reshape(2, 1, 1, 1)
    wb = jnp.where(sel == 0, jnp.concatenate([first_in, second_in], axis=-2),
                   jnp.concatenate([second_in, first_in], axis=-2))
    wx = jnp.where(sel == 0, jnp.concatenate([first_out, second_out], axis=-1),
                   jnp.concatenate([second_out, first_out], axis=-1))
    u_first = jnp.concatenate([direct(0), direct(1)], axis=-1)
    u_second = jnp.concatenate([zero, direct(0)], axis=-1)
    swap = lambda m: jnp.concatenate([m[..., 128:], m[..., :128]], axis=-1)
    wu = jnp.where(sel == 0, jnp.concatenate([u_first, u_second], axis=-2),
                   jnp.concatenate([swap(u_second), swap(u_first)], axis=-2))
    l2 = (lam_bar ** 2).reshape(2, nj, 1, 1, lg * p)
    lam2 = jnp.broadcast_to(jnp.concatenate([l2.real, l2.imag], axis=2), (2, nj, 2, 8, lg * p))
    return wb.astype(BF16), wx.astype(BF16), wu.astype(BF16), lam2.astype(F32)


def _s5_kernel(u_ref, wb_ref, wx_ref, wu_ref, lam_ref, x0_ref, *refs, tt, emit_y):
    n_out = 2 if emit_y else 1
    outs, scratch = refs[:n_out], list(refs[n_out:])
    xend_ref = outs[-1]
    parts = []
    for _ in range(S5_PARTS):
        parts.append([scratch.pop(0) for _ in range(4 if emit_y else 3)])
    st_ref = scratch.pop(0)
    direction = pl.program_id(0)
    ti = pl.program_id(2)
    nt = pl.num_programs(2)
    vb, _, lanes = u_ref.shape
    k = lam_ref.shape[-1]
    part_t = tt // S5_PARTS
    n = part_t // 2

    @pl.when(ti == 0)
    def _():
        st_ref[...] = x0_ref[...]

    lr = lam_ref[0]
    li = lam_ref[1]
    state = (st_ref[0], st_ref[1])
    for order, part in enumerate(parts):
        lhs_e, lhs_o, bu_ref = part[:3]
        t0 = jnp.where(direction == 0, order, S5_PARTS - 1 - order) * part_t
        for b in range(vb):
            lhs_e[pl.ds(b, n, stride=vb), :] = u_ref[b, pl.ds(t0, n, stride=2), :]
            lhs_o[pl.ds(b, n, stride=vb), :] = u_ref[b, pl.ds(t0 + 1, n, stride=2), :]
        lhs = jnp.concatenate([lhs_e[...], lhs_o[...]], axis=1).astype(BF16)
        bu_ref[...] = jnp.dot(lhs, wb_ref[...], preferred_element_type=F32)
        if emit_y:
            xs_ref = part[3]
            in_row = pl.multiple_of(jnp.where(direction == 0, 0, n * vb), vb)
            xs_ref[pl.ds(in_row, vb), :] = jnp.concatenate(state, axis=1)

        def step(s, carry, part=part):
            xr, xi = carry
            t = jnp.where(direction == 0, s, n - 1 - s)
            b = part[2][pl.ds(pl.multiple_of(t * vb, vb), vb), :]
            nr = lr * xr - li * xi + b[:, :k]
            ni = lr * xi + li * xr + b[:, k:]
            if emit_y:
                out_row = pl.multiple_of(jnp.where(direction == 0, t + 1, t) * vb, vb)
                part[3][pl.ds(out_row, vb), :] = jnp.concatenate([nr, ni], axis=1)
            return nr, ni

        state = lax.fori_loop(0, n, step, state, unroll=True)
        if emit_y:
            y_ref, ysc_e, ysc_o = outs[0], scratch[0], scratch[1]
            base = pl.multiple_of(jnp.where(direction == 0, 0, vb), vb)
            xs = xs_ref[pl.ds(base, n * vb), :].astype(BF16)
            y = (jnp.dot(xs, wx_ref[...], preferred_element_type=F32)
                 + jnp.dot(lhs, wu_ref[...], preferred_element_type=F32))
            ysc_e[...] = y[:, :lanes]
            ysc_o[...] = y[:, lanes:]
            for b in range(vb):
                y_ref[b, pl.ds(t0, n, stride=2), :] = ysc_e[pl.ds(b, n, stride=vb), :]
                y_ref[b, pl.ds(t0 + 1, n, stride=2), :] = ysc_o[pl.ds(b, n, stride=vb), :]
    st_ref[0] = state[0]
    st_ref[1] = state[1]

    @pl.when(ti == nt - 1)
    def _():
        xend_ref[0] = state[0]
        xend_ref[1] = state[1]


S5_PARTS = 2


def s5_scan(u, wts, x0, vbatch, emit_y, tt=512):
    width = u.shape[1]
    vseq = u.shape[0] // vbatch
    wb, wx, wu, lam2 = wts['wb'], wts['wx'], wts['wu'], wts['lam2']
    nj = wb.shape[1]
    k = lam2.shape[-1]
    lanes = width // nj
    nt = vseq // tt
    n = tt // S5_PARTS // 2

    def tmap(d, t):
        return jnp.where(d == 0, t, nt - 1 - t)

    state_spec = pl.BlockSpec((None, None, 2, vbatch, k), lambda d, j, t: (d, j, 0, 0, 0))
    state_sds = jax.ShapeDtypeStruct((2, nj, 2, vbatch, k), F32)
    out_specs, out_shape = [state_spec], [state_sds]
    half = pltpu.VMEM((n * vbatch, lanes), F32)
    part = [half, half, pltpu.VMEM((n * vbatch, 2 * k), F32)]
    if emit_y:
        part.append(pltpu.VMEM(((n + 1) * vbatch, 2 * k), F32))
    scratch = part * S5_PARTS + [pltpu.VMEM((2, vbatch, k), F32)]
    if emit_y:
        out_specs.insert(0, pl.BlockSpec((None, vbatch, tt, lanes), lambda d, j, t: (d, 0, tmap(d, t), j)))
        out_shape.insert(0, jax.ShapeDtypeStruct((2, vbatch, vseq, width), F32))
        scratch += [half, half]
    outs = pl.pallas_call(
        functools.partial(_s5_kernel, tt=tt, emit_y=emit_y),
        grid=(2, nj, nt),
        in_specs=[
            pl.BlockSpec((vbatch, tt, lanes), lambda d, j, t: (0, tmap(d, t), j)),
            pl.BlockSpec((None, None, 2 * lanes, 2 * k), lambda d, j, t: (d, j, 0, 0)),
            pl.BlockSpec((None, None, 2 * k, 2 * lanes), lambda d, j, t: (d, j, 0, 0)),
            pl.BlockSpec((None, None, 2 * lanes, 2 * lanes), lambda d, j, t: (d, j, 0, 0)),
            state_spec, state_spec,
        ],
        out_specs=out_specs,
        out_shape=out_shape,
        scratch_shapes=scratch,
        compiler_params=_cparams(("parallel", "parallel", "arbitrary")),
        name="s5_scan" if emit_y else "s5_states",
    )(u.reshape(vbatch, vseq, width), wb, wx, wu, lam2, x0)
    if emit_y:
        return outs[0].reshape(2, vbatch * vseq, width), outs[1]
    return None, outs[0]


def s5_segment_carries(xend, lam_bar, seg_len, batch, nseg):
    _, nj, _, _, k = xend.shape
    lam_pow = (lam_bar ** seg_len).reshape(2, nj, 1, k)
    e = lax.complex(xend[:, :, 0], xend[:, :, 1]).reshape(2, nj, batch, nseg, k)
    fwd = [jnp.zeros((nj, batch, k), jnp.complex64)]
    for s in range(1, nseg):
        fwd.append(lam_pow[0] * fwd[-1] + e[0, :, :, s - 1])
    bwd = [jnp.zeros((nj, batch, k), jnp.complex64)]
    for s in range(nseg - 2, -1, -1):
        bwd.append(lam_pow[1] * bwd[-1] + e[1, :, :, s + 1])
    bwd = bwd[::-1]
    x_in = jnp.stack([jnp.stack(fwd, axis=2), jnp.stack(bwd, axis=2)])
    x_in = x_in.reshape(2, nj, batch * nseg, k)
    return jnp.stack([x_in.real, x_in.imag], axis=2).astype(F32)


def _gelu_tanh(x):
    return 0.5 * x * (1.0 + jnp.tanh(math.sqrt(2.0 / math.pi) * (x + 0.044715 * x * x * x)))


def _glu_kernel(yf_ref, yb_ref, u_ref, d_ref, wa_ref, wb_ref, o_ref):
    y = _gelu_tanh(yf_ref[...] + yb_ref[...] + d_ref[...] * u_ref[...]).astype(BF16)
    za = jnp.dot(y, wa_ref[...], preferred_element_type=F32)
    zb = jnp.dot(y, wb_ref[...], preferred_element_type=F32)
    o_ref[...] = (za * _sigmoid(zb)).astype(o_ref.dtype)


def ssm_glu(y, u, d_skip, w_glu, tm=512):
    n, width = u.shape
    wa = w_glu[:, :width]
    wb = w_glu[:, width:]
    const = lambda i: (0, 0)
    return pl.pallas_call(
        _glu_kernel,
        grid=(n // tm,),
        in_specs=[
            pl.BlockSpec((None, tm, width), lambda i: (0, i, 0)),
            pl.BlockSpec((None, tm, width), lambda i: (1, i, 0)),
            pl.BlockSpec((tm, width), lambda i: (i, 0)),
            pl.BlockSpec((1, width), const),
            pl.BlockSpec((width, width), const),
            pl.BlockSpec((width, width), const),
        ],
        out_specs=pl.BlockSpec((tm, width), lambda i: (i, 0)),
        out_shape=jax.ShapeDtypeStruct((n, width), BF16),
        compiler_params=_cparams(("parallel",)),
        name="ssm_glu",
    )(y, y, u, d_skip.reshape(1, width).astype(F32), wa, wb)


def _split_bf16(x):
    hi = x.astype(BF16)
    lo = (x - hi.astype(F32)).astype(BF16)
    return hi, lo


def _pack_bf16_pair(a, b):
    wa = pltpu.bitcast(a.astype(BF16).astype(F32), jnp.int32)
    wb = pltpu.bitcast(b.astype(BF16).astype(F32), jnp.int32)
    return wa | lax.shift_right_logical(wb, 16)


def _unpack_bf16_pair(w):
    hi = pltpu.bitcast(w & jnp.int32(-65536), F32).astype(BF16)
    lo = pltpu.bitcast(lax.shift_left(w, 16), F32).astype(BF16)
    return hi, lo


def _mixer_out_kernel(o0, o1, o2, l0, l1, l2, ob_ref, gate_ref, x_ref, wa_ref, wb_ref, wo_ref,
                      g_ref, wr_ref, hbuf_ref, x1_ref, h_ref, lg_ref):
    del hbuf_ref
    d_model = x_ref.shape[1]
    la, lb, lc = l0[...], l1[...], l2[...]
    m = jnp.maximum(jnp.maximum(la, lb), lc)
    ea, eb, ec = jnp.exp(la - m), jnp.exp(lb - m), jnp.exp(lc - m)
    oa = (ea * o0[...] + eb * o1[...] + ec * o2[...]) / (ea + eb + ec)
    pa = jnp.dot(oa.astype(BF16), wa_ref[...], preferred_element_type=F32)
    pb = jnp.dot(ob_ref[...], wb_ref[...], preferred_element_type=F32)
    gates = gate_ref[...].astype(F32)
    merged = gates[:, :d_model] * pa + gates[:, d_model:] * pb
    x1 = x_ref[...] + jnp.dot(merged.astype(BF16), wo_ref[...], preferred_element_type=F32)
    x1_ref[...] = x1
    h = x1 * lax.rsqrt(jnp.mean(x1 * x1, axis=-1, keepdims=True) + EPS) * g_ref[...]
    h_ref[...] = _pack_bf16_pair(h[:, :d_model // 2], h[:, d_model // 2:])
    hh, hl = _split_bf16(h)
    er = lg_ref.shape[1]
    both = jnp.dot(hh, wr_ref[...], preferred_element_type=F32)
    lg_ref[...] = both[:, :er] + both[:, er:] + jnp.dot(hl, wr_ref[:, :er], preferred_element_type=F32)


def mixer_out(o_list, lse_list, o_b, gates, x, w_br_a, w_br_b, w_out, g_ffn, wr_split, h_buf, h_row0, tm=256):
    n, d_model = x.shape
    row = lambda w: pl.BlockSpec((tm, w), lambda i: (i, 0))
    full = lambda a: pl.BlockSpec(a.shape, lambda i: (0,) * a.ndim, pipeline_mode=pl.Buffered(1))
    g2 = g_ffn.reshape(1, d_model).astype(F32)
    er = wr_split.shape[1] // 2
    blk0 = h_row0 // tm
    in_specs = [row(o.shape[1]) for o in o_list] + [row(l.shape[1]) for l in lse_list] + [
        row(o_b.shape[1]), row(gates.shape[1]), row(d_model),
        full(w_br_a), full(w_br_b), full(w_out), full(g2), full(wr_split), pl.BlockSpec(memory_space=pl.ANY)]
    args = [*o_list, *lse_list, o_b, gates, x, w_br_a, w_br_b, w_out, g2, wr_split, h_buf]
    x1, h, logits = pl.pallas_call(
        _mixer_out_kernel,
        grid=(n // tm,),
        in_specs=in_specs,
        out_specs=[row(d_model), pl.BlockSpec((tm, d_model // 2), lambda i: (blk0 + i, 0)), row(er)],
        out_shape=[jax.ShapeDtypeStruct((n, d_model), F32), jax.ShapeDtypeStruct(h_buf.shape, h_buf.dtype),
                   jax.ShapeDtypeStruct((n, er), F32)],
        input_output_aliases={len(args) - 1: 1},
        compiler_params=_cparams(("parallel",)),
        name="mixer_out",
    )(*args)
    return x1, h, logits


def _expert_up_kernel(x_ref, wg_ref, wu_ref, o_ref, wg_s, wu_s):
    @pl.when(pl.program_id(2) == 0)
    def _():
        wg_s[...] = wg_ref[...].astype(BF16)
        wu_s[...] = wu_ref[...].astype(BF16)

    x_hi, x_lo = _unpack_bf16_pair(x_ref[...])
    half = x_hi.shape[1]

    def proj(w_s):
        return (jnp.dot(x_hi, w_s[:half, :], preferred_element_type=F32)
                + jnp.dot(x_lo, w_s[half:, :], preferred_element_type=F32))

    a = proj(wg_s)
    b = proj(wu_s)
    o_ref[...] = (a * _sigmoid(a) * b).astype(o_ref.dtype)


def _expert_up_into_kernel(x_ref, wg_ref, wu_ref, buf_ref, o_ref, wg_s, wu_s):
    del buf_ref
    _expert_up_kernel(x_ref, wg_ref, wu_ref, o_ref, wg_s, wu_s)


def expert_up(xe, w_gate, w_up, e0, hid=None, tm=1024, tf=512):
    n_e, cap, _ = xe.shape
    n_exp, d_model, ff = w_gate.shape
    tm = math.gcd(tm, cap)
    w_spec = pl.BlockSpec((None, d_model, tf), lambda ei, f, m: (e0 + ei, 0, f))
    in_specs = [pl.BlockSpec((None, tm, d_model // 2), lambda ei, f, m: (ei, m, 0)), w_spec, w_spec]
    args = [xe, w_gate, w_up]
    if hid is not None:
        in_specs.append(pl.BlockSpec(memory_space=pl.ANY))
        args.append(hid)
    return pl.pallas_call(
        _expert_up_kernel if hid is None else _expert_up_into_kernel,
        grid=(n_e, ff // tf, cap // tm),
        in_specs=in_specs,
        out_specs=pl.BlockSpec((None, tm, tf), lambda ei, f, m: (e0 + ei, m, f)),
        out_shape=jax.ShapeDtypeStruct((n_exp, cap, ff), BF16),
        scratch_shapes=[pltpu.VMEM((d_model, tf), BF16), pltpu.VMEM((d_model, tf), BF16)],
        input_output_aliases={} if hid is None else {3: 0},
        compiler_params=_cparams(("parallel", "parallel", "arbitrary")),
        name="expert_up",
    )(*args)


def _expert_down_kernel(h_ref, w_ref, o_ref, w_s):
    @pl.when(pl.program_id(2) == 0)
    def _():
        w_s[...] = w_ref[...].astype(BF16)

    acc = jnp.dot(h_ref[...], w_s[...], preferred_element_type=F32)
    half = acc.shape[1] // 2
    o_ref[...] = _pack_bf16_pair(acc[:, :half], acc[:, half:])


def expert_down(hid, w_down, tm=512, tn=1024):
    e, cap, ff = hid.shape
    d_model = w_down.shape[2]
    tm = math.gcd(tm, cap)
    tn = min(tn, d_model)
    return pl.pallas_call(
        _expert_down_kernel,
        grid=(e, d_model // tn, cap // tm),
        in_specs=[
            pl.BlockSpec((None, tm, ff), lambda ei, n, m: (ei, m, 0)),
            pl.BlockSpec((None, ff, tn), lambda ei, n, m: (ei, 0, n)),
        ],
        out_specs=pl.BlockSpec((None, tm, tn // 2), lambda ei, n, m: (ei, m, n)),
        out_shape=jax.ShapeDtypeStruct((e, cap, d_model // 2), jnp.int32),
        scratch_shapes=[pltpu.VMEM((ff, tn), BF16)],
        compiler_params=_cparams(("parallel", "parallel", "arbitrary")),
        name="expert_down",
    )(hid, w_down)


ROUTER_ROWS = 512


def _router_kernel(lg_ref, o_ref, aff_ref, *, n_exp, cap):
    n, lanes = lg_ref.shape
    t = ROUTER_ROWS
    nt = n // t
    real = lax.broadcasted_iota(jnp.int32, (1, lanes), 1) < n_exp

    def rows(i):
        return pl.ds(pl.multiple_of(i * t, t), t)

    def softmax_tile(i, carry):
        x = jnp.where(real, lg_ref[rows(i), :], -jnp.inf)
        e = jnp.exp(x - jnp.max(x, axis=-1, keepdims=True))
        aff_ref[rows(i), :] = e / jnp.sum(e, axis=-1, keepdims=True)
        return carry

    lax.fori_loop(0, nt, softmax_tile, 0)

    def count(pred):
        def body(i, acc):
            hit = pred(pltpu.bitcast(aff_ref[rows(i), :], jnp.int32)).astype(jnp.int32)
            return acc + jnp.sum(hit.reshape(t // 8, 8, lanes), axis=0)
        acc = lax.fori_loop(0, nt, body, jnp.zeros((8, lanes), jnp.int32))
        return jnp.sum(acc, axis=0, keepdims=True)

    thr = jnp.zeros((1, lanes), jnp.int32)
    for bit in range(30, -1, -1):
        cand = thr | jnp.int32(1 << bit)
        thr = jnp.where(count(lambda k, cand=cand: k >= cand) >= cap, cand, thr)
    ties_needed = (cap - count(lambda k: k > thr)).astype(F32)

    r_iota = lax.broadcasted_iota(jnp.int32, (t, t), 0)
    c_iota = lax.broadcasted_iota(jnp.int32, (t, t), 1)
    prefix = (c_iota <= r_iota).astype(BF16)

    def select_tile(i, ties_before):
        aff = aff_ref[rows(i), :]
        key = pltpu.bitcast(aff, jnp.int32)
        tie = (key == thr).astype(BF16)
        tie_rank = ties_before + jnp.dot(prefix, tie, preferred_element_type=F32)
        chosen = (key > thr) | ((key == thr) & (tie_rank <= ties_needed))
        chosen = jnp.where(real & chosen, 1.0, 0.0)
        o_ref[rows(i), :] = jnp.where(real, aff, 0.0) + pltpu.roll(chosen, n_exp, 1)
        return tie_rank[t - 1:t, :]

    lax.fori_loop(0, nt, select_tile, jnp.zeros((1, lanes), F32))


def route_expert_choice(logits, n_exp, cap):
    n, lanes = logits.shape
    whole = pl.BlockSpec(memory_space=pltpu.VMEM)
    routed = pl.pallas_call(
        functools.partial(_router_kernel, n_exp=n_exp, cap=cap),
        in_specs=[whole],
        out_specs=whole,
        out_shape=jax.ShapeDtypeStruct((n, lanes), F32),
        scratch_shapes=[pltpu.VMEM((n, lanes), F32)],
        compiler_params=pltpu.CompilerParams(vmem_limit_bytes=V7X_VMEM_LIMIT),
        name="route_expert_choice",
    )(logits)
    chosen = routed[:, n_exp:2 * n_exp].T.reshape(-1) > 0.5
    pos = jnp.nonzero(chosen, size=n_exp * cap, fill_value=0)[0].astype(jnp.int32)
    gate = routed[:, :n_exp].T.reshape(-1)[pos]
    return gate, pos % n


SC_GATHER_CHUNK = 16


def sc_gather_rows(table, idx):
    n_idx = idx.shape[0]
    width = table.shape[1]
    info = plsc.get_sparse_core_info()
    n_workers = info.num_cores * info.num_subcores
    per_worker = n_idx // n_workers
    ch = SC_GATHER_CHUNK
    assert n_idx % n_workers == 0 and per_worker % (2 * ch) == 0
    mesh = plsc.VectorSubcoreMesh(core_axis_name="c", subcore_axis_name="s")

    @functools.partial(
        pl.kernel, mesh=mesh,
        out_type=jax.ShapeDtypeStruct((n_idx, width), table.dtype),
        scratch_types=[pltpu.VMEM((per_worker,), jnp.int32), pltpu.VMEM((ch, width), table.dtype),
                       pltpu.VMEM((ch, width), table.dtype), pltpu.SemaphoreType.DMA, pltpu.SemaphoreType.DMA],
        name="sc_gather_rows",
    )
    def gather(table_hbm, idx_hbm, out_hbm, idx_v, rows_a, rows_b, sem_a, sem_b):
        base = (lax.axis_index("s") * info.num_cores + lax.axis_index("c")) * per_worker
        pltpu.sync_copy(idx_hbm.at[pl.ds(base, per_worker)], idx_v)

        @pl.loop(0, per_worker // (2 * ch))
        def _(c):
            off = c * 2 * ch
            copy_a = pltpu.async_copy(table_hbm.at[idx_v.at[pl.ds(off, ch)]], rows_a, sem_a)
            copy_b = pltpu.async_copy(table_hbm.at[idx_v.at[pl.ds(off + ch, ch)]], rows_b, sem_b)
            copy_a.wait()
            pltpu.sync_copy(rows_a, out_hbm.at[pl.ds(base + off, ch)])
            copy_b.wait()
            pltpu.sync_copy(rows_b, out_hbm.at[pl.ds(base + off + ch, ch)])

    return gather(table, idx)


COMBINE_TOKENS = 512
COMBINE_ROWS = 512


def _combine_kernel(tile_ref, chunk_ref, first_ref, valid_ref, rows_ref, tok_ref, g_ref, o_ref, *, tn):
    p = pl.program_id(0)

    @pl.when(first_ref[p] == 1)
    def _():
        o_ref[...] = jnp.zeros_like(o_ref)

    @pl.when(valid_ref[p] == 1)
    def _():
        t, ch = o_ref.shape[0], tok_ref.shape[1]
        token = tile_ref[p] * t + lax.broadcasted_iota(jnp.int32, (t, ch), 0)
        weights = jnp.where(tok_ref[...] == token, g_ref[...], 0.0).astype(BF16)
        hi, lo = _unpack_bf16_pair(rows_ref[...])
        acc_hi = jnp.dot(weights, hi, preferred_element_type=F32)
        acc_lo = jnp.dot(weights, lo, preferred_element_type=F32)
        half = tn // 2
        for n in range(o_ref.shape[1] // tn):
            o_ref[:, n * tn:n * tn + half] += acc_hi[:, n * half:(n + 1) * half]
            o_ref[:, n * tn + half:(n + 1) * tn] += acc_lo[:, n * half:(n + 1) * half]


def combine_expert_outputs(rows, tok_sorted, gate_sorted, n_tok, tn):
    m, half_d = rows.shape
    t, ch = COMBINE_TOKENS, COMBINE_ROWS
    n_tiles, n_chunks = n_tok // t, m // ch
    n_pairs = n_tiles + n_chunks
    edges = jnp.arange(n_tiles + 1, dtype=jnp.int32) * t
    bounds = jnp.sum(tok_sorted[None, :] < edges[:, None], axis=1, dtype=jnp.int32)
    start, end = bounds[:-1], bounds[1:]
    c_lo = jnp.minimum(start // ch, n_chunks - 1)
    c_hi = jnp.where(end > start, (end - 1) // ch, c_lo)
    count = c_hi - c_lo + 1
    offs = jnp.cumsum(count)
    p = jnp.arange(n_pairs, dtype=jnp.int32)
    tile = jnp.minimum(jnp.sum(offs[None, :] <= p[:, None], axis=1, dtype=jnp.int32), n_tiles - 1)
    first_p = (offs - count)[tile]
    valid = p < offs[-1]
    chunk = jnp.where(valid, c_lo[tile] + p - first_p, c_hi[-1]).astype(jnp.int32)
    first = (valid & (p == first_p)).astype(jnp.int32)
    grid_spec = pltpu.PrefetchScalarGridSpec(
        num_scalar_prefetch=4,
        grid=(n_pairs,),
        in_specs=[
            pl.BlockSpec((ch, half_d), lambda i, tl, ck, fr, vl: (ck[i], 0)),
            pl.BlockSpec((1, ch), lambda i, tl, ck, fr, vl: (0, ck[i])),
            pl.BlockSpec((1, ch), lambda i, tl, ck, fr, vl: (0, ck[i])),
        ],
        out_specs=pl.BlockSpec((t, 2 * half_d), lambda i, tl, ck, fr, vl: (tl[i], 0)),
    )
    return pl.pallas_call(
        functools.partial(_combine_kernel, tn=tn),
        grid_spec=grid_spec,
        out_shape=jax.ShapeDtypeStruct((n_tok, 2 * half_d), F32),
        compiler_params=_cparams(("arbitrary",)),
        name="combine_expert_outputs",
    )(tile, chunk, first, valid.astype(jnp.int32), rows, tok_sorted.reshape(1, m), gate_sorted.reshape(1, m).astype(F32))


def _final_kernel(x1_ref, moe_ref, p_ref, g_ref, wg_ref, wp_ref, o_ref):
    x2 = x1_ref[...] + moe_ref[...]
    h = (x2 * lax.rsqrt(jnp.mean(x2 * x2, axis=-1, keepdims=True) + EPS) * g_ref[...]).astype(BF16)
    gate = _sigmoid(jnp.dot(h, wg_ref[...], preferred_element_type=F32))
    proj = jnp.dot(p_ref[...].astype(BF16), wp_ref[...], preferred_element_type=F32)
    o_ref[...] = x2 + gate * proj


def final_embed(x1, moe, p, g_ple, w_gate, w_proj, tm=512):
    n, d_model = x1.shape
    row = lambda w: pl.BlockSpec((tm, w), lambda i: (i, 0))
    full = lambda a: pl.BlockSpec(a.shape, lambda i: (0,) * a.ndim, pipeline_mode=pl.Buffered(1))
    g2 = g_ple.reshape(1, d_model).astype(F32)
    return pl.pallas_call(
        _final_kernel,
        grid=(n // tm,),
        in_specs=[row(d_model), row(d_model), row(p.shape[1]), full(g2), full(w_gate), full(w_proj)],
        out_specs=row(d_model),
        out_shape=jax.ShapeDtypeStruct((n, d_model), F32),
        compiler_params=_cparams(("parallel",)),
        name="final_embed",
    )(x1, moe, p, g2, w_gate, w_proj)


def _prepare_weights(rel_bias, g_mix, w_in, q_gain, k_gain, ssm_a_re, ssm_a_im, ssm_log_step,
                     ssm_b_re, ssm_b_im, ssm_c_re, ssm_c_im, ssm_d, w_glu, w_br_a, w_br_b, w_out,
                     g_ffn, w_router, g_ple, w_ple_gate, w_ple_proj):
    n_heads = q_gain.shape[0]
    att_w = n_heads * HEAD_DIM
    hw = HEADS_PER_GROUP * HEAD_DIM
    ssm_w = ssm_d.shape[0]
    w_in_bf = w_in.astype(BF16)
    w_qkv, gains = [], []
    qg = q_gain.astype(F32) * (HEAD_DIM ** -0.5)
    kg = k_gain.astype(F32)
    for gi in range(len(ATT_PATTERNS)):
        cols = [w_in_bf[:, s * att_w + gi * hw: s * att_w + (gi + 1) * hw] for s in range(3)]
        w_qkv.append(jnp.concatenate(cols, axis=1))
        hs = slice(gi * HEADS_PER_GROUP, (gi + 1) * HEADS_PER_GROUP)
        gains.append(jnp.stack([qg[hs].reshape(1, hw), kg[hs].reshape(1, hw), jnp.ones((1, hw), F32)]))
    lam_bar, b_bar, c = ssm_discretise(ssm_a_re, ssm_a_im, ssm_log_step, ssm_b_re, ssm_b_im, ssm_c_re, ssm_c_im)
    wb, wx, wu, lam2 = ssm_block_weights(lam_bar, b_bar, c)
    n_exp = w_router.shape[1]
    wr = jnp.pad(w_router.astype(F32), ((0, 0), (0, 128 - n_exp)))
    wr_split = jnp.concatenate(_split_bf16(wr), axis=1)
    return dict(
        rel_bias=rel_bias, bias_tables={}, g_mix=g_mix, w_qkv=w_qkv, gains=gains,
        w_u=w_in_bf[:, 3 * att_w:3 * att_w + ssm_w], w_gates=w_in_bf[:, 3 * att_w + ssm_w:],
        lam_bar=lam_bar, wb=wb, wx=wx, wu=wu, lam2=lam2, ssm_d=ssm_d, w_glu=w_glu.astype(BF16),
        w_br_a=w_br_a.astype(BF16), w_br_b=w_br_b.astype(BF16), w_out=w_out.astype(BF16),
        g_ffn=g_ffn, wr_split=wr_split, n_exp=n_exp, g_ple=g_ple,
        w_ple_gate=w_ple_gate.astype(BF16), w_ple_proj=w_ple_proj.astype(BF16))


def _ssm_branch(hn, wts, batch, seq):
    nseg = max(1, 8 // batch)
    vbatch = batch * nseg
    vseq = seq // nseg
    u = plain_proj(hn, wts['w_u'], F32, False, "u_proj")
    nj, k = wts['lam2'].shape[1], wts['lam2'].shape[-1]
    x0 = jnp.zeros((2, nj, 2, vbatch, k), F32)
    if nseg > 1:
        _, xend = s5_scan(u, wts, x0, vbatch, emit_y=False)
        x0 = s5_segment_carries(xend, wts['lam_bar'], vseq, batch, nseg)
    y, _ = s5_scan(u, wts, x0, vbatch, emit_y=True)
    return ssm_glu(y, u, wts['ssm_d'], wts['w_glu'])


MOE_LEAD_EXPERTS = 1


def _expert_choice(h_all, n_toks, logits_list, n_exp, w_exp_gate, w_exp_up, w_exp_down):
    d_model = w_exp_down.shape[2]
    half_d = h_all.shape[1]
    caps = [CAPACITY_FACTOR * n // n_exp for n in n_toks]
    cap_all = sum(caps)
    routes, picks, row0 = [], [], 0
    for n, logits, cap in zip(n_toks, logits_list, caps):
        gate_flat, idx_flat = route_expert_choice(logits, n_exp, cap)
        routes.append((gate_flat, idx_flat))
        picks.append(idx_flat.reshape(n_exp, cap) + row0)
        row0 += n
    picks = jnp.concatenate(picks, axis=1)
    hid, lead = None, min(MOE_LEAD_EXPERTS, n_exp)
    for e0, e1 in ((0, lead), (lead, n_exp)):
        if e1 > e0:
            xe = sc_gather_rows(h_all, picks[e0:e1].reshape(-1)).reshape(e1 - e0, cap_all, half_d)
            hid = expert_up(xe, w_exp_gate, w_exp_up, e0, hid)
    tn = min(1024, d_model)
    ye = expert_down(hid, w_exp_down, tn=tn).reshape(n_exp * cap_all, half_d)
    outs, slot0 = [], 0
    for n, (gate_flat, idx_flat), cap in zip(n_toks, routes, caps):
        order = jnp.argsort(idx_flat).astype(jnp.int32)
        rows = (order // cap) * cap_all + slot0 + order % cap
        outs.append(combine_expert_outputs(sc_gather_rows(ye, rows), idx_flat[order], gate_flat[order], n, tn))
        slot0 += cap
    return outs


def _mixer(x, wts, h_buf, h_row0):
    batch, seq, d_model = x.shape
    n = batch * seq
    xf = x.reshape(n, d_model)
    dils = [dil for _, dil in ATT_PATTERNS]
    hn_list = rms_bf16(x, wts['g_mix'], dils)
    hn = hn_list[dils.index(1)]
    o_list, lse_list = [], []
    for gi, dil in enumerate(dils):
        qkv = qkv_proj(hn_list[gi], wts['w_qkv'][gi], wts['gains'][gi], f"qkv_proj_d{dil}")
        hs = slice(gi * HEADS_PER_GROUP, (gi + 1) * HEADS_PER_GROUP)
        bias_key = (gi,) + _attn_geometry(seq // dil)
        if bias_key not in wts['bias_tables']:
            wts['bias_tables'][bias_key] = attn_bias_tables(wts['rel_bias'][:, hs], dil, seq // dil)
        bias = wts['bias_tables'][bias_key]
        o, lse = dilated_attention(qkv, bias, batch, seq, dil)
        o_list.append(o)
        lse_list.append(lse)
    o_b = _ssm_branch(hn, wts, batch, seq)
    gates = plain_proj(hn, wts['w_gates'], BF16, True, "gates_proj")
    return mixer_out(o_list, lse_list, o_b, gates, xf, wts['w_br_a'], wts['w_br_b'],
                     wts['w_out'], wts['g_ffn'], wts['wr_split'], h_buf, h_row0)


def kernel(x_prompt, x_sample, p_prompt, p_sample, rel_bias, g_mix, w_in, q_gain, k_gain, ssm_a_re, ssm_a_im, ssm_log_step, ssm_b_re, ssm_b_im, ssm_c_re, ssm_c_im, ssm_d, w_glu, w_br_a, w_br_b, w_out, g_ffn, w_router, w_exp_gate, w_exp_up, w_exp_down, g_ple, w_ple_gate, w_ple_proj):
    assert g_mix.shape[0] == 1, "single-layer trunk"
    wts = _prepare_weights(rel_bias, g_mix[0], w_in[0], q_gain[0], k_gain[0], ssm_a_re[0], ssm_a_im[0],
                           ssm_log_step[0], ssm_b_re[0], ssm_b_im[0], ssm_c_re[0], ssm_c_im[0], ssm_d[0],
                           w_glu[0], w_br_a[0], w_br_b[0], w_out[0], g_ffn[0], w_router[0], g_ple[0],
                           w_ple_gate[0], w_ple_proj[0])
    xs, ps = (x_prompt, x_sample), (p_prompt[0], p_sample[0])
    n_toks = [x.shape[0] * x.shape[1] for x in xs]
    mixed, row0 = [], 0
    h_all = jnp.zeros((sum(n_toks), x_prompt.shape[-1] // 2), jnp.int32)
    for x, n in zip(xs, n_toks):
        x1, h_all, logits = _mixer(x, wts, h_all, row0)
        mixed.append((x1, logits))
        row0 += n
    moe = _expert_choice(h_all, n_toks, [m[1] for m in mixed], wts['n_exp'],
                         w_exp_gate[0], w_exp_up[0], w_exp_down[0])
    outs = []
    for x, p, (x1, _), y in zip(xs, ps, mixed, moe):
        out = final_embed(x1, y, p.reshape(-1, p.shape[-1]), wts['g_ple'], wts['w_ple_gate'], wts['w_ple_proj'])
        outs.append(out.reshape(x.shape))
    return tuple(outs)
```

```python
import functools
import math

import jax
import jax.numpy as jnp
import numpy as np
from jax import lax
from jax.experimental import pallas as pl
from jax.experimental.pallas import tpu as pltpu
from jax.experimental.pallas import tpu_sc as plsc

F32 = jnp.float32
BF16 = jnp.bfloat16

LANES = 128
HEAD_DIM = 128
HEADS_PER_GROUP = 4
ATT_PATTERNS = ((128, 1), (512, 4), (2048, 16))
ATT_SIDE = 64
N_BUCKETS = 32
MAX_DISTANCE = 1024
CAPACITY_FACTOR = 2
EPS = 1e-6
NEG = -1e30
SSM_LANE_GROUPS = 8
V7X_VMEM_LIMIT = 56 * 1024 * 1024

assert all(w // (2 * d) == ATT_SIDE for w, d in ATT_PATTERNS)


def _cparams(semantics, vmem=V7X_VMEM_LIMIT):
    return pltpu.CompilerParams(dimension_semantics=("arbitrary",) * len(semantics), vmem_limit_bytes=vmem)


def _sigmoid(x):
    return 1.0 / (1.0 + jnp.exp(-x))


def _rms_kernel(x_ref, g_ref, *refs, dils):
    o_refs, y_ref, yb_ref = refs[:-2], refs[-2], refs[-1]
    x = x_ref[...]
    y = x * lax.rsqrt(jnp.mean(x * x, axis=-1, keepdims=True) + EPS) * g_ref[...]
    tm = x.shape[0]
    n_chunks = y_ref.shape[0]
    base = RMS_BASE_DIL
    for c in range(n_chunks):
        y_ref[c] = y[:, c * LANES:(c + 1) * LANES]
    for o_ref, dil in sorted(zip(o_refs, dils), key=lambda od: od[1]):
        if dil == 1:
            o_ref[...] = y.astype(o_ref.dtype)
        elif dil == base:
            for r in range(dil):
                rows = [y_ref[c, pl.ds(r, tm // dil, stride=dil), :] for c in range(n_chunks)]
                for c in range(n_chunks):
                    yb_ref[c, r * (tm // dil):(r + 1) * (tm // dil), :] = rows[c]
                o_ref[r] = jnp.concatenate(rows, axis=1).astype(o_ref.dtype)
        else:
            assert base in dils and dil == base * base
            for r in range(dil):
                start = (r % base) * (tm // base) + r // base
                rows = [yb_ref[c, pl.ds(start, tm // dil, stride=base), :] for c in range(n_chunks)]
                o_ref[r] = jnp.concatenate(rows, axis=1).astype(o_ref.dtype)


RMS_BASE_DIL = 4


def rms_bf16(x, g, dils, tm=512):
    batch, seq, d = x.shape
    out_specs, out_shape = [], []
    for dil in dils:
        if dil == 1:
            out_specs.append(pl.BlockSpec((None, tm, d), lambda b, i: (b, i, 0)))
            out_shape.append(jax.ShapeDtypeStruct((batch, seq, d), BF16))
        else:
            out_specs.append(pl.BlockSpec((None, dil, tm // dil, d), lambda b, i: (b, 0, i, 0)))
            out_shape.append(jax.ShapeDtypeStruct((batch, dil, seq // dil, d), BF16))
    outs = pl.pallas_call(
        functools.partial(_rms_kernel, dils=tuple(dils)),
        grid=(batch, seq // tm),
        in_specs=[pl.BlockSpec((None, tm, d), lambda b, i: (b, i, 0)), pl.BlockSpec((1, d), lambda b, i: (0, 0))],
        out_specs=out_specs,
        out_shape=out_shape,
        scratch_shapes=[pltpu.VMEM((d // LANES, tm, LANES), F32)] * 2,
        compiler_params=_cparams(("parallel", "parallel")),
        name="rms_bf16",
    )(x, g.reshape(1, d).astype(F32))
    return [o.reshape(batch * seq, d) for o in outs]


def _head_rms(acc, gain):
    outs = []
    for h in range(acc.shape[1] // HEAD_DIM):
        t = acc[:, h * HEAD_DIM:(h + 1) * HEAD_DIM]
        outs.append(t * lax.rsqrt(jnp.mean(t * t, axis=-1, keepdims=True) + EPS))
    return jnp.concatenate(outs, axis=1) * gain


def _qkv_kernel(x_ref, w_ref, g_ref, o_ref):
    x = x_ref[...]
    tn = o_ref.shape[1] // 3
    for j in range(3):
        acc = jnp.dot(x, w_ref[:, j * tn:(j + 1) * tn], preferred_element_type=F32)
        if j < 2:
            acc = _head_rms(acc, g_ref[j])
        o_ref[:, j * tn:(j + 1) * tn] = acc.astype(o_ref.dtype)


def qkv_proj(hn, w, gains, name, tm=1024):
    n, d_model = hn.shape
    tn = HEADS_PER_GROUP * HEAD_DIM
    return pl.pallas_call(
        _qkv_kernel,
        grid=(n // tm,),
        in_specs=[
            pl.BlockSpec((tm, d_model), lambda i: (i, 0)),
            pl.BlockSpec((d_model, 3 * tn), lambda i: (0, 0), pipeline_mode=pl.Buffered(1)),
            pl.BlockSpec((3, 1, tn), lambda i: (0, 0, 0), pipeline_mode=pl.Buffered(1)),
        ],
        out_specs=pl.BlockSpec((tm, 3 * tn), lambda i: (i, 0)),
        out_shape=jax.ShapeDtypeStruct((n, 3 * tn), BF16),
        compiler_params=_cparams(("parallel",)),
        name=name,
    )(hn, w, gains)


def _proj_kernel(x_ref, w_ref, o_ref, *, act):
    acc = jnp.dot(x_ref[...], w_ref[...], preferred_element_type=F32)
    if act:
        acc = _sigmoid(acc)
    o_ref[...] = acc.astype(o_ref.dtype)


def plain_proj(hn, w, out_dtype, act, name, tm=1024, tn=1024):
    n, d_model = hn.shape
    width = w.shape[1]
    tn = min(tn, width)
    return pl.pallas_call(
        functools.partial(_proj_kernel, act=act),
        grid=(width // tn, n // tm),
        in_specs=[
            pl.BlockSpec((tm, d_model), lambda j, i: (i, 0)),
            pl.BlockSpec((d_model, tn), lambda j, i: (0, j)),
        ],
        out_specs=pl.BlockSpec((tm, tn), lambda j, i: (i, j)),
        out_shape=jax.ShapeDtypeStruct((n, width), out_dtype),
        compiler_params=_cparams(("arbitrary", "parallel")),
        name=name,
    )(hn, w)


def _t5_buckets(rel):
    half = N_BUCKETS // 2
    max_exact = half // 2
    n = np.abs(rel)
    large = max_exact + (np.log(np.maximum(n, 1) / max_exact) / math.log(MAX_DISTANCE / max_exact)
                         * (half - max_exact)).astype(np.int32)
    large = np.minimum(large, half - 1)
    return ((rel > 0).astype(np.int32) * half + np.where(n < max_exact, n, large)).astype(np.int32)


def _attn_geometry(l):
    tqb = min(2 * ATT_SIDE, l)
    win = min(4 * ATT_SIDE, l)
    return tqb, win


def attn_bias_tables(rel_bias_g, dil, l):
    tqb, win = _attn_geometry(l)
    i = np.arange(tqb)[:, None]
    j = np.arange(win)[None, :]
    rel = np.stack([j - i - ATT_SIDE * shift for shift in range(3)])
    valid = np.abs(rel) <= ATT_SIDE
    onehot = (_t5_buckets(dil * rel)[..., None] == np.arange(N_BUCKETS)).astype(np.int8)
    b = jnp.einsum('sijb,bh->shij', jnp.asarray(onehot).astype(F32), rel_bias_g.astype(F32),
                   precision=lax.Precision.HIGHEST)
    return jnp.where(valid[:, None], b, NEG)


def _attn_kernel(q_ref, k_ref, v_ref, bias_ref, o_ref, lse_ref, *stage, l, dil, tql):
    tqb, win = _attn_geometry(l)
    single_step = tql == l
    qi = 0 if single_step else pl.program_id(2)
    base = RMS_BASE_DIL
    tok = tql * dil
    out_o, out_lse = stage if stage else (o_ref, lse_ref)
    group = 2 if (l == win == tqb and dil % 2 == 0) else 1
    if group == 2:
        b0 = bias_ref[0]
        neg = jnp.full_like(b0, NEG)
        pair_bias = jnp.concatenate([jnp.concatenate([b0, neg], axis=1), jnp.concatenate([neg, b0], axis=1)], axis=0)
    for r in range(0, dil, group):
        for c in range(tql // tqb):
            q0 = qi * tql + c * tqb
            if single_step:
                start = min(max(q0 - ATT_SIDE, 0), l - win)
            else:
                start = pl.multiple_of(jnp.clip(q0 - ATT_SIDE, 0, l - win), ATT_SIDE)
            shift = (q0 - start) // ATT_SIDE
            if group == 2:
                q = q_ref[r:r + 2].reshape(2 * tqb, HEAD_DIM)
                kw = k_ref[r:r + 2].reshape(2 * win, HEAD_DIM)
                vw = v_ref[r:r + 2].reshape(2 * win, HEAD_DIM)
                bias = pair_bias
            else:
                q = q_ref[r, c * tqb:(c + 1) * tqb, :]
                kw = k_ref[r, pl.ds(start, win), :]
                vw = v_ref[r, pl.ds(start, win), :]
                bias = bias_ref[shift]
            logits = lax.dot_general(q, kw, (((1,), (1,)), ((), ())), preferred_element_type=F32)
            logits = logits + bias
            m = jnp.max(logits, axis=-1, keepdims=True)
            p = jnp.exp(logits - m)
            s = jnp.sum(p, axis=-1, keepdims=True)
            o = jnp.dot(p.astype(vw.dtype), vw, preferred_element_type=F32) / s
            lse = m + jnp.log(s)
            for g in range(group):
                rr = r + g
                if stage:
                    rows = pl.ds((rr % base) * (tok // base) + c * tqb * base + rr // base, tqb, stride=base)
                elif dil == 1:
                    rows = pl.ds(c * tqb, tqb)
                else:
                    rows = pl.ds(c * tqb * dil + rr, tqb, stride=dil)
                out_o[rows, :] = o[g * tqb:(g + 1) * tqb, :]
                out_lse[rows, :] = jnp.broadcast_to(lse[g * tqb:(g + 1) * tqb, :], (tqb, HEAD_DIM))
    if stage:
        for rb in range(base):
            part = pl.ds(rb * (tok // base), tok // base)
            o_ref[pl.ds(rb, tok // base, stride=base), :] = out_o[part, :]
            lse_ref[pl.ds(rb, tok // base, stride=base), :] = out_lse[part, :]


def dilated_attention(qkv, bias, batch, seq, dil):
    l = seq // dil
    tqb, win = _attn_geometry(l)
    tql = min(l, max(tqb, 32 * tqb // dil))
    tok = tql * dil
    hw = HEADS_PER_GROUP * HEAD_DIM
    x = qkv.reshape(batch, dil, l, 3 * hw)
    out_sds = jax.ShapeDtypeStruct((batch, seq, hw), F32)
    out_spec = pl.BlockSpec((None, tok, HEAD_DIM), lambda b, h, i: (b, i, h))
    o, lse = pl.pallas_call(
        functools.partial(_attn_kernel, l=l, dil=dil, tql=tql),
        grid=(batch, HEADS_PER_GROUP, l // tql),
        in_specs=[
            pl.BlockSpec((None, dil, tql, HEAD_DIM), lambda b, h, i: (b, 0, i, h)),
            pl.BlockSpec((None, dil, l, HEAD_DIM), lambda b, h, i: (b, 0, 0, HEADS_PER_GROUP + h)),
            pl.BlockSpec((None, dil, l, HEAD_DIM), lambda b, h, i: (b, 0, 0, 2 * HEADS_PER_GROUP + h)),
            pl.BlockSpec((3, None, tqb, win), lambda b, h, i: (0, h, 0, 0)),
        ],
        out_specs=[out_spec, out_spec],
        out_shape=[out_sds, out_sds],
        scratch_shapes=[pltpu.VMEM((tok, HEAD_DIM), F32)] * 2 if dil == RMS_BASE_DIL ** 2 else [],
        compiler_params=_cparams(("parallel", "parallel", "parallel")),
        name=f"dilated_attn_d{dil}",
    )(x, x, x, bias)
    return o.reshape(batch * seq, hw), lse.reshape(batch * seq, hw)


def ssm_discretise(a_re, a_im, log_step, b_re, b_im, c_re, c_im):
    lam = lax.complex(a_re.astype(F32), a_im.astype(F32))
    step = jnp.exp(log_step.astype(F32))[..., None]
    lam_bar = jnp.exp(lam * step)
    b_bar = ((lam_bar - 1.0) / lam)[..., None] * lax.complex(b_re.astype(F32), b_im.astype(F32))
    c = lax.complex(c_re.astype(F32), c_im.astype(F32))
    return lam_bar, b_bar, c


def _block_diag(m):
    _, nj, lg, a, b = m.shape
    return jnp.einsum('djgab,gk->djgakb', m, jnp.eye(lg, dtype=F32)).reshape(2, nj, lg * a, lg * b)


def ssm_block_weights(lam_bar, b_bar, c):
    _, g, p, h = b_bar.shape
    lg = SSM_LANE_GROUPS
    nj = g // lg
    lam = lam_bar[..., None]
    ct = jnp.swapaxes(c, -1, -2)

    def grp(m):
        return m.reshape((2, nj, lg) + m.shape[2:])

    def state_in(bmat):
        t = grp(jnp.swapaxes(bmat, -1, -2))
        return jnp.concatenate([_block_diag(t.real), _block_diag(t.imag)], axis=-1)

    def state_out(cmat):
        t = grp(cmat)
        return jnp.concatenate([_block_diag(t.real), _block_diag(-t.imag)], axis=-2)

    def direct(k):
        m = jnp.einsum('dgpc,dgph->dghc', ct * lam ** k, b_bar).real
        return _block_diag(grp(m))

    first_in, second_in = state_in(lam * b_bar), state_in(b_bar)
    first_out, second_out = state_out(ct * lam), state_out(ct * lam ** 2)
    zero = jnp.zeros_like(direct(0))
    sel = jnp.array([0, 1]).reshape(2, 1, 1, 1)
    wb = jnp.where(sel == 0, jnp.concatenate([first_in, second_in], axis=-2),
                   jnp.concatenate([second_in, first_in], axis=-2))
    wx = jnp.where(sel == 0, jnp.concatenate([first_out, second_out], axis=-1),
                   jnp.concatenate([second_out, first_out], axis=-1))
    u_first = jnp.concatenate([direct(0), direct(1)], axis=-1)
    u_second = jnp.concatenate([zero, direct(0)], axis=-1)
    swap = lambda m: jnp.concatenate([m[..., 128:], m[..., :128]], axis=-1)
    wu = jnp.where(sel == 0, jnp.concatenate([u_first, u_second], axis=-2),
                   jnp.concatenate([swap(u_second), swap(u_first)], axis=-2))
    l2 = (lam_bar ** 2).reshape(2, nj, 1, 1, lg * p)
    lam2 = jnp.broadcast_to(jnp.concatenate([l2.real, l2.imag], axis=2), (2, nj, 2, 8, lg * p))
    return wb.astype(BF16), wx.astype(BF16), wu.astype(BF16), lam2.astype(F32)


def _s5_kernel(u_ref, wb_ref, wx_ref, wu_ref, lam_ref, x0_ref, *refs, tt, emit_y):
    n_out = 2 if emit_y else 1
    outs, scratch = refs[:n_out], list(refs[n_out:])
    xend_ref = outs[-1]
    parts = []
    for _ in range(S5_PARTS):
        parts.append([scratch.pop(0) for _ in range(4 if emit_y else 3)])
    st_ref = scratch.pop(0)
    direction = pl.program_id(0)
    ti = pl.program_id(2)
    nt = pl.num_programs(2)
    vb, _, lanes = u_ref.shape
    k = lam_ref.shape[-1]
    part_t = tt // S5_PARTS
    n = part_t // 2

    @pl.when(ti == 0)
    def _():
        st_ref[...] = x0_ref[...]

    lr = lam_ref[0]
    li = lam_ref[1]
    state = (st_ref[0], st_ref[1])
    for order, part in enumerate(parts):
        lhs_e, lhs_o, bu_ref = part[:3]
        t0 = jnp.where(direction == 0, order, S5_PARTS - 1 - order) * part_t
        for b in range(vb):
            lhs_e[pl.ds(b, n, stride=vb), :] = u_ref[b, pl.ds(t0, n, stride=2), :]
            lhs_o[pl.ds(b, n, stride=vb), :] = u_ref[b, pl.ds(t0 + 1, n, stride=2), :]
        lhs = jnp.concatenate([lhs_e[...], lhs_o[...]], axis=1).astype(BF16)
        bu_ref[...] = jnp.dot(lhs, wb_ref[...], preferred_element_type=F32)
        if emit_y:
            xs_ref = part[3]
            in_row = pl.multiple_of(jnp.where(direction == 0, 0, n * vb), vb)
            xs_ref[pl.ds(in_row, vb), :] = jnp.concatenate(state, axis=1)

        def step(s, carry, part=part):
            xr, xi = carry
            t = jnp.where(direction == 0, s, n - 1 - s)
            b = part[2][pl.ds(pl.multiple_of(t * vb, vb), vb), :]
            nr = lr * xr - li * xi + b[:, :k]
            ni = lr * xi + li * xr + b[:, k:]
            if emit_y:
                out_row = pl.multiple_of(jnp.where(direction == 0, t + 1, t) * vb, vb)
                part[3][pl.ds(out_row, vb), :] = jnp.concatenate([nr, ni], axis=1)
            return nr, ni

        state = lax.fori_loop(0, n, step, state, unroll=True)
        if emit_y:
            y_ref, ysc_e, ysc_o = outs[0], scratch[0], scratch[1]
            base = pl.multiple_of(jnp.where(direction == 0, 0, vb), vb)
            xs = xs_ref[pl.ds(base, n * vb), :].astype(BF16)
            y = (jnp.dot(xs, wx_ref[...], preferred_element_type=F32)
                 + jnp.dot(lhs, wu_ref[...], preferred_element_type=F32))
            ysc_e[...] = y[:, :lanes]
            ysc_o[...] = y[:, lanes:]
            for b in range(vb):
                y_ref[b, pl.ds(t0, n, stride=2), :] = ysc_e[pl.ds(b, n, stride=vb), :]
                y_ref[b, pl.ds(t0 + 1, n, stride=2), :] = ysc_o[pl.ds(b, n, stride=vb), :]
    st_ref[0] = state[0]
    st_ref[1] = state[1]

    @pl.when(ti == nt - 1)
    def _():
        xend_ref[0] = state[0]
        xend_ref[1] = state[1]


S5_PARTS = 2


def s5_scan(u, wts, x0, vbatch, emit_y, tt=512):
    width = u.shape[1]
    vseq = u.shape[0] // vbatch
    wb, wx, wu, lam2 = wts['wb'], wts['wx'], wts['wu'], wts['lam2']
    nj = wb.shape[1]
    k = lam2.shape[-1]
    lanes = width // nj
    nt = vseq // tt
    n = tt // S5_PARTS // 2

    def tmap(d, t):
        return jnp.where(d == 0, t, nt - 1 - t)

    state_spec = pl.BlockSpec((None, None, 2, vbatch, k), lambda d, j, t: (d, j, 0, 0, 0))
    state_sds = jax.ShapeDtypeStruct((2, nj, 2, vbatch, k), F32)
    out_specs, out_shape = [state_spec], [state_sds]
    half = pltpu.VMEM((n * vbatch, lanes), F32)
    part = [half, half, pltpu.VMEM((n * vbatch, 2 * k), F32)]
    if emit_y:
        part.append(pltpu.VMEM(((n + 1) * vbatch, 2 * k), F32))
    scratch = part * S5_PARTS + [pltpu.VMEM((2, vbatch, k), F32)]
    if emit_y:
        out_specs.insert(0, pl.BlockSpec((None, vbatch, tt, lanes), lambda d, j, t: (d, 0, tmap(d, t), j)))
        out_shape.insert(0, jax.ShapeDtypeStruct((2, vbatch, vseq, width), F32))
        scratch += [half, half]
    outs = pl.pallas_call(
        functools.partial(_s5_kernel, tt=tt, emit_y=emit_y),
        grid=(2, nj, nt),
        in_specs=[
            pl.BlockSpec((vbatch, tt, lanes), lambda d, j, t: (0, tmap(d, t), j)),
            pl.BlockSpec((None, None, 2 * lanes, 2 * k), lambda d, j, t: (d, j, 0, 0)),
            pl.BlockSpec((None, None, 2 * k, 2 * lanes), lambda d, j, t: (d, j, 0, 0)),
            pl.BlockSpec((None, None, 2 * lanes, 2 * lanes), lambda d, j, t: (d, j, 0, 0)),
            state_spec, state_spec,
        ],
        out_specs=out_specs,
        out_shape=out_shape,
        scratch_shapes=scratch,
        compiler_params=_cparams(("parallel", "parallel", "arbitrary")),
        name="s5_scan" if emit_y else "s5_states",
    )(u.reshape(vbatch, vseq, width), wb, wx, wu, lam2, x0)
    if emit_y:
        return outs[0].reshape(2, vbatch * vseq, width), outs[1]
    return None, outs[0]


def s5_segment_carries(xend, lam_bar, seg_len, batch, nseg):
    _, nj, _, _, k = xend.shape
    lam_pow = (lam_bar ** seg_len).reshape(2, nj, 1, k)
    e = lax.complex(xend[:, :, 0], xend[:, :, 1]).reshape(2, nj, batch, nseg, k)
    fwd = [jnp.zeros((nj, batch, k), jnp.complex64)]
    for s in range(1, nseg):
        fwd.append(lam_pow[0] * fwd[-1] + e[0, :, :, s - 1])
    bwd = [jnp.zeros((nj, batch, k), jnp.complex64)]
    for s in range(nseg - 2, -1, -1):
        bwd.append(lam_pow[1] * bwd[-1] + e[1, :, :, s + 1])
    bwd = bwd[::-1]
    x_in = jnp.stack([jnp.stack(fwd, axis=2), jnp.stack(bwd, axis=2)])
    x_in = x_in.reshape(2, nj, batch * nseg, k)
    return jnp.stack([x_in.real, x_in.imag], axis=2).astype(F32)


def _gelu_tanh(x):
    return 0.5 * x * (1.0 + jnp.tanh(math.sqrt(2.0 / math.pi) * (x + 0.044715 * x * x * x)))


def _glu_kernel(yf_ref, yb_ref, u_ref, d_ref, wa_ref, wb_ref, o_ref):
    y = _gelu_tanh(yf_ref[...] + yb_ref[...] + d_ref[...] * u_ref[...]).astype(BF16)
    za = jnp.dot(y, wa_ref[...], preferred_element_type=F32)
    zb = jnp.dot(y, wb_ref[...], preferred_element_type=F32)
    o_ref[...] = (za * _sigmoid(zb)).astype(o_ref.dtype)


def ssm_glu(y, u, d_skip, w_glu, tm=512):
    n, width = u.shape
    wa = w_glu[:, :width]
    wb = w_glu[:, width:]
    const = lambda i: (0, 0)
    return pl.pallas_call(
        _glu_kernel,
        grid=(n // tm,),
        in_specs=[
            pl.BlockSpec((None, tm, width), lambda i: (0, i, 0)),
            pl.BlockSpec((None, tm, width), lambda i: (1, i, 0)),
            pl.BlockSpec((tm, width), lambda i: (i, 0)),
            pl.BlockSpec((1, width), const),
            pl.BlockSpec((width, width), const),
            pl.BlockSpec((width, width), const),
        ],
        out_specs=pl.BlockSpec((tm, width), lambda i: (i, 0)),
        out_shape=jax.ShapeDtypeStruct((n, width), BF16),
        compiler_params=_cparams(("parallel",)),
        name="ssm_glu",
    )(y, y, u, d_skip.reshape(1, width).astype(F32), wa, wb)


def _split_bf16(x):
    hi = x.astype(BF16)
    lo = (x - hi.astype(F32)).astype(BF16)
    return hi, lo


def _pack_bf16_pair(a, b):
    wa = pltpu.bitcast(a.astype(BF16).astype(F32), jnp.int32)
    wb = pltpu.bitcast(b.astype(BF16).astype(F32), jnp.int32)
    return wa | lax.shift_right_logical(wb, 16)


def _unpack_bf16_pair(w):
    hi = pltpu.bitcast(w & jnp.int32(-65536), F32).astype(BF16)
    lo = pltpu.bitcast(lax.shift_left(w, 16), F32).astype(BF16)
    return hi, lo


def _mixer_out_kernel(o0, o1, o2, l0, l1, l2, ob_ref, gate_ref, x_ref, wa_ref, wb_ref, wo_ref,
                      g_ref, wr_ref, hbuf_ref, x1_ref, h_ref, lg_ref):
    del hbuf_ref
    d_model = x_ref.shape[1]
    la, lb, lc = l0[...], l1[...], l2[...]
    m = jnp.maximum(jnp.maximum(la, lb), lc)
    ea, eb, ec = jnp.exp(la - m), jnp.exp(lb - m), jnp.exp(lc - m)
    oa = (ea * o0[...] + eb * o1[...] + ec * o2[...]) / (ea + eb + ec)
    pa = jnp.dot(oa.astype(BF16), wa_ref[...], preferred_element_type=F32)
    pb = jnp.dot(ob_ref[...], wb_ref[...], preferred_element_type=F32)
    gates = gate_ref[...].astype(F32)
    merged = gates[:, :d_model] * pa + gates[:, d_model:] * pb
    x1 = x_ref[...] + jnp.dot(merged.astype(BF16), wo_ref[...], preferred_element_type=F32)
    x1_ref[...] = x1
    h = x1 * lax.rsqrt(jnp.mean(x1 * x1, axis=-1, keepdims=True) + EPS) * g_ref[...]
    h_ref[...] = _pack_bf16_pair(h[:, :d_model // 2], h[:, d_model // 2:])
    hh, hl = _split_bf16(h)
    er = lg_ref.shape[1]
    both = jnp.dot(hh, wr_ref[...], preferred_element_type=F32)
    lg_ref[...] = both[:, :er] + both[:, er:] + jnp.dot(hl, wr_ref[:, :er], preferred_element_type=F32)


def mixer_out(o_list, lse_list, o_b, gates, x, w_br_a, w_br_b, w_out, g_ffn, wr_split, h_buf, h_row0, tm=256):
    n, d_model = x.shape
    row = lambda w: pl.BlockSpec((tm, w), lambda i: (i, 0))
    full = lambda a: pl.BlockSpec(a.shape, lambda i: (0,) * a.ndim, pipeline_mode=pl.Buffered(1))
    g2 = g_ffn.reshape(1, d_model).astype(F32)
    er = wr_split.shape[1] // 2
    blk0 = h_row0 // tm
    in_specs = [row(o.shape[1]) for o in o_list] + [row(l.shape[1]) for l in lse_list] + [
        row(o_b.shape[1]), row(gates.shape[1]), row(d_model),
        full(w_br_a), full(w_br_b), full(w_out), full(g2), full(wr_split), pl.BlockSpec(memory_space=pl.ANY)]
    args = [*o_list, *lse_list, o_b, gates, x, w_br_a, w_br_b, w_out, g2, wr_split, h_buf]
    x1, h, logits = pl.pallas_call(
        _mixer_out_kernel,
        grid=(n // tm,),
        in_specs=in_specs,
        out_specs=[row(d_model), pl.BlockSpec((tm, d_model // 2), lambda i: (blk0 + i, 0)), row(er)],
        out_shape=[jax.ShapeDtypeStruct((n, d_model), F32), jax.ShapeDtypeStruct(h_buf.shape, h_buf.dtype),
                   jax.ShapeDtypeStruct((n, er), F32)],
        input_output_aliases={len(args) - 1: 1},
        compiler_params=_cparams(("parallel",)),
        name="mixer_out",
    )(*args)
    return x1, h, logits


def _expert_up_kernel(x_ref, wg_ref, wu_ref, o_ref, wg_s, wu_s):
    @pl.when(pl.program_id(2) == 0)
    def _():
        wg_s[...] = wg_ref[...].astype(BF16)
        wu_s[...] = wu_ref[...].astype(BF16)

    x_hi, x_lo = _unpack_bf16_pair(x_ref[...])
    half = x_hi.shape[1]

    def proj(w_s):
        return (jnp.dot(x_hi, w_s[:half, :], preferred_element_type=F32)
                + jnp.dot(x_lo, w_s[half:, :], preferred_element_type=F32))

    a = proj(wg_s)
    b = proj(wu_s)
    o_ref[...] = (a * _sigmoid(a) * b).astype(o_ref.dtype)


def _expert_up_into_kernel(x_ref, wg_ref, wu_ref, buf_ref, o_ref, wg_s, wu_s):
    del buf_ref
    _expert_up_kernel(x_ref, wg_ref, wu_ref, o_ref, wg_s, wu_s)


def expert_up(xe, w_gate, w_up, e0, hid=None, tm=1024, tf=512):
    n_e, cap, _ = xe.shape
    n_exp, d_model, ff = w_gate.shape
    tm = math.gcd(tm, cap)
    w_spec = pl.BlockSpec((None, d_model, tf), lambda ei, f, m: (e0 + ei, 0, f))
    in_specs = [pl.BlockSpec((None, tm, d_model // 2), lambda ei, f, m: (ei, m, 0)), w_spec, w_spec]
    args = [xe, w_gate, w_up]
    if hid is not None:
        in_specs.append(pl.BlockSpec(memory_space=pl.ANY))
        args.append(hid)
    return pl.pallas_call(
        _expert_up_kernel if hid is None else _expert_up_into_kernel,
        grid=(n_e, ff // tf, cap // tm),
        in_specs=in_specs,
        out_specs=pl.BlockSpec((None, tm, tf), lambda ei, f, m: (e0 + ei, m, f)),
        out_shape=jax.ShapeDtypeStruct((n_exp, cap, ff), BF16),
        scratch_shapes=[pltpu.VMEM((d_model, tf), BF16), pltpu.VMEM((d_model, tf), BF16)],
        input_output_aliases={} if hid is None else {3: 0},
        compiler_params=_cparams(("parallel", "parallel", "arbitrary")),
        name="expert_up",
    )(*args)


def _expert_down_kernel(h_ref, w_ref, o_ref, w_a, w_b, *, n_chunks):
    s, m = pl.program_id(0), pl.program_id(1)
    rows = w_ref.shape[0]
    chunk = pl.ds(pl.multiple_of(jnp.minimum(m, n_chunks - 1) * rows, rows), rows)

    @pl.when((s == 0) & (m == 0))
    def _():
        w_b[...] = jnp.zeros_like(w_b)

    def period(fill, use):
        fill[chunk, :] = w_ref[...].astype(BF16)
        acc = jnp.dot(h_ref[...], use[...], preferred_element_type=F32)
        half = acc.shape[1] // 2
        o_ref[...] = _pack_bf16_pair(acc[:, :half], acc[:, half:])

    pl.when(s % 2 == 0)(lambda: period(w_a, w_b))
    pl.when(s % 2 == 1)(lambda: period(w_b, w_a))


def expert_down(hid, w_down, tm=512, tn=1024):
    e, cap, ff = hid.shape
    d_model = w_down.shape[2]
    tm = math.gcd(tm, cap)
    tn = min(tn, d_model)
    nn, row_steps = d_model // tn, cap // tm
    n_tiles = e * nn
    n_chunks = math.gcd(row_steps, ff // 16)

    def fetched(s):
        t = jnp.minimum(s, n_tiles - 1)
        return t // nn, t % nn

    def used(s):
        t = jnp.maximum(s - 1, 0)
        return t // nn, t % nn

    return pl.pallas_call(
        functools.partial(_expert_down_kernel, n_chunks=n_chunks),
        grid=(n_tiles + 1, row_steps),
        in_specs=[
            pl.BlockSpec((None, tm, ff), lambda s, m: (used(s)[0], m, 0)),
            pl.BlockSpec((None, ff // n_chunks, tn),
                         lambda s, m: (fetched(s)[0], jnp.minimum(m, n_chunks - 1), fetched(s)[1])),
        ],
        out_specs=pl.BlockSpec((None, tm, tn // 2), lambda s, m: (jnp.where(s == 0, e, used(s)[0]), m, used(s)[1])),
        out_shape=jax.ShapeDtypeStruct((e + 1, cap, d_model // 2), jnp.int32),
        scratch_shapes=[pltpu.VMEM((ff, tn), BF16)] * 2,
        compiler_params=_cparams(("arbitrary", "arbitrary")),
        name="expert_down",
    )(hid, w_down)


ROUTER_ROWS = 512


def _router_kernel(lg_ref, o_ref, aff_ref, *, n_exp, cap):
    n, lanes = lg_ref.shape
    t = ROUTER_ROWS
    nt = n // t
    real = lax.broadcasted_iota(jnp.int32, (1, lanes), 1) < n_exp

    def rows(i):
        return pl.ds(pl.multiple_of(i * t, t), t)

    def softmax_tile(i, carry):
        x = jnp.where(real, lg_ref[rows(i), :], -jnp.inf)
        e = jnp.exp(x - jnp.max(x, axis=-1, keepdims=True))
        aff_ref[rows(i), :] = e / jnp.sum(e, axis=-1, keepdims=True)
        return carry

    lax.fori_loop(0, nt, softmax_tile, 0)

    def count(pred):
        def body(i, acc):
            hit = pred(pltpu.bitcast(aff_ref[rows(i), :], jnp.int32)).astype(jnp.int32)
            return acc + jnp.sum(hit.reshape(t // 8, 8, lanes), axis=0)
        acc = lax.fori_loop(0, nt, body, jnp.zeros((8, lanes), jnp.int32))
        return jnp.sum(acc, axis=0, keepdims=True)

    thr = jnp.zeros((1, lanes), jnp.int32)
    for bit in range(30, -1, -1):
        cand = thr | jnp.int32(1 << bit)
        thr = jnp.where(count(lambda k, cand=cand: k >= cand) >= cap, cand, thr)
    ties_needed = (cap - count(lambda k: k > thr)).astype(F32)

    r_iota = lax.broadcasted_iota(jnp.int32, (t, t), 0)
    c_iota = lax.broadcasted_iota(jnp.int32, (t, t), 1)
    prefix = (c_iota <= r_iota).astype(BF16)

    def select_tile(i, ties_before):
        aff = aff_ref[rows(i), :]
        key = pltpu.bitcast(aff, jnp.int32)
        tie = (key == thr).astype(BF16)
        tie_rank = ties_before + jnp.dot(prefix, tie, preferred_element_type=F32)
        chosen = (key > thr) | ((key == thr) & (tie_rank <= ties_needed))
        chosen = jnp.where(real & chosen, 1.0, 0.0)
        o_ref[rows(i), :] = jnp.where(real, aff, 0.0) + pltpu.roll(chosen, n_exp, 1)
        return tie_rank[t - 1:t, :]

    lax.fori_loop(0, nt, select_tile, jnp.zeros((1, lanes), F32))


def route_expert_choice(logits, n_exp, cap):
    n, lanes = logits.shape
    whole = pl.BlockSpec(memory_space=pltpu.VMEM)
    routed = pl.pallas_call(
        functools.partial(_router_kernel, n_exp=n_exp, cap=cap),
        in_specs=[whole],
        out_specs=whole,
        out_shape=jax.ShapeDtypeStruct((n, lanes), F32),
        scratch_shapes=[pltpu.VMEM((n, lanes), F32)],
        compiler_params=pltpu.CompilerParams(vmem_limit_bytes=V7X_VMEM_LIMIT),
        name="route_expert_choice",
    )(logits)
    chosen = routed[:, n_exp:2 * n_exp].T.reshape(-1) > 0.5
    pos = jnp.nonzero(chosen, size=n_exp * cap, fill_value=0)[0].astype(jnp.int32)
    gate = routed[:, :n_exp].T.reshape(-1)[pos]
    return gate, pos % n


SC_GATHER_CHUNK = 16


def sc_gather_rows(table, idx):
    n_idx = idx.shape[0]
    width = table.shape[1]
    info = plsc.get_sparse_core_info()
    n_workers = info.num_cores * info.num_subcores
    per_worker = n_idx // n_workers
    ch = SC_GATHER_CHUNK
    assert n_idx % n_workers == 0 and per_worker % (2 * ch) == 0
    mesh = plsc.VectorSubcoreMesh(core_axis_name="c", subcore_axis_name="s")

    @functools.partial(
        pl.kernel, mesh=mesh,
        out_type=jax.ShapeDtypeStruct((n_idx, width), table.dtype),
        scratch_types=[pltpu.VMEM((per_worker,), jnp.int32), pltpu.VMEM((ch, width), table.dtype),
                       pltpu.VMEM((ch, width), table.dtype), pltpu.SemaphoreType.DMA, pltpu.SemaphoreType.DMA],
        name="sc_gather_rows",
    )
    def gather(table_hbm, idx_hbm, out_hbm, idx_v, rows_a, rows_b, sem_a, sem_b):
        base = (lax.axis_index("s") * info.num_cores + lax.axis_index("c")) * per_worker
        pltpu.sync_copy(idx_hbm.at[pl.ds(base, per_worker)], idx_v)

        @pl.loop(0, per_worker // (2 * ch))
        def _(c):
            off = c * 2 * ch
            copy_a = pltpu.async_copy(table_hbm.at[idx_v.at[pl.ds(off, ch)]], rows_a, sem_a)
            copy_b = pltpu.async_copy(table_hbm.at[idx_v.at[pl.ds(off + ch, ch)]], rows_b, sem_b)
            copy_a.wait()
            pltpu.sync_copy(rows_a, out_hbm.at[pl.ds(base + off, ch)])
            copy_b.wait()
            pltpu.sync_copy(rows_b, out_hbm.at[pl.ds(base + off + ch, ch)])

    return gather(table, idx)


COMBINE_TOKENS = 512
COMBINE_ROWS = 512


def _combine_kernel(tile_ref, chunk_ref, first_ref, valid_ref, rows_ref, tok_ref, g_ref, o_ref, *, tn):
    p = pl.program_id(0)

    @pl.when(first_ref[p] == 1)
    def _():
        o_ref[...] = jnp.zeros_like(o_ref)

    @pl.when(valid_ref[p] == 1)
    def _():
        t, ch = o_ref.shape[0], tok_ref.shape[1]
        token = tile_ref[p] * t + lax.broadcasted_iota(jnp.int32, (t, ch), 0)
        weights = jnp.where(tok_ref[...] == token, g_ref[...], 0.0).astype(BF16)
        hi, lo = _unpack_bf16_pair(rows_ref[...])
        acc_hi = jnp.dot(weights, hi, preferred_element_type=F32)
        acc_lo = jnp.dot(weights, lo, preferred_element_type=F32)
        half = tn // 2
        for n in range(o_ref.shape[1] // tn):
            o_ref[:, n * tn:n * tn + half] += acc_hi[:, n * half:(n + 1) * half]
            o_ref[:, n * tn + half:(n + 1) * tn] += acc_lo[:, n * half:(n + 1) * half]


def combine_expert_outputs(rows, tok_sorted, gate_sorted, n_tok, tn):
    m, half_d = rows.shape
    t, ch = COMBINE_TOKENS, COMBINE_ROWS
    n_tiles, n_chunks = n_tok // t, m // ch
    n_pairs = n_tiles + n_chunks
    edges = jnp.arange(n_tiles + 1, dtype=jnp.int32) * t
    bounds = jnp.sum(tok_sorted[None, :] < edges[:, None], axis=1, dtype=jnp.int32)
    start, end = bounds[:-1], bounds[1:]
    c_lo = jnp.minimum(start // ch, n_chunks - 1)
    c_hi = jnp.where(end > start, (end - 1) // ch, c_lo)
    count = c_hi - c_lo + 1
    offs = jnp.cumsum(count)
    p = jnp.arange(n_pairs, dtype=jnp.int32)
    tile = jnp.minimum(jnp.sum(offs[None, :] <= p[:, None], axis=1, dtype=jnp.int32), n_tiles - 1)
    first_p = (offs - count)[tile]
    valid = p < offs[-1]
    chunk = jnp.where(valid, c_lo[tile] + p - first_p, c_hi[-1]).astype(jnp.int32)
    first = (valid & (p == first_p)).astype(jnp.int32)
    grid_spec = pltpu.PrefetchScalarGridSpec(
        num_scalar_prefetch=4,
        grid=(n_pairs,),
        in_specs=[
            pl.BlockSpec((ch, half_d), lambda i, tl, ck, fr, vl: (ck[i], 0)),
            pl.BlockSpec((1, ch), lambda i, tl, ck, fr, vl: (0, ck[i])),
            pl.BlockSpec((1, ch), lambda i, tl, ck, fr, vl: (0, ck[i])),
        ],
        out_specs=pl.BlockSpec((t, 2 * half_d), lambda i, tl, ck, fr, vl: (tl[i], 0)),
    )
    return pl.pallas_call(
        functools.partial(_combine_kernel, tn=tn),
        grid_spec=grid_spec,
        out_shape=jax.ShapeDtypeStruct((n_tok, 2 * half_d), F32),
        compiler_params=_cparams(("arbitrary",)),
        name="combine_expert_outputs",
    )(tile, chunk, first, valid.astype(jnp.int32), rows, tok_sorted.reshape(1, m), gate_sorted.reshape(1, m).astype(F32))


def _final_kernel(x1_ref, moe_ref, p_ref, g_ref, wg_ref, wp_ref, o_ref):
    x2 = x1_ref[...] + moe_ref[...]
    h = (x2 * lax.rsqrt(jnp.mean(x2 * x2, axis=-1, keepdims=True) + EPS) * g_ref[...]).astype(BF16)
    gate = _sigmoid(jnp.dot(h, wg_ref[...], preferred_element_type=F32))
    proj = jnp.dot(p_ref[...].astype(BF16), wp_ref[...], preferred_element_type=F32)
    o_ref[...] = x2 + gate * proj


def final_embed(x1, moe, p, g_ple, w_gate, w_proj, tm=512):
    n, d_model = x1.shape
    row = lambda w: pl.BlockSpec((tm, w), lambda i: (i, 0))
    full = lambda a: pl.BlockSpec(a.shape, lambda i: (0,) * a.ndim, pipeline_mode=pl.Buffered(1))
    g2 = g_ple.reshape(1, d_model).astype(F32)
    return pl.pallas_call(
        _final_kernel,
        grid=(n // tm,),
        in_specs=[row(d_model), row(d_model), row(p.shape[1]), full(g2), full(w_gate), full(w_proj)],
        out_specs=row(d_model),
        out_shape=jax.ShapeDtypeStruct((n, d_model), F32),
        compiler_params=_cparams(("parallel",)),
        name="final_embed",
    )(x1, moe, p, g2, w_gate, w_proj)


def _prepare_weights(rel_bias, g_mix, w_in, q_gain, k_gain, ssm_a_re, ssm_a_im, ssm_log_step,
                     ssm_b_re, ssm_b_im, ssm_c_re, ssm_c_im, ssm_d, w_glu, w_br_a, w_br_b, w_out,
                     g_ffn, w_router, g_ple, w_ple_gate, w_ple_proj):
    n_heads = q_gain.shape[0]
    att_w = n_heads * HEAD_DIM
    hw = HEADS_PER_GROUP * HEAD_DIM
    ssm_w = ssm_d.shape[0]
    w_in_bf = w_in.astype(BF16)
    w_qkv, gains = [], []
    qg = q_gain.astype(F32) * (HEAD_DIM ** -0.5)
    kg = k_gain.astype(F32)
    for gi in range(len(ATT_PATTERNS)):
        cols = [w_in_bf[:, s * att_w + gi * hw: s * att_w + (gi + 1) * hw] for s in range(3)]
        w_qkv.append(jnp.concatenate(cols, axis=1))
        hs = slice(gi * HEADS_PER_GROUP, (gi + 1) * HEADS_PER_GROUP)
        gains.append(jnp.stack([qg[hs].reshape(1, hw), kg[hs].reshape(1, hw), jnp.ones((1, hw), F32)]))
    lam_bar, b_bar, c = ssm_discretise(ssm_a_re, ssm_a_im, ssm_log_step, ssm_b_re, ssm_b_im, ssm_c_re, ssm_c_im)
    wb, wx, wu, lam2 = ssm_block_weights(lam_bar, b_bar, c)
    n_exp = w_router.shape[1]
    wr = jnp.pad(w_router.astype(F32), ((0, 0), (0, 128 - n_exp)))
    wr_split = jnp.concatenate(_split_bf16(wr), axis=1)
    return dict(
        rel_bias=rel_bias, bias_tables={}, g_mix=g_mix, w_qkv=w_qkv, gains=gains,
        w_u=w_in_bf[:, 3 * att_w:3 * att_w + ssm_w], w_gates=w_in_bf[:, 3 * att_w + ssm_w:],
        lam_bar=lam_bar, wb=wb, wx=wx, wu=wu, lam2=lam2, ssm_d=ssm_d, w_glu=w_glu.astype(BF16),
        w_br_a=w_br_a.astype(BF16), w_br_b=w_br_b.astype(BF16), w_out=w_out.astype(BF16),
        g_ffn=g_ffn, wr_split=wr_split, n_exp=n_exp, g_ple=g_ple,
        w_ple_gate=w_ple_gate.astype(BF16), w_ple_proj=w_ple_proj.astype(BF16))


def _ssm_branch(hn, wts, batch, seq):
    nseg = max(1, 8 // batch)
    vbatch = batch * nseg
    vseq = seq // nseg
    u = plain_proj(hn, wts['w_u'], F32, False, "u_proj")
    nj, k = wts['lam2'].shape[1], wts['lam2'].shape[-1]
    x0 = jnp.zeros((2, nj, 2, vbatch, k), F32)
    if nseg > 1:
        _, xend = s5_scan(u, wts, x0, vbatch, emit_y=False)
        x0 = s5_segment_carries(xend, wts['lam_bar'], vseq, batch, nseg)
    y, _ = s5_scan(u, wts, x0, vbatch, emit_y=True)
    return ssm_glu(y, u, wts['ssm_d'], wts['w_glu'])


MOE_LEAD_EXPERTS = 1


def _expert_choice(h_all, n_toks, logits_list, n_exp, w_exp_gate, w_exp_up, w_exp_down):
    d_model = w_exp_down.shape[2]
    half_d = h_all.shape[1]
    caps = [CAPACITY_FACTOR * n // n_exp for n in n_toks]
    cap_all = sum(caps)
    routes, picks, row0 = [], [], 0
    for n, logits, cap in zip(n_toks, logits_list, caps):
        gate_flat, idx_flat = route_expert_choice(logits, n_exp, cap)
        routes.append((gate_flat, idx_flat))
        picks.append(idx_flat.reshape(n_exp, cap) + row0)
        row0 += n
    picks = jnp.concatenate(picks, axis=1)
    hid, lead = None, min(MOE_LEAD_EXPERTS, n_exp)
    for e0, e1 in ((0, lead), (lead, n_exp)):
        if e1 > e0:
            xe = sc_gather_rows(h_all, picks[e0:e1].reshape(-1)).reshape(e1 - e0, cap_all, half_d)
            hid = expert_up(xe, w_exp_gate, w_exp_up, e0, hid)
    tn = min(1024, d_model)
    ye = expert_down(hid, w_exp_down, tn=tn).reshape(-1, half_d)
    outs, slot0 = [], 0
    for n, (gate_flat, idx_flat), cap in zip(n_toks, routes, caps):
        order = jnp.argsort(idx_flat).astype(jnp.int32)
        rows = (order // cap) * cap_all + slot0 + order % cap
        outs.append(combine_expert_outputs(sc_gather_rows(ye, rows), idx_flat[order], gate_flat[order], n, tn))
        slot0 += cap
    return outs


def _mixer(x, wts, h_buf, h_row0):
    batch, seq, d_model = x.shape
    n = batch * seq
    xf = x.reshape(n, d_model)
    dils = [dil for _, dil in ATT_PATTERNS]
    hn_list = rms_bf16(x, wts['g_mix'], dils)
    hn = hn_list[dils.index(1)]
    o_list, lse_list = [], []
    for gi, dil in enumerate(dils):
        qkv = qkv_proj(hn_list[gi], wts['w_qkv'][gi], wts['gains'][gi], f"qkv_proj_d{dil}")
        hs = slice(gi * HEADS_PER_GROUP, (gi + 1) * HEADS_PER_GROUP)
        bias_key = (gi,) + _attn_geometry(seq // dil)
        if bias_key not in wts['bias_tables']:
            wts['bias_tables'][bias_key] = attn_bias_tables(wts['rel_bias'][:, hs], dil, seq // dil)
        bias = wts['bias_tables'][bias_key]
        o, lse = dilated_attention(qkv, bias, batch, seq, dil)
        o_list.append(o)
        lse_list.append(lse)
    o_b = _ssm_branch(hn, wts, batch, seq)
    gates = plain_proj(hn, wts['w_gates'], BF16, True, "gates_proj")
    return mixer_out(o_list, lse_list, o_b, gates, xf, wts['w_br_a'], wts['w_br_b'],
                     wts['w_out'], wts['g_ffn'], wts['wr_split'], h_buf, h_row0)


def kernel(x_prompt, x_sample, p_prompt, p_sample, rel_bias, g_mix, w_in, q_gain, k_gain, ssm_a_re, ssm_a_im, ssm_log_step, ssm_b_re, ssm_b_im, ssm_c_re, ssm_c_im, ssm_d, w_glu, w_br_a, w_br_b, w_out, g_ffn, w_router, w_exp_gate, w_exp_up, w_exp_down, g_ple, w_ple_gate, w_ple_proj):
    assert g_mix.shape[0] == 1, "single-layer trunk"
    wts = _prepare_weights(rel_bias, g_mix[0], w_in[0], q_gain[0], k_gain[0], ssm_a_re[0], ssm_a_im[0],
                           ssm_log_step[0], ssm_b_re[0], ssm_b_im[0], ssm_c_re[0], ssm_c_im[0], ssm_d[0],
                           w_glu[0], w_br_a[0], w_br_b[0], w_out[0], g_ffn[0], w_router[0], g_ple[0],
                           w_ple_gate[0], w_ple_proj[0])
    xs, ps = (x_prompt, x_sample), (p_prompt[0], p_sample[0])
    n_toks = [x.shape[0] * x.shape[1] for x in xs]
    mixed, row0 = [], 0
    h_all = jnp.zeros((sum(n_toks), x_prompt.shape[-1] // 2), jnp.int32)
    for x, n in zip(xs, n_toks):
        x1, h_all, logits = _mixer(x, wts, h_all, row0)
        mixed.append((x1, logits))
        row0 += n
    moe = _expert_choice(h_all, n_toks, [m[1] for m in mixed], wts['n_exp'],
                         w_exp_gate[0], w_exp_up[0], w_exp_down[0])
    outs = []
    for x, p, (x1, _), y in zip(xs, ps, mixed, moe):
        out = final_embed(x1, y, p.reshape(-1, p.shape[-1]), wts['g_ple'], wts['w_ple_gate'], wts['w_ple_proj'])
        outs.append(out.reshape(x.shape))
    return tuple(outs)
```
